```python
import functools
import jax, jax.numpy as jnp
from jax import lax
import numpy as np

D_MODEL = 1024
BATCH = 32
SEQ = 256
DEPTH = 1
DEC_BATCH = 2
DEC_SEQ = 4096
PAST_LEN = 512

GRID_W = 64
NA_HEADS = 8
NA_HEAD_DIM = 64
NA_WIDTH = NA_HEADS * NA_HEAD_DIM
NA_KR = 8
NA_KC = 16
Q_BLOCK = 128
GDN_HEADS = 4
GDN_DK = 128
GDN_DV = 128
GDN_QK_WIDTH = GDN_HEADS * GDN_DK
GDN_V_WIDTH = GDN_HEADS * GDN_DV
GDN_CONV_CH = 2 * GDN_QK_WIDTH + GDN_V_WIDTH
CONV_K = 5
CHUNK = 64
N_EXPERTS = 256
TOP_K = 8
N_GROUPS = 8
TOPK_GROUPS = 4
EXPERT_DIM = 256
SHARED_DIM = 256
ROUTED_SCALE = 2.5
EPS = 1e-6
SPLITS = (3 * NA_WIDTH,
          3 * NA_WIDTH + GDN_CONV_CH,
          3 * NA_WIDTH + GDN_CONV_CH + GDN_V_WIDTH,
          3 * NA_WIDTH + GDN_CONV_CH + GDN_V_WIDTH + 2 * GDN_HEADS,
          3 * NA_WIDTH + GDN_CONV_CH + GDN_V_WIDTH + 4 * GDN_HEADS)
IN_WIDTH = SPLITS[-1] + 2 * D_MODEL

kernel_name = "hybrid_na_gdn_moe_diffusion_step"


def rms_norm(x, g):
    xf = x.astype(jnp.float32)
    y = xf * lax.rsqrt(jnp.mean(xf * xf, axis=-1, keepdims=True) + EPS)
    return (y * g.astype(jnp.float32)).astype(x.dtype)


def l2_normalize(x):
    xf = x.astype(jnp.float32)
    return (xf * lax.rsqrt(jnp.sum(xf * xf, axis=-1, keepdims=True) + EPS)).astype(x.dtype)


def modulate(h, shift, scale):
    return h * (1 + scale[:, None]) + shift[:, None]


def context_attention(q, k, v):
    B, S, H, d = q.shape
    nb = S // Q_BLOCK
    qb = q.reshape(B, nb, Q_BLOCK, H, d).transpose(1, 0, 2, 3, 4)

    def one_block(qi):
        s = jnp.einsum('bqhd,bkhd->bhqk', qi, k).astype(jnp.float32) * (d ** -0.5)
        p = jax.nn.softmax(s, axis=-1).astype(v.dtype)
        return jnp.einsum('bhqk,bkhd->bqhd', p, v)

    out = lax.map(one_block, qb)
    return out.transpose(1, 0, 2, 3, 4).reshape(B, S, H * d)


def neighbourhood_attention(q, k, v, ck, cv, rpb):
    B, N, H, d = q.shape
    rows = N // GRID_W
    kr = min(NA_KR, rows)
    r = np.arange(rows)
    row_start = np.clip(r - kr // 2, 0, rows - kr)
    row_idx = row_start[:, None] + np.arange(kr)[None, :]
    col = np.arange(GRID_W)
    col_start = np.clip(col - NA_KC // 2, 0, GRID_W - NA_KC)
    col_in = (col[None, :] >= col_start[:, None]) & (col[None, :] < col_start[:, None] + NA_KC)
    dr = row_idx - r[:, None] + (NA_KR - 1)
    dc = np.clip(col[None, :] - col[:, None], -(NA_KC - 1), NA_KC - 1) + (NA_KC - 1)
    qg = q.reshape(B, rows, GRID_W, H, d)
    kb = k.reshape(B, rows, GRID_W, H, d)[:, row_idx]
    vb = v.reshape(B, rows, GRID_W, H, d)[:, row_idx]
    scale = d ** -0.5
    bias = rpb.astype(jnp.float32)[:, dr[:, None, :, None], dc[None, :, None, :]]
    s_loc = jnp.einsum('brqhd,brkwhd->bhrqkw', qg, kb).astype(jnp.float32) * scale + bias[None]
    s_loc = jnp.where(col_in[:, None, :], s_loc, -jnp.inf).reshape(B, H, rows, GRID_W, kr * GRID_W)
    s_ctx = jnp.einsum('brqhd,bphd->bhrqp', qg, ck).astype(jnp.float32) * scale
    p = jax.nn.softmax(jnp.concatenate([s_loc, s_ctx], axis=-1), axis=-1).astype(v.dtype)
    p_loc = p[..., :kr * GRID_W].reshape(B, H, rows, GRID_W, kr, GRID_W)
    p_ctx = p[..., kr * GRID_W:]
    out = (jnp.einsum('bhrqkw,brkwhd->brqhd', p_loc, vb)
           + jnp.einsum('bhrqp,bphd->brqhd', p_ctx, cv))
    return out.reshape(B, N, H * d)


def centred_short_conv(x, w):
    T = x.shape[1]
    pad = CONV_K // 2
    xp = jnp.pad(x, ((0, 0), (pad, pad), (0, 0)))
    y = xp[:, 0:T] * w[0]
    for j in range(1, CONV_K):
        y = y + xp[:, j:j + T] * w[j]
    return jax.nn.silu(y)


def chunk_gated_delta_rule(q, k, v, g, beta, s0):
    f32 = jnp.float32
    B, T, H, dk = q.shape
    dv = v.shape[-1]
    n = T // CHUNK

    def chunks(t):
        return t.astype(f32).reshape(B, n, CHUNK, H, -1).transpose(1, 0, 3, 2, 4)

    qc = chunks(q) * (dk ** -0.5)
    kc = chunks(k)
    vc = chunks(v)
    bc = chunks(beta[..., None])
    gc = jnp.cumsum(chunks(g[..., None])[..., 0], axis=-1)
    idx = jnp.arange(CHUNK)
    incl = idx[:, None] >= idx[None, :]
    strict = idx[:, None] > idx[None, :]
    diff = gc[..., :, None] - gc[..., None, :]
    decay = jnp.where(incl, jnp.exp(jnp.where(incl, diff, 0.0)), 0.0)
    kb = kc * bc
    lmat = jnp.where(strict, jnp.einsum('nbhid,nbhjd->nbhij', kb, kc) * decay, 0.0)
    eye = jnp.eye(CHUNK, dtype=f32)
    tmat = lax.linalg.triangular_solve(lmat + eye, jnp.broadcast_to(eye, lmat.shape),
                                       left_side=True, lower=True, unit_diagonal=True)
    u = tmat @ (vc * bc)
    w = tmat @ (kb * jnp.exp(gc)[..., None])
    intra = jnp.where(incl, jnp.einsum('nbhid,nbhjd->nbhij', qc, kc) * decay, 0.0)

    def step(s, xs):
        q_i, k_i, u_i, w_i, a_i, g_i = xs
        v_new = u_i - w_i @ s
        o_i = (q_i * jnp.exp(g_i)[..., None]) @ s + a_i @ v_new
        g_last = g_i[..., -1:]
        s = s * jnp.exp(g_last)[..., None] + jnp.einsum(
            'bhck,bhcv->bhkv', k_i * jnp.exp(g_last - g_i)[..., None], v_new)
        return s, o_i

    s_fin, o = lax.scan(step, s0.astype(f32), (qc, kc, u, w, intra, gc))
    o = o.transpose(1, 0, 3, 2, 4).reshape(B, T, H, dv)
    return o.astype(v.dtype), s_fin


def gated_deltanet_bidir(q, k, v, g, beta, s0):
    o_f, s_f = chunk_gated_delta_rule(q, k, v, g[:, :, 0], beta[:, :, 0], s0[:, 0])
    rev = functools.partial(jnp.flip, axis=1)
    o_b, s_b = chunk_gated_delta_rule(rev(q), rev(k), rev(v), rev(g[:, :, 1]), rev(beta[:, :, 1]), s0[:, 1])
    return o_f + rev(o_b), jnp.stack([s_f, s_b], axis=1)


def gdn_features(qkv, b_logit, a_logit, conv_w, gdn_a_log, gdn_dt_bias):
    B, T, _ = qkv.shape
    qkv = centred_short_conv(qkv, conv_w)
    q, k, v = jnp.split(qkv, [GDN_QK_WIDTH, 2 * GDN_QK_WIDTH], axis=-1)
    q = l2_normalize(q.reshape(B, T, GDN_HEADS, GDN_DK))
    k = l2_normalize(k.reshape(B, T, GDN_HEADS, GDN_DK))
    v = v.reshape(B, T, GDN_HEADS, GDN_DV)
    beta = jax.nn.sigmoid(b_logit.astype(jnp.float32)).reshape(B, T, 2, GDN_HEADS)
    g = -jnp.exp(gdn_a_log.astype(jnp.float32)) * jax.nn.softplus(
        a_logit.astype(jnp.float32).reshape(B, T, 2, GDN_HEADS) + gdn_dt_bias.astype(jnp.float32))
    return q, k, v, g, beta


def moe_ffn(h, w_router, router_bias, w_exp_gate, w_exp_up, w_exp_down, w_sh_gate, w_sh_up, w_sh_down):
    shp = h.shape
    x = h.reshape(-1, D_MODEL)
    T = x.shape[0]
    scores = jax.nn.sigmoid((x @ w_router).astype(jnp.float32))
    biased = scores + router_bias.astype(jnp.float32)
    grp_score = lax.top_k(biased.reshape(T, N_GROUPS, -1), 2)[0].sum(-1)
    top_grp = lax.top_k(grp_score, TOPK_GROUPS)[1]
    grp_mask = (top_grp[..., None] == jnp.arange(N_GROUPS)).any(axis=1)
    masked = jnp.where(jnp.repeat(grp_mask, N_EXPERTS // N_GROUPS, axis=1), biased, -jnp.inf)
    top_e = lax.top_k(masked, TOP_K)[1]
    wts = jnp.take_along_axis(scores, top_e, axis=1)
    wts = wts / jnp.sum(wts, axis=-1, keepdims=True) * ROUTED_SCALE
    flat_e = top_e.reshape(-1)
    order = jnp.argsort(flat_e)
    tok = order // TOP_K
    xs = x[tok]
    sizes = jnp.bincount(flat_e, length=N_EXPERTS).astype(jnp.int32)
    hg = lax.ragged_dot(xs, w_exp_gate, sizes)
    hu = lax.ragged_dot(xs, w_exp_up, sizes)
    ye = lax.ragged_dot(jax.nn.silu(hg) * hu, w_exp_down, sizes)
    ye = ye * wts.reshape(-1)[order][:, None].astype(ye.dtype)
    routed = jnp.zeros_like(x).at[tok].add(ye)
    shared = (jax.nn.silu(x @ w_sh_gate) * (x @ w_sh_up)) @ w_sh_down
    return (routed + shared).reshape(shp)


def trunk_layer(x, cvec, attend, s0, w_ada, b_ada, g_pre_mix, g_post_mix, g_pre_ffn, g_post_ffn,
                w_in, conv_w, gdn_a_log, gdn_dt_bias, gdn_norm_w, w_na_up, w_gdn_up, w_out,
                w_router, router_bias, w_exp_gate, w_exp_up, w_exp_down, w_sh_gate, w_sh_up, w_sh_down):
    B, T, _ = x.shape
    sh1, sc1, gt1, sh2, sc2, gt2 = jnp.split(jax.nn.silu(cvec) @ w_ada + b_ada, 6, axis=-1)
    h = modulate(rms_norm(x, g_pre_mix), sh1, sc1)
    na_qkv, gdn_qkv, z, b_logit, a_logit, gate_logit = jnp.split(h @ w_in, list(SPLITS), axis=-1)
    q, k, v = (t.reshape(B, T, NA_HEADS, NA_HEAD_DIM) for t in jnp.split(na_qkv, 3, axis=-1))
    na_o = attend(q, k, v)
    gq, gk, gv, g, beta = gdn_features(gdn_qkv, b_logit, a_logit, conv_w, gdn_a_log, gdn_dt_bias)
    o, s_fin = gated_deltanet_bidir(gq, gk, gv, g, beta, s0)
    gdn_o = (rms_norm(o, gdn_norm_w) * jax.nn.silu(z.reshape(B, T, GDN_HEADS, GDN_DV))).reshape(B, T, GDN_V_WIDTH)
    gate_na, gate_gdn = jnp.split(jax.nn.sigmoid(gate_logit), 2, axis=-1)
    mix = (gate_na * (na_o @ w_na_up) + gate_gdn * (gdn_o @ w_gdn_up)) @ w_out
    x = x + gt1[:, None] * rms_norm(mix, g_post_mix)
    h = modulate(rms_norm(x, g_pre_ffn), sh2, sc2)
    ffn = moe_ffn(h, w_router, router_bias, w_exp_gate, w_exp_up, w_exp_down, w_sh_gate, w_sh_up, w_sh_down)
    x = x + gt2[:, None] * rms_norm(ffn, g_post_ffn)
    return x, k, v, s_fin


def setup_inputs(seed: int = 0) -> dict:
    key = jax.random.key(seed)
    ks = iter(jax.random.split(key, 40))
    f32 = jnp.float32

    def nrm(shape, scale):
        return scale * jax.random.normal(next(ks), shape, f32)

    def gain(shape):
        return 1.0 + nrm(shape, 0.05)

    L, D = DEPTH, D_MODEL
    a_log = jnp.log(jax.random.uniform(next(ks), (L, 2, GDN_HEADS), f32, 1.0, 16.0))
    dt = jnp.exp(jax.random.uniform(next(ks), (L, 2, GDN_HEADS), f32, float(np.log(1e-3)), float(np.log(1e-1))))
    dt_bias = dt + jnp.log(-jnp.expm1(-dt))
    return {
        "x_prompt": nrm((BATCH, SEQ, D), 1.0),
        "x_sample": nrm((DEC_BATCH, DEC_SEQ, D), 1.0),
        "cache_na_k": nrm((DEC_BATCH, L, PAST_LEN, NA_HEADS, NA_HEAD_DIM), 1.0),
        "cache_na_v": nrm((DEC_BATCH, L, PAST_LEN, NA_HEADS, NA_HEAD_DIM), 1.0),
        "state_gdn": nrm((DEC_BATCH, L, 2, GDN_HEADS, GDN_DK, GDN_DV), 0.3),
        "c": nrm((DEC_BATCH, D), 1.0),
        "c_ctx": nrm((D,), 1.0),
        "w_ada": nrm((L, D, 6 * D), 0.5 * D ** -0.5),
        "b_ada": nrm((L, 6 * D), 0.02),
        "g_pre_mix": gain((L, D)),
        "g_post_mix": gain((L, D)),
        "g_pre_ffn": gain((L, D)),
        "g_post_ffn": gain((L, D)),
        "w_in": nrm((L, D, IN_WIDTH), D ** -0.5),
        "conv_w": nrm((L, CONV_K, GDN_CONV_CH), CONV_K ** -0.5),
        "gdn_a_log": a_log,
        "gdn_dt_bias": dt_bias,
        "gdn_norm_w": gain((L, GDN_DV)),
        "na_rpb": nrm((L, NA_HEADS, 2 * NA_KR - 1, 2 * NA_KC - 1), 0.1),
        "w_na_up": nrm((L, NA_WIDTH, D), NA_WIDTH ** -0.5),
        "w_gdn_up": nrm((L, GDN_V_WIDTH, D), GDN_V_WIDTH ** -0.5),
        "w_out": nrm((L, D, D), D ** -0.5),
        "w_router": nrm((L, D, N_EXPERTS), D ** -0.5),
        "router_bias": nrm((L, N_EXPERTS), 0.01),
        "w_exp_gate": nrm((L, N_EXPERTS, D, EXPERT_DIM), D ** -0.5),
        "w_exp_up": nrm((L, N_EXPERTS, D, EXPERT_DIM), D ** -0.5),
        "w_exp_down": nrm((L, N_EXPERTS, EXPERT_DIM, D), EXPERT_DIM ** -0.5),
        "w_sh_gate": nrm((L, D, SHARED_DIM), D ** -0.5),
        "w_sh_up": nrm((L, D, SHARED_DIM), D ** -0.5),
        "w_sh_down": nrm((L, SHARED_DIM, D), SHARED_DIM ** -0.5),
    }


def reference(x_prompt, x_sample, cache_na_k, cache_na_v, state_gdn, c, c_ctx,
              w_ada, b_ada, g_pre_mix, g_post_mix, g_pre_ffn, g_post_ffn,
              w_in, conv_w, gdn_a_log, gdn_dt_bias, gdn_norm_w, na_rpb, w_na_up, w_gdn_up, w_out,
              w_router, router_bias, w_exp_gate, w_exp_up, w_exp_down, w_sh_gate, w_sh_up, w_sh_down):
    y_prompt = x_prompt
    y_sample = x_sample
    zero_state = jnp.zeros((x_prompt.shape[0], 2, GDN_HEADS, GDN_DK, GDN_DV), jnp.float32)
    new_k, new_v, new_s = [], [], []
    for l in range(DEPTH):
        lw = (w_ada[l], b_ada[l], g_pre_mix[l], g_post_mix[l], g_pre_ffn[l], g_post_ffn[l],
              w_in[l], conv_w[l], gdn_a_log[l], gdn_dt_bias[l], gdn_norm_w[l], w_na_up[l], w_gdn_up[l], w_out[l],
              w_router[l], router_bias[l], w_exp_gate[l], w_exp_up[l], w_exp_down[l],
              w_sh_gate[l], w_sh_up[l], w_sh_down[l])
        y_prompt, k_ctx, v_ctx, s_ctx = trunk_layer(y_prompt, c_ctx[None], context_attention, zero_state, *lw)
        new_k.append(k_ctx)
        new_v.append(v_ctx)
        new_s.append(s_ctx)
        attend = functools.partial(neighbourhood_attention, ck=cache_na_k[:, l], cv=cache_na_v[:, l], rpb=na_rpb[l])
        y_sample, _, _, _ = trunk_layer(y_sample, c, attend, state_gdn[:, l], *lw)
    new_na_k = jnp.stack(new_k, axis=1)
    new_na_v = jnp.stack(new_v, axis=1)
    new_gdn = jnp.stack(new_s, axis=1)
    return (y_prompt, y_sample, new_na_k, new_na_v, new_gdn)
```

```python
import functools

import numpy as np
import jax
import jax.numpy as jnp
from jax import lax
from jax.experimental import pallas as pl
from jax.experimental.pallas import tpu as pltpu

F32 = jnp.float32
BF16 = jnp.bfloat16
HI = lax.Precision.HIGHEST

D_MODEL = 1024
GRID_W = 64
NA_HEADS = 8
NA_HEAD_DIM = 64
NA_WIDTH = NA_HEADS * NA_HEAD_DIM
NA_KR = 8
NA_KC = 16
GDN_HEADS = 4
GDN_DK = 128
GDN_DV = 128
GDN_QK_WIDTH = GDN_HEADS * GDN_DK
GDN_V_WIDTH = GDN_HEADS * GDN_DV
GDN_CONV_CH = 2 * GDN_QK_WIDTH + GDN_V_WIDTH
CONV_K = 5
CHUNK = 64
N_EXPERTS = 256
TOP_K = 8
N_GROUPS = 8
TOPK_GROUPS = 4
EXPERT_DIM = 256
ROUTED_SCALE = 2.5
EPS = 1e-6
S_NA = 3 * NA_WIDTH
S_GDN = S_NA + GDN_CONV_CH
S_Z = S_GDN + GDN_V_WIDTH
S_B = S_Z + 2 * GDN_HEADS
S_A = S_B + 2 * GDN_HEADS

LANES = 128
TOK_TILE = 256
NA_QROWS = 8
NA_SPAN = 16
MOE_ROWS = 128
VMEM_LIMIT = 56 * 1024 * 1024
NEG_INF = float("-inf")


def _cparams(sem, vmem=VMEM_LIMIT):
    return pltpu.CompilerParams(dimension_semantics=sem, vmem_limit_bytes=vmem)


def _silu(x):
    return x * jax.nn.sigmoid(x)


def _rms(x, g):
    return x * lax.rsqrt(jnp.mean(x * x, axis=-1, keepdims=True) + EPS) * g


def _dot(a, b):
    return jnp.dot(a, b, preferred_element_type=F32)


def _dot_nt(a, b, precision=None):
    return lax.dot_general(a, b, (((1,), (1,)), ((), ())), precision=precision,
                           preferred_element_type=F32)


def _ada_kernel(c_ref, w_ref, b_ref, o_ref):
    o_ref[...] = jnp.dot(_silu(c_ref[...]), w_ref[...], precision=HI,
                         preferred_element_type=F32) + b_ref[...]


def _ada(cv, w_ada, b_ada):
    n = w_ada.shape[1]
    tn = 512
    return pl.pallas_call(
        _ada_kernel,
        grid=(n // tn,),
        in_specs=[pl.BlockSpec((8, D_MODEL), lambda j: (0, 0)),
                  pl.BlockSpec((D_MODEL, tn), lambda j: (0, j)),
                  pl.BlockSpec((1, tn), lambda j: (0, j))],
        out_specs=pl.BlockSpec((8, tn), lambda j: (0, j)),
        out_shape=jax.ShapeDtypeStruct((8, n), F32),
        compiler_params=_cparams(("parallel",)),
        name="ada",
    )(cv, w_ada, b_ada.reshape(1, n))


_PM_WIDTHS = (NA_WIDTH, NA_WIDTH, NA_WIDTH, GDN_CONV_CH, GDN_V_WIDTH, 2 * D_MODEL, LANES)


def _premix_kernel(x_ref, mod_ref, g_ref, w_ref, *o_refs):
    mod = mod_ref[0]
    h = _rms(x_ref[...], g_ref[...]) * (1.0 + mod[:, D_MODEL:2 * D_MODEL]) + mod[:, 0:D_MODEL]
    hb = h.astype(BF16)
    off = 0
    for o_ref, wd in zip(o_refs, _PM_WIDTHS):
        for c0 in range(0, wd, 512):
            c1 = min(c0 + 512, wd)
            o_ref[:, c0:c1] = _dot(hb, w_ref[:, off + c0:off + c1])
        off += wd


def _premix(x, mods3, g, w_cat, row_of_tile):
    n = x.shape[0]
    wtot = w_cat.shape[1]
    tm = TOK_TILE
    return pl.pallas_call(
        _premix_kernel,
        grid=(n // tm,),
        in_specs=[pl.BlockSpec((tm, D_MODEL), lambda i: (i, 0)),
                  pl.BlockSpec((1, 1, 6 * D_MODEL), lambda i: (row_of_tile(i), 0, 0)),
                  pl.BlockSpec((1, D_MODEL), lambda i: (0, 0)),
                  pl.BlockSpec((D_MODEL, wtot), lambda i: (0, 0))],
        out_specs=[pl.BlockSpec((tm, wd), lambda i: (i, 0)) for wd in _PM_WIDTHS],
        out_shape=[jax.ShapeDtypeStruct((n, wd), F32) for wd in _PM_WIDTHS],
        compiler_params=_cparams(("parallel",)),
        name="premix",
    )(x, mods3, g.reshape(1, D_MODEL), w_cat)


def _softmax_rows(s):
    m = jnp.max(s, axis=-1, keepdims=True)
    p = jnp.exp(s - m)
    return p / jnp.sum(p, axis=-1, keepdims=True)


def _ctx_attn_kernel(q_ref, k_ref, v_ref, o_ref):
    scale = NA_HEAD_DIM ** -0.5
    for hp in range(NA_HEADS // 2):
        outs = []
        for h in (2 * hp, 2 * hp + 1):
            sl = slice(h * NA_HEAD_DIM, (h + 1) * NA_HEAD_DIM)
            q = q_ref[0, :, sl].astype(BF16)
            k = k_ref[0, :, sl].astype(BF16)
            v = v_ref[0, :, sl].astype(BF16)
            p = _softmax_rows(_dot_nt(q, k) * scale)
            outs.append(_dot(p.astype(BF16), v))
        o_ref[0, :, hp * LANES:(hp + 1) * LANES] = jnp.concatenate(outs, axis=-1)


def _ctx_attn(q, k, v):
    b, t, w = q.shape
    spec = pl.BlockSpec((1, t, w), lambda i: (i, 0, 0))
    return pl.pallas_call(
        _ctx_attn_kernel,
        grid=(b,),
        in_specs=[spec, spec, spec],
        out_specs=spec,
        out_shape=jax.ShapeDtypeStruct((b, t, w), F32),
        compiler_params=_cparams(("parallel",)),
        name="ctx_attn",
    )(q, k, v)


def _na_span_base(j, rows):
    return np.clip(NA_QROWS * j - NA_KR // 2, 0, rows - NA_SPAN)


def _na_bias_tables(rpb, rows):
    col = np.arange(GRID_W)
    dcm = np.clip(col[None, :] - col[:, None], -(NA_KC - 1), NA_KC - 1) + (NA_KC - 1)
    onehot = (dcm[None] == np.arange(2 * NA_KC - 1)[:, None, None]).astype(np.float32)
    tab = jnp.einsum('hrd,dqk->hrqk', rpb.astype(F32), jnp.asarray(onehot), precision=HI)
    col_start = np.clip(col - NA_KC // 2, 0, GRID_W - NA_KC)
    col_in = (col[None, :] >= col_start[:, None]) & (col[None, :] < col_start[:, None] + NA_KC)
    tab = jnp.where(jnp.asarray(col_in)[None, None], tab, NEG_INF)
    n_dr = 2 * NA_KR - 1
    tab = jnp.concatenate([tab, jnp.full((NA_HEADS, 1, GRID_W, GRID_W), NEG_INF, F32)], axis=1)
    nblk = rows // NA_QROWS
    idx = np.full((3, NA_QROWS, NA_SPAN), n_dr, np.int32)
    for p, j in enumerate((0, 1, nblk - 1)):
        base = _na_span_base(j, rows)
        for ri in range(NA_QROWS):
            r = NA_QROWS * j + ri
            rs = np.clip(r - NA_KR // 2, 0, rows - NA_KR)
            for ki in range(NA_SPAN):
                kr = base + ki
                if rs <= kr < rs + NA_KR:
                    idx[p, ri, ki] = kr - r + NA_KR - 1
    bias = tab[:, idx]
    bias = bias.transpose(1, 0, 2, 4, 3, 5)
    return bias.reshape(3, NA_HEADS, NA_QROWS * GRID_W, NA_SPAN * GRID_W)


def _na_attn_kernel(q_ref, k_ref, v_ref, ck_ref, cv_ref, bias_ref, o_ref, *, rows):
    j = pl.program_id(2)
    scale = NA_HEAD_DIM ** -0.5
    base = jnp.clip(NA_QROWS * j - NA_KR // 2, 0, rows - NA_SPAN)
    start = pl.multiple_of(base * GRID_W, GRID_W)
    span = NA_SPAN * GRID_W
    q = q_ref[0, 0].astype(BF16)
    kl = k_ref[0, 0, pl.ds(start, span), :].astype(BF16)
    vl = v_ref[0, 0, pl.ds(start, span), :].astype(BF16)
    s_loc = _dot_nt(q, kl) * scale + bias_ref[0, 0]
    s_ctx = _dot_nt(q, ck_ref[0, 0].astype(BF16)) * scale
    m = jnp.maximum(jnp.max(s_loc, axis=-1, keepdims=True), jnp.max(s_ctx, axis=-1, keepdims=True))
    p_loc = jnp.exp(s_loc - m)
    p_ctx = jnp.exp(s_ctx - m)
    den = jnp.sum(p_loc, axis=-1, keepdims=True) + jnp.sum(p_ctx, axis=-1, keepdims=True)
    p_loc = (p_loc / den).astype(BF16)
    p_ctx = (p_ctx / den).astype(BF16)
    o_ref[0, 0] = _dot(p_loc, vl) + _dot(p_ctx, cv_ref[0, 0].astype(BF16))


def _na_attn(q, k, v, ck, cv, bias):
    b, h, n, d = q.shape
    p = ck.shape[2]
    rows = n // GRID_W
    nblk = rows // NA_QROWS
    qb = NA_QROWS * GRID_W

    def pattern(j):
        return jnp.where(j == 0, 0, jnp.where(j == nblk - 1, 2, 1))

    full = pl.BlockSpec((1, 1, n, d), lambda bi, hi, j: (bi, hi, 0, 0))
    ctx = pl.BlockSpec((1, 1, p, d), lambda bi, hi, j: (bi, hi, 0, 0))
    blk = pl.BlockSpec((1, 1, qb, d), lambda bi, hi, j: (bi, hi, j, 0))
    return pl.pallas_call(
        functools.partial(_na_attn_kernel, rows=rows),
        grid=(b, h, nblk),
        in_specs=[blk, full, full, ctx, ctx,
                  pl.BlockSpec((1, 1, qb, NA_SPAN * GRID_W), lambda bi, hi, j: (pattern(j), hi, 0, 0))],
        out_specs=blk,
        out_shape=jax.ShapeDtypeStruct((b, h, n, d), F32),
        compiler_params=_cparams(("parallel", "parallel", "arbitrary")),
        name="na_attn",
    )(q, k, v, ck, cv, bias)


def _gdn_conv_kernel(x_ref, w_ref, o_ref):
    c = pl.program_id(1)
    x = x_ref[0]
    t = x.shape[0]
    row = lax.broadcasted_iota(jnp.int32, x.shape, 0)
    y = jnp.zeros_like(x)
    for jj in range(CONV_K):
        o = jj - CONV_K // 2
        xs = x if o == 0 else pltpu.roll(x, (-o) % t, 0)
        xs = jnp.where((row + o >= 0) & (row + o < t), xs, 0.0)
        y = y + xs * w_ref[jj:jj + 1, :]
    y = _silu(y)
    nrm = lax.rsqrt(jnp.sum(y * y, axis=-1, keepdims=True) + EPS)
    n_qk = 2 * GDN_HEADS
    o_ref[0, 0] = jnp.where(c < n_qk, y * nrm, y)


def _gdn_conv(x, conv_w):
    b, t, ch = x.shape
    nc = ch // LANES
    return pl.pallas_call(
        _gdn_conv_kernel,
        grid=(b, nc),
        in_specs=[pl.BlockSpec((1, t, LANES), lambda bi, c: (bi, 0, c)),
                  pl.BlockSpec((CONV_K, LANES), lambda bi, c: (0, c))],
        out_specs=pl.BlockSpec((1, 1, t, LANES), lambda bi, c: (bi, c, 0, 0)),
        out_shape=jax.ShapeDtypeStruct((b, nc, t, LANES), F32),
        compiler_params=_cparams(("parallel", "parallel")),
        name="gdn_conv",
    )(x, conv_w)


def _gdn_chunk_kernel(qkv_ref, ba_ref, alog_ref, dt_ref, s0_ref, o_ref, sfin_ref, s_ref):
    d = pl.program_id(1)
    c = pl.program_id(2)
    nh = GDN_HEADS

    @pl.when(c == 0)
    def _():
        s_ref[...] = s0_ref[0, 0]

    fwd = d == 0
    ba = ba_ref[0]
    beta_all = jax.nn.sigmoid(ba)
    z = ba + dt_ref[...]
    softplus = jnp.maximum(z, 0.0) + jnp.log1p(jnp.exp(-jnp.abs(z)))
    g_all = -jnp.exp(alog_ref[...]) * softplus
    ii = lax.broadcasted_iota(jnp.int32, (CHUNK, CHUNK), 0)
    jj = lax.broadcasted_iota(jnp.int32, (CHUNK, CHUNK), 1)
    lag = jnp.where(fwd, ii - jj, jj - ii)
    incl = lag >= 0
    strict = lag > 0
    eye = (ii == jj).astype(F32)
    gc_all = jnp.dot(incl.astype(F32), g_all, precision=HI, preferred_element_type=F32)
    lane = lax.broadcasted_iota(jnp.int32, (CHUNK, LANES), 1)
    dsel = jnp.where(fwd, 0, nh)
    for h in range(nh):
        gcol = jnp.where(fwd, gc_all[:, 2 * nh + h:2 * nh + h + 1],
                         gc_all[:, 3 * nh + h:3 * nh + h + 1])
        beta = jnp.where(fwd, beta_all[:, h:h + 1], beta_all[:, nh + h:nh + h + 1])
        sel = (lane == 2 * nh + dsel + h).astype(F32)
        grow = _dot_nt(sel, gc_all, precision=HI)
        q = qkv_ref[0, h] * (GDN_DK ** -0.5)
        k = qkv_ref[0, nh + h]
        v = qkv_ref[0, 2 * nh + h]
        decay = jnp.where(incl, jnp.exp(jnp.where(incl, gcol - grow, 0.0)), 0.0)
        kb = k * beta
        lmat = jnp.where(strict, _dot_nt(kb, k, precision=HI) * decay, 0.0)
        pw = -lmat
        tmat = eye + pw
        for _ in range(5):
            pw = jnp.dot(pw, pw, precision=HI, preferred_element_type=F32)
            tmat = tmat + jnp.dot(tmat, pw, precision=HI, preferred_element_type=F32)
        eg = jnp.exp(gcol)
        u = jnp.dot(tmat, v * beta, precision=HI, preferred_element_type=F32)
        w = jnp.dot(tmat, kb * eg, precision=HI, preferred_element_type=F32)
        intra = jnp.where(incl, _dot_nt(q, k, precision=HI) * decay, 0.0)
        s = s_ref[h]
        v_new = u - jnp.dot(w, s, precision=HI, preferred_element_type=F32)
        o = (jnp.dot(q * eg, s, precision=HI, preferred_element_type=F32)
             + jnp.dot(intra, v_new, precision=HI, preferred_element_type=F32))
        g_last = jnp.where(fwd, gcol[CHUNK - 1:CHUNK, :], gcol[0:1, :])
        kd = k * jnp.exp(g_last - gcol)
        s_new = s * jnp.exp(g_last) + lax.dot_general(
            kd, v_new, (((0,), (0,)), ((), ())), precision=HI, preferred_element_type=F32)
        s_ref[h] = s_new
        o_ref[0, 0, :, h * GDN_DV:(h + 1) * GDN_DV] = o

    @pl.when(c == pl.num_programs(2) - 1)
    def _():
        sfin_ref[0, 0] = s_ref[...]


def _gdn_chunks(qkv, ba, alog_row, dt_row, s0):
    b, _, t, _ = qkv.shape
    n = t // CHUNK

    def cidx(d, c):
        return c + d * (n - 1 - 2 * c)

    return pl.pallas_call(
        _gdn_chunk_kernel,
        grid=(b, 2, n),
        in_specs=[pl.BlockSpec((1, 3 * GDN_HEADS, CHUNK, LANES), lambda bi, d, c: (bi, 0, cidx(d, c), 0)),
                  pl.BlockSpec((1, CHUNK, LANES), lambda bi, d, c: (bi, cidx(d, c), 0)),
                  pl.BlockSpec((1, LANES), lambda bi, d, c: (0, 0)),
                  pl.BlockSpec((1, LANES), lambda bi, d, c: (0, 0)),
                  pl.BlockSpec((1, 1, GDN_HEADS, GDN_DK, GDN_DV), lambda bi, d, c: (bi, d, 0, 0, 0))],
        out_specs=[pl.BlockSpec((1, 1, CHUNK, GDN_V_WIDTH), lambda bi, d, c: (d, bi, cidx(d, c), 0)),
                   pl.BlockSpec((1, 1, GDN_HEADS, GDN_DK, GDN_DV), lambda bi, d, c: (bi, d, 0, 0, 0))],
        out_shape=[jax.ShapeDtypeStruct((2, b, t, GDN_V_WIDTH), F32),
                   jax.ShapeDtypeStruct((b, 2, GDN_HEADS, GDN_DK, GDN_DV), F32)],
        scratch_shapes=[pltpu.VMEM((GDN_HEADS, GDN_DK, GDN_DV), F32)],
        compiler_params=_cparams(("parallel", "parallel", "arbitrary")),
        name="gdn_chunks",
    )(qkv, ba, alog_row, dt_row, s0)


def _pack_bf16_pairs(h):
    half = D_MODEL // 2
    lo = pltpu.bitcast(h[:, :half].astype(BF16).astype(F32), jnp.uint32)
    hi = pltpu.bitcast(h[:, half:].astype(BF16).astype(F32), jnp.uint32)
    return (hi & jnp.uint32(0xFFFF0000)) | (lo >> 16)


def _unpack_bf16_pairs(w):
    lo = pltpu.bitcast(w << 16, F32).astype(BF16)
    hi = pltpu.bitcast(w & jnp.uint32(0xFFFF0000), F32).astype(BF16)
    return lo, hi


def _mix_kernel(x_ref, na_ref, o_ref, z_ref, gate_ref, mod_ref, gnw_ref, gpost_ref, gpre_ref,
                wna_ref, wgdn_ref, wout_ref, x1_ref, h2_ref, h2p_ref):
    mod = mod_ref[0]
    o = o_ref[0] + o_ref[1]
    parts = []
    for h in range(GDN_HEADS):
        sl = slice(h * GDN_DV, (h + 1) * GDN_DV)
        parts.append(_rms(o[:, sl], gnw_ref[...]) * _silu(z_ref[:, sl]))
    gdn_o = jnp.concatenate(parts, axis=-1)
    a = _dot(na_ref[...].astype(BF16), wna_ref[...])
    b = _dot(gdn_o.astype(BF16), wgdn_ref[...])
    gate = jax.nn.sigmoid(gate_ref[...])
    pre = gate[:, :D_MODEL] * a + gate[:, D_MODEL:] * b
    mix = _dot(pre.astype(BF16), wout_ref[...])
    x1 = x_ref[...] + mod[:, 2 * D_MODEL:3 * D_MODEL] * _rms(mix, gpost_ref[...])
    x1_ref[...] = x1
    h2 = _rms(x1, gpre_ref[...]) * (1.0 + mod[:, 4 * D_MODEL:5 * D_MODEL]) + mod[:, 3 * D_MODEL:4 * D_MODEL]
    h2_ref[...] = h2
    h2p_ref[...] = _pack_bf16_pairs(h2)


def _mix(x, na_o, o2, z, gate, mods3, gnw, gpost, gpre, wna, wgdn, wout, row_of_tile):
    n = x.shape[0]
    tm = TOK_TILE
    row = lambda w: pl.BlockSpec((tm, w), lambda i: (i, 0))
    const = lambda r, c: pl.BlockSpec((r, c), lambda i: (0, 0))
    return pl.pallas_call(
        _mix_kernel,
        grid=(n // tm,),
        in_specs=[row(D_MODEL), row(NA_WIDTH),
                  pl.BlockSpec((2, tm, GDN_V_WIDTH), lambda i: (0, i, 0)),
                  row(GDN_V_WIDTH), row(2 * D_MODEL),
                  pl.BlockSpec((1, 1, 6 * D_MODEL), lambda i: (row_of_tile(i), 0, 0)),
                  const(1, GDN_DV), const(1, D_MODEL), const(1, D_MODEL),
                  const(NA_WIDTH, D_MODEL), const(GDN_V_WIDTH, D_MODEL), const(D_MODEL, D_MODEL)],
        out_specs=[row(D_MODEL), row(D_MODEL), row(D_MODEL // 2)],
        out_shape=[jax.ShapeDtypeStruct((n, D_MODEL), F32),
                   jax.ShapeDtypeStruct((n, D_MODEL), F32),
                   jax.ShapeDtypeStruct((n, D_MODEL // 2), jnp.uint32)],
        compiler_params=_cparams(("parallel",)),
        name="mix",
    )(x, na_o, o2, z, gate, mods3, gnw.reshape(1, GDN_DV), gpost.reshape(1, D_MODEL),
      gpre.reshape(1, D_MODEL), wna, wgdn, wout)


def _router_kernel(h_ref, w_ref, b_ref, e_ref, wt_ref, cnt_ref):
    i = pl.program_id(0)
    tm = h_ref.shape[0]
    ne = N_EXPERTS
    per = ne // N_GROUPS
    logits = jnp.dot(h_ref[...], w_ref[...], precision=HI, preferred_element_type=F32)
    scores = jax.nn.sigmoid(logits)
    biased = scores + b_ref[...]
    lane = lax.broadcasted_iota(jnp.int32, (tm, ne), 1).astype(F32)
    grp = jnp.floor(lane * (1.0 / per))
    lane_g = lax.broadcasted_iota(jnp.int32, (tm, LANES), 1).astype(F32)

    def first_max(x, iota, n):
        m = jnp.max(x, axis=-1, keepdims=True)
        idx = jnp.min(jnp.where(x == m, iota, float(n)), axis=-1, keepdims=True)
        return m, idx

    gs = jnp.full((tm, LANES), NEG_INF, F32)
    for g in range(N_GROUPS):
        mg = jnp.where(grp == float(g), biased, NEG_INF)
        m1, i1 = first_max(mg, lane, ne)
        m2 = jnp.max(jnp.where(lane == i1, NEG_INF, mg), axis=-1, keepdims=True)
        gs = jnp.where(lane_g == float(g), m1 + m2, gs)
    emask = jnp.zeros((tm, ne), jnp.bool_)
    for _ in range(TOPK_GROUPS):
        _, gi = first_max(gs, lane_g, LANES)
        gs = jnp.where(lane_g == gi, NEG_INF, gs)
        emask = emask | (grp == gi)
    masked = jnp.where(emask, biased, NEG_INF)
    e_out = jnp.zeros((tm, LANES), F32)
    w_out = jnp.zeros((tm, LANES), F32)
    onehot = jnp.zeros((tm, ne), F32)
    for kk in range(TOP_K):
        _, ei = first_max(masked, lane, ne)
        hit = lane == ei
        masked = jnp.where(hit, NEG_INF, masked)
        wk = jnp.sum(jnp.where(hit, scores, 0.0), axis=-1, keepdims=True)
        onehot = onehot + hit.astype(F32)
        e_out = jnp.where(lane_g == float(kk), ei, e_out)
        w_out = jnp.where(lane_g == float(kk), wk, w_out)
    wsum = jnp.sum(w_out, axis=-1, keepdims=True)
    e_ref[...] = e_out.astype(jnp.int32)
    wt_ref[...] = w_out / wsum * ROUTED_SCALE

    @pl.when(i == 0)
    def _():
        cnt_ref[...] = jnp.zeros_like(cnt_ref)

    cnt_ref[...] += jnp.sum(onehot, axis=0, keepdims=True)


def _router(h2, w_router, router_bias):
    n = h2.shape[0]
    tm = TOK_TILE
    return pl.pallas_call(
        _router_kernel,
        grid=(n // tm,),
        in_specs=[pl.BlockSpec((tm, D_MODEL), lambda i: (i, 0)),
                  pl.BlockSpec((D_MODEL, N_EXPERTS), lambda i: (0, 0)),
                  pl.BlockSpec((1, N_EXPERTS), lambda i: (0, 0))],
        out_specs=[pl.BlockSpec((tm, LANES), lambda i: (i, 0)),
                   pl.BlockSpec((tm, LANES), lambda i: (i, 0)),
                   pl.BlockSpec((1, N_EXPERTS), lambda i: (0, 0))],
        out_shape=[jax.ShapeDtypeStruct((n, LANES), jnp.int32),
                   jax.ShapeDtypeStruct((n, LANES), F32),
                   jax.ShapeDtypeStruct((1, N_EXPERTS), F32)],
        compiler_params=_cparams(("arbitrary",)),
        name="router",
    )(h2, w_router, router_bias.reshape(1, N_EXPERTS))


def _moe_kernel(tok_ref, wts_ref, starts_ref, h2p_hbm, wg_ref, wu_ref, wd_ref, out_hbm,
                h2p_ref, acc_ref, xs_ref, ye_ref, sem):
    e = pl.program_id(0)
    n_pairs = tok_ref.shape[0]
    half = D_MODEL // 2

    @pl.when(e == 0)
    def _():
        cp = pltpu.make_async_copy(h2p_hbm, h2p_ref, sem.at[0])
        cp.start()
        acc_ref[...] = jnp.zeros_like(acc_ref)
        cp.wait()

    start = starts_ref[e]
    cnt = starts_ref[e + 1] - start
    n_tiles = (cnt + MOE_ROWS - 1) // MOE_ROWS
    wg = wg_ref[0].astype(BF16)
    wu = wu_ref[0].astype(BF16)
    wd = wd_ref[0].astype(BF16)

    def tile_body(t, carry):
        base = start + t * MOE_ROWS
        n_valid = jnp.minimum(MOE_ROWS, cnt - t * MOE_ROWS)

        def gather_row(r, c):
            tok = tok_ref[jnp.minimum(base + r, n_pairs - 1)]
            xs_ref[pl.ds(r, 1), :] = h2p_ref[pl.ds(tok, 1), :]
            return c

        lax.fori_loop(0, MOE_ROWS, gather_row, 0, unroll=8)
        x_lo, x_hi = _unpack_bf16_pairs(xs_ref[...])
        hg = _dot(x_lo, wg[:half]) + _dot(x_hi, wg[half:])
        hu = _dot(x_lo, wu[:half]) + _dot(x_hi, wu[half:])
        act = (_silu(hg) * hu).astype(BF16)
        ye_ref[...] = _dot(act, wd)

        def scatter_row(r, c):
            tok = tok_ref[base + r]
            acc_ref[pl.ds(tok, 1), :] += wts_ref[base + r] * ye_ref[pl.ds(r, 1), :]
            return c

        lax.fori_loop(0, n_valid, scatter_row, 0)
        return carry

    lax.fori_loop(0, n_tiles, tile_body, 0)

    @pl.when(e == pl.num_programs(0) - 1)
    def _():
        cp = pltpu.make_async_copy(acc_ref, out_hbm, sem.at[1])
        cp.start()
        cp.wait()


def _moe(tok_sorted, wts_sorted, starts, h2p, wg, wu, wd):
    n = h2p.shape[0]
    grid_spec = pltpu.PrefetchScalarGridSpec(
        num_scalar_prefetch=3,
        grid=(N_EXPERTS,),
        in_specs=[pl.BlockSpec(memory_space=pl.ANY),
                  pl.BlockSpec((1, D_MODEL, EXPERT_DIM), lambda e, *_: (e, 0, 0)),
                  pl.BlockSpec((1, D_MODEL, EXPERT_DIM), lambda e, *_: (e, 0, 0)),
                  pl.BlockSpec((1, EXPERT_DIM, D_MODEL), lambda e, *_: (e, 0, 0))],
        out_specs=pl.BlockSpec(memory_space=pl.ANY),
        scratch_shapes=[pltpu.VMEM((n, D_MODEL // 2), jnp.uint32),
                        pltpu.VMEM((n, D_MODEL), F32),
                        pltpu.VMEM((MOE_ROWS, D_MODEL // 2), jnp.uint32),
                        pltpu.VMEM((MOE_ROWS, D_MODEL), F32),
                        pltpu.SemaphoreType.DMA((2,))],
    )
    return pl.pallas_call(
        _moe_kernel,
        grid_spec=grid_spec,
        out_shape=jax.ShapeDtypeStruct((n, D_MODEL), F32),
        compiler_params=_cparams(("arbitrary",), vmem=60 * 1024 * 1024),
        name="moe",
    )(tok_sorted, wts_sorted, starts, h2p, wg, wu, wd)


def _final_kernel(x1_ref, h2p_ref, r_ref, mod_ref, g_ref, wg_ref, wu_ref, wd_ref, y_ref):
    half = D_MODEL // 2
    mod = mod_ref[0]
    lo, hi = _unpack_bf16_pairs(h2p_ref[...])
    hg = _dot(lo, wg_ref[:half, :]) + _dot(hi, wg_ref[half:, :])
    hu = _dot(lo, wu_ref[:half, :]) + _dot(hi, wu_ref[half:, :])
    shared = _dot((_silu(hg) * hu).astype(BF16), wd_ref[...])
    ffn = r_ref[...] + shared
    y_ref[...] = x1_ref[...] + mod[:, 5 * D_MODEL:6 * D_MODEL] * _rms(ffn, g_ref[...])


def _final(x1, h2p, routed, mods3, g, wg, wu, wd, row_of_tile):
    n = x1.shape[0]
    tm = TOK_TILE
    sd = wg.shape[1]
    row = lambda w: pl.BlockSpec((tm, w), lambda i: (i, 0))
    const = lambda r, c: pl.BlockSpec((r, c), lambda i: (0, 0))
    return pl.pallas_call(
        _final_kernel,
        grid=(n // tm,),
        in_specs=[row(D_MODEL), row(D_MODEL // 2), row(D_MODEL),
                  pl.BlockSpec((1, 1, 6 * D_MODEL), lambda i: (row_of_tile(i), 0, 0)),
                  const(1, D_MODEL), const(D_MODEL, sd), const(D_MODEL, sd), const(sd, D_MODEL)],
        out_specs=row(D_MODEL),
        out_shape=jax.ShapeDtypeStruct((n, D_MODEL), F32),
        compiler_params=_cparams(("parallel",)),
        name="final",
    )(x1, h2p, routed, mods3, g.reshape(1, D_MODEL), wg, wu, wd)


def _trunk(x3, mods3, row_of_tile, attend, s0, wts):
    b, t, _ = x3.shape
    n = b * t
    x = x3.reshape(n, D_MODEL)
    q, k, v, gdn, z, gate, ba = _premix(x, mods3, wts["g_pre_mix"], wts["w_cat"], row_of_tile)
    na_o = attend(q, k, v).reshape(n, NA_WIDTH)
    qkv = _gdn_conv(gdn.reshape(b, t, GDN_CONV_CH), wts["conv_w"])
    o2, s_fin = _gdn_chunks(qkv, ba.reshape(b, t, LANES), wts["alog_row"], wts["dt_row"], s0)
    x1, h2, h2p = _mix(x, na_o, o2.reshape(2, n, GDN_V_WIDTH), z, gate, mods3, wts["gdn_norm_w"],
                       wts["g_post_mix"], wts["g_pre_ffn"], wts["w_na_up"], wts["w_gdn_up"],
                       wts["w_out"], row_of_tile)
    top_e, top_w, counts = _router(h2, wts["w_router"], wts["router_bias"])
    flat_e = top_e[:, :TOP_K].reshape(-1)
    flat_w = top_w[:, :TOP_K].reshape(-1)
    pair_tok = jnp.arange(n * TOP_K, dtype=jnp.int32) // TOP_K
    _, tok_sorted, wts_sorted = lax.sort((flat_e, pair_tok, flat_w), num_keys=1, is_stable=True)
    starts = jnp.concatenate([jnp.zeros((1,), jnp.int32),
                              jnp.cumsum(counts.reshape(-1).astype(jnp.int32))])
    routed = _moe(tok_sorted, wts_sorted, starts, h2p, wts["w_exp_gate"], wts["w_exp_up"],
                  wts["w_exp_down"])
    y = _final(x1, h2p, routed, mods3, wts["g_post_ffn"], wts["w_sh_gate"], wts["w_sh_up"],
               wts["w_sh_down"], row_of_tile)
    return y.reshape(b, t, D_MODEL), k, v, s_fin


def kernel(x_prompt, x_sample, cache_na_k, cache_na_v, state_gdn, c, c_ctx, w_ada, b_ada, g_pre_mix,
           g_post_mix, g_pre_ffn, g_post_ffn, w_in, conv_w, gdn_a_log, gdn_dt_bias, gdn_norm_w, na_rpb,
           w_na_up, w_gdn_up, w_out, w_router, router_bias, w_exp_gate, w_exp_up, w_exp_down, w_sh_gate,
           w_sh_up, w_sh_down):
    depth = w_ada.shape[0]
    bp, tp, _ = x_prompt.shape
    bs, ts, _ = x_sample.shape
    y_prompt, y_sample = x_prompt, x_sample
    zero_state = jnp.zeros((bp, 2, GDN_HEADS, GDN_DK, GDN_DV), F32)
    new_k, new_v, new_s = [], [], []
    for l in range(depth):
        cv = jnp.concatenate([c_ctx[None], c, jnp.zeros((8 - 1 - bs, D_MODEL), F32)], axis=0)
        mods3 = _ada(cv, w_ada[l], b_ada[l]).reshape(8, 1, 6 * D_MODEL)
        wl = w_in[l]
        w_cat = jnp.concatenate(
            [wl[:, :S_Z], wl[:, S_A:], wl[:, S_Z:S_A],
             jnp.zeros((D_MODEL, LANES - 4 * GDN_HEADS), F32)], axis=1).astype(BF16)
        pad = jnp.zeros((2 * GDN_HEADS,), F32)
        tail = jnp.zeros((LANES - 4 * GDN_HEADS,), F32)
        wts = dict(
            w_cat=w_cat, g_pre_mix=g_pre_mix[l], g_post_mix=g_post_mix[l], g_pre_ffn=g_pre_ffn[l],
            g_post_ffn=g_post_ffn[l], conv_w=conv_w[l], gdn_norm_w=gdn_norm_w[l],
            alog_row=jnp.concatenate([pad, gdn_a_log[l].reshape(-1), tail]).reshape(1, LANES),
            dt_row=jnp.concatenate([pad, gdn_dt_bias[l].reshape(-1), tail]).reshape(1, LANES),
            w_na_up=w_na_up[l].astype(BF16), w_gdn_up=w_gdn_up[l].astype(BF16),
            w_out=w_out[l].astype(BF16), w_router=w_router[l], router_bias=router_bias[l],
            w_exp_gate=w_exp_gate[l], w_exp_up=w_exp_up[l], w_exp_down=w_exp_down[l],
            w_sh_gate=w_sh_gate[l].astype(BF16), w_sh_up=w_sh_up[l].astype(BF16),
            w_sh_down=w_sh_down[l].astype(BF16))

        def ctx_attend(q, k, v):
            return _ctx_attn(q.reshape(bp, tp, NA_WIDTH), k.reshape(bp, tp, NA_WIDTH),
                             v.reshape(bp, tp, NA_WIDTH))

        y_prompt, k_ctx, v_ctx, s_ctx = _trunk(y_prompt, mods3, lambda i: 0, ctx_attend, zero_state, wts)
        new_k.append(k_ctx.reshape(bp, tp, NA_HEADS, NA_HEAD_DIM))
        new_v.append(v_ctx.reshape(bp, tp, NA_HEADS, NA_HEAD_DIM))
        new_s.append(s_ctx)

        bias = _na_bias_tables(na_rpb[l], ts // GRID_W)
        ck = cache_na_k[:, l].transpose(0, 2, 1, 3)
        cvv = cache_na_v[:, l].transpose(0, 2, 1, 3)

        def heads(a):
            return a.reshape(bs, ts, NA_HEADS, NA_HEAD_DIM).transpose(0, 2, 1, 3)

        def na_attend(q, k, v):
            o = _na_attn(heads(q), heads(k), heads(v), ck, cvv, bias)
            return o.transpose(0, 2, 1, 3)

        tiles_per_seq = ts // TOK_TILE
        y_sample, _, _, _ = _trunk(y_sample, mods3, lambda i: 1 + i // tiles_per_seq, na_attend,
                                   state_gdn[:, l], wts)
    return (y_prompt, y_sample, jnp.stack(new_k, axis=1), jnp.stack(new_v, axis=1),
            jnp.stack(new_s, axis=1))
```

```python
import functools

import numpy as np
import jax
import jax.numpy as jnp
from jax import lax
from jax.experimental import pallas as pl
from jax.experimental.pallas import tpu as pltpu

F32 = jnp.float32
BF16 = jnp.bfloat16
HI = lax.Precision.HIGHEST

D_MODEL = 1024
GRID_W = 64
NA_HEADS = 8
NA_HEAD_DIM = 64
NA_WIDTH = NA_HEADS * NA_HEAD_DIM
NA_KR = 8
NA_KC = 16
GDN_HEADS = 4
GDN_DK = 128
GDN_DV = 128
GDN_QK_WIDTH = GDN_HEADS * GDN_DK
GDN_V_WIDTH = GDN_HEADS * GDN_DV
GDN_CONV_CH = 2 * GDN_QK_WIDTH + GDN_V_WIDTH
CONV_K = 5
CHUNK = 64
N_EXPERTS = 256
TOP_K = 8
N_GROUPS = 8
TOPK_GROUPS = 4
EXPERT_DIM = 256
ROUTED_SCALE = 2.5
EPS = 1e-6
S_NA = 3 * NA_WIDTH
S_GDN = S_NA + GDN_CONV_CH
S_Z = S_GDN + GDN_V_WIDTH
S_B = S_Z + 2 * GDN_HEADS
S_A = S_B + 2 * GDN_HEADS

LANES = 128
TOK_TILE = 256
NA_QROWS = 8
NA_SPAN = 16
MOE_ROWS = 128
GDN_CHUNKS_PER_STEP = 4
VMEM_LIMIT = 56 * 1024 * 1024
NEG_INF = float("-inf")


def _cparams(sem, vmem=VMEM_LIMIT):
    return pltpu.CompilerParams(dimension_semantics=sem, vmem_limit_bytes=vmem)


def _silu(x):
    return x * jax.nn.sigmoid(x)


def _rms(x, g):
    return x * lax.rsqrt(jnp.mean(x * x, axis=-1, keepdims=True) + EPS) * g


def _dot(a, b):
    return jnp.dot(a, b, preferred_element_type=F32)


def _dot_nt(a, b, precision=None):
    return lax.dot_general(a, b, (((1,), (1,)), ((), ())), precision=precision,
                           preferred_element_type=F32)


def _ada_kernel(c_ref, w_ref, b_ref, o_ref):
    o_ref[...] = jnp.dot(_silu(c_ref[...]), w_ref[...], precision=HI,
                         preferred_element_type=F32) + b_ref[...]


def _ada(cv, w_ada, b_ada):
    n = w_ada.shape[1]
    tn = 512
    return pl.pallas_call(
        _ada_kernel,
        grid=(n // tn,),
        in_specs=[pl.BlockSpec((8, D_MODEL), lambda j: (0, 0)),
                  pl.BlockSpec((D_MODEL, tn), lambda j: (0, j)),
                  pl.BlockSpec((1, tn), lambda j: (0, j))],
        out_specs=pl.BlockSpec((8, tn), lambda j: (0, j)),
        out_shape=jax.ShapeDtypeStruct((8, n), F32),
        compiler_params=_cparams(("parallel",)),
        name="ada",
    )(cv, w_ada, b_ada.reshape(1, n))


_PM_WIDTHS = (NA_WIDTH, NA_WIDTH, NA_WIDTH, GDN_CONV_CH, GDN_V_WIDTH, 2 * D_MODEL, LANES)


def _premix_kernel(x_ref, mod_ref, g_ref, w_ref, *o_refs):
    mod = mod_ref[0]
    h = _rms(x_ref[...], g_ref[...]) * (1.0 + mod[:, D_MODEL:2 * D_MODEL]) + mod[:, 0:D_MODEL]
    hb = h.astype(BF16)
    off = 0
    for o_ref, wd in zip(o_refs, _PM_WIDTHS):
        for c0 in range(0, wd, 512):
            c1 = min(c0 + 512, wd)
            o_ref[:, c0:c1] = _dot(hb, w_ref[:, off + c0:off + c1])
        off += wd


def _premix(x, mods3, g, w_cat, row_of_tile):
    n = x.shape[0]
    wtot = w_cat.shape[1]
    tm = TOK_TILE
    return pl.pallas_call(
        _premix_kernel,
        grid=(n // tm,),
        in_specs=[pl.BlockSpec((tm, D_MODEL), lambda i: (i, 0)),
                  pl.BlockSpec((1, 1, 6 * D_MODEL), lambda i: (row_of_tile(i), 0, 0)),
                  pl.BlockSpec((1, D_MODEL), lambda i: (0, 0)),
                  pl.BlockSpec((D_MODEL, wtot), lambda i: (0, 0))],
        out_specs=[pl.BlockSpec((tm, wd), lambda i: (i, 0)) for wd in _PM_WIDTHS],
        out_shape=[jax.ShapeDtypeStruct((n, wd), F32) for wd in _PM_WIDTHS],
        compiler_params=_cparams(("parallel",)),
        name="premix",
    )(x, mods3, g.reshape(1, D_MODEL), w_cat)


def _softmax_rows(s):
    m = jnp.max(s, axis=-1, keepdims=True)
    p = jnp.exp(s - m)
    return p / jnp.sum(p, axis=-1, keepdims=True)


def _ctx_attn_kernel(q_ref, k_ref, v_ref, o_ref):
    scale = NA_HEAD_DIM ** -0.5
    for hp in range(NA_HEADS // 2):
        outs = []
        for h in (2 * hp, 2 * hp + 1):
            sl = slice(h * NA_HEAD_DIM, (h + 1) * NA_HEAD_DIM)
            q = q_ref[0, :, sl].astype(BF16)
            k = k_ref[0, :, sl].astype(BF16)
            v = v_ref[0, :, sl].astype(BF16)
            p = _softmax_rows(_dot_nt(q, k) * scale)
            outs.append(_dot(p.astype(BF16), v))
        o_ref[0, :, hp * LANES:(hp + 1) * LANES] = jnp.concatenate(outs, axis=-1)


def _ctx_attn(q, k, v):
    b, t, w = q.shape
    spec = pl.BlockSpec((1, t, w), lambda i: (i, 0, 0))
    return pl.pallas_call(
        _ctx_attn_kernel,
        grid=(b,),
        in_specs=[spec, spec, spec],
        out_specs=spec,
        out_shape=jax.ShapeDtypeStruct((b, t, w), F32),
        compiler_params=_cparams(("parallel",)),
        name="ctx_attn",
    )(q, k, v)


def _na_span_base(j, rows):
    return np.clip(NA_QROWS * j - NA_KR // 2, 0, rows - NA_SPAN)


def _na_bias_tables(rpb, rows):
    col = np.arange(GRID_W)
    dcm = np.clip(col[None, :] - col[:, None], -(NA_KC - 1), NA_KC - 1) + (NA_KC - 1)
    onehot = (dcm[None] == np.arange(2 * NA_KC - 1)[:, None, None]).astype(np.float32)
    tab = jnp.einsum('hrd,dqk->hrqk', rpb.astype(F32), jnp.asarray(onehot), precision=HI)
    col_start = np.clip(col - NA_KC // 2, 0, GRID_W - NA_KC)
    col_in = (col[None, :] >= col_start[:, None]) & (col[None, :] < col_start[:, None] + NA_KC)
    tab = jnp.where(jnp.asarray(col_in)[None, None], tab, NEG_INF)
    n_dr = 2 * NA_KR - 1
    tab = jnp.concatenate([tab, jnp.full((NA_HEADS, 1, GRID_W, GRID_W), NEG_INF, F32)], axis=1)
    nblk = rows // NA_QROWS
    idx = np.full((3, NA_QROWS, NA_SPAN), n_dr, np.int32)
    for p, j in enumerate((0, 1, nblk - 1)):
        base = _na_span_base(j, rows)
        for ri in range(NA_QROWS):
            r = NA_QROWS * j + ri
            rs = np.clip(r - NA_KR // 2, 0, rows - NA_KR)
            for ki in range(NA_SPAN):
                kr = base + ki
                if rs <= kr < rs + NA_KR:
                    idx[p, ri, ki] = kr - r + NA_KR - 1
    bias = tab[:, idx]
    bias = bias.transpose(1, 0, 2, 4, 3, 5)
    return bias.reshape(3, NA_HEADS, NA_QROWS * GRID_W, NA_SPAN * GRID_W)


def _na_attn_kernel(q_ref, k_ref, v_ref, ck_ref, cv_ref, bias_ref, o_ref, *, rows):
    j = pl.program_id(2)
    scale = NA_HEAD_DIM ** -0.5
    base = jnp.clip(NA_QROWS * j - NA_KR // 2, 0, rows - NA_SPAN)
    start = pl.multiple_of(base * GRID_W, GRID_W)
    span = NA_SPAN * GRID_W
    q = q_ref[0, 0].astype(BF16)
    kl = k_ref[0, 0, pl.ds(start, span), :].astype(BF16)
    vl = v_ref[0, 0, pl.ds(start, span), :].astype(BF16)
    s_loc = _dot_nt(q, kl) * scale + bias_ref[0, 0]
    s_ctx = _dot_nt(q, ck_ref[0, 0].astype(BF16)) * scale
    m = jnp.maximum(jnp.max(s_loc, axis=-1, keepdims=True), jnp.max(s_ctx, axis=-1, keepdims=True))
    p_loc = jnp.exp(s_loc - m)
    p_ctx = jnp.exp(s_ctx - m)
    den = jnp.sum(p_loc, axis=-1, keepdims=True) + jnp.sum(p_ctx, axis=-1, keepdims=True)
    p_loc = (p_loc / den).astype(BF16)
    p_ctx = (p_ctx / den).astype(BF16)
    o_ref[0, 0] = _dot(p_loc, vl) + _dot(p_ctx, cv_ref[0, 0].astype(BF16))


def _na_attn(q, k, v, ck, cv, bias):
    b, h, n, d = q.shape
    p = ck.shape[2]
    rows = n // GRID_W
    nblk = rows // NA_QROWS
    qb = NA_QROWS * GRID_W

    def pattern(j):
        return jnp.where(j == 0, 0, jnp.where(j == nblk - 1, 2, 1))

    full = pl.BlockSpec((1, 1, n, d), lambda bi, hi, j: (bi, hi, 0, 0))
    ctx = pl.BlockSpec((1, 1, p, d), lambda bi, hi, j: (bi, hi, 0, 0))
    blk = pl.BlockSpec((1, 1, qb, d), lambda bi, hi, j: (bi, hi, j, 0))
    return pl.pallas_call(
        functools.partial(_na_attn_kernel, rows=rows),
        grid=(b, h, nblk),
        in_specs=[blk, full, full, ctx, ctx,
                  pl.BlockSpec((1, 1, qb, NA_SPAN * GRID_W), lambda bi, hi, j: (pattern(j), hi, 0, 0))],
        out_specs=blk,
        out_shape=jax.ShapeDtypeStruct((b, h, n, d), F32),
        compiler_params=_cparams(("parallel", "parallel", "arbitrary")),
        name="na_attn",
    )(q, k, v, ck, cv, bias)


def _gdn_conv_kernel(x_ref, w_ref, o_ref):
    c = pl.program_id(1)
    x = x_ref[0]
    t = x.shape[0]
    row = lax.broadcasted_iota(jnp.int32, x.shape, 0)
    y = jnp.zeros_like(x)
    for jj in range(CONV_K):
        o = jj - CONV_K // 2
        xs = x if o == 0 else pltpu.roll(x, (-o) % t, 0)
        xs = jnp.where((row + o >= 0) & (row + o < t), xs, 0.0)
        y = y + xs * w_ref[jj:jj + 1, :]
    y = _silu(y)
    nrm = lax.rsqrt(jnp.sum(y * y, axis=-1, keepdims=True) + EPS)
    n_qk = 2 * GDN_HEADS
    o_ref[0, 0] = jnp.where(c < n_qk, y * nrm, y)


def _gdn_conv(x, conv_w):
    b, t, ch = x.shape
    nc = ch // LANES
    return pl.pallas_call(
        _gdn_conv_kernel,
        grid=(b, nc),
        in_specs=[pl.BlockSpec((1, t, LANES), lambda bi, c: (bi, 0, c)),
                  pl.BlockSpec((CONV_K, LANES), lambda bi, c: (0, c))],
        out_specs=pl.BlockSpec((1, 1, t, LANES), lambda bi, c: (bi, c, 0, 0)),
        out_shape=jax.ShapeDtypeStruct((b, nc, t, LANES), F32),
        compiler_params=_cparams(("parallel", "parallel")),
        name="gdn_conv",
    )(x, conv_w)


def _bdot(a, b):
    return jnp.dot(a.astype(BF16), b.astype(BF16), preferred_element_type=F32)


def _split_bf16(x):
    hi = x.astype(BF16)
    return hi, (x - hi.astype(F32)).astype(BF16)


def _dot3(a, b):
    m = a.shape[0]
    ah, al = _split_bf16(a)
    bh, bl = _split_bf16(b)
    top = _dot(jnp.concatenate([ah, al], axis=0), bh)
    return top[:m] + top[m:] + _dot(ah, bl)


def _gdn_chunk_kernel(qkv_ref, ba_ref, alog_ref, dt_ref, s0_ref, o_ref, sfin_ref, s_ref, *, reverse, cb):
    c = pl.program_id(1)
    nh = GDN_HEADS

    @pl.when(c == 0)
    def _():
        s_ref[...] = s0_ref[0, 0]

    ii = lax.broadcasted_iota(jnp.int32, (CHUNK, CHUNK), 0)
    jj = lax.broadcasted_iota(jnp.int32, (CHUNK, CHUNK), 1)
    lag = (jj - ii) if reverse else (ii - jj)
    incl = lag >= 0
    strict = lag > 0
    eye = (ii == jj).astype(F32)
    tri = incl.astype(F32)
    bcol = nh if reverse else 0
    gcol0 = (3 if reverse else 2) * nh
    sub8 = lax.broadcasted_iota(jnp.int32, (8, LANES), 0)
    lane8 = lax.broadcasted_iota(jnp.int32, (8, LANES), 1)
    sel8 = (lane8 == gcol0 + sub8).astype(F32)
    units = [(ci, h) for ci in range(cb) for h in range(nh)]
    gc_alls, beta_alls, grow8s = [], [], []
    for ci in range(cb):
        ba = ba_ref[0, ci * CHUNK:(ci + 1) * CHUNK, :]
        z = ba + dt_ref[...]
        softplus = jnp.maximum(z, 0.0) + jnp.log1p(jnp.exp(-jnp.abs(z)))
        g_all = -jnp.exp(alog_ref[...]) * softplus
        gc_all = jnp.dot(tri, g_all, precision=HI, preferred_element_type=F32)
        gc_alls.append(gc_all)
        beta_alls.append(jax.nn.sigmoid(ba))
        grow8s.append(_dot_nt(sel8, gc_all, precision=HI))
    gcol = [gc_alls[ci][:, gcol0 + h:gcol0 + h + 1] for ci, h in units]
    beta = [beta_alls[ci][:, bcol + h:bcol + h + 1] for ci, h in units]
    rows = [slice(ci * CHUNK, (ci + 1) * CHUNK) for ci, _ in units]
    k = [qkv_ref[0, nh + h, rows[u], :] for u, (_, h) in enumerate(units)]
    kb = [k[u] * beta[u] for u in range(len(units))]
    q = [qkv_ref[0, h, rows[u], :] * (GDN_DK ** -0.5) for u, (_, h) in enumerate(units)]
    kq = [_dot_nt(jnp.concatenate([kb[u], q[u]], axis=0).astype(BF16), k[u].astype(BF16))
          for u in range(len(units))]
    decay = [jnp.where(incl, jnp.exp(jnp.where(incl, gcol[u] - grow8s[ci][h:h + 1, :], 0.0)), 0.0)
             for u, (ci, h) in enumerate(units)]
    intra = [jnp.where(incl, kq[u][CHUNK:] * decay[u], 0.0) for u in range(len(units))]
    pw = [-jnp.where(strict, kq[u][:CHUNK] * decay[u], 0.0) for u in range(len(units))]
    tmat = [eye + p for p in pw]
    pw = [_dot3(p, p) for p in pw]
    for _ in range(4):
        pt = [_dot3(jnp.concatenate([pw[u], tmat[u]], axis=0), pw[u]) for u in range(len(units))]
        pw = [x[:CHUNK] for x in pt]
        tmat = [tmat[u] + pt[u][CHUNK:] for u in range(len(units))]
    tmat = [tmat[u] + _dot3(tmat[u], pw[u]) for u in range(len(units))]
    eg = [jnp.exp(g) for g in gcol]
    uw = [_bdot(tmat[u], jnp.concatenate(
        [qkv_ref[0, 2 * nh + h, rows[u], :] * beta[u], kb[u] * eg[u]], axis=1))
        for u, (_, h) in enumerate(units)]
    g_last = [g[0:1, :] if reverse else g[CHUNK - 1:CHUNK, :] for g in gcol]
    kd = [k[u] * jnp.exp(g_last[u] - gcol[u]) for u in range(len(units))]
    qe = [q[u] * eg[u] for u in range(len(units))]
    s = [s_ref[h] for h in range(nh)]
    for ci in (reversed(range(cb)) if reverse else range(cb)):
        us = [ci * nh + h for h in range(nh)]
        wq = [_bdot(jnp.concatenate([uw[u][:, GDN_DV:], qe[u]], axis=0), s[h]) for h, u in enumerate(us)]
        v_new = [uw[u][:, :GDN_DV] - wq[h][:CHUNK] for h, u in enumerate(us)]
        for h, u in enumerate(us):
            o_ref[0, rows[u], h * GDN_DV:(h + 1) * GDN_DV] = wq[h][CHUNK:] + _bdot(intra[u], v_new[h])
        s = [s[h] * jnp.exp(g_last[u]) + lax.dot_general(
            kd[u].astype(BF16), v_new[h].astype(BF16), (((0,), (0,)), ((), ())), preferred_element_type=F32)
            for h, u in enumerate(us)]
    for h in range(nh):
        s_ref[h] = s[h]

    @pl.when(c == pl.num_programs(1) - 1)
    def _():
        sfin_ref[0] = s_ref[...]


def _gdn_chunks(qkv, ba, alog_row, dt_row, s0, reverse):
    b, _, t, _ = qkv.shape
    cb = GDN_CHUNKS_PER_STEP
    rows = cb * CHUNK
    n = t // rows
    d = 1 if reverse else 0

    def blk(c):
        return n - 1 - c if reverse else c

    return pl.pallas_call(
        functools.partial(_gdn_chunk_kernel, reverse=reverse, cb=cb),
        grid=(b, n),
        in_specs=[pl.BlockSpec((1, 3 * GDN_HEADS, rows, LANES), lambda bi, c: (bi, 0, blk(c), 0)),
                  pl.BlockSpec((1, rows, LANES), lambda bi, c: (bi, blk(c), 0)),
                  pl.BlockSpec((1, LANES), lambda bi, c: (0, 0)),
                  pl.BlockSpec((1, LANES), lambda bi, c: (0, 0)),
                  pl.BlockSpec((1, 1, GDN_HEADS, GDN_DK, GDN_DV), lambda bi, c: (bi, d, 0, 0, 0))],
        out_specs=[pl.BlockSpec((1, rows, GDN_V_WIDTH), lambda bi, c: (bi, blk(c), 0)),
                   pl.BlockSpec((1, GDN_HEADS, GDN_DK, GDN_DV), lambda bi, c: (bi, 0, 0, 0))],
        out_shape=[jax.ShapeDtypeStruct((b, t, GDN_V_WIDTH), F32),
                   jax.ShapeDtypeStruct((b, GDN_HEADS, GDN_DK, GDN_DV), F32)],
        scratch_shapes=[pltpu.VMEM((GDN_HEADS, GDN_DK, GDN_DV), F32)],
        compiler_params=_cparams(("parallel", "arbitrary")),
        name="gdn_bwd" if reverse else "gdn_fwd",
    )(qkv, ba, alog_row, dt_row, s0)


def _pack_bf16_pairs(h):
    half = D_MODEL // 2
    lo = pltpu.bitcast(h[:, :half].astype(BF16).astype(F32), jnp.uint32)
    hi = pltpu.bitcast(h[:, half:].astype(BF16).astype(F32), jnp.uint32)
    return (hi & jnp.uint32(0xFFFF0000)) | (lo >> 16)


def _unpack_bf16_pairs(w):
    lo = pltpu.bitcast(w << 16, F32).astype(BF16)
    hi = pltpu.bitcast(w & jnp.uint32(0xFFFF0000), F32).astype(BF16)
    return lo, hi


def _mix_kernel(x_ref, na_ref, of_ref, ob_ref, z_ref, gate_ref, mod_ref, gnw_ref, gpost_ref, gpre_ref,
                wna_ref, wgdn_ref, wout_ref, x1_ref, h2_ref, h2p_ref):
    mod = mod_ref[0]
    o = of_ref[...] + ob_ref[...]
    parts = []
    for h in range(GDN_HEADS):
        sl = slice(h * GDN_DV, (h + 1) * GDN_DV)
        parts.append(_rms(o[:, sl], gnw_ref[...]) * _silu(z_ref[:, sl]))
    gdn_o = jnp.concatenate(parts, axis=-1)
    a = _dot(na_ref[...].astype(BF16), wna_ref[...])
    b = _dot(gdn_o.astype(BF16), wgdn_ref[...])
    gate = jax.nn.sigmoid(gate_ref[...])
    pre = gate[:, :D_MODEL] * a + gate[:, D_MODEL:] * b
    mix = _dot(pre.astype(BF16), wout_ref[...])
    x1 = x_ref[...] + mod[:, 2 * D_MODEL:3 * D_MODEL] * _rms(mix, gpost_ref[...])
    x1_ref[...] = x1
    h2 = _rms(x1, gpre_ref[...]) * (1.0 + mod[:, 4 * D_MODEL:5 * D_MODEL]) + mod[:, 3 * D_MODEL:4 * D_MODEL]
    h2_ref[...] = h2
    h2p_ref[...] = _pack_bf16_pairs(h2)


def _mix(x, na_o, o_f, o_b, z, gate, mods3, gnw, gpost, gpre, wna, wgdn, wout, row_of_tile):
    n = x.shape[0]
    tm = TOK_TILE
    row = lambda w: pl.BlockSpec((tm, w), lambda i: (i, 0))
    const = lambda r, c: pl.BlockSpec((r, c), lambda i: (0, 0))
    return pl.pallas_call(
        _mix_kernel,
        grid=(n // tm,),
        in_specs=[row(D_MODEL), row(NA_WIDTH), row(GDN_V_WIDTH), row(GDN_V_WIDTH),
                  row(GDN_V_WIDTH), row(2 * D_MODEL),
                  pl.BlockSpec((1, 1, 6 * D_MODEL), lambda i: (row_of_tile(i), 0, 0)),
                  const(1, GDN_DV), const(1, D_MODEL), const(1, D_MODEL),
                  const(NA_WIDTH, D_MODEL), const(GDN_V_WIDTH, D_MODEL), const(D_MODEL, D_MODEL)],
        out_specs=[row(D_MODEL), row(D_MODEL), row(D_MODEL // 2)],
        out_shape=[jax.ShapeDtypeStruct((n, D_MODEL), F32),
                   jax.ShapeDtypeStruct((n, D_MODEL), F32),
                   jax.ShapeDtypeStruct((n, D_MODEL // 2), jnp.uint32)],
        compiler_params=_cparams(("parallel",)),
        name="mix",
    )(x, na_o, o_f, o_b, z, gate, mods3, gnw.reshape(1, GDN_DV), gpost.reshape(1, D_MODEL),
      gpre.reshape(1, D_MODEL), wna, wgdn, wout)


def _router_kernel(h_ref, w_ref, b_ref, e_ref, wt_ref, cnt_ref):
    i = pl.program_id(0)
    tm = h_ref.shape[0]
    ne = N_EXPERTS
    per = ne // N_GROUPS
    logits = jnp.dot(h_ref[...], w_ref[...], precision=HI, preferred_element_type=F32)
    scores = jax.nn.sigmoid(logits)
    biased = scores + b_ref[...]
    lane = lax.broadcasted_iota(jnp.int32, (tm, ne), 1).astype(F32)
    grp = jnp.floor(lane * (1.0 / per))
    lane_g = lax.broadcasted_iota(jnp.int32, (tm, LANES), 1).astype(F32)

    def first_max(x, iota, n):
        m = jnp.max(x, axis=-1, keepdims=True)
        idx = jnp.min(jnp.where(x == m, iota, float(n)), axis=-1, keepdims=True)
        return m, idx

    gs = jnp.full((tm, LANES), NEG_INF, F32)
    for g in range(N_GROUPS):
        mg = jnp.where(grp == float(g), biased, NEG_INF)
        m1, i1 = first_max(mg, lane, ne)
        m2 = jnp.max(jnp.where(lane == i1, NEG_INF, mg), axis=-1, keepdims=True)
        gs = jnp.where(lane_g == float(g), m1 + m2, gs)
    emask = jnp.zeros((tm, ne), jnp.bool_)
    for _ in range(TOPK_GROUPS):
        _, gi = first_max(gs, lane_g, LANES)
        gs = jnp.where(lane_g == gi, NEG_INF, gs)
        emask = emask | (grp == gi)
    masked = jnp.where(emask, biased, NEG_INF)
    e_out = jnp.zeros((tm, LANES), F32)
    w_out = jnp.zeros((tm, LANES), F32)
    onehot = jnp.zeros((tm, ne), F32)
    for kk in range(TOP_K):
        _, ei = first_max(masked, lane, ne)
        hit = lane == ei
        masked = jnp.where(hit, NEG_INF, masked)
        wk = jnp.sum(jnp.where(hit, scores, 0.0), axis=-1, keepdims=True)
        onehot = onehot + hit.astype(F32)
        e_out = jnp.where(lane_g == float(kk), ei, e_out)
        w_out = jnp.where(lane_g == float(kk), wk, w_out)
    wsum = jnp.sum(w_out, axis=-1, keepdims=True)
    e_ref[...] = e_out.astype(jnp.int32)
    wt_ref[...] = w_out / wsum * ROUTED_SCALE

    @pl.when(i == 0)
    def _():
        cnt_ref[...] = jnp.zeros_like(cnt_ref)

    cnt_ref[...] += jnp.sum(onehot, axis=0, keepdims=True)


def _router(h2, w_router, router_bias):
    n = h2.shape[0]
    tm = TOK_TILE
    return pl.pallas_call(
        _router_kernel,
        grid=(n // tm,),
        in_specs=[pl.BlockSpec((tm, D_MODEL), lambda i: (i, 0)),
                  pl.BlockSpec((D_MODEL, N_EXPERTS), lambda i: (0, 0)),
                  pl.BlockSpec((1, N_EXPERTS), lambda i: (0, 0))],
        out_specs=[pl.BlockSpec((tm, LANES), lambda i: (i, 0)),
                   pl.BlockSpec((tm, LANES), lambda i: (i, 0)),
                   pl.BlockSpec((1, N_EXPERTS), lambda i: (0, 0))],
        out_shape=[jax.ShapeDtypeStruct((n, LANES), jnp.int32),
                   jax.ShapeDtypeStruct((n, LANES), F32),
                   jax.ShapeDtypeStruct((1, N_EXPERTS), F32)],
        compiler_params=_cparams(("arbitrary",)),
        name="router",
    )(h2, w_router, router_bias.reshape(1, N_EXPERTS))


def _moe_kernel(tok_ref, wts_ref, starts_ref, h2p_hbm, wg_ref, wu_ref, wd_ref, out_hbm,
                h2p_ref, acc_ref, xs_ref, ye_ref, sem):
    e = pl.program_id(0)
    n_pairs = tok_ref.shape[0]
    half = D_MODEL // 2

    @pl.when(e == 0)
    def _():
        cp = pltpu.make_async_copy(h2p_hbm, h2p_ref, sem.at[0])
        cp.start()
        acc_ref[...] = jnp.zeros_like(acc_ref)
        cp.wait()

    start = starts_ref[e]
    cnt = starts_ref[e + 1] - start
    n_tiles = (cnt + MOE_ROWS - 1) // MOE_ROWS
    wg = wg_ref[0].astype(BF16)
    wu = wu_ref[0].astype(BF16)
    wd = wd_ref[0].astype(BF16)

    def tile_body(t, carry):
        base = start + t * MOE_ROWS
        n_valid = jnp.minimum(MOE_ROWS, cnt - t * MOE_ROWS)

        def gather_row(r, c):
            tok = tok_ref[jnp.minimum(base + r, n_pairs - 1)]
            xs_ref[pl.ds(r, 1), :] = h2p_ref[pl.ds(tok, 1), :]
            return c

        lax.fori_loop(0, MOE_ROWS, gather_row, 0, unroll=8)
        x_lo, x_hi = _unpack_bf16_pairs(xs_ref[...])
        hg = _dot(x_lo, wg[:half]) + _dot(x_hi, wg[half:])
        hu = _dot(x_lo, wu[:half]) + _dot(x_hi, wu[half:])
        act = (_silu(hg) * hu).astype(BF16)
        ye_ref[...] = _dot(act, wd)

        def scatter_row(r, c):
            tok = tok_ref[base + r]
            acc_ref[pl.ds(tok, 1), :] += wts_ref[base + r] * ye_ref[pl.ds(r, 1), :]
            return c

        lax.fori_loop(0, n_valid, scatter_row, 0)
        return carry

    lax.fori_loop(0, n_tiles, tile_body, 0)

    @pl.when(e == pl.num_programs(0) - 1)
    def _():
        cp = pltpu.make_async_copy(acc_ref, out_hbm, sem.at[1])
        cp.start()
        cp.wait()


def _moe(tok_sorted, wts_sorted, starts, h2p, wg, wu, wd):
    n = h2p.shape[0]
    grid_spec = pltpu.PrefetchScalarGridSpec(
        num_scalar_prefetch=3,
        grid=(N_EXPERTS,),
        in_specs=[pl.BlockSpec(memory_space=pl.ANY),
                  pl.BlockSpec((1, D_MODEL, EXPERT_DIM), lambda e, *_: (e, 0, 0)),
                  pl.BlockSpec((1, D_MODEL, EXPERT_DIM), lambda e, *_: (e, 0, 0)),
                  pl.BlockSpec((1, EXPERT_DIM, D_MODEL), lambda e, *_: (e, 0, 0))],
        out_specs=pl.BlockSpec(memory_space=pl.ANY),
        scratch_shapes=[pltpu.VMEM((n, D_MODEL // 2), jnp.uint32),
                        pltpu.VMEM((n, D_MODEL), F32),
                        pltpu.VMEM((MOE_ROWS, D_MODEL // 2), jnp.uint32),
                        pltpu.VMEM((MOE_ROWS, D_MODEL), F32),
                        pltpu.SemaphoreType.DMA((2,))],
    )
    return pl.pallas_call(
        _moe_kernel,
        grid_spec=grid_spec,
        out_shape=jax.ShapeDtypeStruct((n, D_MODEL), F32),
        compiler_params=_cparams(("arbitrary",), vmem=60 * 1024 * 1024),
        name="moe",
    )(tok_sorted, wts_sorted, starts, h2p, wg, wu, wd)


def _final_kernel(x1_ref, h2p_ref, r_ref, mod_ref, g_ref, wg_ref, wu_ref, wd_ref, y_ref):
    half = D_MODEL // 2
    mod = mod_ref[0]
    lo, hi = _unpack_bf16_pairs(h2p_ref[...])
    hg = _dot(lo, wg_ref[:half, :]) + _dot(hi, wg_ref[half:, :])
    hu = _dot(lo, wu_ref[:half, :]) + _dot(hi, wu_ref[half:, :])
    shared = _dot((_silu(hg) * hu).astype(BF16), wd_ref[...])
    ffn = r_ref[...] + shared
    y_ref[...] = x1_ref[...] + mod[:, 5 * D_MODEL:6 * D_MODEL] * _rms(ffn, g_ref[...])


def _final(x1, h2p, routed, mods3, g, wg, wu, wd, row_of_tile):
    n = x1.shape[0]
    tm = TOK_TILE
    sd = wg.shape[1]
    row = lambda w: pl.BlockSpec((tm, w), lambda i: (i, 0))
    const = lambda r, c: pl.BlockSpec((r, c), lambda i: (0, 0))
    return pl.pallas_call(
        _final_kernel,
        grid=(n // tm,),
        in_specs=[row(D_MODEL), row(D_MODEL // 2), row(D_MODEL),
                  pl.BlockSpec((1, 1, 6 * D_MODEL), lambda i: (row_of_tile(i), 0, 0)),
                  const(1, D_MODEL), const(D_MODEL, sd), const(D_MODEL, sd), const(sd, D_MODEL)],
        out_specs=row(D_MODEL),
        out_shape=jax.ShapeDtypeStruct((n, D_MODEL), F32),
        compiler_params=_cparams(("parallel",)),
        name="final",
    )(x1, h2p, routed, mods3, g.reshape(1, D_MODEL), wg, wu, wd)


def _trunk(x3, mods3, row_of_tile, attend, s0, wts):
    b, t, _ = x3.shape
    n = b * t
    x = x3.reshape(n, D_MODEL)
    q, k, v, gdn, z, gate, ba = _premix(x, mods3, wts["g_pre_mix"], wts["w_cat"], row_of_tile)
    na_o = attend(q, k, v).reshape(n, NA_WIDTH)
    qkv = _gdn_conv(gdn.reshape(b, t, GDN_CONV_CH), wts["conv_w"])
    ba3 = ba.reshape(b, t, LANES)
    o_f, s_f = _gdn_chunks(qkv, ba3, wts["alog_row"], wts["dt_row"], s0, reverse=False)
    o_b, s_b = _gdn_chunks(qkv, ba3, wts["alog_row"], wts["dt_row"], s0, reverse=True)
    s_fin = jnp.stack([s_f, s_b], axis=1)
    x1, h2, h2p = _mix(x, na_o, o_f.reshape(n, GDN_V_WIDTH), o_b.reshape(n, GDN_V_WIDTH), z, gate, mods3,
                       wts["gdn_norm_w"],
                       wts["g_post_mix"], wts["g_pre_ffn"], wts["w_na_up"], wts["w_gdn_up"],
                       wts["w_out"], row_of_tile)
    top_e, top_w, counts = _router(h2, wts["w_router"], wts["router_bias"])
    flat_e = top_e[:, :TOP_K].reshape(-1)
    flat_w = top_w[:, :TOP_K].reshape(-1)
    pair_tok = jnp.arange(n * TOP_K, dtype=jnp.int32) // TOP_K
    _, tok_sorted, wts_sorted = lax.sort((flat_e, pair_tok, flat_w), num_keys=1, is_stable=True)
    starts = jnp.concatenate([jnp.zeros((1,), jnp.int32),
                              jnp.cumsum(counts.reshape(-1).astype(jnp.int32))])
    routed = _moe(tok_sorted, wts_sorted, starts, h2p, wts["w_exp_gate"], wts["w_exp_up"],
                  wts["w_exp_down"])
    y = _final(x1, h2p, routed, mods3, wts["g_post_ffn"], wts["w_sh_gate"], wts["w_sh_up"],
               wts["w_sh_down"], row_of_tile)
    return y.reshape(b, t, D_MODEL), k, v, s_fin


def kernel(x_prompt, x_sample, cache_na_k, cache_na_v, state_gdn, c, c_ctx, w_ada, b_ada, g_pre_mix,
           g_post_mix, g_pre_ffn, g_post_ffn, w_in, conv_w, gdn_a_log, gdn_dt_bias, gdn_norm_w, na_rpb,
           w_na_up, w_gdn_up, w_out, w_router, router_bias, w_exp_gate, w_exp_up, w_exp_down, w_sh_gate,
           w_sh_up, w_sh_down):
    depth = w_ada.shape[0]
    bp, tp, _ = x_prompt.shape
    bs, ts, _ = x_sample.shape
    y_prompt, y_sample = x_prompt, x_sample
    zero_state = jnp.zeros((bp, 2, GDN_HEADS, GDN_DK, GDN_DV), F32)
    new_k, new_v, new_s = [], [], []
    for l in range(depth):
        cv = jnp.concatenate([c_ctx[None], c, jnp.zeros((8 - 1 - bs, D_MODEL), F32)], axis=0)
        mods3 = _ada(cv, w_ada[l], b_ada[l]).reshape(8, 1, 6 * D_MODEL)
        wl = w_in[l]
        w_cat = jnp.concatenate(
            [wl[:, :S_Z], wl[:, S_A:], wl[:, S_Z:S_A],
             jnp.zeros((D_MODEL, LANES - 4 * GDN_HEADS), F32)], axis=1).astype(BF16)
        pad = jnp.zeros((2 * GDN_HEADS,), F32)
        tail = jnp.zeros((LANES - 4 * GDN_HEADS,), F32)
        wts = dict(
            w_cat=w_cat, g_pre_mix=g_pre_mix[l], g_post_mix=g_post_mix[l], g_pre_ffn=g_pre_ffn[l],
            g_post_ffn=g_post_ffn[l], conv_w=conv_w[l], gdn_norm_w=gdn_norm_w[l],
            alog_row=jnp.concatenate([pad, gdn_a_log[l].reshape(-1), tail]).reshape(1, LANES),
            dt_row=jnp.concatenate([pad, gdn_dt_bias[l].reshape(-1), tail]).reshape(1, LANES),
            w_na_up=w_na_up[l].astype(BF16), w_gdn_up=w_gdn_up[l].astype(BF16),
            w_out=w_out[l].astype(BF16), w_router=w_router[l], router_bias=router_bias[l],
            w_exp_gate=w_exp_gate[l], w_exp_up=w_exp_up[l], w_exp_down=w_exp_down[l],
            w_sh_gate=w_sh_gate[l].astype(BF16), w_sh_up=w_sh_up[l].astype(BF16),
            w_sh_down=w_sh_down[l].astype(BF16))

        def ctx_attend(q, k, v):
            return _ctx_attn(q.reshape(bp, tp, NA_WIDTH), k.reshape(bp, tp, NA_WIDTH),
                             v.reshape(bp, tp, NA_WIDTH))

        y_prompt, k_ctx, v_ctx, s_ctx = _trunk(y_prompt, mods3, lambda i: 0, ctx_attend, zero_state, wts)
        new_k.append(k_ctx.reshape(bp, tp, NA_HEADS, NA_HEAD_DIM))
        new_v.append(v_ctx.reshape(bp, tp, NA_HEADS, NA_HEAD_DIM))
        new_s.append(s_ctx)

        bias = _na_bias_tables(na_rpb[l], ts // GRID_W)
        ck = cache_na_k[:, l].transpose(0, 2, 1, 3)
        cvv = cache_na_v[:, l].transpose(0, 2, 1, 3)

        def heads(a):
            return a.reshape(bs, ts, NA_HEADS, NA_HEAD_DIM).transpose(0, 2, 1, 3)

        def na_attend(q, k, v):
            o = _na_attn(heads(q), heads(k), heads(v), ck, cvv, bias)
            return o.transpose(0, 2, 1, 3)

        tiles_per_seq = ts // TOK_TILE
        y_sample, _, _, _ = _trunk(y_sample, mods3, lambda i: 1 + i // tiles_per_seq, na_attend,
                                   state_gdn[:, l], wts)
    return (y_prompt, y_sample, jnp.stack(new_k, axis=1), jnp.stack(new_v, axis=1),
            jnp.stack(new_s, axis=1))
```

```python
import functools

import numpy as np
import jax
import jax.numpy as jnp
from jax import lax
from jax.experimental import pallas as pl
from jax.experimental.pallas import tpu as pltpu

F32 = jnp.float32
BF16 = jnp.bfloat16
HI = lax.Precision.HIGHEST

D_MODEL = 1024
GRID_W = 64
NA_HEADS = 8
NA_HEAD_DIM = 64
NA_WIDTH = NA_HEADS * NA_HEAD_DIM
NA_KR = 8
NA_KC = 16
GDN_HEADS = 4
GDN_DK = 128
GDN_DV = 128
GDN_QK_WIDTH = GDN_HEADS * GDN_DK
GDN_V_WIDTH = GDN_HEADS * GDN_DV
GDN_CONV_CH = 2 * GDN_QK_WIDTH + GDN_V_WIDTH
CONV_K = 5
CHUNK = 64
N_EXPERTS = 256
TOP_K = 8
N_GROUPS = 8
TOPK_GROUPS = 4
EXPERT_DIM = 256
ROUTED_SCALE = 2.5
EPS = 1e-6
S_NA = 3 * NA_WIDTH
S_GDN = S_NA + GDN_CONV_CH
S_Z = S_GDN + GDN_V_WIDTH
S_B = S_Z + 2 * GDN_HEADS
S_A = S_B + 2 * GDN_HEADS

LANES = 128
TOK_TILE = 256
NA_QROWS = 8
NA_SPAN = 16
MOE_ROWS = 128
MOE_ROW_GROUP = 8
GDN_CHUNKS_PER_STEP = 4
VMEM_LIMIT = 56 * 1024 * 1024
NEG_INF = float("-inf")


def _cparams(sem, vmem=VMEM_LIMIT):
    return pltpu.CompilerParams(dimension_semantics=sem, vmem_limit_bytes=vmem)


def _silu(x):
    return x * jax.nn.sigmoid(x)


def _rms(x, g):
    return x * lax.rsqrt(jnp.mean(x * x, axis=-1, keepdims=True) + EPS) * g


def _dot(a, b):
    return jnp.dot(a, b, preferred_element_type=F32)


def _dot_nt(a, b, precision=None):
    return lax.dot_general(a, b, (((1,), (1,)), ((), ())), precision=precision,
                           preferred_element_type=F32)


def _ada_kernel(c_ref, w_ref, b_ref, o_ref):
    o_ref[...] = jnp.dot(_silu(c_ref[...]), w_ref[...], precision=HI,
                         preferred_element_type=F32) + b_ref[...]


def _ada(cv, w_ada, b_ada):
    n = w_ada.shape[1]
    tn = 512
    return pl.pallas_call(
        _ada_kernel,
        grid=(n // tn,),
        in_specs=[pl.BlockSpec((8, D_MODEL), lambda j: (0, 0)),
                  pl.BlockSpec((D_MODEL, tn), lambda j: (0, j)),
                  pl.BlockSpec((1, tn), lambda j: (0, j))],
        out_specs=pl.BlockSpec((8, tn), lambda j: (0, j)),
        out_shape=jax.ShapeDtypeStruct((8, n), F32),
        compiler_params=_cparams(("parallel",)),
        name="ada",
    )(cv, w_ada, b_ada.reshape(1, n))


_PM_WIDTHS = (NA_WIDTH, NA_WIDTH, NA_WIDTH, GDN_CONV_CH, GDN_V_WIDTH, 2 * D_MODEL, LANES)


def _premix_kernel(x_ref, mod_ref, g_ref, w_ref, *o_refs):
    mod = mod_ref[0]
    h = _rms(x_ref[...], g_ref[...]) * (1.0 + mod[:, D_MODEL:2 * D_MODEL]) + mod[:, 0:D_MODEL]
    hb = h.astype(BF16)
    off = 0
    for o_ref, wd in zip(o_refs, _PM_WIDTHS):
        for c0 in range(0, wd, 512):
            c1 = min(c0 + 512, wd)
            o_ref[:, c0:c1] = _dot(hb, w_ref[:, off + c0:off + c1])
        off += wd


def _premix(x, mods3, g, w_cat, row_of_tile):
    n = x.shape[0]
    wtot = w_cat.shape[1]
    tm = TOK_TILE
    return pl.pallas_call(
        _premix_kernel,
        grid=(n // tm,),
        in_specs=[pl.BlockSpec((tm, D_MODEL), lambda i: (i, 0)),
                  pl.BlockSpec((1, 1, 6 * D_MODEL), lambda i: (row_of_tile(i), 0, 0)),
                  pl.BlockSpec((1, D_MODEL), lambda i: (0, 0)),
                  pl.BlockSpec((D_MODEL, wtot), lambda i: (0, 0))],
        out_specs=[pl.BlockSpec((tm, wd), lambda i: (i, 0)) for wd in _PM_WIDTHS],
        out_shape=[jax.ShapeDtypeStruct((n, wd), F32) for wd in _PM_WIDTHS],
        compiler_params=_cparams(("parallel",)),
        name="premix",
    )(x, mods3, g.reshape(1, D_MODEL), w_cat)


def _softmax_rows(s):
    m = jnp.max(s, axis=-1, keepdims=True)
    p = jnp.exp(s - m)
    return p / jnp.sum(p, axis=-1, keepdims=True)


def _ctx_attn_kernel(q_ref, k_ref, v_ref, o_ref):
    scale = NA_HEAD_DIM ** -0.5
    for hp in range(NA_HEADS // 2):
        outs = []
        for h in (2 * hp, 2 * hp + 1):
            sl = slice(h * NA_HEAD_DIM, (h + 1) * NA_HEAD_DIM)
            q = q_ref[0, :, sl].astype(BF16)
            k = k_ref[0, :, sl].astype(BF16)
            v = v_ref[0, :, sl].astype(BF16)
            p = _softmax_rows(_dot_nt(q, k) * scale)
            outs.append(_dot(p.astype(BF16), v))
        o_ref[0, :, hp * LANES:(hp + 1) * LANES] = jnp.concatenate(outs, axis=-1)


def _ctx_attn(q, k, v):
    b, t, w = q.shape
    spec = pl.BlockSpec((1, t, w), lambda i: (i, 0, 0))
    return pl.pallas_call(
        _ctx_attn_kernel,
        grid=(b,),
        in_specs=[spec, spec, spec],
        out_specs=spec,
        out_shape=jax.ShapeDtypeStruct((b, t, w), F32),
        compiler_params=_cparams(("parallel",)),
        name="ctx_attn",
    )(q, k, v)


def _na_span_base(j, rows):
    return np.clip(NA_QROWS * j - NA_KR // 2, 0, rows - NA_SPAN)


def _na_bias_tables(rpb, rows):
    col = np.arange(GRID_W)
    dcm = np.clip(col[None, :] - col[:, None], -(NA_KC - 1), NA_KC - 1) + (NA_KC - 1)
    onehot = (dcm[None] == np.arange(2 * NA_KC - 1)[:, None, None]).astype(np.float32)
    tab = jnp.einsum('hrd,dqk->hrqk', rpb.astype(F32), jnp.asarray(onehot), precision=HI)
    col_start = np.clip(col - NA_KC // 2, 0, GRID_W - NA_KC)
    col_in = (col[None, :] >= col_start[:, None]) & (col[None, :] < col_start[:, None] + NA_KC)
    tab = jnp.where(jnp.asarray(col_in)[None, None], tab, NEG_INF)
    n_dr = 2 * NA_KR - 1
    tab = jnp.concatenate([tab, jnp.full((NA_HEADS, 1, GRID_W, GRID_W), NEG_INF, F32)], axis=1)
    nblk = rows // NA_QROWS
    idx = np.full((3, NA_QROWS, NA_SPAN), n_dr, np.int32)
    for p, j in enumerate((0, 1, nblk - 1)):
        base = _na_span_base(j, rows)
        for ri in range(NA_QROWS):
            r = NA_QROWS * j + ri
            rs = np.clip(r - NA_KR // 2, 0, rows - NA_KR)
            for ki in range(NA_SPAN):
                kr = base + ki
                if rs <= kr < rs + NA_KR:
                    idx[p, ri, ki] = kr - r + NA_KR - 1
    bias = tab[:, idx]
    bias = bias.transpose(1, 0, 2, 4, 3, 5)
    return bias.reshape(3, NA_HEADS, NA_QROWS * GRID_W, NA_SPAN * GRID_W)


def _na_attn_kernel(q_ref, k_ref, v_ref, ck_ref, cv_ref, bias_ref, o_ref, *, rows):
    j = pl.program_id(2)
    scale = NA_HEAD_DIM ** -0.5
    base = jnp.clip(NA_QROWS * j - NA_KR // 2, 0, rows - NA_SPAN)
    start = pl.multiple_of(base * GRID_W, GRID_W)
    span = NA_SPAN * GRID_W
    q = q_ref[0, 0].astype(BF16)
    kl = k_ref[0, 0, pl.ds(start, span), :].astype(BF16)
    vl = v_ref[0, 0, pl.ds(start, span), :].astype(BF16)
    s_loc = _dot_nt(q, kl) * scale + bias_ref[0, 0]
    s_ctx = _dot_nt(q, ck_ref[0, 0].astype(BF16)) * scale
    m = jnp.maximum(jnp.max(s_loc, axis=-1, keepdims=True), jnp.max(s_ctx, axis=-1, keepdims=True))
    p_loc = jnp.exp(s_loc - m)
    p_ctx = jnp.exp(s_ctx - m)
    den = jnp.sum(p_loc, axis=-1, keepdims=True) + jnp.sum(p_ctx, axis=-1, keepdims=True)
    p_loc = (p_loc / den).astype(BF16)
    p_ctx = (p_ctx / den).astype(BF16)
    o_ref[0, 0] = _dot(p_loc, vl) + _dot(p_ctx, cv_ref[0, 0].astype(BF16))


def _na_attn(q, k, v, ck, cv, bias):
    b, h, n, d = q.shape
    p = ck.shape[2]
    rows = n // GRID_W
    nblk = rows // NA_QROWS
    qb = NA_QROWS * GRID_W

    def pattern(j):
        return jnp.where(j == 0, 0, jnp.where(j == nblk - 1, 2, 1))

    full = pl.BlockSpec((1, 1, n, d), lambda bi, hi, j: (bi, hi, 0, 0))
    ctx = pl.BlockSpec((1, 1, p, d), lambda bi, hi, j: (bi, hi, 0, 0))
    blk = pl.BlockSpec((1, 1, qb, d), lambda bi, hi, j: (bi, hi, j, 0))
    return pl.pallas_call(
        functools.partial(_na_attn_kernel, rows=rows),
        grid=(b, h, nblk),
        in_specs=[blk, full, full, ctx, ctx,
                  pl.BlockSpec((1, 1, qb, NA_SPAN * GRID_W), lambda bi, hi, j: (pattern(j), hi, 0, 0))],
        out_specs=blk,
        out_shape=jax.ShapeDtypeStruct((b, h, n, d), F32),
        compiler_params=_cparams(("parallel", "parallel", "arbitrary")),
        name="na_attn",
    )(q, k, v, ck, cv, bias)


def _gdn_conv_kernel(x_ref, w_ref, o_ref):
    c = pl.program_id(1)
    x = x_ref[0]
    t = x.shape[0]
    row = lax.broadcasted_iota(jnp.int32, x.shape, 0)
    y = jnp.zeros_like(x)
    for jj in range(CONV_K):
        o = jj - CONV_K // 2
        xs = x if o == 0 else pltpu.roll(x, (-o) % t, 0)
        xs = jnp.where((row + o >= 0) & (row + o < t), xs, 0.0)
        y = y + xs * w_ref[jj:jj + 1, :]
    y = _silu(y)
    nrm = lax.rsqrt(jnp.sum(y * y, axis=-1, keepdims=True) + EPS)
    n_qk = 2 * GDN_HEADS
    o_ref[0, 0] = jnp.where(c < n_qk, y * nrm, y)


def _gdn_conv(x, conv_w):
    b, t, ch = x.shape
    nc = ch // LANES
    return pl.pallas_call(
        _gdn_conv_kernel,
        grid=(b, nc),
        in_specs=[pl.BlockSpec((1, t, LANES), lambda bi, c: (bi, 0, c)),
                  pl.BlockSpec((CONV_K, LANES), lambda bi, c: (0, c))],
        out_specs=pl.BlockSpec((1, 1, t, LANES), lambda bi, c: (bi, c, 0, 0)),
        out_shape=jax.ShapeDtypeStruct((b, nc, t, LANES), F32),
        compiler_params=_cparams(("parallel", "parallel")),
        name="gdn_conv",
    )(x, conv_w)


def _bdot(a, b):
    return jnp.dot(a.astype(BF16), b.astype(BF16), preferred_element_type=F32)


def _split_bf16(x):
    hi = x.astype(BF16)
    return hi, (x - hi.astype(F32)).astype(BF16)


def _dot3(a, b):
    m = a.shape[0]
    ah, al = _split_bf16(a)
    bh, bl = _split_bf16(b)
    top = _dot(jnp.concatenate([ah, al], axis=0), bh)
    return top[:m] + top[m:] + _dot(ah, bl)


def _gdn_chunk_kernel(qkv_ref, ba_ref, alog_ref, dt_ref, s0_ref, o_ref, sfin_ref, s_ref, *, reverse, cb):
    c = pl.program_id(1)
    nh = GDN_HEADS

    @pl.when(c == 0)
    def _():
        s_ref[...] = s0_ref[0, 0]

    ii = lax.broadcasted_iota(jnp.int32, (CHUNK, CHUNK), 0)
    jj = lax.broadcasted_iota(jnp.int32, (CHUNK, CHUNK), 1)
    lag = (jj - ii) if reverse else (ii - jj)
    incl = lag >= 0
    strict = lag > 0
    eye = (ii == jj).astype(F32)
    tri = incl.astype(F32)
    bcol = nh if reverse else 0
    gcol0 = (3 if reverse else 2) * nh
    sub8 = lax.broadcasted_iota(jnp.int32, (8, LANES), 0)
    lane8 = lax.broadcasted_iota(jnp.int32, (8, LANES), 1)
    sel8 = (lane8 == gcol0 + sub8).astype(F32)
    units = [(ci, h) for ci in range(cb) for h in range(nh)]
    gc_alls, beta_alls, grow8s = [], [], []
    for ci in range(cb):
        ba = ba_ref[0, ci * CHUNK:(ci + 1) * CHUNK, :]
        z = ba + dt_ref[...]
        softplus = jnp.maximum(z, 0.0) + jnp.log1p(jnp.exp(-jnp.abs(z)))
        g_all = -jnp.exp(alog_ref[...]) * softplus
        gc_all = jnp.dot(tri, g_all, precision=HI, preferred_element_type=F32)
        gc_alls.append(gc_all)
        beta_alls.append(jax.nn.sigmoid(ba))
        grow8s.append(_dot_nt(sel8, gc_all, precision=HI))
    gcol = [gc_alls[ci][:, gcol0 + h:gcol0 + h + 1] for ci, h in units]
    beta = [beta_alls[ci][:, bcol + h:bcol + h + 1] for ci, h in units]
    rows = [slice(ci * CHUNK, (ci + 1) * CHUNK) for ci, _ in units]
    k = [qkv_ref[0, nh + h, rows[u], :] for u, (_, h) in enumerate(units)]
    kb = [k[u] * beta[u] for u in range(len(units))]
    q = [qkv_ref[0, h, rows[u], :] * (GDN_DK ** -0.5) for u, (_, h) in enumerate(units)]
    kq = [_dot_nt(jnp.concatenate([kb[u], q[u]], axis=0).astype(BF16), k[u].astype(BF16))
          for u in range(len(units))]
    decay = [jnp.where(incl, jnp.exp(jnp.where(incl, gcol[u] - grow8s[ci][h:h + 1, :], 0.0)), 0.0)
             for u, (ci, h) in enumerate(units)]
    intra = [jnp.where(incl, kq[u][CHUNK:] * decay[u], 0.0) for u in range(len(units))]
    pw = [-jnp.where(strict, kq[u][:CHUNK] * decay[u], 0.0) for u in range(len(units))]
    tmat = [eye + p for p in pw]
    pw = [_dot3(p, p) for p in pw]
    for _ in range(4):
        pt = [_dot3(jnp.concatenate([pw[u], tmat[u]], axis=0), pw[u]) for u in range(len(units))]
        pw = [x[:CHUNK] for x in pt]
        tmat = [tmat[u] + pt[u][CHUNK:] for u in range(len(units))]
    tmat = [tmat[u] + _dot3(tmat[u], pw[u]) for u in range(len(units))]
    eg = [jnp.exp(g) for g in gcol]
    uw = [_bdot(tmat[u], jnp.concatenate(
        [qkv_ref[0, 2 * nh + h, rows[u], :] * beta[u], kb[u] * eg[u]], axis=1))
        for u, (_, h) in enumerate(units)]
    g_last = [g[0:1, :] if reverse else g[CHUNK - 1:CHUNK, :] for g in gcol]
    kd = [k[u] * jnp.exp(g_last[u] - gcol[u]) for u in range(len(units))]
    qe = [q[u] * eg[u] for u in range(len(units))]
    s = [s_ref[h] for h in range(nh)]
    for ci in (reversed(range(cb)) if reverse else range(cb)):
        us = [ci * nh + h for h in range(nh)]
        wq = [_bdot(jnp.concatenate([uw[u][:, GDN_DV:], qe[u]], axis=0), s[h]) for h, u in enumerate(us)]
        v_new = [uw[u][:, :GDN_DV] - wq[h][:CHUNK] for h, u in enumerate(us)]
        for h, u in enumerate(us):
            o_ref[0, rows[u], h * GDN_DV:(h + 1) * GDN_DV] = wq[h][CHUNK:] + _bdot(intra[u], v_new[h])
        s = [s[h] * jnp.exp(g_last[u]) + lax.dot_general(
            kd[u].astype(BF16), v_new[h].astype(BF16), (((0,), (0,)), ((), ())), preferred_element_type=F32)
            for h, u in enumerate(us)]
    for h in range(nh):
        s_ref[h] = s[h]

    @pl.when(c == pl.num_programs(1) - 1)
    def _():
        sfin_ref[0] = s_ref[...]


def _gdn_chunks(qkv, ba, alog_row, dt_row, s0, reverse):
    b, _, t, _ = qkv.shape
    cb = GDN_CHUNKS_PER_STEP
    rows = cb * CHUNK
    n = t // rows
    d = 1 if reverse else 0

    def blk(c):
        return n - 1 - c if reverse else c

    return pl.pallas_call(
        functools.partial(_gdn_chunk_kernel, reverse=reverse, cb=cb),
        grid=(b, n),
        in_specs=[pl.BlockSpec((1, 3 * GDN_HEADS, rows, LANES), lambda bi, c: (bi, 0, blk(c), 0)),
                  pl.BlockSpec((1, rows, LANES), lambda bi, c: (bi, blk(c), 0)),
                  pl.BlockSpec((1, LANES), lambda bi, c: (0, 0)),
                  pl.BlockSpec((1, LANES), lambda bi, c: (0, 0)),
                  pl.BlockSpec((1, 1, GDN_HEADS, GDN_DK, GDN_DV), lambda bi, c: (bi, d, 0, 0, 0))],
        out_specs=[pl.BlockSpec((1, rows, GDN_V_WIDTH), lambda bi, c: (bi, blk(c), 0)),
                   pl.BlockSpec((1, GDN_HEADS, GDN_DK, GDN_DV), lambda bi, c: (bi, 0, 0, 0))],
        out_shape=[jax.ShapeDtypeStruct((b, t, GDN_V_WIDTH), F32),
                   jax.ShapeDtypeStruct((b, GDN_HEADS, GDN_DK, GDN_DV), F32)],
        scratch_shapes=[pltpu.VMEM((GDN_HEADS, GDN_DK, GDN_DV), F32)],
        compiler_params=_cparams(("parallel", "arbitrary")),
        name="gdn_bwd" if reverse else "gdn_fwd",
    )(qkv, ba, alog_row, dt_row, s0)


def _pack_bf16_pairs(h):
    half = D_MODEL // 2
    lo = pltpu.bitcast(h[:, :half].astype(BF16).astype(F32), jnp.uint32)
    hi = pltpu.bitcast(h[:, half:].astype(BF16).astype(F32), jnp.uint32)
    return (hi & jnp.uint32(0xFFFF0000)) | (lo >> 16)


def _unpack_bf16_pairs(w):
    lo = pltpu.bitcast(w << 16, F32).astype(BF16)
    hi = pltpu.bitcast(w & jnp.uint32(0xFFFF0000), F32).astype(BF16)
    return lo, hi


def _mix_kernel(x_ref, na_ref, of_ref, ob_ref, z_ref, gate_ref, mod_ref, gnw_ref, gpost_ref, gpre_ref,
                wna_ref, wgdn_ref, wout_ref, x1_ref, h2_ref, h2p_ref):
    mod = mod_ref[0]
    o = of_ref[...] + ob_ref[...]
    parts = []
    for h in range(GDN_HEADS):
        sl = slice(h * GDN_DV, (h + 1) * GDN_DV)
        parts.append(_rms(o[:, sl], gnw_ref[...]) * _silu(z_ref[:, sl]))
    gdn_o = jnp.concatenate(parts, axis=-1)
    a = _dot(na_ref[...].astype(BF16), wna_ref[...])
    b = _dot(gdn_o.astype(BF16), wgdn_ref[...])
    gate = jax.nn.sigmoid(gate_ref[...])
    pre = gate[:, :D_MODEL] * a + gate[:, D_MODEL:] * b
    mix = _dot(pre.astype(BF16), wout_ref[...])
    x1 = x_ref[...] + mod[:, 2 * D_MODEL:3 * D_MODEL] * _rms(mix, gpost_ref[...])
    x1_ref[...] = x1
    h2 = _rms(x1, gpre_ref[...]) * (1.0 + mod[:, 4 * D_MODEL:5 * D_MODEL]) + mod[:, 3 * D_MODEL:4 * D_MODEL]
    h2_ref[...] = h2
    h2p_ref[...] = _pack_bf16_pairs(h2)


def _mix(x, na_o, o_f, o_b, z, gate, mods3, gnw, gpost, gpre, wna, wgdn, wout, row_of_tile):
    n = x.shape[0]
    tm = TOK_TILE
    row = lambda w: pl.BlockSpec((tm, w), lambda i: (i, 0))
    const = lambda r, c: pl.BlockSpec((r, c), lambda i: (0, 0))
    return pl.pallas_call(
        _mix_kernel,
        grid=(n // tm,),
        in_specs=[row(D_MODEL), row(NA_WIDTH), row(GDN_V_WIDTH), row(GDN_V_WIDTH),
                  row(GDN_V_WIDTH), row(2 * D_MODEL),
                  pl.BlockSpec((1, 1, 6 * D_MODEL), lambda i: (row_of_tile(i), 0, 0)),
                  const(1, GDN_DV), const(1, D_MODEL), const(1, D_MODEL),
                  const(NA_WIDTH, D_MODEL), const(GDN_V_WIDTH, D_MODEL), const(D_MODEL, D_MODEL)],
        out_specs=[row(D_MODEL), row(D_MODEL), row(D_MODEL // 2)],
        out_shape=[jax.ShapeDtypeStruct((n, D_MODEL), F32),
                   jax.ShapeDtypeStruct((n, D_MODEL), F32),
                   jax.ShapeDtypeStruct((n, D_MODEL // 2), jnp.uint32)],
        compiler_params=_cparams(("parallel",)),
        name="mix",
    )(x, na_o, o_f, o_b, z, gate, mods3, gnw.reshape(1, GDN_DV), gpost.reshape(1, D_MODEL),
      gpre.reshape(1, D_MODEL), wna, wgdn, wout)


def _router_kernel(h_ref, w_ref, b_ref, e_ref, wt_ref, cnt_ref):
    i = pl.program_id(0)
    tm = h_ref.shape[0]
    ne = N_EXPERTS
    per = ne // N_GROUPS
    logits = jnp.dot(h_ref[...], w_ref[...], precision=HI, preferred_element_type=F32)
    scores = jax.nn.sigmoid(logits)
    biased = scores + b_ref[...]
    lane = lax.broadcasted_iota(jnp.int32, (tm, ne), 1).astype(F32)
    grp = jnp.floor(lane * (1.0 / per))
    lane_g = lax.broadcasted_iota(jnp.int32, (tm, LANES), 1).astype(F32)

    def first_max(x, iota, n):
        m = jnp.max(x, axis=-1, keepdims=True)
        idx = jnp.min(jnp.where(x == m, iota, float(n)), axis=-1, keepdims=True)
        return m, idx

    gs = jnp.full((tm, LANES), NEG_INF, F32)
    for g in range(N_GROUPS):
        mg = jnp.where(grp == float(g), biased, NEG_INF)
        m1, i1 = first_max(mg, lane, ne)
        m2 = jnp.max(jnp.where(lane == i1, NEG_INF, mg), axis=-1, keepdims=True)
        gs = jnp.where(lane_g == float(g), m1 + m2, gs)
    emask = jnp.zeros((tm, ne), jnp.bool_)
    for _ in range(TOPK_GROUPS):
        _, gi = first_max(gs, lane_g, LANES)
        gs = jnp.where(lane_g == gi, NEG_INF, gs)
        emask = emask | (grp == gi)
    masked = jnp.where(emask, biased, NEG_INF)
    e_out = jnp.zeros((tm, LANES), F32)
    w_out = jnp.zeros((tm, LANES), F32)
    onehot = jnp.zeros((tm, ne), F32)
    for kk in range(TOP_K):
        _, ei = first_max(masked, lane, ne)
        hit = lane == ei
        masked = jnp.where(hit, NEG_INF, masked)
        wk = jnp.sum(jnp.where(hit, scores, 0.0), axis=-1, keepdims=True)
        onehot = onehot + hit.astype(F32)
        e_out = jnp.where(lane_g == float(kk), ei, e_out)
        w_out = jnp.where(lane_g == float(kk), wk, w_out)
    wsum = jnp.sum(w_out, axis=-1, keepdims=True)
    e_ref[...] = e_out.astype(jnp.int32)
    wt_ref[...] = w_out / wsum * ROUTED_SCALE

    @pl.when(i == 0)
    def _():
        cnt_ref[...] = jnp.zeros_like(cnt_ref)

    cnt_ref[...] += jnp.sum(onehot, axis=0, keepdims=True)


def _router(h2, w_router, router_bias):
    n = h2.shape[0]
    tm = TOK_TILE
    return pl.pallas_call(
        _router_kernel,
        grid=(n // tm,),
        in_specs=[pl.BlockSpec((tm, D_MODEL), lambda i: (i, 0)),
                  pl.BlockSpec((D_MODEL, N_EXPERTS), lambda i: (0, 0)),
                  pl.BlockSpec((1, N_EXPERTS), lambda i: (0, 0))],
        out_specs=[pl.BlockSpec((tm, LANES), lambda i: (i, 0)),
                   pl.BlockSpec((tm, LANES), lambda i: (i, 0)),
                   pl.BlockSpec((1, N_EXPERTS), lambda i: (0, 0))],
        out_shape=[jax.ShapeDtypeStruct((n, LANES), jnp.int32),
                   jax.ShapeDtypeStruct((n, LANES), F32),
                   jax.ShapeDtypeStruct((1, N_EXPERTS), F32)],
        compiler_params=_cparams(("arbitrary",)),
        name="router",
    )(h2, w_router, router_bias.reshape(1, N_EXPERTS))


def _moe_kernel(tok_ref, seg_ref, ntile_ref, h2p_hbm, w2d_ref, wg_ref, wu_ref, wd_ref, out_hbm,
                h2p_ref, acc_ref, xs_ref, ye_ref, sem):
    e = pl.program_id(0)
    half = D_MODEL // 2
    n_tok = out_hbm.shape[0]
    grp = MOE_ROW_GROUP

    @pl.when(e == 0)
    def _():
        cp = pltpu.make_async_copy(h2p_hbm, h2p_ref.at[pl.ds(0, n_tok)], sem.at[0])
        cp.start()
        acc_ref[...] = jnp.zeros_like(acc_ref)
        h2p_ref[pl.ds(n_tok, grp), :] = jnp.zeros((grp, half), jnp.uint32)
        cp.wait()

    seg = seg_ref[e]
    n_tiles = ntile_ref[e]
    wg = wg_ref[0].astype(BF16)
    wu = wu_ref[0].astype(BF16)
    wd = wd_ref[0].astype(BF16)
    lane = lax.broadcasted_iota(jnp.int32, (8, LANES), 1)

    def tile_body(t, carry):
        base = seg + t * MOE_ROWS
        for r in range(MOE_ROWS):
            xs_ref[r:r + 1, :] = h2p_ref[pl.ds(tok_ref[base + r], 1), :]
        x_lo, x_hi = _unpack_bf16_pairs(xs_ref[...])
        hg = _dot(x_lo, wg[:half]) + _dot(x_hi, wg[half:])
        hu = _dot(x_lo, wu[:half]) + _dot(x_hi, wu[half:])
        act = (_silu(hg) * hu).astype(BF16)
        q = base // LANES
        sh = base % LANES
        rot = (LANES - sh) % LANES
        row_a = pltpu.roll(jnp.broadcast_to(w2d_ref[pl.ds(q, 1), :], (8, LANES)), rot, 1)
        row_b = pltpu.roll(jnp.broadcast_to(w2d_ref[pl.ds(q + 1, 1), :], (8, LANES)), rot, 1)
        w_row = jnp.where(lane + sh < LANES, row_a, row_b)[0:1, :]
        w_col = jnp.broadcast_to(w_row, (MOE_ROWS, LANES)).T
        ye_ref[...] = _dot(act, wd) * jnp.concatenate([w_col] * (D_MODEL // LANES), axis=1)
        for g in range(MOE_ROWS // grp):
            toks = [tok_ref[base + g * grp + j] for j in range(grp)]
            rows = [acc_ref[pl.ds(toks[j], 1), :] + ye_ref[g * grp + j:g * grp + j + 1, :]
                    for j in range(grp)]
            for j in range(grp):
                acc_ref[pl.ds(toks[j], 1), :] = rows[j]
        return carry

    lax.fori_loop(0, n_tiles, tile_body, 0)

    @pl.when(e == pl.num_programs(0) - 1)
    def _():
        cp = pltpu.make_async_copy(acc_ref.at[pl.ds(0, n_tok)], out_hbm, sem.at[1])
        cp.start()
        cp.wait()


def _moe(tok_sorted, seg_start, n_tiles, h2p, w2d, wg, wu, wd):
    n = h2p.shape[0]
    grid_spec = pltpu.PrefetchScalarGridSpec(
        num_scalar_prefetch=3,
        grid=(N_EXPERTS,),
        in_specs=[pl.BlockSpec(memory_space=pl.ANY),
                  pl.BlockSpec(w2d.shape, lambda e, *_: (0, 0)),
                  pl.BlockSpec((1, D_MODEL, EXPERT_DIM), lambda e, *_: (e, 0, 0)),
                  pl.BlockSpec((1, D_MODEL, EXPERT_DIM), lambda e, *_: (e, 0, 0)),
                  pl.BlockSpec((1, EXPERT_DIM, D_MODEL), lambda e, *_: (e, 0, 0))],
        out_specs=pl.BlockSpec(memory_space=pl.ANY),
        scratch_shapes=[pltpu.VMEM((n + MOE_ROW_GROUP, D_MODEL // 2), jnp.uint32),
                        pltpu.VMEM((n + MOE_ROW_GROUP, D_MODEL), F32),
                        pltpu.VMEM((MOE_ROWS, D_MODEL // 2), jnp.uint32),
                        pltpu.VMEM((MOE_ROWS, D_MODEL), F32),
                        pltpu.SemaphoreType.DMA((2,))],
    )
    return pl.pallas_call(
        _moe_kernel,
        grid_spec=grid_spec,
        out_shape=jax.ShapeDtypeStruct((n, D_MODEL), F32),
        compiler_params=_cparams(("arbitrary",), vmem=60 * 1024 * 1024),
        name="moe",
    )(tok_sorted, seg_start, n_tiles, h2p, w2d, wg, wu, wd)


def _moe_dispatch_plan(top_e, top_w, counts, n):
    spare = MOE_ROWS - 1
    flat_e = jnp.concatenate([top_e[:, :TOP_K].reshape(-1),
                              jnp.repeat(jnp.arange(N_EXPERTS, dtype=jnp.int32), spare)])
    flat_t = jnp.concatenate([jnp.arange(n * TOP_K, dtype=jnp.int32) // TOP_K,
                              jnp.full((N_EXPERTS * spare,), n, jnp.int32)])
    flat_w = jnp.concatenate([top_w[:, :TOP_K].reshape(-1), jnp.zeros((N_EXPERTS * spare,), F32)])
    _, tok_sorted, w_sorted = lax.sort((flat_e, flat_t, flat_w), num_keys=1, is_stable=True)
    total = flat_e.shape[0]
    table_rows = -(-total // LANES) + 1
    table_rows = -(-table_rows // 8) * 8
    w2d = jnp.concatenate([w_sorted, jnp.zeros((table_rows * LANES - total,), F32)]).reshape(table_rows, LANES)
    cnt = counts.reshape(-1).astype(jnp.int32)
    seg_start = jnp.cumsum(cnt) - cnt + spare * jnp.arange(N_EXPERTS, dtype=jnp.int32)
    n_tiles = (cnt + spare) // MOE_ROWS
    return tok_sorted, w2d, seg_start, n_tiles


def _final_kernel(x1_ref, h2p_ref, r_ref, mod_ref, g_ref, wg_ref, wu_ref, wd_ref, y_ref):
    half = D_MODEL // 2
    mod = mod_ref[0]
    lo, hi = _unpack_bf16_pairs(h2p_ref[...])
    hg = _dot(lo, wg_ref[:half, :]) + _dot(hi, wg_ref[half:, :])
    hu = _dot(lo, wu_ref[:half, :]) + _dot(hi, wu_ref[half:, :])
    shared = _dot((_silu(hg) * hu).astype(BF16), wd_ref[...])
    ffn = r_ref[...] + shared
    y_ref[...] = x1_ref[...] + mod[:, 5 * D_MODEL:6 * D_MODEL] * _rms(ffn, g_ref[...])


def _final(x1, h2p, routed, mods3, g, wg, wu, wd, row_of_tile):
    n = x1.shape[0]
    tm = TOK_TILE
    sd = wg.shape[1]
    row = lambda w: pl.BlockSpec((tm, w), lambda i: (i, 0))
    const = lambda r, c: pl.BlockSpec((r, c), lambda i: (0, 0))
    return pl.pallas_call(
        _final_kernel,
        grid=(n // tm,),
        in_specs=[row(D_MODEL), row(D_MODEL // 2), row(D_MODEL),
                  pl.BlockSpec((1, 1, 6 * D_MODEL), lambda i: (row_of_tile(i), 0, 0)),
                  const(1, D_MODEL), const(D_MODEL, sd), const(D_MODEL, sd), const(sd, D_MODEL)],
        out_specs=row(D_MODEL),
        out_shape=jax.ShapeDtypeStruct((n, D_MODEL), F32),
        compiler_params=_cparams(("parallel",)),
        name="final",
    )(x1, h2p, routed, mods3, g.reshape(1, D_MODEL), wg, wu, wd)


def _trunk(x3, mods3, row_of_tile, attend, s0, wts):
    b, t, _ = x3.shape
    n = b * t
    x = x3.reshape(n, D_MODEL)
    q, k, v, gdn, z, gate, ba = _premix(x, mods3, wts["g_pre_mix"], wts["w_cat"], row_of_tile)
    na_o = attend(q, k, v).reshape(n, NA_WIDTH)
    qkv = _gdn_conv(gdn.reshape(b, t, GDN_CONV_CH), wts["conv_w"])
    ba3 = ba.reshape(b, t, LANES)
    o_f, s_f = _gdn_chunks(qkv, ba3, wts["alog_row"], wts["dt_row"], s0, reverse=False)
    o_b, s_b = _gdn_chunks(qkv, ba3, wts["alog_row"], wts["dt_row"], s0, reverse=True)
    s_fin = jnp.stack([s_f, s_b], axis=1)
    x1, h2, h2p = _mix(x, na_o, o_f.reshape(n, GDN_V_WIDTH), o_b.reshape(n, GDN_V_WIDTH), z, gate, mods3,
                       wts["gdn_norm_w"],
                       wts["g_post_mix"], wts["g_pre_ffn"], wts["w_na_up"], wts["w_gdn_up"],
                       wts["w_out"], row_of_tile)
    top_e, top_w, counts = _router(h2, wts["w_router"], wts["router_bias"])
    tok_sorted, w2d, seg_start, n_tiles = _moe_dispatch_plan(top_e, top_w, counts, n)
    routed = _moe(tok_sorted, seg_start, n_tiles, h2p, w2d, wts["w_exp_gate"], wts["w_exp_up"],
                  wts["w_exp_down"])
    y = _final(x1, h2p, routed, mods3, wts["g_post_ffn"], wts["w_sh_gate"], wts["w_sh_up"],
               wts["w_sh_down"], row_of_tile)
    return y.reshape(b, t, D_MODEL), k, v, s_fin


def kernel(x_prompt, x_sample, cache_na_k, cache_na_v, state_gdn, c, c_ctx, w_ada, b_ada, g_pre_mix,
           g_post_mix, g_pre_ffn, g_post_ffn, w_in, conv_w, gdn_a_log, gdn_dt_bias, gdn_norm_w, na_rpb,
           w_na_up, w_gdn_up, w_out, w_router, router_bias, w_exp_gate, w_exp_up, w_exp_down, w_sh_gate,
           w_sh_up, w_sh_down):
    depth = w_ada.shape[0]
    bp, tp, _ = x_prompt.shape
    bs, ts, _ = x_sample.shape
    y_prompt, y_sample = x_prompt, x_sample
    zero_state = jnp.zeros((bp, 2, GDN_HEADS, GDN_DK, GDN_DV), F32)
    new_k, new_v, new_s = [], [], []
    for l in range(depth):
        cv = jnp.concatenate([c_ctx[None], c, jnp.zeros((8 - 1 - bs, D_MODEL), F32)], axis=0)
        mods3 = _ada(cv, w_ada[l], b_ada[l]).reshape(8, 1, 6 * D_MODEL)
        wl = w_in[l]
        w_cat = jnp.concatenate(
            [wl[:, :S_Z], wl[:, S_A:], wl[:, S_Z:S_A],
             jnp.zeros((D_MODEL, LANES - 4 * GDN_HEADS), F32)], axis=1).astype(BF16)
        pad = jnp.zeros((2 * GDN_HEADS,), F32)
        tail = jnp.zeros((LANES - 4 * GDN_HEADS,), F32)
        wts = dict(
            w_cat=w_cat, g_pre_mix=g_pre_mix[l], g_post_mix=g_post_mix[l], g_pre_ffn=g_pre_ffn[l],
            g_post_ffn=g_post_ffn[l], conv_w=conv_w[l], gdn_norm_w=gdn_norm_w[l],
            alog_row=jnp.concatenate([pad, gdn_a_log[l].reshape(-1), tail]).reshape(1, LANES),
            dt_row=jnp.concatenate([pad, gdn_dt_bias[l].reshape(-1), tail]).reshape(1, LANES),
            w_na_up=w_na_up[l].astype(BF16), w_gdn_up=w_gdn_up[l].astype(BF16),
            w_out=w_out[l].astype(BF16), w_router=w_router[l], router_bias=router_bias[l],
            w_exp_gate=w_exp_gate[l], w_exp_up=w_exp_up[l], w_exp_down=w_exp_down[l],
            w_sh_gate=w_sh_gate[l].astype(BF16), w_sh_up=w_sh_up[l].astype(BF16),
            w_sh_down=w_sh_down[l].astype(BF16))

        def ctx_attend(q, k, v):
            return _ctx_attn(q.reshape(bp, tp, NA_WIDTH), k.reshape(bp, tp, NA_WIDTH),
                             v.reshape(bp, tp, NA_WIDTH))

        y_prompt, k_ctx, v_ctx, s_ctx = _trunk(y_prompt, mods3, lambda i: 0, ctx_attend, zero_state, wts)
        new_k.append(k_ctx.reshape(bp, tp, NA_HEADS, NA_HEAD_DIM))
        new_v.append(v_ctx.reshape(bp, tp, NA_HEADS, NA_HEAD_DIM))
        new_s.append(s_ctx)

        bias = _na_bias_tables(na_rpb[l], ts // GRID_W)
        ck = cache_na_k[:, l].transpose(0, 2, 1, 3)
        cvv = cache_na_v[:, l].transpose(0, 2, 1, 3)

        def heads(a):
            return a.reshape(bs, ts, NA_HEADS, NA_HEAD_DIM).transpose(0, 2, 1, 3)

        def na_attend(q, k, v):
            o = _na_attn(heads(q), heads(k), heads(v), ck, cvv, bias)
            return o.transpose(0, 2, 1, 3)

        tiles_per_seq = ts // TOK_TILE
        y_sample, _, _, _ = _trunk(y_sample, mods3, lambda i: 1 + i // tiles_per_seq, na_attend,
                                   state_gdn[:, l], wts)
    return (y_prompt, y_sample, jnp.stack(new_k, axis=1), jnp.stack(new_v, axis=1),
            jnp.stack(new_s, axis=1))
```

```python
import functools

import numpy as np
import jax
import jax.numpy as jnp
from jax import lax
from jax.experimental import pallas as pl
from jax.experimental.pallas import tpu as pltpu

F32 = jnp.float32
BF16 = jnp.bfloat16
HI = lax.Precision.HIGHEST

D_MODEL = 1024
GRID_W = 64
NA_HEADS = 8
NA_HEAD_DIM = 64
NA_WIDTH = NA_HEADS * NA_HEAD_DIM
NA_KR = 8
NA_KC = 16
GDN_HEADS = 4
GDN_DK = 128
GDN_DV = 128
GDN_QK_WIDTH = GDN_HEADS * GDN_DK
GDN_V_WIDTH = GDN_HEADS * GDN_DV
GDN_CONV_CH = 2 * GDN_QK_WIDTH + GDN_V_WIDTH
CONV_K = 5
CHUNK = 64
N_EXPERTS = 256
TOP_K = 8
N_GROUPS = 8
TOPK_GROUPS = 4
EXPERT_DIM = 256
ROUTED_SCALE = 2.5
EPS = 1e-6
S_NA = 3 * NA_WIDTH
S_GDN = S_NA + GDN_CONV_CH
S_Z = S_GDN + GDN_V_WIDTH
S_B = S_Z + 2 * GDN_HEADS
S_A = S_B + 2 * GDN_HEADS

LANES = 128
TOK_TILE = 256
NA_QROWS = 8
NA_SPAN = 16
MOE_ROWS = 128
MOE_ROW_GROUP = 8
GDN_CHUNKS_PER_STEP = 4
GDN_CONV_BLOCK_ROWS = 4096
VMEM_LIMIT = 56 * 1024 * 1024
NEG_INF = float("-inf")


def _cparams(sem, vmem=VMEM_LIMIT):
    return pltpu.CompilerParams(dimension_semantics=sem, vmem_limit_bytes=vmem)


def _silu(x):
    return x * jax.nn.sigmoid(x)


def _rms(x, g):
    return x * lax.rsqrt(jnp.mean(x * x, axis=-1, keepdims=True) + EPS) * g


def _dot(a, b):
    return jnp.dot(a, b, preferred_element_type=F32)


def _dot_nt(a, b, precision=None):
    return lax.dot_general(a, b, (((1,), (1,)), ((), ())), precision=precision,
                           preferred_element_type=F32)


def _ada_kernel(c_ref, w_ref, b_ref, o_ref):
    o_ref[...] = jnp.dot(_silu(c_ref[...]), w_ref[...], precision=HI,
                         preferred_element_type=F32) + b_ref[...]


def _ada(cv, w_ada, b_ada):
    n = w_ada.shape[1]
    tn = 512
    return pl.pallas_call(
        _ada_kernel,
        grid=(n // tn,),
        in_specs=[pl.BlockSpec((8, D_MODEL), lambda j: (0, 0)),
                  pl.BlockSpec((D_MODEL, tn), lambda j: (0, j)),
                  pl.BlockSpec((1, tn), lambda j: (0, j))],
        out_specs=pl.BlockSpec((8, tn), lambda j: (0, j)),
        out_shape=jax.ShapeDtypeStruct((8, n), F32),
        compiler_params=_cparams(("parallel",)),
        name="ada",
    )(cv, w_ada, b_ada.reshape(1, n))


_PM_WIDTHS = (NA_WIDTH, NA_WIDTH, NA_WIDTH, GDN_CONV_CH, GDN_V_WIDTH, 2 * D_MODEL, LANES)


def _premix_kernel(x_ref, mod_ref, g_ref, w_ref, *o_refs):
    mod = mod_ref[0]
    h = _rms(x_ref[...], g_ref[...]) * (1.0 + mod[:, D_MODEL:2 * D_MODEL]) + mod[:, 0:D_MODEL]
    hb = h.astype(BF16)
    off = 0
    for o_ref, wd in zip(o_refs, _PM_WIDTHS):
        for c0 in range(0, wd, 512):
            c1 = min(c0 + 512, wd)
            o_ref[:, c0:c1] = _dot(hb, w_ref[:, off + c0:off + c1])
        off += wd


def _premix(x, mods3, g, w_cat, row_of_tile):
    n = x.shape[0]
    wtot = w_cat.shape[1]
    tm = TOK_TILE
    return pl.pallas_call(
        _premix_kernel,
        grid=(n // tm,),
        in_specs=[pl.BlockSpec((tm, D_MODEL), lambda i: (i, 0)),
                  pl.BlockSpec((1, 1, 6 * D_MODEL), lambda i: (row_of_tile(i), 0, 0)),
                  pl.BlockSpec((1, D_MODEL), lambda i: (0, 0)),
                  pl.BlockSpec((D_MODEL, wtot), lambda i: (0, 0))],
        out_specs=[pl.BlockSpec((tm, wd), lambda i: (i, 0)) for wd in _PM_WIDTHS],
        out_shape=[jax.ShapeDtypeStruct((n, wd), F32) for wd in _PM_WIDTHS],
        compiler_params=_cparams(("parallel",)),
        name="premix",
    )(x, mods3, g.reshape(1, D_MODEL), w_cat)


def _softmax_rows(s):
    m = jnp.max(s, axis=-1, keepdims=True)
    p = jnp.exp(s - m)
    return p / jnp.sum(p, axis=-1, keepdims=True)


def _ctx_attn_kernel(q_ref, k_ref, v_ref, o_ref):
    scale = NA_HEAD_DIM ** -0.5
    for hp in range(NA_HEADS // 2):
        outs = []
        for h in (2 * hp, 2 * hp + 1):
            sl = slice(h * NA_HEAD_DIM, (h + 1) * NA_HEAD_DIM)
            q = q_ref[0, :, sl].astype(BF16)
            k = k_ref[0, :, sl].astype(BF16)
            v = v_ref[0, :, sl].astype(BF16)
            p = _softmax_rows(_dot_nt(q, k) * scale)
            outs.append(_dot(p.astype(BF16), v))
        o_ref[0, :, hp * LANES:(hp + 1) * LANES] = jnp.concatenate(outs, axis=-1)


def _ctx_attn(q, k, v):
    b, t, w = q.shape
    spec = pl.BlockSpec((1, t, w), lambda i: (i, 0, 0))
    return pl.pallas_call(
        _ctx_attn_kernel,
        grid=(b,),
        in_specs=[spec, spec, spec],
        out_specs=spec,
        out_shape=jax.ShapeDtypeStruct((b, t, w), F32),
        compiler_params=_cparams(("parallel",)),
        name="ctx_attn",
    )(q, k, v)


def _na_span_base(j, rows):
    return np.clip(NA_QROWS * j - NA_KR // 2, 0, rows - NA_SPAN)


def _na_bias_tables(rpb, rows):
    col = np.arange(GRID_W)
    dcm = np.clip(col[None, :] - col[:, None], -(NA_KC - 1), NA_KC - 1) + (NA_KC - 1)
    onehot = (dcm[None] == np.arange(2 * NA_KC - 1)[:, None, None]).astype(np.float32)
    tab = jnp.einsum('hrd,dqk->hrqk', rpb.astype(F32), jnp.asarray(onehot), precision=HI)
    col_start = np.clip(col - NA_KC // 2, 0, GRID_W - NA_KC)
    col_in = (col[None, :] >= col_start[:, None]) & (col[None, :] < col_start[:, None] + NA_KC)
    tab = jnp.where(jnp.asarray(col_in)[None, None], tab, NEG_INF)
    n_dr = 2 * NA_KR - 1
    tab = jnp.concatenate([tab, jnp.full((NA_HEADS, 1, GRID_W, GRID_W), NEG_INF, F32)], axis=1)
    nblk = rows // NA_QROWS
    idx = np.full((3, NA_QROWS, NA_SPAN), n_dr, np.int32)
    for p, j in enumerate((0, 1, nblk - 1)):
        base = _na_span_base(j, rows)
        for ri in range(NA_QROWS):
            r = NA_QROWS * j + ri
            rs = np.clip(r - NA_KR // 2, 0, rows - NA_KR)
            for ki in range(NA_SPAN):
                kr = base + ki
                if rs <= kr < rs + NA_KR:
                    idx[p, ri, ki] = kr - r + NA_KR - 1
    bias = tab[:, idx]
    bias = bias.transpose(1, 0, 2, 4, 3, 5)
    return bias.reshape(3, NA_HEADS, NA_QROWS * GRID_W, NA_SPAN * GRID_W)


def _na_attn_kernel(q_ref, k_ref, v_ref, ck_ref, cv_ref, bias_ref, o_ref, *, rows):
    j = pl.program_id(2)
    scale = NA_HEAD_DIM ** -0.5
    base = jnp.clip(NA_QROWS * j - NA_KR // 2, 0, rows - NA_SPAN)
    start = pl.multiple_of(base * GRID_W, GRID_W)
    span = NA_SPAN * GRID_W
    q = q_ref[0]
    kl = k_ref[0, pl.ds(start, span), :].astype(BF16)
    vl = v_ref[0, pl.ds(start, span), :].astype(BF16)
    ck = ck_ref[0].astype(BF16)
    cv = cv_ref[0].astype(BF16)
    first = lax.broadcasted_iota(jnp.int32, q.shape, 1) < NA_HEAD_DIM
    outs = []
    for hh in range(2):
        qm = jnp.where(first if hh == 0 else ~first, q, 0.0).astype(BF16)
        s_loc = _dot_nt(qm, kl) * scale + bias_ref[0, hh]
        s_ctx = _dot_nt(qm, ck) * scale
        m = jnp.maximum(jnp.max(s_loc, axis=-1, keepdims=True), jnp.max(s_ctx, axis=-1, keepdims=True))
        p_loc = jnp.exp(s_loc - m)
        p_ctx = jnp.exp(s_ctx - m)
        den = jnp.sum(p_loc, axis=-1, keepdims=True) + jnp.sum(p_ctx, axis=-1, keepdims=True)
        p_loc = (p_loc / den).astype(BF16)
        p_ctx = (p_ctx / den).astype(BF16)
        outs.append(_dot(p_loc, vl) + _dot(p_ctx, cv))
    o_ref[0] = jnp.where(first, outs[0], outs[1])


def _na_attn(q, k, v, ck, cv, bias):
    b, n, w = q.shape
    p = ck.shape[1]
    rows = n // GRID_W
    nblk = rows // NA_QROWS
    qb = NA_QROWS * GRID_W

    def pattern(j):
        return jnp.where(j == 0, 0, jnp.where(j == nblk - 1, 2, 1))

    full = pl.BlockSpec((1, n, LANES), lambda bi, hp, j: (bi, 0, hp))
    ctx = pl.BlockSpec((1, p, LANES), lambda bi, hp, j: (bi, 0, hp))
    blk = pl.BlockSpec((1, qb, LANES), lambda bi, hp, j: (bi, j, hp))
    return pl.pallas_call(
        functools.partial(_na_attn_kernel, rows=rows),
        grid=(b, w // LANES, nblk),
        in_specs=[blk, full, full, ctx, ctx,
                  pl.BlockSpec((1, 2, qb, NA_SPAN * GRID_W), lambda bi, hp, j: (pattern(j), hp, 0, 0))],
        out_specs=blk,
        out_shape=jax.ShapeDtypeStruct((b, n, w), F32),
        compiler_params=_cparams(("parallel", "parallel", "arbitrary")),
        name="na_attn",
    )(q, k, v, ck, cv, bias)


def _gdn_conv_kernel(x_ref, w_ref, o_ref, *, groups):
    c = pl.program_id(1)
    t = x_ref.shape[1]
    row = lax.broadcasted_iota(jnp.int32, (t, LANES), 0)
    n_qk = 2 * GDN_HEADS
    for i in range(groups):
        lanes = slice(i * LANES, (i + 1) * LANES)
        x = x_ref[0, :, lanes]
        y = jnp.zeros_like(x)
        for jj in range(CONV_K):
            o = jj - CONV_K // 2
            xs = x if o == 0 else pltpu.roll(x, (-o) % t, 0)
            xs = jnp.where((row + o >= 0) & (row + o < t), xs, 0.0)
            y = y + xs * w_ref[jj:jj + 1, lanes]
        y = _silu(y)
        nrm = lax.rsqrt(jnp.sum(y * y, axis=-1, keepdims=True) + EPS)
        o_ref[0, i] = jnp.where(c * groups + i < n_qk, y * nrm, y)


def _gdn_conv(x, conv_w):
    b, t, ch = x.shape
    nc = ch // LANES
    groups = max(1, min(nc, GDN_CONV_BLOCK_ROWS // t))
    return pl.pallas_call(
        functools.partial(_gdn_conv_kernel, groups=groups),
        grid=(b, nc // groups),
        in_specs=[pl.BlockSpec((1, t, groups * LANES), lambda bi, c: (bi, 0, c)),
                  pl.BlockSpec((CONV_K, groups * LANES), lambda bi, c: (0, c))],
        out_specs=pl.BlockSpec((1, groups, t, LANES), lambda bi, c: (bi, c, 0, 0)),
        out_shape=jax.ShapeDtypeStruct((b, nc, t, LANES), F32),
        compiler_params=_cparams(("parallel", "parallel")),
        name="gdn_conv",
    )(x, conv_w)


def _bdot(a, b):
    return jnp.dot(a.astype(BF16), b.astype(BF16), preferred_element_type=F32)


def _split_bf16(x):
    hi = x.astype(BF16)
    return hi, (x - hi.astype(F32)).astype(BF16)


def _dot3(a, b):
    m = a.shape[0]
    ah, al = _split_bf16(a)
    bh, bl = _split_bf16(b)
    top = _dot(jnp.concatenate([ah, al], axis=0), bh)
    return top[:m] + top[m:] + _dot(ah, bl)


def _gdn_chunk_kernel(qkv_ref, ba_ref, alog_ref, dt_ref, s0_ref, o_ref, sfin_ref, s_ref, *, reverse, cb):
    c = pl.program_id(1)
    nh = GDN_HEADS

    @pl.when(c == 0)
    def _():
        s_ref[...] = s0_ref[0, 0]

    ii = lax.broadcasted_iota(jnp.int32, (CHUNK, CHUNK), 0)
    jj = lax.broadcasted_iota(jnp.int32, (CHUNK, CHUNK), 1)
    lag = (jj - ii) if reverse else (ii - jj)
    incl = lag >= 0
    strict = lag > 0
    eye = (ii == jj).astype(F32)
    tri = incl.astype(F32)
    bcol = nh if reverse else 0
    gcol0 = (3 if reverse else 2) * nh
    sub8 = lax.broadcasted_iota(jnp.int32, (8, LANES), 0)
    lane8 = lax.broadcasted_iota(jnp.int32, (8, LANES), 1)
    sel8 = (lane8 == gcol0 + sub8).astype(F32)
    units = [(ci, h) for ci in range(cb) for h in range(nh)]
    gc_alls, beta_alls, grow8s = [], [], []
    for ci in range(cb):
        ba = ba_ref[0, ci * CHUNK:(ci + 1) * CHUNK, :]
        z = ba + dt_ref[...]
        softplus = jnp.maximum(z, 0.0) + jnp.log1p(jnp.exp(-jnp.abs(z)))
        g_all = -jnp.exp(alog_ref[...]) * softplus
        gc_all = jnp.dot(tri, g_all, precision=HI, preferred_element_type=F32)
        gc_alls.append(gc_all)
        beta_alls.append(jax.nn.sigmoid(ba))
        grow8s.append(_dot_nt(sel8, gc_all, precision=HI))
    gcol = [gc_alls[ci][:, gcol0 + h:gcol0 + h + 1] for ci, h in units]
    beta = [beta_alls[ci][:, bcol + h:bcol + h + 1] for ci, h in units]
    rows = [slice(ci * CHUNK, (ci + 1) * CHUNK) for ci, _ in units]
    k = [qkv_ref[0, nh + h, rows[u], :] for u, (_, h) in enumerate(units)]
    kb = [k[u] * beta[u] for u in range(len(units))]
    q = [qkv_ref[0, h, rows[u], :] * (GDN_DK ** -0.5) for u, (_, h) in enumerate(units)]
    kq = [_dot_nt(jnp.concatenate([kb[u], q[u]], axis=0).astype(BF16), k[u].astype(BF16))
          for u in range(len(units))]
    decay = [jnp.where(incl, jnp.exp(jnp.where(incl, gcol[u] - grow8s[ci][h:h + 1, :], 0.0)), 0.0)
             for u, (ci, h) in enumerate(units)]
    intra = [jnp.where(incl, kq[u][CHUNK:] * decay[u], 0.0) for u in range(len(units))]
    pw = [-jnp.where(strict, kq[u][:CHUNK] * decay[u], 0.0) for u in range(len(units))]
    tmat = [eye + p for p in pw]
    pw = [_dot3(p, p) for p in pw]
    for _ in range(4):
        pt = [_dot3(jnp.concatenate([pw[u], tmat[u]], axis=0), pw[u]) for u in range(len(units))]
        pw = [x[:CHUNK] for x in pt]
        tmat = [tmat[u] + pt[u][CHUNK:] for u in range(len(units))]
    tmat = [tmat[u] + _dot3(tmat[u], pw[u]) for u in range(len(units))]
    eg = [jnp.exp(g) for g in gcol]
    uw = [_bdot(tmat[u], jnp.concatenate(
        [qkv_ref[0, 2 * nh + h, rows[u], :] * beta[u], kb[u] * eg[u]], axis=1))
        for u, (_, h) in enumerate(units)]
    g_last = [g[0:1, :] if reverse else g[CHUNK - 1:CHUNK, :] for g in gcol]
    kd = [k[u] * jnp.exp(g_last[u] - gcol[u]) for u in range(len(units))]
    qe = [q[u] * eg[u] for u in range(len(units))]
    s = [s_ref[h] for h in range(nh)]
    for ci in (reversed(range(cb)) if reverse else range(cb)):
        us = [ci * nh + h for h in range(nh)]
        wq = [_bdot(jnp.concatenate([uw[u][:, GDN_DV:], qe[u]], axis=0), s[h]) for h, u in enumerate(us)]
        v_new = [uw[u][:, :GDN_DV] - wq[h][:CHUNK] for h, u in enumerate(us)]
        for h, u in enumerate(us):
            o_ref[0, rows[u], h * GDN_DV:(h + 1) * GDN_DV] = wq[h][CHUNK:] + _bdot(intra[u], v_new[h])
        s = [s[h] * jnp.exp(g_last[u]) + lax.dot_general(
            kd[u].astype(BF16), v_new[h].astype(BF16), (((0,), (0,)), ((), ())), preferred_element_type=F32)
            for h, u in enumerate(us)]
    for h in range(nh):
        s_ref[h] = s[h]

    @pl.when(c == pl.num_programs(1) - 1)
    def _():
        sfin_ref[0] = s_ref[...]


def _gdn_chunks(qkv, ba, alog_row, dt_row, s0, reverse):
    b, _, t, _ = qkv.shape
    cb = GDN_CHUNKS_PER_STEP
    rows = cb * CHUNK
    n = t // rows
    d = 1 if reverse else 0

    def blk(c):
        return n - 1 - c if reverse else c

    return pl.pallas_call(
        functools.partial(_gdn_chunk_kernel, reverse=reverse, cb=cb),
        grid=(b, n),
        in_specs=[pl.BlockSpec((1, 3 * GDN_HEADS, rows, LANES), lambda bi, c: (bi, 0, blk(c), 0)),
                  pl.BlockSpec((1, rows, LANES), lambda bi, c: (bi, blk(c), 0)),
                  pl.BlockSpec((1, LANES), lambda bi, c: (0, 0)),
                  pl.BlockSpec((1, LANES), lambda bi, c: (0, 0)),
                  pl.BlockSpec((1, 1, GDN_HEADS, GDN_DK, GDN_DV), lambda bi, c: (bi, d, 0, 0, 0))],
        out_specs=[pl.BlockSpec((1, rows, GDN_V_WIDTH), lambda bi, c: (bi, blk(c), 0)),
                   pl.BlockSpec((1, GDN_HEADS, GDN_DK, GDN_DV), lambda bi, c: (bi, 0, 0, 0))],
        out_shape=[jax.ShapeDtypeStruct((b, t, GDN_V_WIDTH), F32),
                   jax.ShapeDtypeStruct((b, GDN_HEADS, GDN_DK, GDN_DV), F32)],
        scratch_shapes=[pltpu.VMEM((GDN_HEADS, GDN_DK, GDN_DV), F32)],
        compiler_params=_cparams(("parallel", "arbitrary")),
        name="gdn_bwd" if reverse else "gdn_fwd",
    )(qkv, ba, alog_row, dt_row, s0)


def _pack_bf16_pairs(h):
    half = D_MODEL // 2
    lo = pltpu.bitcast(h[:, :half].astype(BF16).astype(F32), jnp.uint32)
    hi = pltpu.bitcast(h[:, half:].astype(BF16).astype(F32), jnp.uint32)
    return (hi & jnp.uint32(0xFFFF0000)) | (lo >> 16)


def _unpack_bf16_pairs(w):
    lo = pltpu.bitcast(w << 16, F32).astype(BF16)
    hi = pltpu.bitcast(w & jnp.uint32(0xFFFF0000), F32).astype(BF16)
    return lo, hi


def _mix_kernel(x_ref, na_ref, of_ref, ob_ref, z_ref, gate_ref, mod_ref, gnw_ref, gpost_ref, gpre_ref,
                wna_ref, wgdn_ref, wout_ref, x1_ref, h2_ref, h2p_ref):
    mod = mod_ref[0]
    o = of_ref[...] + ob_ref[...]
    parts = []
    for h in range(GDN_HEADS):
        sl = slice(h * GDN_DV, (h + 1) * GDN_DV)
        parts.append(_rms(o[:, sl], gnw_ref[...]) * _silu(z_ref[:, sl]))
    gdn_o = jnp.concatenate(parts, axis=-1)
    a = _dot(na_ref[...].astype(BF16), wna_ref[...])
    b = _dot(gdn_o.astype(BF16), wgdn_ref[...])
    gate = jax.nn.sigmoid(gate_ref[...])
    pre = gate[:, :D_MODEL] * a + gate[:, D_MODEL:] * b
    mix = _dot(pre.astype(BF16), wout_ref[...])
    x1 = x_ref[...] + mod[:, 2 * D_MODEL:3 * D_MODEL] * _rms(mix, gpost_ref[...])
    x1_ref[...] = x1
    h2 = _rms(x1, gpre_ref[...]) * (1.0 + mod[:, 4 * D_MODEL:5 * D_MODEL]) + mod[:, 3 * D_MODEL:4 * D_MODEL]
    h2_ref[...] = h2
    h2p_ref[...] = _pack_bf16_pairs(h2)


def _mix(x, na_o, o_f, o_b, z, gate, mods3, gnw, gpost, gpre, wna, wgdn, wout, row_of_tile):
    n = x.shape[0]
    tm = TOK_TILE
    row = lambda w: pl.BlockSpec((tm, w), lambda i: (i, 0))
    const = lambda r, c: pl.BlockSpec((r, c), lambda i: (0, 0))
    return pl.pallas_call(
        _mix_kernel,
        grid=(n // tm,),
        in_specs=[row(D_MODEL), row(NA_WIDTH), row(GDN_V_WIDTH), row(GDN_V_WIDTH),
                  row(GDN_V_WIDTH), row(2 * D_MODEL),
                  pl.BlockSpec((1, 1, 6 * D_MODEL), lambda i: (row_of_tile(i), 0, 0)),
                  const(1, GDN_DV), const(1, D_MODEL), const(1, D_MODEL),
                  const(NA_WIDTH, D_MODEL), const(GDN_V_WIDTH, D_MODEL), const(D_MODEL, D_MODEL)],
        out_specs=[row(D_MODEL), row(D_MODEL), row(D_MODEL // 2)],
        out_shape=[jax.ShapeDtypeStruct((n, D_MODEL), F32),
                   jax.ShapeDtypeStruct((n, D_MODEL), F32),
                   jax.ShapeDtypeStruct((n, D_MODEL // 2), jnp.uint32)],
        compiler_params=_cparams(("parallel",)),
        name="mix",
    )(x, na_o, o_f, o_b, z, gate, mods3, gnw.reshape(1, GDN_DV), gpost.reshape(1, D_MODEL),
      gpre.reshape(1, D_MODEL), wna, wgdn, wout)


def _router_kernel(h_ref, w_ref, b_ref, e_ref, wt_ref, cnt_ref):
    i = pl.program_id(0)
    tm = h_ref.shape[0]
    ne = N_EXPERTS
    per = ne // N_GROUPS
    logits = jnp.dot(h_ref[...], w_ref[...], precision=HI, preferred_element_type=F32)
    scores = jax.nn.sigmoid(logits)
    biased = scores + b_ref[...]
    lane = lax.broadcasted_iota(jnp.int32, (tm, ne), 1).astype(F32)
    grp = jnp.floor(lane * (1.0 / per))
    lane_g = lax.broadcasted_iota(jnp.int32, (tm, LANES), 1).astype(F32)

    def first_max(x, iota, n):
        m = jnp.max(x, axis=-1, keepdims=True)
        idx = jnp.min(jnp.where(x == m, iota, float(n)), axis=-1, keepdims=True)
        return m, idx

    gs = jnp.full((tm, LANES), NEG_INF, F32)
    for g in range(N_GROUPS):
        mg = jnp.where(grp == float(g), biased, NEG_INF)
        m1, i1 = first_max(mg, lane, ne)
        m2 = jnp.max(jnp.where(lane == i1, NEG_INF, mg), axis=-1, keepdims=True)
        gs = jnp.where(lane_g == float(g), m1 + m2, gs)
    emask = jnp.zeros((tm, ne), jnp.bool_)
    for _ in range(TOPK_GROUPS):
        _, gi = first_max(gs, lane_g, LANES)
        gs = jnp.where(lane_g == gi, NEG_INF, gs)
        emask = emask | (grp == gi)
    masked = jnp.where(emask, biased, NEG_INF)
    e_out = jnp.zeros((tm, LANES), F32)
    w_out = jnp.zeros((tm, LANES), F32)
    onehot = jnp.zeros((tm, ne), F32)
    for kk in range(TOP_K):
        _, ei = first_max(masked, lane, ne)
        hit = lane == ei
        masked = jnp.where(hit, NEG_INF, masked)
        wk = jnp.sum(jnp.where(hit, scores, 0.0), axis=-1, keepdims=True)
        onehot = onehot + hit.astype(F32)
        e_out = jnp.where(lane_g == float(kk), ei, e_out)
        w_out = jnp.where(lane_g == float(kk), wk, w_out)
    wsum = jnp.sum(w_out, axis=-1, keepdims=True)
    e_ref[...] = e_out.astype(jnp.int32)
    wt_ref[...] = w_out / wsum * ROUTED_SCALE

    @pl.when(i == 0)
    def _():
        cnt_ref[...] = jnp.zeros_like(cnt_ref)

    cnt_ref[...] += jnp.sum(onehot, axis=0, keepdims=True)


def _router(h2, w_router, router_bias):
    n = h2.shape[0]
    tm = TOK_TILE
    return pl.pallas_call(
        _router_kernel,
        grid=(n // tm,),
        in_specs=[pl.BlockSpec((tm, D_MODEL), lambda i: (i, 0)),
                  pl.BlockSpec((D_MODEL, N_EXPERTS), lambda i: (0, 0)),
                  pl.BlockSpec((1, N_EXPERTS), lambda i: (0, 0))],
        out_specs=[pl.BlockSpec((tm, LANES), lambda i: (i, 0)),
                   pl.BlockSpec((tm, LANES), lambda i: (i, 0)),
                   pl.BlockSpec((1, N_EXPERTS), lambda i: (0, 0))],
        out_shape=[jax.ShapeDtypeStruct((n, LANES), jnp.int32),
                   jax.ShapeDtypeStruct((n, LANES), F32),
                   jax.ShapeDtypeStruct((1, N_EXPERTS), F32)],
        compiler_params=_cparams(("arbitrary",)),
        name="router",
    )(h2, w_router, router_bias.reshape(1, N_EXPERTS))


def _moe_kernel(tok_ref, seg_ref, cnt_ref, h2p_hbm, w2d_ref, wg_ref, wu_ref, wd_ref, out_hbm,
                h2p_ref, acc_ref, xs_ref, ye_ref, sem):
    e = pl.program_id(0)
    half = D_MODEL // 2
    n_tok = out_hbm.shape[0]
    grp = MOE_ROW_GROUP

    @pl.when(e == 0)
    def _():
        cp = pltpu.make_async_copy(h2p_hbm, h2p_ref.at[pl.ds(0, n_tok)], sem.at[0])
        cp.start()
        acc_ref[...] = jnp.zeros_like(acc_ref)
        h2p_ref[pl.ds(n_tok, grp), :] = jnp.zeros((grp, half), jnp.uint32)
        cp.wait()

    seg = seg_ref[e]
    cnt = cnt_ref[e]
    wg = wg_ref[0].astype(BF16)
    wu = wu_ref[0].astype(BF16)
    wd = wd_ref[0].astype(BF16)
    lane = lax.broadcasted_iota(jnp.int32, (8, LANES), 1)

    def do_tile(base, n_valid):
        def tok_of(r):
            tok = tok_ref[base + r]
            return tok if n_valid is None else jnp.where(r < n_valid, tok, n_tok)

        for r in range(MOE_ROWS):
            xs_ref[r:r + 1, :] = h2p_ref[pl.ds(tok_of(r), 1), :]
        x_lo, x_hi = _unpack_bf16_pairs(xs_ref[...])
        hg = _dot(x_lo, wg[:half]) + _dot(x_hi, wg[half:])
        hu = _dot(x_lo, wu[:half]) + _dot(x_hi, wu[half:])
        act = (_silu(hg) * hu).astype(BF16)
        q = base // LANES
        sh = base % LANES
        rot = (LANES - sh) % LANES
        row_a = pltpu.roll(jnp.broadcast_to(w2d_ref[pl.ds(q, 1), :], (8, LANES)), rot, 1)
        row_b = pltpu.roll(jnp.broadcast_to(w2d_ref[pl.ds(q + 1, 1), :], (8, LANES)), rot, 1)
        w_row = jnp.where(lane + sh < LANES, row_a, row_b)[0:1, :]
        w_col = jnp.broadcast_to(w_row, (MOE_ROWS, LANES)).T
        ye_ref[...] = _dot(act, wd) * jnp.concatenate([w_col] * (D_MODEL // LANES), axis=1)
        for g in range(MOE_ROWS // grp):
            toks = [tok_of(g * grp + j) for j in range(grp)]
            rows = [acc_ref[pl.ds(toks[j], 1), :] + ye_ref[g * grp + j:g * grp + j + 1, :]
                    for j in range(grp)]
            for j in range(grp):
                acc_ref[pl.ds(toks[j], 1), :] = rows[j]

    def full_tile(t, carry):
        do_tile(seg + t * MOE_ROWS, None)
        return carry

    n_full = cnt // MOE_ROWS
    lax.fori_loop(0, n_full, full_tile, 0)
    rem = cnt - n_full * MOE_ROWS

    @pl.when(rem > 0)
    def _():
        do_tile(seg + n_full * MOE_ROWS, rem)

    @pl.when(e == pl.num_programs(0) - 1)
    def _():
        cp = pltpu.make_async_copy(acc_ref.at[pl.ds(0, n_tok)], out_hbm, sem.at[1])
        cp.start()
        cp.wait()


def _moe(tok_sorted, seg_start, seg_count, h2p, w2d, wg, wu, wd):
    n = h2p.shape[0]
    grid_spec = pltpu.PrefetchScalarGridSpec(
        num_scalar_prefetch=3,
        grid=(N_EXPERTS,),
        in_specs=[pl.BlockSpec(memory_space=pl.ANY),
                  pl.BlockSpec(w2d.shape, lambda e, *_: (0, 0)),
                  pl.BlockSpec((1, D_MODEL, EXPERT_DIM), lambda e, *_: (e, 0, 0)),
                  pl.BlockSpec((1, D_MODEL, EXPERT_DIM), lambda e, *_: (e, 0, 0)),
                  pl.BlockSpec((1, EXPERT_DIM, D_MODEL), lambda e, *_: (e, 0, 0))],
        out_specs=pl.BlockSpec(memory_space=pl.ANY),
        scratch_shapes=[pltpu.VMEM((n + MOE_ROW_GROUP, D_MODEL // 2), jnp.uint32),
                        pltpu.VMEM((n + MOE_ROW_GROUP, D_MODEL), F32),
                        pltpu.VMEM((MOE_ROWS, D_MODEL // 2), jnp.uint32),
                        pltpu.VMEM((MOE_ROWS, D_MODEL), F32),
                        pltpu.SemaphoreType.DMA((2,))],
    )
    return pl.pallas_call(
        _moe_kernel,
        grid_spec=grid_spec,
        out_shape=jax.ShapeDtypeStruct((n, D_MODEL), F32),
        compiler_params=_cparams(("arbitrary",), vmem=60 * 1024 * 1024),
        name="moe",
    )(tok_sorted, seg_start, seg_count, h2p, w2d, wg, wu, wd)


def _moe_dispatch_plan(top_e, top_w, counts, n):
    flat_e = top_e[:, :TOP_K].reshape(-1)
    flat_t = jnp.arange(n * TOP_K, dtype=jnp.int32) // TOP_K
    flat_w = top_w[:, :TOP_K].reshape(-1)
    _, tok_sorted, w_sorted = lax.sort((flat_e, flat_t, flat_w), num_keys=1, is_stable=True)
    total = flat_e.shape[0]
    tok_sorted = jnp.concatenate([tok_sorted, jnp.full((MOE_ROWS,), n, jnp.int32)])
    table_rows = -(-(total // LANES + 2) // 8) * 8
    w2d = jnp.concatenate([w_sorted, jnp.zeros((table_rows * LANES - total,), F32)]).reshape(table_rows, LANES)
    cnt = counts.reshape(-1).astype(jnp.int32)
    return tok_sorted, w2d, jnp.cumsum(cnt) - cnt, cnt


def _final_kernel(x1_ref, h2p_ref, r_ref, mod_ref, g_ref, wg_ref, wu_ref, wd_ref, y_ref):
    half = D_MODEL // 2
    mod = mod_ref[0]
    lo, hi = _unpack_bf16_pairs(h2p_ref[...])
    hg = _dot(lo, wg_ref[:half, :]) + _dot(hi, wg_ref[half:, :])
    hu = _dot(lo, wu_ref[:half, :]) + _dot(hi, wu_ref[half:, :])
    shared = _dot((_silu(hg) * hu).astype(BF16), wd_ref[...])
    ffn = r_ref[...] + shared
    y_ref[...] = x1_ref[...] + mod[:, 5 * D_MODEL:6 * D_MODEL] * _rms(ffn, g_ref[...])


def _final(x1, h2p, routed, mods3, g, wg, wu, wd, row_of_tile):
    n = x1.shape[0]
    tm = TOK_TILE
    sd = wg.shape[1]
    row = lambda w: pl.BlockSpec((tm, w), lambda i: (i, 0))
    const = lambda r, c: pl.BlockSpec((r, c), lambda i: (0, 0))
    return pl.pallas_call(
        _final_kernel,
        grid=(n // tm,),
        in_specs=[row(D_MODEL), row(D_MODEL // 2), row(D_MODEL),
                  pl.BlockSpec((1, 1, 6 * D_MODEL), lambda i: (row_of_tile(i), 0, 0)),
                  const(1, D_MODEL), const(D_MODEL, sd), const(D_MODEL, sd), const(sd, D_MODEL)],
        out_specs=row(D_MODEL),
        out_shape=jax.ShapeDtypeStruct((n, D_MODEL), F32),
        compiler_params=_cparams(("parallel",)),
        name="final",
    )(x1, h2p, routed, mods3, g.reshape(1, D_MODEL), wg, wu, wd)


def _trunk(x3, mods3, row_of_tile, attend, s0, wts):
    b, t, _ = x3.shape
    n = b * t
    x = x3.reshape(n, D_MODEL)
    q, k, v, gdn, z, gate, ba = _premix(x, mods3, wts["g_pre_mix"], wts["w_cat"], row_of_tile)
    na_o = attend(q, k, v).reshape(n, NA_WIDTH)
    qkv = _gdn_conv(gdn.reshape(b, t, GDN_CONV_CH), wts["conv_w"])
    ba3 = ba.reshape(b, t, LANES)
    o_f, s_f = _gdn_chunks(qkv, ba3, wts["alog_row"], wts["dt_row"], s0, reverse=False)
    o_b, s_b = _gdn_chunks(qkv, ba3, wts["alog_row"], wts["dt_row"], s0, reverse=True)
    s_fin = jnp.stack([s_f, s_b], axis=1)
    x1, h2, h2p = _mix(x, na_o, o_f.reshape(n, GDN_V_WIDTH), o_b.reshape(n, GDN_V_WIDTH), z, gate, mods3,
                       wts["gdn_norm_w"],
                       wts["g_post_mix"], wts["g_pre_ffn"], wts["w_na_up"], wts["w_gdn_up"],
                       wts["w_out"], row_of_tile)
    top_e, top_w, counts = _router(h2, wts["w_router"], wts["router_bias"])
    tok_sorted, w2d, seg_start, seg_count = _moe_dispatch_plan(top_e, top_w, counts, n)
    routed = _moe(tok_sorted, seg_start, seg_count, h2p, w2d, wts["w_exp_gate"], wts["w_exp_up"],
                  wts["w_exp_down"])
    y = _final(x1, h2p, routed, mods3, wts["g_post_ffn"], wts["w_sh_gate"], wts["w_sh_up"],
               wts["w_sh_down"], row_of_tile)
    return y.reshape(b, t, D_MODEL), k, v, s_fin


def kernel(x_prompt, x_sample, cache_na_k, cache_na_v, state_gdn, c, c_ctx, w_ada, b_ada, g_pre_mix,
           g_post_mix, g_pre_ffn, g_post_ffn, w_in, conv_w, gdn_a_log, gdn_dt_bias, gdn_norm_w, na_rpb,
           w_na_up, w_gdn_up, w_out, w_router, router_bias, w_exp_gate, w_exp_up, w_exp_down, w_sh_gate,
           w_sh_up, w_sh_down):
    depth = w_ada.shape[0]
    bp, tp, _ = x_prompt.shape
    bs, ts, _ = x_sample.shape
    y_prompt, y_sample = x_prompt, x_sample
    zero_state = jnp.zeros((bp, 2, GDN_HEADS, GDN_DK, GDN_DV), F32)
    new_k, new_v, new_s = [], [], []
    for l in range(depth):
        cv = jnp.concatenate([c_ctx[None], c, jnp.zeros((8 - 1 - bs, D_MODEL), F32)], axis=0)
        mods3 = _ada(cv, w_ada[l], b_ada[l]).reshape(8, 1, 6 * D_MODEL)
        wl = w_in[l]
        w_cat = jnp.concatenate(
            [wl[:, :S_Z], wl[:, S_A:], wl[:, S_Z:S_A],
             jnp.zeros((D_MODEL, LANES - 4 * GDN_HEADS), F32)], axis=1).astype(BF16)
        pad = jnp.zeros((2 * GDN_HEADS,), F32)
        tail = jnp.zeros((LANES - 4 * GDN_HEADS,), F32)
        wts = dict(
            w_cat=w_cat, g_pre_mix=g_pre_mix[l], g_post_mix=g_post_mix[l], g_pre_ffn=g_pre_ffn[l],
            g_post_ffn=g_post_ffn[l], conv_w=conv_w[l], gdn_norm_w=gdn_norm_w[l],
            alog_row=jnp.concatenate([pad, gdn_a_log[l].reshape(-1), tail]).reshape(1, LANES),
            dt_row=jnp.concatenate([pad, gdn_dt_bias[l].reshape(-1), tail]).reshape(1, LANES),
            w_na_up=w_na_up[l].astype(BF16), w_gdn_up=w_gdn_up[l].astype(BF16),
            w_out=w_out[l].astype(BF16), w_router=w_router[l], router_bias=router_bias[l],
            w_exp_gate=w_exp_gate[l], w_exp_up=w_exp_up[l], w_exp_down=w_exp_down[l],
            w_sh_gate=w_sh_gate[l].astype(BF16), w_sh_up=w_sh_up[l].astype(BF16),
            w_sh_down=w_sh_down[l].astype(BF16))

        def ctx_attend(q, k, v):
            return _ctx_attn(q.reshape(bp, tp, NA_WIDTH), k.reshape(bp, tp, NA_WIDTH),
                             v.reshape(bp, tp, NA_WIDTH))

        y_prompt, k_ctx, v_ctx, s_ctx = _trunk(y_prompt, mods3, lambda i: 0, ctx_attend, zero_state, wts)
        new_k.append(k_ctx.reshape(bp, tp, NA_HEADS, NA_HEAD_DIM))
        new_v.append(v_ctx.reshape(bp, tp, NA_HEADS, NA_HEAD_DIM))
        new_s.append(s_ctx)

        bias = _na_bias_tables(na_rpb[l], ts // GRID_W)
        ck = cache_na_k[:, l].reshape(bs, -1, NA_WIDTH)
        cvv = cache_na_v[:, l].reshape(bs, -1, NA_WIDTH)

        def na_attend(q, k, v):
            return _na_attn(q.reshape(bs, ts, NA_WIDTH), k.reshape(bs, ts, NA_WIDTH),
                            v.reshape(bs, ts, NA_WIDTH), ck, cvv, bias)

        tiles_per_seq = ts // TOK_TILE
        y_sample, _, _, _ = _trunk(y_sample, mods3, lambda i: 1 + i // tiles_per_seq, na_attend,
                                   state_gdn[:, l], wts)
    return (y_prompt, y_sample, jnp.stack(new_k, axis=1), jnp.stack(new_v, axis=1),
            jnp.stack(new_s, axis=1))
```

```python
import functools

import numpy as np
import jax
import jax.numpy as jnp
from jax import lax
from jax.experimental import pallas as pl
from jax.experimental.pallas import tpu as pltpu

F32 = jnp.float32
BF16 = jnp.bfloat16
HI = lax.Precision.HIGHEST

D_MODEL = 1024
GRID_W = 64
NA_HEADS = 8
NA_HEAD_DIM = 64
NA_WIDTH = NA_HEADS * NA_HEAD_DIM
NA_KR = 8
NA_KC = 16
GDN_HEADS = 4
GDN_DK = 128
GDN_DV = 128
GDN_QK_WIDTH = GDN_HEADS * GDN_DK
GDN_V_WIDTH = GDN_HEADS * GDN_DV
GDN_CONV_CH = 2 * GDN_QK_WIDTH + GDN_V_WIDTH
CONV_K = 5
CHUNK = 64
N_EXPERTS = 256
TOP_K = 8
N_GROUPS = 8
TOPK_GROUPS = 4
EXPERT_DIM = 256
ROUTED_SCALE = 2.5
EPS = 1e-6
S_NA = 3 * NA_WIDTH
S_GDN = S_NA + GDN_CONV_CH
S_Z = S_GDN + GDN_V_WIDTH
S_B = S_Z + 2 * GDN_HEADS
S_A = S_B + 2 * GDN_HEADS

LANES = 128
TOK_TILE = 256
NA_QROWS = 8
NA_SPAN = 16
MOE_ROWS = 128
MOE_ROW_GROUP = 8
GDN_CHUNKS_PER_STEP = 4
GDN_CONV_BLOCK_ROWS = 4096
VMEM_LIMIT = 56 * 1024 * 1024
NEG_INF = float("-inf")


def _cparams(sem, vmem=VMEM_LIMIT):
    return pltpu.CompilerParams(dimension_semantics=sem, vmem_limit_bytes=vmem)


def _silu(x):
    return x * jax.nn.sigmoid(x)


def _rms(x, g):
    return x * lax.rsqrt(jnp.mean(x * x, axis=-1, keepdims=True) + EPS) * g


def _dot(a, b):
    return jnp.dot(a, b, preferred_element_type=F32)


def _dot_nt(a, b, precision=None):
    return lax.dot_general(a, b, (((1,), (1,)), ((), ())), precision=precision,
                           preferred_element_type=F32)


def _ada_kernel(c_ref, w_ref, b_ref, o_ref):
    o_ref[...] = jnp.dot(_silu(c_ref[...]), w_ref[...], precision=HI,
                         preferred_element_type=F32) + b_ref[...]


def _ada(cv, w_ada, b_ada):
    n = w_ada.shape[1]
    tn = 512
    return pl.pallas_call(
        _ada_kernel,
        grid=(n // tn,),
        in_specs=[pl.BlockSpec((8, D_MODEL), lambda j: (0, 0)),
                  pl.BlockSpec((D_MODEL, tn), lambda j: (0, j)),
                  pl.BlockSpec((1, tn), lambda j: (0, j))],
        out_specs=pl.BlockSpec((8, tn), lambda j: (0, j)),
        out_shape=jax.ShapeDtypeStruct((8, n), F32),
        compiler_params=_cparams(("parallel",)),
        name="ada",
    )(cv, w_ada, b_ada.reshape(1, n))


_PM_WIDTHS = (NA_WIDTH, NA_WIDTH, NA_WIDTH, GDN_CONV_CH, GDN_V_WIDTH, 2 * D_MODEL, LANES)


def _premix_kernel(x_ref, mod_ref, g_ref, w_ref, *o_refs):
    mod = mod_ref[0]
    h = _rms(x_ref[...], g_ref[...]) * (1.0 + mod[:, D_MODEL:2 * D_MODEL]) + mod[:, 0:D_MODEL]
    hb = h.astype(BF16)
    off = 0
    for o_ref, wd in zip(o_refs, _PM_WIDTHS):
        for c0 in range(0, wd, 512):
            c1 = min(c0 + 512, wd)
            o_ref[:, c0:c1] = _dot(hb, w_ref[:, off + c0:off + c1])
        off += wd


def _premix(x, mods3, g, w_cat, row_of_tile):
    n = x.shape[0]
    wtot = w_cat.shape[1]
    tm = TOK_TILE
    return pl.pallas_call(
        _premix_kernel,
        grid=(n // tm,),
        in_specs=[pl.BlockSpec((tm, D_MODEL), lambda i: (i, 0)),
                  pl.BlockSpec((1, 1, 6 * D_MODEL), lambda i: (row_of_tile(i), 0, 0)),
                  pl.BlockSpec((1, D_MODEL), lambda i: (0, 0)),
                  pl.BlockSpec((D_MODEL, wtot), lambda i: (0, 0))],
        out_specs=[pl.BlockSpec((tm, wd), lambda i: (i, 0)) for wd in _PM_WIDTHS],
        out_shape=[jax.ShapeDtypeStruct((n, wd), F32) for wd in _PM_WIDTHS],
        compiler_params=_cparams(("parallel",)),
        name="premix",
    )(x, mods3, g.reshape(1, D_MODEL), w_cat)


def _softmax_rows(s):
    m = jnp.max(s, axis=-1, keepdims=True)
    p = jnp.exp(s - m)
    return p / jnp.sum(p, axis=-1, keepdims=True)


def _ctx_attn_kernel(q_ref, k_ref, v_ref, o_ref):
    scale = NA_HEAD_DIM ** -0.5
    for hp in range(NA_HEADS // 2):
        outs = []
        for h in (2 * hp, 2 * hp + 1):
            sl = slice(h * NA_HEAD_DIM, (h + 1) * NA_HEAD_DIM)
            q = q_ref[0, :, sl].astype(BF16)
            k = k_ref[0, :, sl].astype(BF16)
            v = v_ref[0, :, sl].astype(BF16)
            p = _softmax_rows(_dot_nt(q, k) * scale)
            outs.append(_dot(p.astype(BF16), v))
        o_ref[0, :, hp * LANES:(hp + 1) * LANES] = jnp.concatenate(outs, axis=-1)


def _ctx_attn(q, k, v):
    b, t, w = q.shape
    spec = pl.BlockSpec((1, t, w), lambda i: (i, 0, 0))
    return pl.pallas_call(
        _ctx_attn_kernel,
        grid=(b,),
        in_specs=[spec, spec, spec],
        out_specs=spec,
        out_shape=jax.ShapeDtypeStruct((b, t, w), F32),
        compiler_params=_cparams(("parallel",)),
        name="ctx_attn",
    )(q, k, v)


def _na_span_base(j, rows):
    return np.clip(NA_QROWS * j - NA_KR // 2, 0, rows - NA_SPAN)


NA_DR_PAD = NA_QROWS


def _na_bias_tables(rpb, rows):
    col = np.arange(GRID_W)
    dcm = np.clip(col[None, :] - col[:, None], -(NA_KC - 1), NA_KC - 1) + (NA_KC - 1)
    onehot = (dcm[None] == np.arange(2 * NA_KC - 1)[:, None, None]).astype(np.float32)
    tab = jnp.einsum('hrd,dqk->hrqk', rpb.astype(F32), jnp.asarray(onehot), precision=HI)
    col_start = np.clip(col - NA_KC // 2, 0, GRID_W - NA_KC)
    col_in = (col[None, :] >= col_start[:, None]) & (col[None, :] < col_start[:, None] + NA_KC)
    tab = jnp.where(jnp.asarray(col_in)[None, None], tab, NEG_INF)
    n_dr = 2 * NA_KR - 1
    n_side = NA_DR_PAD + NA_SPAN - n_dr + 1
    blank_lo = jnp.full((NA_HEADS, NA_DR_PAD, GRID_W, GRID_W), NEG_INF, F32)
    blank_hi = jnp.full((NA_HEADS, n_side, GRID_W, GRID_W), NEG_INF, F32)
    padded = jnp.concatenate([blank_lo, tab, blank_hi], axis=1)
    pair_tab = jnp.concatenate([padded[:, :-1], padded[:, 1:]], axis=-1)
    nblk = rows // NA_QROWS
    mask = np.full((3, NA_QROWS, NA_SPAN), NEG_INF, np.float32)
    for p, j in enumerate((0, 1, nblk - 1)):
        base = _na_span_base(j, rows)
        for ri in range(NA_QROWS):
            r = NA_QROWS * j + ri
            rs = np.clip(r - NA_KR // 2, 0, rows - NA_KR)
            for ki in range(NA_SPAN):
                if rs <= base + ki < rs + NA_KR:
                    mask[p, ri, ki] = 0.0
    row_mask = jnp.asarray(np.repeat(mask, GRID_W, axis=2))
    return pair_tab, row_mask


def _na_attn_kernel(q_ref, k_ref, v_ref, ck_ref, cv_ref, tab_ref, mask_ref, o_ref, *, rows):
    j = pl.program_id(2)
    scale = NA_HEAD_DIM ** -0.5
    base = jnp.clip(NA_QROWS * j - NA_KR // 2, 0, rows - NA_SPAN)
    start = pl.multiple_of(base * GRID_W, GRID_W)
    span = NA_SPAN * GRID_W
    q = q_ref[0]
    kl = k_ref[0, pl.ds(start, span), :].astype(BF16)
    vl = v_ref[0, pl.ds(start, span), :].astype(BF16)
    ck = ck_ref[0].astype(BF16)
    cv = cv_ref[0].astype(BF16)
    first = lax.broadcasted_iota(jnp.int32, q.shape, 1) < NA_HEAD_DIM
    off = base - NA_QROWS * j + (NA_KR - 1) + NA_DR_PAD
    outs = []
    for hh in range(2):
        qm = jnp.where(first if hh == 0 else ~first, q, 0.0).astype(BF16)
        s_raw = _dot_nt(qm, kl) * scale
        blocks = []
        for ri in range(NA_QROWS):
            rws = slice(ri * GRID_W, (ri + 1) * GRID_W)
            pieces = [s_raw[rws, m * LANES:(m + 1) * LANES] + tab_ref[hh, off + 2 * m - ri]
                      for m in range(NA_SPAN // 2)]
            blocks.append(jnp.concatenate(pieces, axis=1) + mask_ref[0, ri:ri + 1, :])
        s_loc = jnp.concatenate(blocks, axis=0)
        s_ctx = _dot_nt(qm, ck) * scale
        m = jnp.maximum(jnp.max(s_loc, axis=-1, keepdims=True), jnp.max(s_ctx, axis=-1, keepdims=True))
        p_loc = jnp.exp(s_loc - m)
        p_ctx = jnp.exp(s_ctx - m)
        den = jnp.sum(p_loc, axis=-1, keepdims=True) + jnp.sum(p_ctx, axis=-1, keepdims=True)
        p_loc = (p_loc / den).astype(BF16)
        p_ctx = (p_ctx / den).astype(BF16)
        outs.append(_dot(p_loc, vl) + _dot(p_ctx, cv))
    o_ref[0] = jnp.where(first, outs[0], outs[1])


def _na_attn(q, k, v, ck, cv, pair_tab, row_mask):
    b, n, w = q.shape
    p = ck.shape[1]
    rows = n // GRID_W
    nblk = rows // NA_QROWS
    qb = NA_QROWS * GRID_W

    def pattern(j):
        return jnp.where(j == 0, 0, jnp.where(j == nblk - 1, 2, 1))

    full = pl.BlockSpec((1, n, LANES), lambda bi, hp, j: (bi, 0, hp))
    ctx = pl.BlockSpec((1, p, LANES), lambda bi, hp, j: (bi, 0, hp))
    blk = pl.BlockSpec((1, qb, LANES), lambda bi, hp, j: (bi, j, hp))
    return pl.pallas_call(
        functools.partial(_na_attn_kernel, rows=rows),
        grid=(b, w // LANES, nblk),
        in_specs=[blk, full, full, ctx, ctx,
                  pl.BlockSpec((2,) + pair_tab.shape[1:], lambda bi, hp, j: (hp, 0, 0, 0)),
                  pl.BlockSpec((1,) + row_mask.shape[1:], lambda bi, hp, j: (pattern(j), 0, 0))],
        out_specs=blk,
        out_shape=jax.ShapeDtypeStruct((b, n, w), F32),
        compiler_params=_cparams(("parallel", "parallel", "arbitrary")),
        name="na_attn",
    )(q, k, v, ck, cv, pair_tab, row_mask)


def _gdn_conv_kernel(x_ref, w_ref, o_ref, *, groups):
    c = pl.program_id(1)
    t = x_ref.shape[1]
    row = lax.broadcasted_iota(jnp.int32, (t, LANES), 0)
    n_qk = 2 * GDN_HEADS
    for i in range(groups):
        lanes = slice(i * LANES, (i + 1) * LANES)
        x = x_ref[0, :, lanes]
        y = jnp.zeros_like(x)
        for jj in range(CONV_K):
            o = jj - CONV_K // 2
            xs = x if o == 0 else pltpu.roll(x, (-o) % t, 0)
            xs = jnp.where((row + o >= 0) & (row + o < t), xs, 0.0)
            y = y + xs * w_ref[jj:jj + 1, lanes]
        y = _silu(y)
        nrm = lax.rsqrt(jnp.sum(y * y, axis=-1, keepdims=True) + EPS)
        o_ref[0, i] = jnp.where(c * groups + i < n_qk, y * nrm, y)


def _gdn_conv(x, conv_w):
    b, t, ch = x.shape
    nc = ch // LANES
    groups = max(1, min(nc, GDN_CONV_BLOCK_ROWS // t))
    return pl.pallas_call(
        functools.partial(_gdn_conv_kernel, groups=groups),
        grid=(b, nc // groups),
        in_specs=[pl.BlockSpec((1, t, groups * LANES), lambda bi, c: (bi, 0, c)),
                  pl.BlockSpec((CONV_K, groups * LANES), lambda bi, c: (0, c))],
        out_specs=pl.BlockSpec((1, groups, t, LANES), lambda bi, c: (bi, c, 0, 0)),
        out_shape=jax.ShapeDtypeStruct((b, nc, t, LANES), F32),
        compiler_params=_cparams(("parallel", "parallel")),
        name="gdn_conv",
    )(x, conv_w)


def _bdot(a, b):
    return jnp.dot(a.astype(BF16), b.astype(BF16), preferred_element_type=F32)


def _split_bf16(x):
    hi = x.astype(BF16)
    return hi, (x - hi.astype(F32)).astype(BF16)


def _dot3(a, b):
    m = a.shape[0]
    ah, al = _split_bf16(a)
    bh, bl = _split_bf16(b)
    top = _dot(jnp.concatenate([ah, al], axis=0), bh)
    return top[:m] + top[m:] + _dot(ah, bl)


def _gdn_chunk_kernel(qkv_ref, ba_ref, alog_ref, dt_ref, s0_ref, o_ref, sfin_ref, s_ref, *, reverse, cb):
    c = pl.program_id(1)
    nh = GDN_HEADS

    @pl.when(c == 0)
    def _():
        s_ref[...] = s0_ref[0, 0]

    ii = lax.broadcasted_iota(jnp.int32, (CHUNK, CHUNK), 0)
    jj = lax.broadcasted_iota(jnp.int32, (CHUNK, CHUNK), 1)
    lag = (jj - ii) if reverse else (ii - jj)
    incl = lag >= 0
    strict = lag > 0
    eye = (ii == jj).astype(F32)
    tri = incl.astype(F32)
    bcol = nh if reverse else 0
    gcol0 = (3 if reverse else 2) * nh
    sub8 = lax.broadcasted_iota(jnp.int32, (8, LANES), 0)
    lane8 = lax.broadcasted_iota(jnp.int32, (8, LANES), 1)
    sel8 = (lane8 == gcol0 + sub8).astype(F32)
    units = [(ci, h) for ci in range(cb) for h in range(nh)]
    gc_alls, beta_alls, grow8s = [], [], []
    for ci in range(cb):
        ba = ba_ref[0, ci * CHUNK:(ci + 1) * CHUNK, :]
        z = ba + dt_ref[...]
        softplus = jnp.maximum(z, 0.0) + jnp.log1p(jnp.exp(-jnp.abs(z)))
        g_all = -jnp.exp(alog_ref[...]) * softplus
        gc_all = jnp.dot(tri, g_all, precision=HI, preferred_element_type=F32)
        gc_alls.append(gc_all)
        beta_alls.append(jax.nn.sigmoid(ba))
        grow8s.append(_dot_nt(sel8, gc_all, precision=HI))
    gcol = [gc_alls[ci][:, gcol0 + h:gcol0 + h + 1] for ci, h in units]
    beta = [beta_alls[ci][:, bcol + h:bcol + h + 1] for ci, h in units]
    rows = [slice(ci * CHUNK, (ci + 1) * CHUNK) for ci, _ in units]
    k = [qkv_ref[0, nh + h, rows[u], :] for u, (_, h) in enumerate(units)]
    kb = [k[u] * beta[u] for u in range(len(units))]
    q = [qkv_ref[0, h, rows[u], :] * (GDN_DK ** -0.5) for u, (_, h) in enumerate(units)]
    kq = [_dot_nt(jnp.concatenate([kb[u], q[u]], axis=0).astype(BF16), k[u].astype(BF16))
          for u in range(len(units))]
    decay = [jnp.where(incl, jnp.exp(jnp.where(incl, gcol[u] - grow8s[ci][h:h + 1, :], 0.0)), 0.0)
             for u, (ci, h) in enumerate(units)]
    intra = [jnp.where(incl, kq[u][CHUNK:] * decay[u], 0.0) for u in range(len(units))]
    pw = [-jnp.where(strict, kq[u][:CHUNK] * decay[u], 0.0) for u in range(len(units))]
    tmat = [eye + p for p in pw]
    pw = [_dot3(p, p) for p in pw]
    for _ in range(4):
        pt = [_dot3(jnp.concatenate([pw[u], tmat[u]], axis=0), pw[u]) for u in range(len(units))]
        pw = [x[:CHUNK] for x in pt]
        tmat = [tmat[u] + pt[u][CHUNK:] for u in range(len(units))]
    tmat = [tmat[u] + _dot3(tmat[u], pw[u]) for u in range(len(units))]
    eg = [jnp.exp(g) for g in gcol]
    uw = [_bdot(tmat[u], jnp.concatenate(
        [qkv_ref[0, 2 * nh + h, rows[u], :] * beta[u], kb[u] * eg[u]], axis=1))
        for u, (_, h) in enumerate(units)]
    g_last = [g[0:1, :] if reverse else g[CHUNK - 1:CHUNK, :] for g in gcol]
    kd = [k[u] * jnp.exp(g_last[u] - gcol[u]) for u in range(len(units))]
    qe = [q[u] * eg[u] for u in range(len(units))]
    s = [s_ref[h] for h in range(nh)]
    for ci in (reversed(range(cb)) if reverse else range(cb)):
        us = [ci * nh + h for h in range(nh)]
        wq = [_bdot(jnp.concatenate([uw[u][:, GDN_DV:], qe[u]], axis=0), s[h]) for h, u in enumerate(us)]
        v_new = [uw[u][:, :GDN_DV] - wq[h][:CHUNK] for h, u in enumerate(us)]
        for h, u in enumerate(us):
            o_ref[0, rows[u], h * GDN_DV:(h + 1) * GDN_DV] = wq[h][CHUNK:] + _bdot(intra[u], v_new[h])
        s = [s[h] * jnp.exp(g_last[u]) + lax.dot_general(
            kd[u].astype(BF16), v_new[h].astype(BF16), (((0,), (0,)), ((), ())), preferred_element_type=F32)
            for h, u in enumerate(us)]
    for h in range(nh):
        s_ref[h] = s[h]

    @pl.when(c == pl.num_programs(1) - 1)
    def _():
        sfin_ref[0] = s_ref[...]


def _gdn_chunks(qkv, ba, alog_row, dt_row, s0, reverse):
    b, _, t, _ = qkv.shape
    cb = GDN_CHUNKS_PER_STEP
    rows = cb * CHUNK
    n = t // rows
    d = 1 if reverse else 0

    def blk(c):
        return n - 1 - c if reverse else c

    return pl.pallas_call(
        functools.partial(_gdn_chunk_kernel, reverse=reverse, cb=cb),
        grid=(b, n),
        in_specs=[pl.BlockSpec((1, 3 * GDN_HEADS, rows, LANES), lambda bi, c: (bi, 0, blk(c), 0)),
                  pl.BlockSpec((1, rows, LANES), lambda bi, c: (bi, blk(c), 0)),
                  pl.BlockSpec((1, LANES), lambda bi, c: (0, 0)),
                  pl.BlockSpec((1, LANES), lambda bi, c: (0, 0)),
                  pl.BlockSpec((1, 1, GDN_HEADS, GDN_DK, GDN_DV), lambda bi, c: (bi, d, 0, 0, 0))],
        out_specs=[pl.BlockSpec((1, rows, GDN_V_WIDTH), lambda bi, c: (bi, blk(c), 0)),
                   pl.BlockSpec((1, GDN_HEADS, GDN_DK, GDN_DV), lambda bi, c: (bi, 0, 0, 0))],
        out_shape=[jax.ShapeDtypeStruct((b, t, GDN_V_WIDTH), F32),
                   jax.ShapeDtypeStruct((b, GDN_HEADS, GDN_DK, GDN_DV), F32)],
        scratch_shapes=[pltpu.VMEM((GDN_HEADS, GDN_DK, GDN_DV), F32)],
        compiler_params=_cparams(("parallel", "arbitrary")),
        name="gdn_bwd" if reverse else "gdn_fwd",
    )(qkv, ba, alog_row, dt_row, s0)


def _pack_bf16_pairs(h):
    half = D_MODEL // 2
    lo = pltpu.bitcast(h[:, :half].astype(BF16).astype(F32), jnp.uint32)
    hi = pltpu.bitcast(h[:, half:].astype(BF16).astype(F32), jnp.uint32)
    return (hi & jnp.uint32(0xFFFF0000)) | (lo >> 16)


def _unpack_bf16_pairs(w):
    lo = pltpu.bitcast(w << 16, F32).astype(BF16)
    hi = pltpu.bitcast(w & jnp.uint32(0xFFFF0000), F32).astype(BF16)
    return lo, hi


def _mix_kernel(x_ref, na_ref, of_ref, ob_ref, z_ref, gate_ref, mod_ref, gnw_ref, gpost_ref, gpre_ref,
                wna_ref, wgdn_ref, wout_ref, x1_ref, h2_ref, h2p_ref):
    mod = mod_ref[0]
    o = of_ref[...] + ob_ref[...]
    parts = []
    for h in range(GDN_HEADS):
        sl = slice(h * GDN_DV, (h + 1) * GDN_DV)
        parts.append(_rms(o[:, sl], gnw_ref[...]) * _silu(z_ref[:, sl]))
    gdn_o = jnp.concatenate(parts, axis=-1)
    a = _dot(na_ref[...].astype(BF16), wna_ref[...])
    b = _dot(gdn_o.astype(BF16), wgdn_ref[...])
    gate = jax.nn.sigmoid(gate_ref[...])
    pre = gate[:, :D_MODEL] * a + gate[:, D_MODEL:] * b
    mix = _dot(pre.astype(BF16), wout_ref[...])
    x1 = x_ref[...] + mod[:, 2 * D_MODEL:3 * D_MODEL] * _rms(mix, gpost_ref[...])
    x1_ref[...] = x1
    h2 = _rms(x1, gpre_ref[...]) * (1.0 + mod[:, 4 * D_MODEL:5 * D_MODEL]) + mod[:, 3 * D_MODEL:4 * D_MODEL]
    h2_ref[...] = h2
    h2p_ref[...] = _pack_bf16_pairs(h2)


def _mix(x, na_o, o_f, o_b, z, gate, mods3, gnw, gpost, gpre, wna, wgdn, wout, row_of_tile):
    n = x.shape[0]
    tm = TOK_TILE
    row = lambda w: pl.BlockSpec((tm, w), lambda i: (i, 0))
    const = lambda r, c: pl.BlockSpec((r, c), lambda i: (0, 0))
    return pl.pallas_call(
        _mix_kernel,
        grid=(n // tm,),
        in_specs=[row(D_MODEL), row(NA_WIDTH), row(GDN_V_WIDTH), row(GDN_V_WIDTH),
                  row(GDN_V_WIDTH), row(2 * D_MODEL),
                  pl.BlockSpec((1, 1, 6 * D_MODEL), lambda i: (row_of_tile(i), 0, 0)),
                  const(1, GDN_DV), const(1, D_MODEL), const(1, D_MODEL),
                  const(NA_WIDTH, D_MODEL), const(GDN_V_WIDTH, D_MODEL), const(D_MODEL, D_MODEL)],
        out_specs=[row(D_MODEL), row(D_MODEL), row(D_MODEL // 2)],
        out_shape=[jax.ShapeDtypeStruct((n, D_MODEL), F32),
                   jax.ShapeDtypeStruct((n, D_MODEL), F32),
                   jax.ShapeDtypeStruct((n, D_MODEL // 2), jnp.uint32)],
        compiler_params=_cparams(("parallel",)),
        name="mix",
    )(x, na_o, o_f, o_b, z, gate, mods3, gnw.reshape(1, GDN_DV), gpost.reshape(1, D_MODEL),
      gpre.reshape(1, D_MODEL), wna, wgdn, wout)


def _router_kernel(h_ref, w_ref, b_ref, e_ref, wt_ref, cnt_ref):
    i = pl.program_id(0)
    tm = h_ref.shape[0]
    ne = N_EXPERTS
    per = ne // N_GROUPS
    logits = jnp.dot(h_ref[...], w_ref[...], precision=HI, preferred_element_type=F32)
    scores = jax.nn.sigmoid(logits)
    biased = scores + b_ref[...]
    lane = lax.broadcasted_iota(jnp.int32, (tm, ne), 1).astype(F32)
    grp = jnp.floor(lane * (1.0 / per))
    lane_g = lax.broadcasted_iota(jnp.int32, (tm, LANES), 1).astype(F32)

    def first_max(x, iota, n):
        m = jnp.max(x, axis=-1, keepdims=True)
        idx = jnp.min(jnp.where(x == m, iota, float(n)), axis=-1, keepdims=True)
        return m, idx

    gs = jnp.full((tm, LANES), NEG_INF, F32)
    for g in range(N_GROUPS):
        mg = jnp.where(grp == float(g), biased, NEG_INF)
        m1, i1 = first_max(mg, lane, ne)
        m2 = jnp.max(jnp.where(lane == i1, NEG_INF, mg), axis=-1, keepdims=True)
        gs = jnp.where(lane_g == float(g), m1 + m2, gs)
    emask = jnp.zeros((tm, ne), jnp.bool_)
    for _ in range(TOPK_GROUPS):
        _, gi = first_max(gs, lane_g, LANES)
        gs = jnp.where(lane_g == gi, NEG_INF, gs)
        emask = emask | (grp == gi)
    masked = jnp.where(emask, biased, NEG_INF)
    e_out = jnp.zeros((tm, LANES), F32)
    w_out = jnp.zeros((tm, LANES), F32)
    onehot = jnp.zeros((tm, ne), F32)
    for kk in range(TOP_K):
        _, ei = first_max(masked, lane, ne)
        hit = lane == ei
        masked = jnp.where(hit, NEG_INF, masked)
        wk = jnp.sum(jnp.where(hit, scores, 0.0), axis=-1, keepdims=True)
        onehot = onehot + hit.astype(F32)
        e_out = jnp.where(lane_g == float(kk), ei, e_out)
        w_out = jnp.where(lane_g == float(kk), wk, w_out)
    wsum = jnp.sum(w_out, axis=-1, keepdims=True)
    e_ref[...] = e_out.astype(jnp.int32)
    wt_ref[...] = w_out / wsum * ROUTED_SCALE

    @pl.when(i == 0)
    def _():
        cnt_ref[...] = jnp.zeros_like(cnt_ref)

    cnt_ref[...] += jnp.sum(onehot, axis=0, keepdims=True)


def _router(h2, w_router, router_bias):
    n = h2.shape[0]
    tm = TOK_TILE
    return pl.pallas_call(
        _router_kernel,
        grid=(n // tm,),
        in_specs=[pl.BlockSpec((tm, D_MODEL), lambda i: (i, 0)),
                  pl.BlockSpec((D_MODEL, N_EXPERTS), lambda i: (0, 0)),
                  pl.BlockSpec((1, N_EXPERTS), lambda i: (0, 0))],
        out_specs=[pl.BlockSpec((tm, LANES), lambda i: (i, 0)),
                   pl.BlockSpec((tm, LANES), lambda i: (i, 0)),
                   pl.BlockSpec((1, N_EXPERTS), lambda i: (0, 0))],
        out_shape=[jax.ShapeDtypeStruct((n, LANES), jnp.int32),
                   jax.ShapeDtypeStruct((n, LANES), F32),
                   jax.ShapeDtypeStruct((1, N_EXPERTS), F32)],
        compiler_params=_cparams(("arbitrary",)),
        name="router",
    )(h2, w_router, router_bias.reshape(1, N_EXPERTS))


def _moe_kernel(tok_ref, seg_ref, cnt_ref, h2p_hbm, w2d_ref, wg_ref, wu_ref, wd_ref, out_hbm,
                h2p_ref, acc_ref, xs_ref, ye_ref, sem):
    e = pl.program_id(0)
    half = D_MODEL // 2
    n_tok = out_hbm.shape[0]
    grp = MOE_ROW_GROUP

    @pl.when(e == 0)
    def _():
        cp = pltpu.make_async_copy(h2p_hbm, h2p_ref.at[pl.ds(0, n_tok)], sem.at[0])
        cp.start()
        acc_ref[...] = jnp.zeros_like(acc_ref)
        h2p_ref[pl.ds(n_tok, grp), :] = jnp.zeros((grp, half), jnp.uint32)
        cp.wait()

    seg = seg_ref[e]
    cnt = cnt_ref[e]
    wg = wg_ref[0].astype(BF16)
    wu = wu_ref[0].astype(BF16)
    wd = wd_ref[0].astype(BF16)
    lane = lax.broadcasted_iota(jnp.int32, (8, LANES), 1)

    def do_tile(base, n_valid):
        def tok_of(r):
            tok = tok_ref[base + r]
            return tok if n_valid is None else jnp.where(r < n_valid, tok, n_tok)

        for r in range(MOE_ROWS):
            xs_ref[r:r + 1, :] = h2p_ref[pl.ds(tok_of(r), 1), :]
        x_lo, x_hi = _unpack_bf16_pairs(xs_ref[...])
        hg = _dot(x_lo, wg[:half]) + _dot(x_hi, wg[half:])
        hu = _dot(x_lo, wu[:half]) + _dot(x_hi, wu[half:])
        act = (_silu(hg) * hu).astype(BF16)
        q = base // LANES
        sh = base % LANES
        rot = (LANES - sh) % LANES
        row_a = pltpu.roll(jnp.broadcast_to(w2d_ref[pl.ds(q, 1), :], (8, LANES)), rot, 1)
        row_b = pltpu.roll(jnp.broadcast_to(w2d_ref[pl.ds(q + 1, 1), :], (8, LANES)), rot, 1)
        w_row = jnp.where(lane + sh < LANES, row_a, row_b)[0:1, :]
        w_col = jnp.broadcast_to(w_row, (MOE_ROWS, LANES)).T
        ye_ref[...] = _dot(act, wd) * jnp.concatenate([w_col] * (D_MODEL // LANES), axis=1)
        for g in range(MOE_ROWS // grp):
            toks = [tok_of(g * grp + j) for j in range(grp)]
            rows = [acc_ref[pl.ds(toks[j], 1), :] + ye_ref[g * grp + j:g * grp + j + 1, :]
                    for j in range(grp)]
            for j in range(grp):
                acc_ref[pl.ds(toks[j], 1), :] = rows[j]

    def full_tile(t, carry):
        do_tile(seg + t * MOE_ROWS, None)
        return carry

    n_full = cnt // MOE_ROWS
    lax.fori_loop(0, n_full, full_tile, 0)
    rem = cnt - n_full * MOE_ROWS

    @pl.when(rem > 0)
    def _():
        do_tile(seg + n_full * MOE_ROWS, rem)

    @pl.when(e == pl.num_programs(0) - 1)
    def _():
        cp = pltpu.make_async_copy(acc_ref.at[pl.ds(0, n_tok)], out_hbm, sem.at[1])
        cp.start()
        cp.wait()


def _moe(tok_sorted, seg_start, seg_count, h2p, w2d, wg, wu, wd):
    n = h2p.shape[0]
    grid_spec = pltpu.PrefetchScalarGridSpec(
        num_scalar_prefetch=3,
        grid=(N_EXPERTS,),
        in_specs=[pl.BlockSpec(memory_space=pl.ANY),
                  pl.BlockSpec(w2d.shape, lambda e, *_: (0, 0)),
                  pl.BlockSpec((1, D_MODEL, EXPERT_DIM), lambda e, *_: (e, 0, 0)),
                  pl.BlockSpec((1, D_MODEL, EXPERT_DIM), lambda e, *_: (e, 0, 0)),
                  pl.BlockSpec((1, EXPERT_DIM, D_MODEL), lambda e, *_: (e, 0, 0))],
        out_specs=pl.BlockSpec(memory_space=pl.ANY),
        scratch_shapes=[pltpu.VMEM((n + MOE_ROW_GROUP, D_MODEL // 2), jnp.uint32),
                        pltpu.VMEM((n + MOE_ROW_GROUP, D_MODEL), F32),
                        pltpu.VMEM((MOE_ROWS, D_MODEL // 2), jnp.uint32),
                        pltpu.VMEM((MOE_ROWS, D_MODEL), F32),
                        pltpu.SemaphoreType.DMA((2,))],
    )
    return pl.pallas_call(
        _moe_kernel,
        grid_spec=grid_spec,
        out_shape=jax.ShapeDtypeStruct((n, D_MODEL), F32),
        compiler_params=_cparams(("arbitrary",), vmem=60 * 1024 * 1024),
        name="moe",
    )(tok_sorted, seg_start, seg_count, h2p, w2d, wg, wu, wd)


def _moe_dispatch_plan(top_e, top_w, counts, n):
    flat_e = top_e[:, :TOP_K].reshape(-1)
    flat_t = jnp.arange(n * TOP_K, dtype=jnp.int32) // TOP_K
    flat_w = top_w[:, :TOP_K].reshape(-1)
    _, tok_sorted, w_sorted = lax.sort((flat_e, flat_t, flat_w), num_keys=1, is_stable=True)
    total = flat_e.shape[0]
    tok_sorted = jnp.concatenate([tok_sorted, jnp.full((MOE_ROWS,), n, jnp.int32)])
    table_rows = -(-(total // LANES + 2) // 8) * 8
    w2d = jnp.concatenate([w_sorted, jnp.zeros((table_rows * LANES - total,), F32)]).reshape(table_rows, LANES)
    cnt = counts.reshape(-1).astype(jnp.int32)
    return tok_sorted, w2d, jnp.cumsum(cnt) - cnt, cnt


def _final_kernel(x1_ref, h2p_ref, r_ref, mod_ref, g_ref, wg_ref, wu_ref, wd_ref, y_ref):
    half = D_MODEL // 2
    mod = mod_ref[0]
    lo, hi = _unpack_bf16_pairs(h2p_ref[...])
    hg = _dot(lo, wg_ref[:half, :]) + _dot(hi, wg_ref[half:, :])
    hu = _dot(lo, wu_ref[:half, :]) + _dot(hi, wu_ref[half:, :])
    shared = _dot((_silu(hg) * hu).astype(BF16), wd_ref[...])
    ffn = r_ref[...] + shared
    y_ref[...] = x1_ref[...] + mod[:, 5 * D_MODEL:6 * D_MODEL] * _rms(ffn, g_ref[...])


def _final(x1, h2p, routed, mods3, g, wg, wu, wd, row_of_tile):
    n = x1.shape[0]
    tm = TOK_TILE
    sd = wg.shape[1]
    row = lambda w: pl.BlockSpec((tm, w), lambda i: (i, 0))
    const = lambda r, c: pl.BlockSpec((r, c), lambda i: (0, 0))
    return pl.pallas_call(
        _final_kernel,
        grid=(n // tm,),
        in_specs=[row(D_MODEL), row(D_MODEL // 2), row(D_MODEL),
                  pl.BlockSpec((1, 1, 6 * D_MODEL), lambda i: (row_of_tile(i), 0, 0)),
                  const(1, D_MODEL), const(D_MODEL, sd), const(D_MODEL, sd), const(sd, D_MODEL)],
        out_specs=row(D_MODEL),
        out_shape=jax.ShapeDtypeStruct((n, D_MODEL), F32),
        compiler_params=_cparams(("parallel",)),
        name="final",
    )(x1, h2p, routed, mods3, g.reshape(1, D_MODEL), wg, wu, wd)


def _trunk(x3, mods3, row_of_tile, attend, s0, wts):
    b, t, _ = x3.shape
    n = b * t
    x = x3.reshape(n, D_MODEL)
    q, k, v, gdn, z, gate, ba = _premix(x, mods3, wts["g_pre_mix"], wts["w_cat"], row_of_tile)
    na_o = attend(q, k, v).reshape(n, NA_WIDTH)
    qkv = _gdn_conv(gdn.reshape(b, t, GDN_CONV_CH), wts["conv_w"])
    ba3 = ba.reshape(b, t, LANES)
    o_f, s_f = _gdn_chunks(qkv, ba3, wts["alog_row"], wts["dt_row"], s0, reverse=False)
    o_b, s_b = _gdn_chunks(qkv, ba3, wts["alog_row"], wts["dt_row"], s0, reverse=True)
    s_fin = jnp.stack([s_f, s_b], axis=1)
    x1, h2, h2p = _mix(x, na_o, o_f.reshape(n, GDN_V_WIDTH), o_b.reshape(n, GDN_V_WIDTH), z, gate, mods3,
                       wts["gdn_norm_w"],
                       wts["g_post_mix"], wts["g_pre_ffn"], wts["w_na_up"], wts["w_gdn_up"],
                       wts["w_out"], row_of_tile)
    top_e, top_w, counts = _router(h2, wts["w_router"], wts["router_bias"])
    tok_sorted, w2d, seg_start, seg_count = _moe_dispatch_plan(top_e, top_w, counts, n)
    routed = _moe(tok_sorted, seg_start, seg_count, h2p, w2d, wts["w_exp_gate"], wts["w_exp_up"],
                  wts["w_exp_down"])
    y = _final(x1, h2p, routed, mods3, wts["g_post_ffn"], wts["w_sh_gate"], wts["w_sh_up"],
               wts["w_sh_down"], row_of_tile)
    return y.reshape(b, t, D_MODEL), k, v, s_fin


def kernel(x_prompt, x_sample, cache_na_k, cache_na_v, state_gdn, c, c_ctx, w_ada, b_ada, g_pre_mix,
           g_post_mix, g_pre_ffn, g_post_ffn, w_in, conv_w, gdn_a_log, gdn_dt_bias, gdn_norm_w, na_rpb,
           w_na_up, w_gdn_up, w_out, w_router, router_bias, w_exp_gate, w_exp_up, w_exp_down, w_sh_gate,
           w_sh_up, w_sh_down):
    depth = w_ada.shape[0]
    bp, tp, _ = x_prompt.shape
    bs, ts, _ = x_sample.shape
    y_prompt, y_sample = x_prompt, x_sample
    zero_state = jnp.zeros((bp, 2, GDN_HEADS, GDN_DK, GDN_DV), F32)
    new_k, new_v, new_s = [], [], []
    for l in range(depth):
        cv = jnp.concatenate([c_ctx[None], c, jnp.zeros((8 - 1 - bs, D_MODEL), F32)], axis=0)
        mods3 = _ada(cv, w_ada[l], b_ada[l]).reshape(8, 1, 6 * D_MODEL)
        wl = w_in[l]
        w_cat = jnp.concatenate(
            [wl[:, :S_Z], wl[:, S_A:], wl[:, S_Z:S_A],
             jnp.zeros((D_MODEL, LANES - 4 * GDN_HEADS), F32)], axis=1).astype(BF16)
        pad = jnp.zeros((2 * GDN_HEADS,), F32)
        tail = jnp.zeros((LANES - 4 * GDN_HEADS,), F32)
        wts = dict(
            w_cat=w_cat, g_pre_mix=g_pre_mix[l], g_post_mix=g_post_mix[l], g_pre_ffn=g_pre_ffn[l],
            g_post_ffn=g_post_ffn[l], conv_w=conv_w[l], gdn_norm_w=gdn_norm_w[l],
            alog_row=jnp.concatenate([pad, gdn_a_log[l].reshape(-1), tail]).reshape(1, LANES),
            dt_row=jnp.concatenate([pad, gdn_dt_bias[l].reshape(-1), tail]).reshape(1, LANES),
            w_na_up=w_na_up[l].astype(BF16), w_gdn_up=w_gdn_up[l].astype(BF16),
            w_out=w_out[l].astype(BF16), w_router=w_router[l], router_bias=router_bias[l],
            w_exp_gate=w_exp_gate[l], w_exp_up=w_exp_up[l], w_exp_down=w_exp_down[l],
            w_sh_gate=w_sh_gate[l].astype(BF16), w_sh_up=w_sh_up[l].astype(BF16),
            w_sh_down=w_sh_down[l].astype(BF16))

        def ctx_attend(q, k, v):
            return _ctx_attn(q.reshape(bp, tp, NA_WIDTH), k.reshape(bp, tp, NA_WIDTH),
                             v.reshape(bp, tp, NA_WIDTH))

        y_prompt, k_ctx, v_ctx, s_ctx = _trunk(y_prompt, mods3, lambda i: 0, ctx_attend, zero_state, wts)
        new_k.append(k_ctx.reshape(bp, tp, NA_HEADS, NA_HEAD_DIM))
        new_v.append(v_ctx.reshape(bp, tp, NA_HEADS, NA_HEAD_DIM))
        new_s.append(s_ctx)

        pair_tab, row_mask = _na_bias_tables(na_rpb[l], ts // GRID_W)
        ck = cache_na_k[:, l].reshape(bs, -1, NA_WIDTH)
        cvv = cache_na_v[:, l].reshape(bs, -1, NA_WIDTH)

        def na_attend(q, k, v):
            return _na_attn(q.reshape(bs, ts, NA_WIDTH), k.reshape(bs, ts, NA_WIDTH),
                            v.reshape(bs, ts, NA_WIDTH), ck, cvv, pair_tab, row_mask)

        tiles_per_seq = ts // TOK_TILE
        y_sample, _, _, _ = _trunk(y_sample, mods3, lambda i: 1 + i // tiles_per_seq, na_attend,
                                   state_gdn[:, l], wts)
    return (y_prompt, y_sample, jnp.stack(new_k, axis=1), jnp.stack(new_v, axis=1),
            jnp.stack(new_s, axis=1))
```

```python
import functools

import numpy as np
import jax
import jax.numpy as jnp
from jax import lax
from jax.experimental import pallas as pl
from jax.experimental.pallas import tpu as pltpu

F32 = jnp.float32
BF16 = jnp.bfloat16
HI = lax.Precision.HIGHEST

D_MODEL = 1024
GRID_W = 64
NA_HEADS = 8
NA_HEAD_DIM = 64
NA_WIDTH = NA_HEADS * NA_HEAD_DIM
NA_KR = 8
NA_KC = 16
GDN_HEADS = 4
GDN_DK = 128
GDN_DV = 128
GDN_QK_WIDTH = GDN_HEADS * GDN_DK
GDN_V_WIDTH = GDN_HEADS * GDN_DV
GDN_CONV_CH = 2 * GDN_QK_WIDTH + GDN_V_WIDTH
CONV_K = 5
CHUNK = 64
N_EXPERTS = 256
TOP_K = 8
N_GROUPS = 8
TOPK_GROUPS = 4
EXPERT_DIM = 256
ROUTED_SCALE = 2.5
EPS = 1e-6
S_NA = 3 * NA_WIDTH
S_GDN = S_NA + GDN_CONV_CH
S_Z = S_GDN + GDN_V_WIDTH
S_B = S_Z + 2 * GDN_HEADS
S_A = S_B + 2 * GDN_HEADS

LANES = 128
TOK_TILE = 256
NA_QROWS = 8
NA_SPAN = 16
MOE_ROWS = 128
MOE_ROW_GROUP = 8
GDN_CHUNKS_PER_STEP = 4
GDN_CONV_BLOCK_ROWS = 4096
VMEM_LIMIT = 56 * 1024 * 1024
NEG_INF = float("-inf")


def _cparams(sem, vmem=VMEM_LIMIT):
    return pltpu.CompilerParams(dimension_semantics=sem, vmem_limit_bytes=vmem)


def _silu(x):
    return x * jax.nn.sigmoid(x)


def _rms(x, g):
    return x * lax.rsqrt(jnp.mean(x * x, axis=-1, keepdims=True) + EPS) * g


def _dot(a, b):
    return jnp.dot(a, b, preferred_element_type=F32)


def _dot_nt(a, b, precision=None):
    return lax.dot_general(a, b, (((1,), (1,)), ((), ())), precision=precision,
                           preferred_element_type=F32)


def _ada_kernel(c_ref, w_ref, b_ref, o_ref):
    o_ref[...] = jnp.dot(_silu(c_ref[...]), w_ref[...], precision=HI,
                         preferred_element_type=F32) + b_ref[...]


def _ada(cv, w_ada, b_ada):
    n = w_ada.shape[1]
    tn = 512
    return pl.pallas_call(
        _ada_kernel,
        grid=(n // tn,),
        in_specs=[pl.BlockSpec((8, D_MODEL), lambda j: (0, 0)),
                  pl.BlockSpec((D_MODEL, tn), lambda j: (0, j)),
                  pl.BlockSpec((1, tn), lambda j: (0, j))],
        out_specs=pl.BlockSpec((8, tn), lambda j: (0, j)),
        out_shape=jax.ShapeDtypeStruct((8, n), F32),
        compiler_params=_cparams(("parallel",)),
        name="ada",
    )(cv, w_ada, b_ada.reshape(1, n))


_PM_WIDTHS = (NA_WIDTH, NA_WIDTH, NA_WIDTH, GDN_CONV_CH, GDN_V_WIDTH, 2 * D_MODEL, LANES)


def _premix_kernel(x_ref, mod_ref, g_ref, w_ref, *o_refs):
    mod = mod_ref[0]
    h = _rms(x_ref[...], g_ref[...]) * (1.0 + mod[:, D_MODEL:2 * D_MODEL]) + mod[:, 0:D_MODEL]
    hb = h.astype(BF16)
    off = 0
    for o_ref, wd in zip(o_refs, _PM_WIDTHS):
        for c0 in range(0, wd, 512):
            c1 = min(c0 + 512, wd)
            o_ref[:, c0:c1] = _dot(hb, w_ref[:, off + c0:off + c1])
        off += wd


def _premix(x, mods3, g, w_cat, row_of_tile):
    n = x.shape[0]
    wtot = w_cat.shape[1]
    tm = TOK_TILE
    return pl.pallas_call(
        _premix_kernel,
        grid=(n // tm,),
        in_specs=[pl.BlockSpec((tm, D_MODEL), lambda i: (i, 0)),
                  pl.BlockSpec((1, 1, 6 * D_MODEL), lambda i: (row_of_tile(i), 0, 0)),
                  pl.BlockSpec((1, D_MODEL), lambda i: (0, 0)),
                  pl.BlockSpec((D_MODEL, wtot), lambda i: (0, 0))],
        out_specs=[pl.BlockSpec((tm, wd), lambda i: (i, 0)) for wd in _PM_WIDTHS],
        out_shape=[jax.ShapeDtypeStruct((n, wd), F32) for wd in _PM_WIDTHS],
        compiler_params=_cparams(("parallel",)),
        name="premix",
    )(x, mods3, g.reshape(1, D_MODEL), w_cat)


def _softmax_rows(s):
    m = jnp.max(s, axis=-1, keepdims=True)
    p = jnp.exp(s - m)
    return p / jnp.sum(p, axis=-1, keepdims=True)


def _ctx_attn_kernel(q_ref, k_ref, v_ref, o_ref):
    scale = NA_HEAD_DIM ** -0.5
    for hp in range(NA_HEADS // 2):
        outs = []
        for h in (2 * hp, 2 * hp + 1):
            sl = slice(h * NA_HEAD_DIM, (h + 1) * NA_HEAD_DIM)
            q = q_ref[0, :, sl].astype(BF16)
            k = k_ref[0, :, sl].astype(BF16)
            v = v_ref[0, :, sl].astype(BF16)
            p = _softmax_rows(_dot_nt(q, k) * scale)
            outs.append(_dot(p.astype(BF16), v))
        o_ref[0, :, hp * LANES:(hp + 1) * LANES] = jnp.concatenate(outs, axis=-1)


def _ctx_attn(q, k, v):
    b, t, w = q.shape
    spec = pl.BlockSpec((1, t, w), lambda i: (i, 0, 0))
    return pl.pallas_call(
        _ctx_attn_kernel,
        grid=(b,),
        in_specs=[spec, spec, spec],
        out_specs=spec,
        out_shape=jax.ShapeDtypeStruct((b, t, w), F32),
        compiler_params=_cparams(("parallel",)),
        name="ctx_attn",
    )(q, k, v)


def _na_span_base(j, rows):
    return np.clip(NA_QROWS * j - NA_KR // 2, 0, rows - NA_SPAN)


NA_DR_PAD = NA_QROWS


def _na_bias_tables(rpb, rows):
    col = np.arange(GRID_W)
    dcm = np.clip(col[None, :] - col[:, None], -(NA_KC - 1), NA_KC - 1) + (NA_KC - 1)
    onehot = (dcm[None] == np.arange(2 * NA_KC - 1)[:, None, None]).astype(np.float32)
    tab = jnp.einsum('hrd,dqk->hrqk', rpb.astype(F32), jnp.asarray(onehot), precision=HI)
    col_start = np.clip(col - NA_KC // 2, 0, GRID_W - NA_KC)
    col_in = (col[None, :] >= col_start[:, None]) & (col[None, :] < col_start[:, None] + NA_KC)
    tab = jnp.where(jnp.asarray(col_in)[None, None], tab, NEG_INF)
    n_dr = 2 * NA_KR - 1
    n_side = NA_DR_PAD + NA_SPAN - n_dr + 1
    blank_lo = jnp.full((NA_HEADS, NA_DR_PAD, GRID_W, GRID_W), NEG_INF, F32)
    blank_hi = jnp.full((NA_HEADS, n_side, GRID_W, GRID_W), NEG_INF, F32)
    padded = jnp.concatenate([blank_lo, tab, blank_hi], axis=1)
    pair_tab = jnp.concatenate([padded[:, :-1], padded[:, 1:]], axis=-1)
    nblk = rows // NA_QROWS
    mask = np.full((3, NA_QROWS, NA_SPAN), NEG_INF, np.float32)
    for p, j in enumerate((0, 1, nblk - 1)):
        base = _na_span_base(j, rows)
        for ri in range(NA_QROWS):
            r = NA_QROWS * j + ri
            rs = np.clip(r - NA_KR // 2, 0, rows - NA_KR)
            for ki in range(NA_SPAN):
                if rs <= base + ki < rs + NA_KR:
                    mask[p, ri, ki] = 0.0
    row_mask = jnp.asarray(np.repeat(mask, GRID_W, axis=2))
    return pair_tab, row_mask


def _na_attn_kernel(q_ref, k_ref, v_ref, ck_ref, cv_ref, tab_ref, mask_ref, o_ref, *, rows):
    j = pl.program_id(2)
    scale = NA_HEAD_DIM ** -0.5
    base = jnp.clip(NA_QROWS * j - NA_KR // 2, 0, rows - NA_SPAN)
    start = pl.multiple_of(base * GRID_W, GRID_W)
    span = NA_SPAN * GRID_W
    q = q_ref[0]
    kl = k_ref[0, pl.ds(start, span), :].astype(BF16)
    vl = v_ref[0, pl.ds(start, span), :].astype(BF16)
    ck = ck_ref[0].astype(BF16)
    cv = cv_ref[0].astype(BF16)
    first = lax.broadcasted_iota(jnp.int32, q.shape, 1) < NA_HEAD_DIM
    off = base - NA_QROWS * j + (NA_KR - 1) + NA_DR_PAD
    outs = []
    for hh in range(2):
        qm = jnp.where(first if hh == 0 else ~first, q, 0.0).astype(BF16)
        s_raw = _dot_nt(qm, kl) * scale
        blocks = []
        for ri in range(NA_QROWS):
            rws = slice(ri * GRID_W, (ri + 1) * GRID_W)
            pieces = [s_raw[rws, m * LANES:(m + 1) * LANES] + tab_ref[hh, off + 2 * m - ri]
                      for m in range(NA_SPAN // 2)]
            blocks.append(jnp.concatenate(pieces, axis=1) + mask_ref[0, ri:ri + 1, :])
        s_loc = jnp.concatenate(blocks, axis=0)
        s_ctx = _dot_nt(qm, ck) * scale
        m = jnp.maximum(jnp.max(s_loc, axis=-1, keepdims=True), jnp.max(s_ctx, axis=-1, keepdims=True))
        p_loc = jnp.exp(s_loc - m)
        p_ctx = jnp.exp(s_ctx - m)
        den = jnp.sum(p_loc, axis=-1, keepdims=True) + jnp.sum(p_ctx, axis=-1, keepdims=True)
        p_loc = (p_loc / den).astype(BF16)
        p_ctx = (p_ctx / den).astype(BF16)
        outs.append(_dot(p_loc, vl) + _dot(p_ctx, cv))
    o_ref[0] = jnp.where(first, outs[0], outs[1])


def _na_attn(q, k, v, ck, cv, pair_tab, row_mask):
    b, n, w = q.shape
    p = ck.shape[1]
    rows = n // GRID_W
    nblk = rows // NA_QROWS
    qb = NA_QROWS * GRID_W

    def pattern(j):
        return jnp.where(j == 0, 0, jnp.where(j == nblk - 1, 2, 1))

    full = pl.BlockSpec((1, n, LANES), lambda bi, hp, j: (bi, 0, hp))
    ctx = pl.BlockSpec((1, p, LANES), lambda bi, hp, j: (bi, 0, hp))
    blk = pl.BlockSpec((1, qb, LANES), lambda bi, hp, j: (bi, j, hp))
    return pl.pallas_call(
        functools.partial(_na_attn_kernel, rows=rows),
        grid=(b, w // LANES, nblk),
        in_specs=[blk, full, full, ctx, ctx,
                  pl.BlockSpec((2,) + pair_tab.shape[1:], lambda bi, hp, j: (hp, 0, 0, 0)),
                  pl.BlockSpec((1,) + row_mask.shape[1:], lambda bi, hp, j: (pattern(j), 0, 0))],
        out_specs=blk,
        out_shape=jax.ShapeDtypeStruct((b, n, w), F32),
        compiler_params=_cparams(("parallel", "parallel", "arbitrary")),
        name="na_attn",
    )(q, k, v, ck, cv, pair_tab, row_mask)


def _gdn_conv_kernel(x_ref, w_ref, o_ref, *, groups):
    c = pl.program_id(1)
    t = x_ref.shape[1]
    row = lax.broadcasted_iota(jnp.int32, (t, LANES), 0)
    n_qk = 2 * GDN_HEADS
    for i in range(groups):
        lanes = slice(i * LANES, (i + 1) * LANES)
        x = x_ref[0, :, lanes]
        y = jnp.zeros_like(x)
        for jj in range(CONV_K):
            o = jj - CONV_K // 2
            xs = x if o == 0 else pltpu.roll(x, (-o) % t, 0)
            xs = jnp.where((row + o >= 0) & (row + o < t), xs, 0.0)
            y = y + xs * w_ref[jj:jj + 1, lanes]
        y = _silu(y)
        nrm = lax.rsqrt(jnp.sum(y * y, axis=-1, keepdims=True) + EPS)
        o_ref[0, i] = jnp.where(c * groups + i < n_qk, y * nrm, y)


def _gdn_conv(x, conv_w):
    b, t, ch = x.shape
    nc = ch // LANES
    groups = max(1, min(nc, GDN_CONV_BLOCK_ROWS // t))
    return pl.pallas_call(
        functools.partial(_gdn_conv_kernel, groups=groups),
        grid=(b, nc // groups),
        in_specs=[pl.BlockSpec((1, t, groups * LANES), lambda bi, c: (bi, 0, c)),
                  pl.BlockSpec((CONV_K, groups * LANES), lambda bi, c: (0, c))],
        out_specs=pl.BlockSpec((1, groups, t, LANES), lambda bi, c: (bi, c, 0, 0)),
        out_shape=jax.ShapeDtypeStruct((b, nc, t, LANES), F32),
        compiler_params=_cparams(("parallel", "parallel")),
        name="gdn_conv",
    )(x, conv_w)


def _bdot(a, b):
    return jnp.dot(a.astype(BF16), b.astype(BF16), preferred_element_type=F32)


def _split_bf16(x):
    hi = x.astype(BF16)
    return hi, (x - hi.astype(F32)).astype(BF16)


def _dot3(a, b):
    m = a.shape[0]
    ah, al = _split_bf16(a)
    bh, bl = _split_bf16(b)
    top = _dot(jnp.concatenate([ah, al], axis=0), bh)
    return top[:m] + top[m:] + _dot(ah, bl)


def _gdn_chunk_kernel(qkv_ref, ba_ref, alog_ref, dt_ref, s0_ref, o_ref, sfin_ref, s_ref, *, reverse, cb):
    c = pl.program_id(1)
    nh = GDN_HEADS

    @pl.when(c == 0)
    def _():
        s_ref[...] = s0_ref[0, 0]

    ii = lax.broadcasted_iota(jnp.int32, (CHUNK, CHUNK), 0)
    jj = lax.broadcasted_iota(jnp.int32, (CHUNK, CHUNK), 1)
    lag = (jj - ii) if reverse else (ii - jj)
    incl = lag >= 0
    strict = lag > 0
    eye = (ii == jj).astype(F32)
    tri = incl.astype(F32)
    bcol = nh if reverse else 0
    gcol0 = (3 if reverse else 2) * nh
    sub8 = lax.broadcasted_iota(jnp.int32, (8, LANES), 0)
    lane8 = lax.broadcasted_iota(jnp.int32, (8, LANES), 1)
    sel8 = (lane8 == gcol0 + sub8).astype(F32)
    units = [(ci, h) for ci in range(cb) for h in range(nh)]
    gc_alls, beta_alls, grow8s = [], [], []
    for ci in range(cb):
        ba = ba_ref[0, ci * CHUNK:(ci + 1) * CHUNK, :]
        z = ba + dt_ref[...]
        softplus = jnp.maximum(z, 0.0) + jnp.log1p(jnp.exp(-jnp.abs(z)))
        g_all = -jnp.exp(alog_ref[...]) * softplus
        gc_all = jnp.dot(tri, g_all, precision=HI, preferred_element_type=F32)
        gc_alls.append(gc_all)
        beta_alls.append(jax.nn.sigmoid(ba))
        grow8s.append(_dot_nt(sel8, gc_all, precision=HI))
    gcol = [gc_alls[ci][:, gcol0 + h:gcol0 + h + 1] for ci, h in units]
    beta = [beta_alls[ci][:, bcol + h:bcol + h + 1] for ci, h in units]
    rows = [slice(ci * CHUNK, (ci + 1) * CHUNK) for ci, _ in units]
    k = [qkv_ref[0, nh + h, rows[u], :] for u, (_, h) in enumerate(units)]
    kb = [k[u] * beta[u] for u in range(len(units))]
    q = [qkv_ref[0, h, rows[u], :] * (GDN_DK ** -0.5) for u, (_, h) in enumerate(units)]
    kq = [_dot_nt(jnp.concatenate([kb[u], q[u]], axis=0).astype(BF16), k[u].astype(BF16))
          for u in range(len(units))]
    decay = [jnp.where(incl, jnp.exp(jnp.where(incl, gcol[u] - grow8s[ci][h:h + 1, :], 0.0)), 0.0)
             for u, (ci, h) in enumerate(units)]
    intra = [jnp.where(incl, kq[u][CHUNK:] * decay[u], 0.0) for u in range(len(units))]
    pw = [-jnp.where(strict, kq[u][:CHUNK] * decay[u], 0.0) for u in range(len(units))]
    tmat = [eye + p for p in pw]
    pw = [_dot3(p, p) for p in pw]
    for _ in range(4):
        pt = [_dot3(jnp.concatenate([pw[u], tmat[u]], axis=0), pw[u]) for u in range(len(units))]
        pw = [x[:CHUNK] for x in pt]
        tmat = [tmat[u] + pt[u][CHUNK:] for u in range(len(units))]
    tmat = [tmat[u] + _dot3(tmat[u], pw[u]) for u in range(len(units))]
    eg = [jnp.exp(g) for g in gcol]
    uw = [_bdot(tmat[u], jnp.concatenate(
        [qkv_ref[0, 2 * nh + h, rows[u], :] * beta[u], kb[u] * eg[u]], axis=1))
        for u, (_, h) in enumerate(units)]
    g_last = [g[0:1, :] if reverse else g[CHUNK - 1:CHUNK, :] for g in gcol]
    kd = [k[u] * jnp.exp(g_last[u] - gcol[u]) for u in range(len(units))]
    qe = [q[u] * eg[u] for u in range(len(units))]
    s = [s_ref[h] for h in range(nh)]
    for ci in (reversed(range(cb)) if reverse else range(cb)):
        us = [ci * nh + h for h in range(nh)]
        wq = [_bdot(jnp.concatenate([uw[u][:, GDN_DV:], qe[u]], axis=0), s[h]) for h, u in enumerate(us)]
        v_new = [uw[u][:, :GDN_DV] - wq[h][:CHUNK] for h, u in enumerate(us)]
        for h, u in enumerate(us):
            o_ref[0, rows[u], h * GDN_DV:(h + 1) * GDN_DV] = wq[h][CHUNK:] + _bdot(intra[u], v_new[h])
        s = [s[h] * jnp.exp(g_last[u]) + lax.dot_general(
            kd[u].astype(BF16), v_new[h].astype(BF16), (((0,), (0,)), ((), ())), preferred_element_type=F32)
            for h, u in enumerate(us)]
    for h in range(nh):
        s_ref[h] = s[h]

    @pl.when(c == pl.num_programs(1) - 1)
    def _():
        sfin_ref[0] = s_ref[...]


def _gdn_chunks(qkv, ba, alog_row, dt_row, s0, reverse):
    b, _, t, _ = qkv.shape
    cb = GDN_CHUNKS_PER_STEP
    rows = cb * CHUNK
    n = t // rows
    d = 1 if reverse else 0

    def blk(c):
        return n - 1 - c if reverse else c

    return pl.pallas_call(
        functools.partial(_gdn_chunk_kernel, reverse=reverse, cb=cb),
        grid=(b, n),
        in_specs=[pl.BlockSpec((1, 3 * GDN_HEADS, rows, LANES), lambda bi, c: (bi, 0, blk(c), 0)),
                  pl.BlockSpec((1, rows, LANES), lambda bi, c: (bi, blk(c), 0)),
                  pl.BlockSpec((1, LANES), lambda bi, c: (0, 0)),
                  pl.BlockSpec((1, LANES), lambda bi, c: (0, 0)),
                  pl.BlockSpec((1, 1, GDN_HEADS, GDN_DK, GDN_DV), lambda bi, c: (bi, d, 0, 0, 0))],
        out_specs=[pl.BlockSpec((1, rows, GDN_V_WIDTH), lambda bi, c: (bi, blk(c), 0)),
                   pl.BlockSpec((1, GDN_HEADS, GDN_DK, GDN_DV), lambda bi, c: (bi, 0, 0, 0))],
        out_shape=[jax.ShapeDtypeStruct((b, t, GDN_V_WIDTH), F32),
                   jax.ShapeDtypeStruct((b, GDN_HEADS, GDN_DK, GDN_DV), F32)],
        scratch_shapes=[pltpu.VMEM((GDN_HEADS, GDN_DK, GDN_DV), F32)],
        compiler_params=_cparams(("parallel", "arbitrary")),
        name="gdn_bwd" if reverse else "gdn_fwd",
    )(qkv, ba, alog_row, dt_row, s0)


def _pack_bf16_pairs(h):
    half = D_MODEL // 2
    lo = pltpu.bitcast(h[:, :half].astype(BF16).astype(F32), jnp.uint32)
    hi = pltpu.bitcast(h[:, half:].astype(BF16).astype(F32), jnp.uint32)
    return (hi & jnp.uint32(0xFFFF0000)) | (lo >> 16)


def _unpack_bf16_pairs(w):
    lo = pltpu.bitcast(w << 16, F32).astype(BF16)
    hi = pltpu.bitcast(w & jnp.uint32(0xFFFF0000), F32).astype(BF16)
    return lo, hi


def _mix_kernel(x_ref, na_ref, of_ref, ob_ref, z_ref, gate_ref, mod_ref, gnw_ref, gpost_ref, gpre_ref,
                wna_ref, wgdn_ref, wout_ref, x1_ref, h2_ref, h2p_ref):
    mod = mod_ref[0]
    o = of_ref[...] + ob_ref[...]
    parts = []
    for h in range(GDN_HEADS):
        sl = slice(h * GDN_DV, (h + 1) * GDN_DV)
        parts.append(_rms(o[:, sl], gnw_ref[...]) * _silu(z_ref[:, sl]))
    gdn_o = jnp.concatenate(parts, axis=-1)
    a = _dot(na_ref[...].astype(BF16), wna_ref[...])
    b = _dot(gdn_o.astype(BF16), wgdn_ref[...])
    gate = jax.nn.sigmoid(gate_ref[...])
    pre = gate[:, :D_MODEL] * a + gate[:, D_MODEL:] * b
    mix = _dot(pre.astype(BF16), wout_ref[...])
    x1 = x_ref[...] + mod[:, 2 * D_MODEL:3 * D_MODEL] * _rms(mix, gpost_ref[...])
    x1_ref[...] = x1
    h2 = _rms(x1, gpre_ref[...]) * (1.0 + mod[:, 4 * D_MODEL:5 * D_MODEL]) + mod[:, 3 * D_MODEL:4 * D_MODEL]
    h2_ref[...] = h2
    h2p_ref[...] = _pack_bf16_pairs(h2)


def _mix(x, na_o, o_f, o_b, z, gate, mods3, gnw, gpost, gpre, wna, wgdn, wout, row_of_tile):
    n = x.shape[0]
    tm = TOK_TILE
    row = lambda w: pl.BlockSpec((tm, w), lambda i: (i, 0))
    const = lambda r, c: pl.BlockSpec((r, c), lambda i: (0, 0))
    return pl.pallas_call(
        _mix_kernel,
        grid=(n // tm,),
        in_specs=[row(D_MODEL), row(NA_WIDTH), row(GDN_V_WIDTH), row(GDN_V_WIDTH),
                  row(GDN_V_WIDTH), row(2 * D_MODEL),
                  pl.BlockSpec((1, 1, 6 * D_MODEL), lambda i: (row_of_tile(i), 0, 0)),
                  const(1, GDN_DV), const(1, D_MODEL), const(1, D_MODEL),
                  const(NA_WIDTH, D_MODEL), const(GDN_V_WIDTH, D_MODEL), const(D_MODEL, D_MODEL)],
        out_specs=[row(D_MODEL), row(D_MODEL), row(D_MODEL // 2)],
        out_shape=[jax.ShapeDtypeStruct((n, D_MODEL), F32),
                   jax.ShapeDtypeStruct((n, D_MODEL), F32),
                   jax.ShapeDtypeStruct((n, D_MODEL // 2), jnp.uint32)],
        compiler_params=_cparams(("parallel",)),
        name="mix",
    )(x, na_o, o_f, o_b, z, gate, mods3, gnw.reshape(1, GDN_DV), gpost.reshape(1, D_MODEL),
      gpre.reshape(1, D_MODEL), wna, wgdn, wout)


def _router_kernel(h_ref, w_ref, b_ref, e_ref, wt_ref, cnt_ref):
    i = pl.program_id(0)
    tm = h_ref.shape[0]
    ne = N_EXPERTS
    per = ne // N_GROUPS
    logits = jnp.dot(h_ref[...], w_ref[...], precision=HI, preferred_element_type=F32)
    scores = jax.nn.sigmoid(logits)
    biased = scores + b_ref[...]
    lane = lax.broadcasted_iota(jnp.int32, (tm, ne), 1).astype(F32)
    grp = jnp.floor(lane * (1.0 / per))
    lane_g = lax.broadcasted_iota(jnp.int32, (tm, LANES), 1).astype(F32)

    def first_max(x, iota, n):
        m = jnp.max(x, axis=-1, keepdims=True)
        idx = jnp.min(jnp.where(x == m, iota, float(n)), axis=-1, keepdims=True)
        return m, idx

    gs = jnp.full((tm, LANES), NEG_INF, F32)
    for g in range(N_GROUPS):
        mg = jnp.where(grp == float(g), biased, NEG_INF)
        m1, i1 = first_max(mg, lane, ne)
        m2 = jnp.max(jnp.where(lane == i1, NEG_INF, mg), axis=-1, keepdims=True)
        gs = jnp.where(lane_g == float(g), m1 + m2, gs)
    emask = jnp.zeros((tm, ne), jnp.bool_)
    for _ in range(TOPK_GROUPS):
        _, gi = first_max(gs, lane_g, LANES)
        gs = jnp.where(lane_g == gi, NEG_INF, gs)
        emask = emask | (grp == gi)
    masked = jnp.where(emask, biased, NEG_INF)
    e_out = jnp.zeros((tm, LANES), F32)
    w_out = jnp.zeros((tm, LANES), F32)
    onehot = jnp.zeros((tm, ne), F32)
    for kk in range(TOP_K):
        _, ei = first_max(masked, lane, ne)
        hit = lane == ei
        masked = jnp.where(hit, NEG_INF, masked)
        wk = jnp.sum(jnp.where(hit, scores, 0.0), axis=-1, keepdims=True)
        onehot = onehot + hit.astype(F32)
        e_out = jnp.where(lane_g == float(kk), ei, e_out)
        w_out = jnp.where(lane_g == float(kk), wk, w_out)
    wsum = jnp.sum(w_out, axis=-1, keepdims=True)
    e_ref[...] = e_out.astype(jnp.int32)
    wt_ref[...] = w_out / wsum * ROUTED_SCALE

    @pl.when(i == 0)
    def _():
        cnt_ref[...] = jnp.zeros_like(cnt_ref)

    cnt_ref[...] += jnp.sum(onehot, axis=0, keepdims=True)


def _router(h2, w_router, router_bias):
    n = h2.shape[0]
    tm = TOK_TILE
    return pl.pallas_call(
        _router_kernel,
        grid=(n // tm,),
        in_specs=[pl.BlockSpec((tm, D_MODEL), lambda i: (i, 0)),
                  pl.BlockSpec((D_MODEL, N_EXPERTS), lambda i: (0, 0)),
                  pl.BlockSpec((1, N_EXPERTS), lambda i: (0, 0))],
        out_specs=[pl.BlockSpec((tm, LANES), lambda i: (i, 0)),
                   pl.BlockSpec((tm, LANES), lambda i: (i, 0)),
                   pl.BlockSpec((1, N_EXPERTS), lambda i: (0, 0))],
        out_shape=[jax.ShapeDtypeStruct((n, LANES), jnp.int32),
                   jax.ShapeDtypeStruct((n, LANES), F32),
                   jax.ShapeDtypeStruct((1, N_EXPERTS), F32)],
        compiler_params=_cparams(("arbitrary",)),
        name="router",
    )(h2, w_router, router_bias.reshape(1, N_EXPERTS))


_ST_SLOT, _ST_READY, _ST_PEND_BASE, _ST_PEND_VALID = range(4)


def _moe_kernel(tok_ref, seg_ref, cnt_ref, h2p_hbm, w2d_ref, wg_ref, wu_ref, wd_ref, out_hbm,
                h2p_ref, acc_ref, xs_ref, ye_ref, st_ref, sem):
    e = pl.program_id(0)
    n_exp = pl.num_programs(0)
    half = D_MODEL // 2
    xw = half // LANES
    yw = D_MODEL // LANES
    n_tok = out_hbm.shape[0] // yw
    grp = MOE_ROW_GROUP

    def gather_tile(base, n_valid, slot):
        for r in range(MOE_ROWS):
            tok = jnp.where(r < n_valid, tok_ref[base + r], n_tok)
            xs_ref[slot, xw * r:xw * (r + 1), :] = h2p_ref[pl.ds(pl.multiple_of(tok * xw, xw), xw), :]

    def scatter_tile(base, n_valid, slot):
        for g in range(MOE_ROWS // grp):
            toks = [pl.multiple_of(
                jnp.where(g * grp + j < n_valid, tok_ref[base + g * grp + j], n_tok) * yw, yw)
                for j in range(grp)]
            rows = [acc_ref[pl.ds(toks[j], yw), :] + ye_ref[slot, yw * (g * grp + j):yw * (g * grp + j + 1), :]
                    for j in range(grp)]
            for j in range(grp):
                acc_ref[pl.ds(toks[j], yw), :] = rows[j]

    seg = seg_ref[e]
    cnt = cnt_ref[e]
    n_tiles = (cnt + MOE_ROWS - 1) // MOE_ROWS

    @pl.when(e == 0)
    def _():
        cp = pltpu.make_async_copy(h2p_hbm, h2p_ref.at[pl.ds(0, n_tok * xw)], sem.at[0])
        cp.start()
        acc_ref[...] = jnp.zeros_like(acc_ref)
        ye_ref[...] = jnp.zeros_like(ye_ref)
        h2p_ref[pl.ds(n_tok * xw, grp * xw), :] = jnp.zeros((grp * xw, LANES), jnp.uint32)
        st_ref[_ST_SLOT] = 0
        st_ref[_ST_READY] = -1
        st_ref[_ST_PEND_BASE] = 0
        st_ref[_ST_PEND_VALID] = 0
        cp.wait()

    @pl.when((n_tiles > 0) & (st_ref[_ST_READY] != e))
    def _():
        gather_tile(seg, jnp.minimum(cnt, MOE_ROWS), st_ref[_ST_SLOT])

    def regroup(w):
        w = w.astype(BF16)
        return [jnp.concatenate([w[LANES * c:LANES * (c + 1)], w[half + LANES * c:half + LANES * (c + 1)]],
                                axis=0) for c in range(xw)]

    wg = regroup(wg_ref[0])
    wu = regroup(wu_ref[0])
    wd = wd_ref[0].astype(BF16)
    lane = lax.broadcasted_iota(jnp.int32, (8, LANES), 1)
    nxt_e = jnp.minimum(e + 1, n_exp - 1)
    nxt_seg = seg_ref[nxt_e]
    nxt_cnt = jnp.where(e + 1 < n_exp, cnt_ref[nxt_e], 0)

    def tile_body(t, carry):
        slot = st_ref[_ST_SLOT]
        base = seg + t * MOE_ROWS
        last = t + 1 == n_tiles
        xk = [jnp.concatenate(_unpack_bf16_pairs(xs_ref[slot, pl.ds(c, MOE_ROWS, stride=xw), :]), axis=1)
              for c in range(xw)]
        g_base = jnp.where(last, nxt_seg, base + MOE_ROWS)
        g_valid = jnp.minimum(MOE_ROWS, jnp.where(last, nxt_cnt, cnt - (t + 1) * MOE_ROWS))
        gather_tile(g_base, g_valid, 1 - slot)
        scatter_tile(st_ref[_ST_PEND_BASE], st_ref[_ST_PEND_VALID], 1 - slot)
        hg = sum(_dot(xk[c], wg[c]) for c in range(xw))
        hu = sum(_dot(xk[c], wu[c]) for c in range(xw))
        act = (_silu(hg) * hu).astype(BF16)
        q = base // LANES
        sh = base % LANES
        rot = (LANES - sh) % LANES
        row_a = pltpu.roll(jnp.broadcast_to(w2d_ref[pl.ds(q, 1), :], (8, LANES)), rot, 1)
        row_b = pltpu.roll(jnp.broadcast_to(w2d_ref[pl.ds(q + 1, 1), :], (8, LANES)), rot, 1)
        w_row = jnp.where(lane + sh < LANES, row_a, row_b)[0:1, :]
        w_col = jnp.broadcast_to(w_row, (MOE_ROWS, LANES)).T
        ye = _dot(act, wd)
        for c in range(yw):
            ye_ref[slot, pl.ds(c, MOE_ROWS, stride=yw), :] = ye[:, LANES * c:LANES * (c + 1)] * w_col
        st_ref[_ST_PEND_BASE] = base
        st_ref[_ST_PEND_VALID] = jnp.minimum(MOE_ROWS, cnt - t * MOE_ROWS)
        st_ref[_ST_SLOT] = 1 - slot
        st_ref[_ST_READY] = jnp.where(last, e + 1, e)
        return carry

    lax.fori_loop(0, n_tiles, tile_body, 0)

    @pl.when(e == n_exp - 1)
    def _():
        scatter_tile(st_ref[_ST_PEND_BASE], st_ref[_ST_PEND_VALID], 1 - st_ref[_ST_SLOT])
        cp = pltpu.make_async_copy(acc_ref.at[pl.ds(0, n_tok * yw)], out_hbm, sem.at[1])
        cp.start()
        cp.wait()


def _moe(tok_sorted, seg_start, seg_count, h2p, w2d, wg, wu, wd):
    n = h2p.shape[0]
    xw = D_MODEL // 2 // LANES
    yw = D_MODEL // LANES
    grid_spec = pltpu.PrefetchScalarGridSpec(
        num_scalar_prefetch=3,
        grid=(N_EXPERTS,),
        in_specs=[pl.BlockSpec(memory_space=pl.ANY),
                  pl.BlockSpec(w2d.shape, lambda e, *_: (0, 0)),
                  pl.BlockSpec((1, D_MODEL, EXPERT_DIM), lambda e, *_: (e, 0, 0)),
                  pl.BlockSpec((1, D_MODEL, EXPERT_DIM), lambda e, *_: (e, 0, 0)),
                  pl.BlockSpec((1, EXPERT_DIM, D_MODEL), lambda e, *_: (e, 0, 0))],
        out_specs=pl.BlockSpec(memory_space=pl.ANY),
        scratch_shapes=[pltpu.VMEM(((n + MOE_ROW_GROUP) * xw, LANES), jnp.uint32),
                        pltpu.VMEM(((n + MOE_ROW_GROUP) * yw, LANES), F32),
                        pltpu.VMEM((2, MOE_ROWS * xw, LANES), jnp.uint32),
                        pltpu.VMEM((2, MOE_ROWS * yw, LANES), F32),
                        pltpu.SMEM((4,), jnp.int32),
                        pltpu.SemaphoreType.DMA((2,))],
    )
    out = pl.pallas_call(
        _moe_kernel,
        grid_spec=grid_spec,
        out_shape=jax.ShapeDtypeStruct((n * yw, LANES), F32),
        compiler_params=_cparams(("arbitrary",), vmem=60 * 1024 * 1024),
        name="moe",
    )(tok_sorted, seg_start, seg_count, h2p.reshape(n * xw, LANES), w2d, wg, wu, wd)
    return out.reshape(n, D_MODEL)


def _moe_routed(top_e, top_w, counts, h2p, wg, wu, wd):
    tok_sorted, w2d, seg_start, seg_count = _moe_dispatch_plan(top_e, top_w, counts, h2p.shape[0])
    return _moe(tok_sorted, seg_start, seg_count, h2p, w2d, wg, wu, wd)


def _moe_dispatch_plan(top_e, top_w, counts, n):
    flat_e = top_e[:, :TOP_K].reshape(-1)
    flat_t = jnp.arange(n * TOP_K, dtype=jnp.int32) // TOP_K
    flat_w = top_w[:, :TOP_K].reshape(-1)
    _, tok_sorted, w_sorted = lax.sort((flat_e, flat_t, flat_w), num_keys=1, is_stable=True)
    total = flat_e.shape[0]
    tok_sorted = jnp.concatenate([tok_sorted, jnp.full((MOE_ROWS,), n, jnp.int32)])
    table_rows = -(-(total // LANES + 2) // 8) * 8
    w2d = jnp.concatenate([w_sorted, jnp.zeros((table_rows * LANES - total,), F32)]).reshape(table_rows, LANES)
    cnt = counts.reshape(-1).astype(jnp.int32)
    return tok_sorted, w2d, jnp.cumsum(cnt) - cnt, cnt


def _final_kernel(x1_ref, h2p_ref, r_ref, mod_ref, g_ref, wg_ref, wu_ref, wd_ref, y_ref):
    half = D_MODEL // 2
    mod = mod_ref[0]
    lo, hi = _unpack_bf16_pairs(h2p_ref[...])
    hg = _dot(lo, wg_ref[:half, :]) + _dot(hi, wg_ref[half:, :])
    hu = _dot(lo, wu_ref[:half, :]) + _dot(hi, wu_ref[half:, :])
    shared = _dot((_silu(hg) * hu).astype(BF16), wd_ref[...])
    ffn = r_ref[...] + shared
    y_ref[...] = x1_ref[...] + mod[:, 5 * D_MODEL:6 * D_MODEL] * _rms(ffn, g_ref[...])


def _final(x1, h2p, routed, mods3, g, wg, wu, wd, row_of_tile):
    n = x1.shape[0]
    tm = TOK_TILE
    sd = wg.shape[1]
    row = lambda w: pl.BlockSpec((tm, w), lambda i: (i, 0))
    const = lambda r, c: pl.BlockSpec((r, c), lambda i: (0, 0))
    return pl.pallas_call(
        _final_kernel,
        grid=(n // tm,),
        in_specs=[row(D_MODEL), row(D_MODEL // 2), row(D_MODEL),
                  pl.BlockSpec((1, 1, 6 * D_MODEL), lambda i: (row_of_tile(i), 0, 0)),
                  const(1, D_MODEL), const(D_MODEL, sd), const(D_MODEL, sd), const(sd, D_MODEL)],
        out_specs=row(D_MODEL),
        out_shape=jax.ShapeDtypeStruct((n, D_MODEL), F32),
        compiler_params=_cparams(("parallel",)),
        name="final",
    )(x1, h2p, routed, mods3, g.reshape(1, D_MODEL), wg, wu, wd)


def _trunk(x3, mods3, row_of_tile, attend, s0, wts):
    b, t, _ = x3.shape
    n = b * t
    x = x3.reshape(n, D_MODEL)
    q, k, v, gdn, z, gate, ba = _premix(x, mods3, wts["g_pre_mix"], wts["w_cat"], row_of_tile)
    na_o = attend(q, k, v).reshape(n, NA_WIDTH)
    qkv = _gdn_conv(gdn.reshape(b, t, GDN_CONV_CH), wts["conv_w"])
    ba3 = ba.reshape(b, t, LANES)
    o_f, s_f = _gdn_chunks(qkv, ba3, wts["alog_row"], wts["dt_row"], s0, reverse=False)
    o_b, s_b = _gdn_chunks(qkv, ba3, wts["alog_row"], wts["dt_row"], s0, reverse=True)
    s_fin = jnp.stack([s_f, s_b], axis=1)
    x1, h2, h2p = _mix(x, na_o, o_f.reshape(n, GDN_V_WIDTH), o_b.reshape(n, GDN_V_WIDTH), z, gate, mods3,
                       wts["gdn_norm_w"],
                       wts["g_post_mix"], wts["g_pre_ffn"], wts["w_na_up"], wts["w_gdn_up"],
                       wts["w_out"], row_of_tile)
    top_e, top_w, counts = _router(h2, wts["w_router"], wts["router_bias"])
    routed = _moe_routed(top_e, top_w, counts, h2p, wts["w_exp_gate"], wts["w_exp_up"], wts["w_exp_down"])
    y = _final(x1, h2p, routed, mods3, wts["g_post_ffn"], wts["w_sh_gate"], wts["w_sh_up"],
               wts["w_sh_down"], row_of_tile)
    return y.reshape(b, t, D_MODEL), k, v, s_fin


def kernel(x_prompt, x_sample, cache_na_k, cache_na_v, state_gdn, c, c_ctx, w_ada, b_ada, g_pre_mix,
           g_post_mix, g_pre_ffn, g_post_ffn, w_in, conv_w, gdn_a_log, gdn_dt_bias, gdn_norm_w, na_rpb,
           w_na_up, w_gdn_up, w_out, w_router, router_bias, w_exp_gate, w_exp_up, w_exp_down, w_sh_gate,
           w_sh_up, w_sh_down):
    depth = w_ada.shape[0]
    bp, tp, _ = x_prompt.shape
    bs, ts, _ = x_sample.shape
    y_prompt, y_sample = x_prompt, x_sample
    zero_state = jnp.zeros((bp, 2, GDN_HEADS, GDN_DK, GDN_DV), F32)
    new_k, new_v, new_s = [], [], []
    for l in range(depth):
        cv = jnp.concatenate([c_ctx[None], c, jnp.zeros((8 - 1 - bs, D_MODEL), F32)], axis=0)
        mods3 = _ada(cv, w_ada[l], b_ada[l]).reshape(8, 1, 6 * D_MODEL)
        wl = w_in[l]
        w_cat = jnp.concatenate(
            [wl[:, :S_Z], wl[:, S_A:], wl[:, S_Z:S_A],
             jnp.zeros((D_MODEL, LANES - 4 * GDN_HEADS), F32)], axis=1).astype(BF16)
        pad = jnp.zeros((2 * GDN_HEADS,), F32)
        tail = jnp.zeros((LANES - 4 * GDN_HEADS,), F32)
        wts = dict(
            w_cat=w_cat, g_pre_mix=g_pre_mix[l], g_post_mix=g_post_mix[l], g_pre_ffn=g_pre_ffn[l],
            g_post_ffn=g_post_ffn[l], conv_w=conv_w[l], gdn_norm_w=gdn_norm_w[l],
            alog_row=jnp.concatenate([pad, gdn_a_log[l].reshape(-1), tail]).reshape(1, LANES),
            dt_row=jnp.concatenate([pad, gdn_dt_bias[l].reshape(-1), tail]).reshape(1, LANES),
            w_na_up=w_na_up[l].astype(BF16), w_gdn_up=w_gdn_up[l].astype(BF16),
            w_out=w_out[l].astype(BF16), w_router=w_router[l], router_bias=router_bias[l],
            w_exp_gate=w_exp_gate[l], w_exp_up=w_exp_up[l], w_exp_down=w_exp_down[l],
            w_sh_gate=w_sh_gate[l].astype(BF16), w_sh_up=w_sh_up[l].astype(BF16),
            w_sh_down=w_sh_down[l].astype(BF16))

        def ctx_attend(q, k, v):
            return _ctx_attn(q.reshape(bp, tp, NA_WIDTH), k.reshape(bp, tp, NA_WIDTH),
                             v.reshape(bp, tp, NA_WIDTH))

        y_prompt, k_ctx, v_ctx, s_ctx = _trunk(y_prompt, mods3, lambda i: 0, ctx_attend, zero_state, wts)
        new_k.append(k_ctx.reshape(bp, tp, NA_HEADS, NA_HEAD_DIM))
        new_v.append(v_ctx.reshape(bp, tp, NA_HEADS, NA_HEAD_DIM))
        new_s.append(s_ctx)

        pair_tab, row_mask = _na_bias_tables(na_rpb[l], ts // GRID_W)
        ck = cache_na_k[:, l].reshape(bs, -1, NA_WIDTH)
        cvv = cache_na_v[:, l].reshape(bs, -1, NA_WIDTH)

        def na_attend(q, k, v):
            return _na_attn(q.reshape(bs, ts, NA_WIDTH), k.reshape(bs, ts, NA_WIDTH),
                            v.reshape(bs, ts, NA_WIDTH), ck, cvv, pair_tab, row_mask)

        tiles_per_seq = ts // TOK_TILE
        y_sample, _, _, _ = _trunk(y_sample, mods3, lambda i: 1 + i // tiles_per_seq, na_attend,
                                   state_gdn[:, l], wts)
    return (y_prompt, y_sample, jnp.stack(new_k, axis=1), jnp.stack(new_v, axis=1),
            jnp.stack(new_s, axis=1))
```

```python
import functools

import numpy as np
import jax
import jax.numpy as jnp
from jax import lax
from jax.experimental import pallas as pl
from jax.experimental.pallas import tpu as pltpu

F32 = jnp.float32
BF16 = jnp.bfloat16
HI = lax.Precision.HIGHEST

D_MODEL = 1024
GRID_W = 64
NA_HEADS = 8
NA_HEAD_DIM = 64
NA_WIDTH = NA_HEADS * NA_HEAD_DIM
NA_KR = 8
NA_KC = 16
GDN_HEADS = 4
GDN_DK = 128
GDN_DV = 128
GDN_QK_WIDTH = GDN_HEADS * GDN_DK
GDN_V_WIDTH = GDN_HEADS * GDN_DV
GDN_CONV_CH = 2 * GDN_QK_WIDTH + GDN_V_WIDTH
CONV_K = 5
CHUNK = 64
N_EXPERTS = 256
TOP_K = 8
N_GROUPS = 8
TOPK_GROUPS = 4
EXPERT_DIM = 256
ROUTED_SCALE = 2.5
EPS = 1e-6
S_NA = 3 * NA_WIDTH
S_GDN = S_NA + GDN_CONV_CH
S_Z = S_GDN + GDN_V_WIDTH
S_B = S_Z + 2 * GDN_HEADS
S_A = S_B + 2 * GDN_HEADS

LANES = 128
TOK_TILE = 256
NA_QROWS = 8
NA_SPAN = 16
MOE_ROWS = 128
PACK_ROWS = D_MODEL // 2 // LANES
ACC_ROWS = D_MODEL // LANES
MOE_ROW_GROUP = 8
GDN_CHUNKS_PER_STEP = 4
GDN_CONV_BLOCK_ROWS = 4096
VMEM_LIMIT = 56 * 1024 * 1024
NEG_INF = float("-inf")


def _cparams(sem, vmem=VMEM_LIMIT):
    return pltpu.CompilerParams(dimension_semantics=sem, vmem_limit_bytes=vmem)


def _silu(x):
    return x * jax.nn.sigmoid(x)


def _rms(x, g):
    return x * lax.rsqrt(jnp.mean(x * x, axis=-1, keepdims=True) + EPS) * g


def _dot(a, b):
    return jnp.dot(a, b, preferred_element_type=F32)


def _dot_nt(a, b, precision=None):
    return lax.dot_general(a, b, (((1,), (1,)), ((), ())), precision=precision,
                           preferred_element_type=F32)


def _ada_kernel(c_ref, w_ref, b_ref, o_ref):
    o_ref[...] = jnp.dot(_silu(c_ref[...]), w_ref[...], precision=HI,
                         preferred_element_type=F32) + b_ref[...]


def _ada(cv, w_ada, b_ada):
    n = w_ada.shape[1]
    tn = 512
    return pl.pallas_call(
        _ada_kernel,
        grid=(n // tn,),
        in_specs=[pl.BlockSpec((8, D_MODEL), lambda j: (0, 0)),
                  pl.BlockSpec((D_MODEL, tn), lambda j: (0, j)),
                  pl.BlockSpec((1, tn), lambda j: (0, j))],
        out_specs=pl.BlockSpec((8, tn), lambda j: (0, j)),
        out_shape=jax.ShapeDtypeStruct((8, n), F32),
        compiler_params=_cparams(("parallel",)),
        name="ada",
    )(cv, w_ada, b_ada.reshape(1, n))


_PM_WIDTHS = (NA_WIDTH, NA_WIDTH, NA_WIDTH, GDN_CONV_CH, GDN_V_WIDTH, 2 * D_MODEL, LANES)


def _premix_kernel(x_ref, mod_ref, g_ref, w_ref, *o_refs):
    mod = mod_ref[0]
    h = _rms(x_ref[...], g_ref[...]) * (1.0 + mod[:, D_MODEL:2 * D_MODEL]) + mod[:, 0:D_MODEL]
    hb = h.astype(BF16)
    off = 0
    for o_ref, wd in zip(o_refs, _PM_WIDTHS):
        for c0 in range(0, wd, 512):
            c1 = min(c0 + 512, wd)
            o_ref[:, c0:c1] = _dot(hb, w_ref[:, off + c0:off + c1])
        off += wd


def _premix(x, mods3, g, w_cat, row_of_tile):
    n = x.shape[0]
    wtot = w_cat.shape[1]
    tm = TOK_TILE
    return pl.pallas_call(
        _premix_kernel,
        grid=(n // tm,),
        in_specs=[pl.BlockSpec((tm, D_MODEL), lambda i: (i, 0)),
                  pl.BlockSpec((1, 1, 6 * D_MODEL), lambda i: (row_of_tile(i), 0, 0)),
                  pl.BlockSpec((1, D_MODEL), lambda i: (0, 0)),
                  pl.BlockSpec((D_MODEL, wtot), lambda i: (0, 0))],
        out_specs=[pl.BlockSpec((tm, wd), lambda i: (i, 0)) for wd in _PM_WIDTHS],
        out_shape=[jax.ShapeDtypeStruct((n, wd), F32) for wd in _PM_WIDTHS],
        compiler_params=_cparams(("parallel",)),
        name="premix",
    )(x, mods3, g.reshape(1, D_MODEL), w_cat)


def _softmax_rows(s):
    m = jnp.max(s, axis=-1, keepdims=True)
    p = jnp.exp(s - m)
    return p / jnp.sum(p, axis=-1, keepdims=True)


def _ctx_attn_kernel(q_ref, k_ref, v_ref, o_ref):
    scale = NA_HEAD_DIM ** -0.5
    for hp in range(NA_HEADS // 2):
        outs = []
        for h in (2 * hp, 2 * hp + 1):
            sl = slice(h * NA_HEAD_DIM, (h + 1) * NA_HEAD_DIM)
            q = q_ref[0, :, sl].astype(BF16)
            k = k_ref[0, :, sl].astype(BF16)
            v = v_ref[0, :, sl].astype(BF16)
            p = _softmax_rows(_dot_nt(q, k) * scale)
            outs.append(_dot(p.astype(BF16), v))
        o_ref[0, :, hp * LANES:(hp + 1) * LANES] = jnp.concatenate(outs, axis=-1)


def _ctx_attn(q, k, v):
    b, t, w = q.shape
    spec = pl.BlockSpec((1, t, w), lambda i: (i, 0, 0))
    return pl.pallas_call(
        _ctx_attn_kernel,
        grid=(b,),
        in_specs=[spec, spec, spec],
        out_specs=spec,
        out_shape=jax.ShapeDtypeStruct((b, t, w), F32),
        compiler_params=_cparams(("parallel",)),
        name="ctx_attn",
    )(q, k, v)


def _na_span_base(j, rows):
    return np.clip(NA_QROWS * j - NA_KR // 2, 0, rows - NA_SPAN)


NA_DR_PAD = NA_QROWS


def _na_bias_tables(rpb, rows):
    col = np.arange(GRID_W)
    dcm = np.clip(col[None, :] - col[:, None], -(NA_KC - 1), NA_KC - 1) + (NA_KC - 1)
    onehot = (dcm[None] == np.arange(2 * NA_KC - 1)[:, None, None]).astype(np.float32)
    tab = jnp.einsum('hrd,dqk->hrqk', rpb.astype(F32), jnp.asarray(onehot), precision=HI)
    col_start = np.clip(col - NA_KC // 2, 0, GRID_W - NA_KC)
    col_in = (col[None, :] >= col_start[:, None]) & (col[None, :] < col_start[:, None] + NA_KC)
    tab = jnp.where(jnp.asarray(col_in)[None, None], tab, NEG_INF)
    n_dr = 2 * NA_KR - 1
    n_side = NA_DR_PAD + NA_SPAN - n_dr + 1
    blank_lo = jnp.full((NA_HEADS, NA_DR_PAD, GRID_W, GRID_W), NEG_INF, F32)
    blank_hi = jnp.full((NA_HEADS, n_side, GRID_W, GRID_W), NEG_INF, F32)
    padded = jnp.concatenate([blank_lo, tab, blank_hi], axis=1)
    pair_tab = jnp.concatenate([padded[:, :-1], padded[:, 1:]], axis=-1)
    nblk = rows // NA_QROWS
    mask = np.full((3, NA_QROWS, NA_SPAN), NEG_INF, np.float32)
    for p, j in enumerate((0, 1, nblk - 1)):
        base = _na_span_base(j, rows)
        for ri in range(NA_QROWS):
            r = NA_QROWS * j + ri
            rs = np.clip(r - NA_KR // 2, 0, rows - NA_KR)
            for ki in range(NA_SPAN):
                if rs <= base + ki < rs + NA_KR:
                    mask[p, ri, ki] = 0.0
    row_mask = jnp.asarray(np.repeat(mask, GRID_W, axis=2))
    return pair_tab, row_mask


def _na_attn_kernel(q_ref, k_ref, v_ref, ck_ref, cv_ref, tab_ref, mask_ref, o_ref, *, rows):
    j = pl.program_id(2)
    scale = NA_HEAD_DIM ** -0.5
    base = jnp.clip(NA_QROWS * j - NA_KR // 2, 0, rows - NA_SPAN)
    start = pl.multiple_of(base * GRID_W, GRID_W)
    span = NA_SPAN * GRID_W
    q = q_ref[0]
    kl = k_ref[0, pl.ds(start, span), :].astype(BF16)
    vl = v_ref[0, pl.ds(start, span), :].astype(BF16)
    ck = ck_ref[0].astype(BF16)
    cv = cv_ref[0].astype(BF16)
    first = lax.broadcasted_iota(jnp.int32, q.shape, 1) < NA_HEAD_DIM
    off = base - NA_QROWS * j + (NA_KR - 1) + NA_DR_PAD
    outs = []
    for hh in range(2):
        qm = jnp.where(first if hh == 0 else ~first, q, 0.0).astype(BF16)
        s_raw = _dot_nt(qm, kl) * scale
        blocks = []
        for ri in range(NA_QROWS):
            rws = slice(ri * GRID_W, (ri + 1) * GRID_W)
            pieces = [s_raw[rws, m * LANES:(m + 1) * LANES] + tab_ref[hh, off + 2 * m - ri]
                      for m in range(NA_SPAN // 2)]
            blocks.append(jnp.concatenate(pieces, axis=1) + mask_ref[0, ri:ri + 1, :])
        s_loc = jnp.concatenate(blocks, axis=0)
        s_ctx = _dot_nt(qm, ck) * scale
        m = jnp.maximum(jnp.max(s_loc, axis=-1, keepdims=True), jnp.max(s_ctx, axis=-1, keepdims=True))
        p_loc = jnp.exp(s_loc - m)
        p_ctx = jnp.exp(s_ctx - m)
        den = jnp.sum(p_loc, axis=-1, keepdims=True) + jnp.sum(p_ctx, axis=-1, keepdims=True)
        p_loc = (p_loc / den).astype(BF16)
        p_ctx = (p_ctx / den).astype(BF16)
        outs.append(_dot(p_loc, vl) + _dot(p_ctx, cv))
    o_ref[0] = jnp.where(first, outs[0], outs[1])


def _na_attn(q, k, v, ck, cv, pair_tab, row_mask):
    b, n, w = q.shape
    p = ck.shape[1]
    rows = n // GRID_W
    nblk = rows // NA_QROWS
    qb = NA_QROWS * GRID_W

    def pattern(j):
        return jnp.where(j == 0, 0, jnp.where(j == nblk - 1, 2, 1))

    full = pl.BlockSpec((1, n, LANES), lambda bi, hp, j: (bi, 0, hp))
    ctx = pl.BlockSpec((1, p, LANES), lambda bi, hp, j: (bi, 0, hp))
    blk = pl.BlockSpec((1, qb, LANES), lambda bi, hp, j: (bi, j, hp))
    return pl.pallas_call(
        functools.partial(_na_attn_kernel, rows=rows),
        grid=(b, w // LANES, nblk),
        in_specs=[blk, full, full, ctx, ctx,
                  pl.BlockSpec((2,) + pair_tab.shape[1:], lambda bi, hp, j: (hp, 0, 0, 0)),
                  pl.BlockSpec((1,) + row_mask.shape[1:], lambda bi, hp, j: (pattern(j), 0, 0))],
        out_specs=blk,
        out_shape=jax.ShapeDtypeStruct((b, n, w), F32),
        compiler_params=_cparams(("parallel", "parallel", "arbitrary")),
        name="na_attn",
    )(q, k, v, ck, cv, pair_tab, row_mask)


def _gdn_conv_kernel(x_ref, w_ref, o_ref, *, groups):
    c = pl.program_id(1)
    t = x_ref.shape[1]
    row = lax.broadcasted_iota(jnp.int32, (t, LANES), 0)
    n_qk = 2 * GDN_HEADS
    for i in range(groups):
        lanes = slice(i * LANES, (i + 1) * LANES)
        x = x_ref[0, :, lanes]
        y = jnp.zeros_like(x)
        for jj in range(CONV_K):
            o = jj - CONV_K // 2
            xs = x if o == 0 else pltpu.roll(x, (-o) % t, 0)
            xs = jnp.where((row + o >= 0) & (row + o < t), xs, 0.0)
            y = y + xs * w_ref[jj:jj + 1, lanes]
        y = _silu(y)
        nrm = lax.rsqrt(jnp.sum(y * y, axis=-1, keepdims=True) + EPS)
        o_ref[0, i] = jnp.where(c * groups + i < n_qk, y * nrm, y)


def _gdn_conv(x, conv_w):
    b, t, ch = x.shape
    nc = ch // LANES
    groups = max(1, min(nc, GDN_CONV_BLOCK_ROWS // t))
    return pl.pallas_call(
        functools.partial(_gdn_conv_kernel, groups=groups),
        grid=(b, nc // groups),
        in_specs=[pl.BlockSpec((1, t, groups * LANES), lambda bi, c: (bi, 0, c)),
                  pl.BlockSpec((CONV_K, groups * LANES), lambda bi, c: (0, c))],
        out_specs=pl.BlockSpec((1, groups, t, LANES), lambda bi, c: (bi, c, 0, 0)),
        out_shape=jax.ShapeDtypeStruct((b, nc, t, LANES), F32),
        compiler_params=_cparams(("parallel", "parallel")),
        name="gdn_conv",
    )(x, conv_w)


def _bdot(a, b):
    return jnp.dot(a.astype(BF16), b.astype(BF16), preferred_element_type=F32)


def _split_bf16(x):
    hi = x.astype(BF16)
    return hi, (x - hi.astype(F32)).astype(BF16)


def _dot3(a, b):
    m = a.shape[0]
    ah, al = _split_bf16(a)
    bh, bl = _split_bf16(b)
    top = _dot(jnp.concatenate([ah, al], axis=0), bh)
    return top[:m] + top[m:] + _dot(ah, bl)


def _gdn_chunk_kernel(qkv_ref, ba_ref, alog_ref, dt_ref, s0_ref, o_ref, sfin_ref, s_ref, *, reverse, cb):
    c = pl.program_id(1)
    nh = GDN_HEADS

    @pl.when(c == 0)
    def _():
        s_ref[...] = s0_ref[0, 0]

    ii = lax.broadcasted_iota(jnp.int32, (CHUNK, CHUNK), 0)
    jj = lax.broadcasted_iota(jnp.int32, (CHUNK, CHUNK), 1)
    lag = (jj - ii) if reverse else (ii - jj)
    incl = lag >= 0
    strict = lag > 0
    eye = (ii == jj).astype(F32)
    tri = incl.astype(F32)
    bcol = nh if reverse else 0
    gcol0 = (3 if reverse else 2) * nh
    sub8 = lax.broadcasted_iota(jnp.int32, (8, LANES), 0)
    lane8 = lax.broadcasted_iota(jnp.int32, (8, LANES), 1)
    sel8 = (lane8 == gcol0 + sub8).astype(F32)
    units = [(ci, h) for ci in range(cb) for h in range(nh)]
    gc_alls, beta_alls, grow8s = [], [], []
    for ci in range(cb):
        ba = ba_ref[0, ci * CHUNK:(ci + 1) * CHUNK, :]
        z = ba + dt_ref[...]
        softplus = jnp.maximum(z, 0.0) + jnp.log1p(jnp.exp(-jnp.abs(z)))
        g_all = -jnp.exp(alog_ref[...]) * softplus
        gc_all = jnp.dot(tri, g_all, precision=HI, preferred_element_type=F32)
        gc_alls.append(gc_all)
        beta_alls.append(jax.nn.sigmoid(ba))
        grow8s.append(_dot_nt(sel8, gc_all, precision=HI))
    gcol = [gc_alls[ci][:, gcol0 + h:gcol0 + h + 1] for ci, h in units]
    beta = [beta_alls[ci][:, bcol + h:bcol + h + 1] for ci, h in units]
    rows = [slice(ci * CHUNK, (ci + 1) * CHUNK) for ci, _ in units]
    k = [qkv_ref[0, nh + h, rows[u], :] for u, (_, h) in enumerate(units)]
    kb = [k[u] * beta[u] for u in range(len(units))]
    q = [qkv_ref[0, h, rows[u], :] * (GDN_DK ** -0.5) for u, (_, h) in enumerate(units)]
    kq = [_dot_nt(jnp.concatenate([kb[u], q[u]], axis=0).astype(BF16), k[u].astype(BF16))
          for u in range(len(units))]
    decay = [jnp.where(incl, jnp.exp(jnp.where(incl, gcol[u] - grow8s[ci][h:h + 1, :], 0.0)), 0.0)
             for u, (ci, h) in enumerate(units)]
    intra = [jnp.where(incl, kq[u][CHUNK:] * decay[u], 0.0) for u in range(len(units))]
    pw = [-jnp.where(strict, kq[u][:CHUNK] * decay[u], 0.0) for u in range(len(units))]
    tmat = [eye + p for p in pw]
    pw = [_dot3(p, p) for p in pw]
    for _ in range(4):
        pt = [_dot3(jnp.concatenate([pw[u], tmat[u]], axis=0), pw[u]) for u in range(len(units))]
        pw = [x[:CHUNK] for x in pt]
        tmat = [tmat[u] + pt[u][CHUNK:] for u in range(len(units))]
    tmat = [tmat[u] + _dot3(tmat[u], pw[u]) for u in range(len(units))]
    eg = [jnp.exp(g) for g in gcol]
    uw = [_bdot(tmat[u], jnp.concatenate(
        [qkv_ref[0, 2 * nh + h, rows[u], :] * beta[u], kb[u] * eg[u]], axis=1))
        for u, (_, h) in enumerate(units)]
    g_last = [g[0:1, :] if reverse else g[CHUNK - 1:CHUNK, :] for g in gcol]
    kd = [k[u] * jnp.exp(g_last[u] - gcol[u]) for u in range(len(units))]
    qe = [q[u] * eg[u] for u in range(len(units))]
    s = [s_ref[h] for h in range(nh)]
    for ci in (reversed(range(cb)) if reverse else range(cb)):
        us = [ci * nh + h for h in range(nh)]
        wq = [_bdot(jnp.concatenate([uw[u][:, GDN_DV:], qe[u]], axis=0), s[h]) for h, u in enumerate(us)]
        v_new = [uw[u][:, :GDN_DV] - wq[h][:CHUNK] for h, u in enumerate(us)]
        for h, u in enumerate(us):
            o_ref[0, rows[u], h * GDN_DV:(h + 1) * GDN_DV] = wq[h][CHUNK:] + _bdot(intra[u], v_new[h])
        s = [s[h] * jnp.exp(g_last[u]) + lax.dot_general(
            kd[u].astype(BF16), v_new[h].astype(BF16), (((0,), (0,)), ((), ())), preferred_element_type=F32)
            for h, u in enumerate(us)]
    for h in range(nh):
        s_ref[h] = s[h]

    @pl.when(c == pl.num_programs(1) - 1)
    def _():
        sfin_ref[0] = s_ref[...]


def _gdn_chunks(qkv, ba, alog_row, dt_row, s0, reverse):
    b, _, t, _ = qkv.shape
    cb = GDN_CHUNKS_PER_STEP
    rows = cb * CHUNK
    n = t // rows
    d = 1 if reverse else 0

    def blk(c):
        return n - 1 - c if reverse else c

    return pl.pallas_call(
        functools.partial(_gdn_chunk_kernel, reverse=reverse, cb=cb),
        grid=(b, n),
        in_specs=[pl.BlockSpec((1, 3 * GDN_HEADS, rows, LANES), lambda bi, c: (bi, 0, blk(c), 0)),
                  pl.BlockSpec((1, rows, LANES), lambda bi, c: (bi, blk(c), 0)),
                  pl.BlockSpec((1, LANES), lambda bi, c: (0, 0)),
                  pl.BlockSpec((1, LANES), lambda bi, c: (0, 0)),
                  pl.BlockSpec((1, 1, GDN_HEADS, GDN_DK, GDN_DV), lambda bi, c: (bi, d, 0, 0, 0))],
        out_specs=[pl.BlockSpec((1, rows, GDN_V_WIDTH), lambda bi, c: (bi, blk(c), 0)),
                   pl.BlockSpec((1, GDN_HEADS, GDN_DK, GDN_DV), lambda bi, c: (bi, 0, 0, 0))],
        out_shape=[jax.ShapeDtypeStruct((b, t, GDN_V_WIDTH), F32),
                   jax.ShapeDtypeStruct((b, GDN_HEADS, GDN_DK, GDN_DV), F32)],
        scratch_shapes=[pltpu.VMEM((GDN_HEADS, GDN_DK, GDN_DV), F32)],
        compiler_params=_cparams(("parallel", "arbitrary")),
        name="gdn_bwd" if reverse else "gdn_fwd",
    )(qkv, ba, alog_row, dt_row, s0)


def _pack_bf16_pairs(h):
    half = D_MODEL // 2
    lo = pltpu.bitcast(h[:, :half].astype(BF16).astype(F32), jnp.uint32)
    hi = pltpu.bitcast(h[:, half:].astype(BF16).astype(F32), jnp.uint32)
    return (hi & jnp.uint32(0xFFFF0000)) | (lo >> 16)


def _unpack_bf16_pairs(w):
    lo = pltpu.bitcast(w << 16, F32).astype(BF16)
    hi = pltpu.bitcast(w & jnp.uint32(0xFFFF0000), F32).astype(BF16)
    return lo, hi


def _mix_kernel(x_ref, na_ref, of_ref, ob_ref, z_ref, gate_ref, mod_ref, gnw_ref, gpost_ref, gpre_ref,
                wna_ref, wgdn_ref, wout_ref, x1_ref, h2_ref, h2p_ref):
    mod = mod_ref[0]
    o = of_ref[...] + ob_ref[...]
    parts = []
    for h in range(GDN_HEADS):
        sl = slice(h * GDN_DV, (h + 1) * GDN_DV)
        parts.append(_rms(o[:, sl], gnw_ref[...]) * _silu(z_ref[:, sl]))
    gdn_o = jnp.concatenate(parts, axis=-1)
    a = _dot(na_ref[...].astype(BF16), wna_ref[...])
    b = _dot(gdn_o.astype(BF16), wgdn_ref[...])
    gate = jax.nn.sigmoid(gate_ref[...])
    pre = gate[:, :D_MODEL] * a + gate[:, D_MODEL:] * b
    mix = _dot(pre.astype(BF16), wout_ref[...])
    x1 = x_ref[...] + mod[:, 2 * D_MODEL:3 * D_MODEL] * _rms(mix, gpost_ref[...])
    x1_ref[...] = x1
    h2 = _rms(x1, gpre_ref[...]) * (1.0 + mod[:, 4 * D_MODEL:5 * D_MODEL]) + mod[:, 3 * D_MODEL:4 * D_MODEL]
    h2_ref[...] = h2
    packed = _pack_bf16_pairs(h2)
    for c in range(PACK_ROWS):
        h2p_ref[pl.ds(c, h2.shape[0], stride=PACK_ROWS), :] = packed[:, c * LANES:(c + 1) * LANES]


def _mix(x, na_o, o_f, o_b, z, gate, mods3, gnw, gpost, gpre, wna, wgdn, wout, row_of_tile):
    n = x.shape[0]
    tm = TOK_TILE
    row = lambda w: pl.BlockSpec((tm, w), lambda i: (i, 0))
    const = lambda r, c: pl.BlockSpec((r, c), lambda i: (0, 0))
    return pl.pallas_call(
        _mix_kernel,
        grid=(n // tm,),
        in_specs=[row(D_MODEL), row(NA_WIDTH), row(GDN_V_WIDTH), row(GDN_V_WIDTH),
                  row(GDN_V_WIDTH), row(2 * D_MODEL),
                  pl.BlockSpec((1, 1, 6 * D_MODEL), lambda i: (row_of_tile(i), 0, 0)),
                  const(1, GDN_DV), const(1, D_MODEL), const(1, D_MODEL),
                  const(NA_WIDTH, D_MODEL), const(GDN_V_WIDTH, D_MODEL), const(D_MODEL, D_MODEL)],
        out_specs=[row(D_MODEL), row(D_MODEL), pl.BlockSpec((tm * PACK_ROWS, LANES), lambda i: (i, 0))],
        out_shape=[jax.ShapeDtypeStruct((n, D_MODEL), F32),
                   jax.ShapeDtypeStruct((n, D_MODEL), F32),
                   jax.ShapeDtypeStruct((n * PACK_ROWS, LANES), jnp.uint32)],
        compiler_params=_cparams(("parallel",)),
        name="mix",
    )(x, na_o, o_f, o_b, z, gate, mods3, gnw.reshape(1, GDN_DV), gpost.reshape(1, D_MODEL),
      gpre.reshape(1, D_MODEL), wna, wgdn, wout)


def _router_kernel(h_ref, wh_ref, wl_ref, b_ref, e_ref, wt_ref, cnt_ref, acc_ref):
    i = pl.program_id(0)
    tm = h_ref.shape[0]
    ne = N_EXPERTS
    per = ne // N_GROUPS
    hh, hl = _split_bf16(h_ref[...])
    wh = wh_ref[...]
    logits = _dot_nt(wh, hh) + _dot_nt(wl_ref[...], hh) + _dot_nt(wh, hl)
    scores = jax.nn.sigmoid(logits)
    biased = scores + jnp.concatenate([b_ref[...]] * (tm // LANES), axis=1)

    def first_max(x):
        n = x.shape[0]
        iota = lax.broadcasted_iota(jnp.int32, x.shape, 0).astype(F32)
        m = jnp.max(x, axis=0, keepdims=True)
        idx = jnp.min(jnp.where(x == m, iota, float(n)), axis=0, keepdims=True)
        return m, idx, iota

    gs_rows = []
    for g in range(N_GROUPS):
        bg = biased[g * per:(g + 1) * per]
        m1, i1, iota = first_max(bg)
        m2 = jnp.max(jnp.where(iota == i1, NEG_INF, bg), axis=0, keepdims=True)
        gs_rows.append(m1 + m2)
    gs = jnp.concatenate(gs_rows, axis=0)
    gsel = jnp.zeros(gs.shape, F32)
    for _ in range(TOPK_GROUPS):
        _, gi, iota = first_max(gs)
        hit = iota == gi
        gs = jnp.where(hit, NEG_INF, gs)
        gsel = jnp.where(hit, 1.0, gsel)
    masked = jnp.concatenate(
        [jnp.where(gsel[g:g + 1] > 0.0, biased[g * per:(g + 1) * per], NEG_INF) for g in range(N_GROUPS)], axis=0)
    onehot = jnp.zeros((ne, tm), F32)
    e_rows, w_rows = [], []
    for _ in range(TOP_K):
        _, ei, iota = first_max(masked)
        hit = iota == ei
        masked = jnp.where(hit, NEG_INF, masked)
        w_rows.append(jnp.sum(jnp.where(hit, scores, 0.0), axis=0, keepdims=True))
        e_rows.append(ei)
        onehot = onehot + hit.astype(F32)
    w_out = jnp.concatenate(w_rows, axis=0)
    e_ref[...] = jnp.concatenate(e_rows, axis=0).astype(jnp.int32)
    wt_ref[...] = w_out / jnp.sum(w_out, axis=0, keepdims=True) * ROUTED_SCALE

    @pl.when(i == 0)
    def _():
        acc_ref[...] = jnp.zeros_like(acc_ref)

    acc_ref[...] += sum(onehot[:, c * LANES:(c + 1) * LANES] for c in range(tm // LANES))

    @pl.when(i == pl.num_programs(0) - 1)
    def _():
        cnt_ref[...] = jnp.broadcast_to(jnp.sum(acc_ref[...], axis=1, keepdims=True), cnt_ref.shape)


def _router(h2, w_rt_hi, w_rt_lo, bias_col):
    n = h2.shape[0]
    tm = TOK_TILE
    return pl.pallas_call(
        _router_kernel,
        grid=(n // tm,),
        in_specs=[pl.BlockSpec((tm, D_MODEL), lambda i: (i, 0)),
                  pl.BlockSpec((N_EXPERTS, D_MODEL), lambda i: (0, 0)),
                  pl.BlockSpec((N_EXPERTS, D_MODEL), lambda i: (0, 0)),
                  pl.BlockSpec((N_EXPERTS, LANES), lambda i: (0, 0))],
        out_specs=[pl.BlockSpec((TOP_K, tm), lambda i: (0, i)),
                   pl.BlockSpec((TOP_K, tm), lambda i: (0, i)),
                   pl.BlockSpec((N_EXPERTS, LANES), lambda i: (0, 0))],
        out_shape=[jax.ShapeDtypeStruct((TOP_K, n), jnp.int32),
                   jax.ShapeDtypeStruct((TOP_K, n), F32),
                   jax.ShapeDtypeStruct((N_EXPERTS, LANES), F32)],
        scratch_shapes=[pltpu.VMEM((N_EXPERTS, LANES), F32)],
        compiler_params=_cparams(("arbitrary",)),
        name="router",
    )(h2, w_rt_hi, w_rt_lo, bias_col)


_ST_SLOT, _ST_READY, _ST_PEND_BASE, _ST_PEND_VALID = range(4)


def _moe_kernel(tok_ref, seg_ref, cnt_ref, h2p_hbm, w2d_ref, wg_ref, wu_ref, wd_ref, out_hbm,
                h2p_ref, acc_ref, xs_ref, ye_ref, st_ref, sem):
    e = pl.program_id(0)
    n_exp = pl.num_programs(0)
    half = D_MODEL // 2
    xw, yw = PACK_ROWS, ACC_ROWS
    n_tok = out_hbm.shape[0] // yw
    grp = MOE_ROW_GROUP

    def gather_tile(base, n_valid, slot):
        for r in range(MOE_ROWS):
            tok = jnp.where(r < n_valid, tok_ref[base + r], n_tok)
            xs_ref[slot, xw * r:xw * (r + 1), :] = h2p_ref[pl.ds(pl.multiple_of(tok * xw, xw), xw), :]

    def scatter_tile(base, n_valid, slot):
        for g in range(MOE_ROWS // grp):
            toks = [pl.multiple_of(
                jnp.where(g * grp + j < n_valid, tok_ref[base + g * grp + j], n_tok) * yw, yw)
                for j in range(grp)]
            rows = [acc_ref[pl.ds(toks[j], yw), :] + ye_ref[slot, yw * (g * grp + j):yw * (g * grp + j + 1), :]
                    for j in range(grp)]
            for j in range(grp):
                acc_ref[pl.ds(toks[j], yw), :] = rows[j]

    seg = seg_ref[e]
    cnt = cnt_ref[e]
    n_tiles = (cnt + MOE_ROWS - 1) // MOE_ROWS

    @pl.when(e == 0)
    def _():
        cp = pltpu.make_async_copy(h2p_hbm, h2p_ref.at[pl.ds(0, n_tok * xw)], sem.at[0])
        cp.start()
        acc_ref[...] = jnp.zeros_like(acc_ref)
        ye_ref[...] = jnp.zeros_like(ye_ref)
        h2p_ref[pl.ds(n_tok * xw, grp * xw), :] = jnp.zeros((grp * xw, LANES), jnp.uint32)
        st_ref[_ST_SLOT] = 0
        st_ref[_ST_READY] = -1
        st_ref[_ST_PEND_BASE] = 0
        st_ref[_ST_PEND_VALID] = 0
        cp.wait()

    @pl.when((n_tiles > 0) & (st_ref[_ST_READY] != e))
    def _():
        gather_tile(seg, jnp.minimum(cnt, MOE_ROWS), st_ref[_ST_SLOT])

    def regroup(w):
        w = w.astype(BF16)
        return [jnp.concatenate([w[LANES * c:LANES * (c + 1)], w[half + LANES * c:half + LANES * (c + 1)]],
                                axis=0) for c in range(xw)]

    wg = regroup(wg_ref[0])
    wu = regroup(wu_ref[0])
    wd = wd_ref[0].astype(BF16)
    lane = lax.broadcasted_iota(jnp.int32, (8, LANES), 1)
    nxt_e = jnp.minimum(e + 1, n_exp - 1)
    nxt_seg = seg_ref[nxt_e]
    nxt_cnt = jnp.where(e + 1 < n_exp, cnt_ref[nxt_e], 0)

    def tile_body(t, carry):
        slot = st_ref[_ST_SLOT]
        base = seg + t * MOE_ROWS
        last = t + 1 == n_tiles
        xk = [jnp.concatenate(_unpack_bf16_pairs(xs_ref[slot, pl.ds(c, MOE_ROWS, stride=xw), :]), axis=1)
              for c in range(xw)]
        g_base = jnp.where(last, nxt_seg, base + MOE_ROWS)
        g_valid = jnp.minimum(MOE_ROWS, jnp.where(last, nxt_cnt, cnt - (t + 1) * MOE_ROWS))
        gather_tile(g_base, g_valid, 1 - slot)
        scatter_tile(st_ref[_ST_PEND_BASE], st_ref[_ST_PEND_VALID], 1 - slot)
        hg = sum(_dot(xk[c], wg[c]) for c in range(xw))
        hu = sum(_dot(xk[c], wu[c]) for c in range(xw))
        act = (_silu(hg) * hu).astype(BF16)
        q = base // LANES
        sh = base % LANES
        rot = (LANES - sh) % LANES
        row_a = pltpu.roll(jnp.broadcast_to(w2d_ref[pl.ds(q, 1), :], (8, LANES)), rot, 1)
        row_b = pltpu.roll(jnp.broadcast_to(w2d_ref[pl.ds(q + 1, 1), :], (8, LANES)), rot, 1)
        w_row = jnp.where(lane + sh < LANES, row_a, row_b)[0:1, :]
        w_col = jnp.broadcast_to(w_row, (MOE_ROWS, LANES)).T
        ye = _dot(act, wd)
        for c in range(yw):
            ye_ref[slot, pl.ds(c, MOE_ROWS, stride=yw), :] = ye[:, LANES * c:LANES * (c + 1)] * w_col
        st_ref[_ST_PEND_BASE] = base
        st_ref[_ST_PEND_VALID] = jnp.minimum(MOE_ROWS, cnt - t * MOE_ROWS)
        st_ref[_ST_SLOT] = 1 - slot
        st_ref[_ST_READY] = jnp.where(last, e + 1, e)
        return carry

    lax.fori_loop(0, n_tiles, tile_body, 0)

    @pl.when(e == n_exp - 1)
    def _():
        scatter_tile(st_ref[_ST_PEND_BASE], st_ref[_ST_PEND_VALID], 1 - st_ref[_ST_SLOT])
        cp = pltpu.make_async_copy(acc_ref.at[pl.ds(0, n_tok * yw)], out_hbm, sem.at[1])
        cp.start()
        cp.wait()


def _moe(tok_sorted, seg_start, seg_count, h2p, w2d, wg, wu, wd):
    xw, yw = PACK_ROWS, ACC_ROWS
    n = h2p.shape[0] // xw
    grid_spec = pltpu.PrefetchScalarGridSpec(
        num_scalar_prefetch=3,
        grid=(N_EXPERTS,),
        in_specs=[pl.BlockSpec(memory_space=pl.ANY),
                  pl.BlockSpec(w2d.shape, lambda e, *_: (0, 0)),
                  pl.BlockSpec((1, D_MODEL, EXPERT_DIM), lambda e, *_: (e, 0, 0)),
                  pl.BlockSpec((1, D_MODEL, EXPERT_DIM), lambda e, *_: (e, 0, 0)),
                  pl.BlockSpec((1, EXPERT_DIM, D_MODEL), lambda e, *_: (e, 0, 0))],
        out_specs=pl.BlockSpec(memory_space=pl.ANY),
        scratch_shapes=[pltpu.VMEM(((n + MOE_ROW_GROUP) * xw, LANES), jnp.uint32),
                        pltpu.VMEM(((n + MOE_ROW_GROUP) * yw, LANES), F32),
                        pltpu.VMEM((2, MOE_ROWS * xw, LANES), jnp.uint32),
                        pltpu.VMEM((2, MOE_ROWS * yw, LANES), F32),
                        pltpu.SMEM((4,), jnp.int32),
                        pltpu.SemaphoreType.DMA((2,))],
    )
    return pl.pallas_call(
        _moe_kernel,
        grid_spec=grid_spec,
        out_shape=jax.ShapeDtypeStruct((n * yw, LANES), F32),
        compiler_params=_cparams(("arbitrary",), vmem=60 * 1024 * 1024),
        name="moe",
    )(tok_sorted, seg_start, seg_count, h2p, w2d, wg, wu, wd)


def _moe_routed(top_e, top_w, counts, h2p, wg, wu, wd):
    tok_sorted, w2d, seg_start, seg_count = _moe_dispatch_plan(top_e, top_w, counts, h2p.shape[0] // PACK_ROWS)
    return _moe(tok_sorted, seg_start, seg_count, h2p, w2d, wg, wu, wd)


def _moe_dispatch_plan(top_e, top_w, counts, n):
    flat_e = top_e.reshape(-1)
    flat_t = jnp.arange(n * TOP_K, dtype=jnp.int32) % n
    flat_w = top_w.reshape(-1)
    _, tok_sorted, w_sorted = lax.sort((flat_e, flat_t, flat_w), num_keys=1, is_stable=True)
    total = flat_e.shape[0]
    tok_sorted = jnp.concatenate([tok_sorted, jnp.full((MOE_ROWS,), n, jnp.int32)])
    table_rows = -(-(total // LANES + 2) // 8) * 8
    w2d = jnp.concatenate([w_sorted, jnp.zeros((table_rows * LANES - total,), F32)]).reshape(table_rows, LANES)
    cnt = counts[:, 0].astype(jnp.int32)
    return tok_sorted, w2d, jnp.cumsum(cnt) - cnt, cnt


def _regroup_rows(w):
    half = D_MODEL // 2
    return jnp.concatenate([w[r0 + LANES * c:r0 + LANES * (c + 1)]
                            for c in range(PACK_ROWS) for r0 in (0, half)], axis=0)


def _final_kernel(x1_ref, h2p_ref, r_ref, mod_ref, g_ref, wg_ref, wu_ref, wd_ref, y_ref):
    tm = x1_ref.shape[0]
    kc = 2 * LANES
    mod = mod_ref[0]
    xk = [jnp.concatenate(_unpack_bf16_pairs(h2p_ref[pl.ds(c, tm, stride=PACK_ROWS), :]), axis=1)
          for c in range(PACK_ROWS)]
    hg = sum(_dot(xk[c], wg_ref[kc * c:kc * (c + 1), :]) for c in range(PACK_ROWS))
    hu = sum(_dot(xk[c], wu_ref[kc * c:kc * (c + 1), :]) for c in range(PACK_ROWS))
    shared = _dot((_silu(hg) * hu).astype(BF16), wd_ref[...])
    routed = jnp.concatenate([r_ref[pl.ds(c, tm, stride=ACC_ROWS), :] for c in range(ACC_ROWS)], axis=1)
    ffn = routed + shared
    y_ref[...] = x1_ref[...] + mod[:, 5 * D_MODEL:6 * D_MODEL] * _rms(ffn, g_ref[...])


def _final(x1, h2p, routed, mods3, g, wg, wu, wd, row_of_tile):
    n = x1.shape[0]
    tm = TOK_TILE
    sd = wg.shape[1]
    row = lambda w: pl.BlockSpec((tm, w), lambda i: (i, 0))
    const = lambda r, c: pl.BlockSpec((r, c), lambda i: (0, 0))
    return pl.pallas_call(
        _final_kernel,
        grid=(n // tm,),
        in_specs=[row(D_MODEL), pl.BlockSpec((tm * PACK_ROWS, LANES), lambda i: (i, 0)),
                  pl.BlockSpec((tm * ACC_ROWS, LANES), lambda i: (i, 0)),
                  pl.BlockSpec((1, 1, 6 * D_MODEL), lambda i: (row_of_tile(i), 0, 0)),
                  const(1, D_MODEL), const(D_MODEL, sd), const(D_MODEL, sd), const(sd, D_MODEL)],
        out_specs=row(D_MODEL),
        out_shape=jax.ShapeDtypeStruct((n, D_MODEL), F32),
        compiler_params=_cparams(("parallel",)),
        name="final",
    )(x1, h2p, routed, mods3, g.reshape(1, D_MODEL), wg, wu, wd)


def _trunk(x3, mods3, row_of_tile, attend, s0, wts):
    b, t, _ = x3.shape
    n = b * t
    x = x3.reshape(n, D_MODEL)
    q, k, v, gdn, z, gate, ba = _premix(x, mods3, wts["g_pre_mix"], wts["w_cat"], row_of_tile)
    na_o = attend(q, k, v).reshape(n, NA_WIDTH)
    qkv = _gdn_conv(gdn.reshape(b, t, GDN_CONV_CH), wts["conv_w"])
    ba3 = ba.reshape(b, t, LANES)
    o_f, s_f = _gdn_chunks(qkv, ba3, wts["alog_row"], wts["dt_row"], s0, reverse=False)
    o_b, s_b = _gdn_chunks(qkv, ba3, wts["alog_row"], wts["dt_row"], s0, reverse=True)
    s_fin = jnp.stack([s_f, s_b], axis=1)
    x1, h2, h2p = _mix(x, na_o, o_f.reshape(n, GDN_V_WIDTH), o_b.reshape(n, GDN_V_WIDTH), z, gate, mods3,
                       wts["gdn_norm_w"],
                       wts["g_post_mix"], wts["g_pre_ffn"], wts["w_na_up"], wts["w_gdn_up"],
                       wts["w_out"], row_of_tile)
    top_e, top_w, counts = _router(h2, wts["w_rt_hi"], wts["w_rt_lo"], wts["router_bias_col"])
    routed = _moe_routed(top_e, top_w, counts, h2p, wts["w_exp_gate"], wts["w_exp_up"], wts["w_exp_down"])
    y = _final(x1, h2p, routed, mods3, wts["g_post_ffn"], wts["w_sh_gate"], wts["w_sh_up"],
               wts["w_sh_down"], row_of_tile)
    return y.reshape(b, t, D_MODEL), k, v, s_fin


def kernel(x_prompt, x_sample, cache_na_k, cache_na_v, state_gdn, c, c_ctx, w_ada, b_ada, g_pre_mix,
           g_post_mix, g_pre_ffn, g_post_ffn, w_in, conv_w, gdn_a_log, gdn_dt_bias, gdn_norm_w, na_rpb,
           w_na_up, w_gdn_up, w_out, w_router, router_bias, w_exp_gate, w_exp_up, w_exp_down, w_sh_gate,
           w_sh_up, w_sh_down):
    depth = w_ada.shape[0]
    bp, tp, _ = x_prompt.shape
    bs, ts, _ = x_sample.shape
    y_prompt, y_sample = x_prompt, x_sample
    zero_state = jnp.zeros((bp, 2, GDN_HEADS, GDN_DK, GDN_DV), F32)
    new_k, new_v, new_s = [], [], []
    for l in range(depth):
        cv = jnp.concatenate([c_ctx[None], c, jnp.zeros((8 - 1 - bs, D_MODEL), F32)], axis=0)
        mods3 = _ada(cv, w_ada[l], b_ada[l]).reshape(8, 1, 6 * D_MODEL)
        wl = w_in[l]
        w_cat = jnp.concatenate(
            [wl[:, :S_Z], wl[:, S_A:], wl[:, S_Z:S_A],
             jnp.zeros((D_MODEL, LANES - 4 * GDN_HEADS), F32)], axis=1).astype(BF16)
        pad = jnp.zeros((2 * GDN_HEADS,), F32)
        tail = jnp.zeros((LANES - 4 * GDN_HEADS,), F32)
        w_rt = w_router[l].astype(F32).T
        w_rt_hi = w_rt.astype(BF16)
        wts = dict(
            w_cat=w_cat, g_pre_mix=g_pre_mix[l], g_post_mix=g_post_mix[l], g_pre_ffn=g_pre_ffn[l],
            g_post_ffn=g_post_ffn[l], conv_w=conv_w[l], gdn_norm_w=gdn_norm_w[l],
            alog_row=jnp.concatenate([pad, gdn_a_log[l].reshape(-1), tail]).reshape(1, LANES),
            dt_row=jnp.concatenate([pad, gdn_dt_bias[l].reshape(-1), tail]).reshape(1, LANES),
            w_na_up=w_na_up[l].astype(BF16), w_gdn_up=w_gdn_up[l].astype(BF16),
            w_out=w_out[l].astype(BF16), w_rt_hi=w_rt_hi, w_rt_lo=(w_rt - w_rt_hi.astype(F32)).astype(BF16),
            router_bias_col=jnp.broadcast_to(router_bias[l].astype(F32)[:, None], (N_EXPERTS, LANES)),
            w_exp_gate=w_exp_gate[l], w_exp_up=w_exp_up[l], w_exp_down=w_exp_down[l],
            w_sh_gate=_regroup_rows(w_sh_gate[l]).astype(BF16), w_sh_up=_regroup_rows(w_sh_up[l]).astype(BF16),
            w_sh_down=w_sh_down[l].astype(BF16))

        def ctx_attend(q, k, v):
            return _ctx_attn(q.reshape(bp, tp, NA_WIDTH), k.reshape(bp, tp, NA_WIDTH),
                             v.reshape(bp, tp, NA_WIDTH))

        y_prompt, k_ctx, v_ctx, s_ctx = _trunk(y_prompt, mods3, lambda i: 0, ctx_attend, zero_state, wts)
        new_k.append(k_ctx.reshape(bp, tp, NA_HEADS, NA_HEAD_DIM))
        new_v.append(v_ctx.reshape(bp, tp, NA_HEADS, NA_HEAD_DIM))
        new_s.append(s_ctx)

        pair_tab, row_mask = _na_bias_tables(na_rpb[l], ts // GRID_W)
        ck = cache_na_k[:, l].reshape(bs, -1, NA_WIDTH)
        cvv = cache_na_v[:, l].reshape(bs, -1, NA_WIDTH)

        def na_attend(q, k, v):
            return _na_attn(q.reshape(bs, ts, NA_WIDTH), k.reshape(bs, ts, NA_WIDTH),
                            v.reshape(bs, ts, NA_WIDTH), ck, cvv, pair_tab, row_mask)

        tiles_per_seq = ts // TOK_TILE
        y_sample, _, _, _ = _trunk(y_sample, mods3, lambda i: 1 + i // tiles_per_seq, na_attend,
                                   state_gdn[:, l], wts)
    return (y_prompt, y_sample, jnp.stack(new_k, axis=1), jnp.stack(new_v, axis=1),
            jnp.stack(new_s, axis=1))
```

```python
import functools

import numpy as np
import jax
import jax.numpy as jnp
from jax import lax
from jax.experimental import pallas as pl
from jax.experimental.pallas import tpu as pltpu

F32 = jnp.float32
BF16 = jnp.bfloat16
HI = lax.Precision.HIGHEST

D_MODEL = 1024
GRID_W = 64
NA_HEADS = 8
NA_HEAD_DIM = 64
NA_WIDTH = NA_HEADS * NA_HEAD_DIM
NA_KR = 8
NA_KC = 16
GDN_HEADS = 4
GDN_DK = 128
GDN_DV = 128
GDN_QK_WIDTH = GDN_HEADS * GDN_DK
GDN_V_WIDTH = GDN_HEADS * GDN_DV
GDN_CONV_CH = 2 * GDN_QK_WIDTH + GDN_V_WIDTH
CONV_K = 5
CHUNK = 64
N_EXPERTS = 256
TOP_K = 8
N_GROUPS = 8
TOPK_GROUPS = 4
EXPERT_DIM = 256
ROUTED_SCALE = 2.5
EPS = 1e-6
S_NA = 3 * NA_WIDTH
S_GDN = S_NA + GDN_CONV_CH
S_Z = S_GDN + GDN_V_WIDTH
S_B = S_Z + 2 * GDN_HEADS
S_A = S_B + 2 * GDN_HEADS

LANES = 128
TOK_TILE = 256
NA_QROWS = 8
NA_SPAN = 16
MOE_ROWS = 128
PACK_ROWS = D_MODEL // 2 // LANES
ACC_ROWS = D_MODEL // LANES
MOE_ROW_GROUP = 8
GDN_CHUNKS_PER_STEP = 4
GDN_CONV_BLOCK_ROWS = 4096
VMEM_LIMIT = 56 * 1024 * 1024
NEG_INF = float("-inf")


def _cparams(sem, vmem=VMEM_LIMIT):
    return pltpu.CompilerParams(dimension_semantics=sem, vmem_limit_bytes=vmem)


def _silu(x):
    return x * jax.nn.sigmoid(x)


def _rms(x, g):
    return x * lax.rsqrt(jnp.mean(x * x, axis=-1, keepdims=True) + EPS) * g


def _dot(a, b):
    return jnp.dot(a, b, preferred_element_type=F32)


def _dot_nt(a, b, precision=None):
    return lax.dot_general(a, b, (((1,), (1,)), ((), ())), precision=precision,
                           preferred_element_type=F32)


def _ada_kernel(c_ref, w_ref, b_ref, o_ref):
    o_ref[...] = jnp.dot(_silu(c_ref[...]), w_ref[...], precision=HI,
                         preferred_element_type=F32) + b_ref[...]


def _ada(cv, w_ada, b_ada):
    n = w_ada.shape[1]
    tn = 512
    return pl.pallas_call(
        _ada_kernel,
        grid=(n // tn,),
        in_specs=[pl.BlockSpec((8, D_MODEL), lambda j: (0, 0)),
                  pl.BlockSpec((D_MODEL, tn), lambda j: (0, j)),
                  pl.BlockSpec((1, tn), lambda j: (0, j))],
        out_specs=pl.BlockSpec((8, tn), lambda j: (0, j)),
        out_shape=jax.ShapeDtypeStruct((8, n), F32),
        compiler_params=_cparams(("parallel",)),
        name="ada",
    )(cv, w_ada, b_ada.reshape(1, n))


_PM_WIDTHS = (NA_WIDTH, NA_WIDTH, NA_WIDTH, GDN_CONV_CH, GDN_V_WIDTH, 2 * D_MODEL, LANES)


def _premix_kernel(x_ref, mod_ref, g_ref, w_ref, *o_refs):
    mod = mod_ref[0]
    h = _rms(x_ref[...], g_ref[...]) * (1.0 + mod[:, D_MODEL:2 * D_MODEL]) + mod[:, 0:D_MODEL]
    hb = h.astype(BF16)
    off = 0
    for o_ref, wd in zip(o_refs, _PM_WIDTHS):
        for c0 in range(0, wd, 512):
            c1 = min(c0 + 512, wd)
            o_ref[:, c0:c1] = _dot(hb, w_ref[:, off + c0:off + c1])
        off += wd


def _premix(x, mods3, g, w_cat, row_of_tile):
    n = x.shape[0]
    wtot = w_cat.shape[1]
    tm = TOK_TILE
    return pl.pallas_call(
        _premix_kernel,
        grid=(n // tm,),
        in_specs=[pl.BlockSpec((tm, D_MODEL), lambda i: (i, 0)),
                  pl.BlockSpec((1, 1, 6 * D_MODEL), lambda i: (row_of_tile(i), 0, 0)),
                  pl.BlockSpec((1, D_MODEL), lambda i: (0, 0)),
                  pl.BlockSpec((D_MODEL, wtot), lambda i: (0, 0))],
        out_specs=[pl.BlockSpec((tm, wd), lambda i: (i, 0)) for wd in _PM_WIDTHS],
        out_shape=[jax.ShapeDtypeStruct((n, wd), F32) for wd in _PM_WIDTHS],
        compiler_params=_cparams(("parallel",)),
        name="premix",
    )(x, mods3, g.reshape(1, D_MODEL), w_cat)


def _softmax_rows(s):
    m = jnp.max(s, axis=-1, keepdims=True)
    p = jnp.exp(s - m)
    return p / jnp.sum(p, axis=-1, keepdims=True)


def _ctx_attn_kernel(q_ref, k_ref, v_ref, o_ref):
    scale = NA_HEAD_DIM ** -0.5
    for hp in range(NA_HEADS // 2):
        outs = []
        for h in (2 * hp, 2 * hp + 1):
            sl = slice(h * NA_HEAD_DIM, (h + 1) * NA_HEAD_DIM)
            q = q_ref[0, :, sl].astype(BF16)
            k = k_ref[0, :, sl].astype(BF16)
            v = v_ref[0, :, sl].astype(BF16)
            p = _softmax_rows(_dot_nt(q, k) * scale)
            outs.append(_dot(p.astype(BF16), v))
        o_ref[0, :, hp * LANES:(hp + 1) * LANES] = jnp.concatenate(outs, axis=-1)


def _ctx_attn(q, k, v):
    b, t, w = q.shape
    spec = pl.BlockSpec((1, t, w), lambda i: (i, 0, 0))
    return pl.pallas_call(
        _ctx_attn_kernel,
        grid=(b,),
        in_specs=[spec, spec, spec],
        out_specs=spec,
        out_shape=jax.ShapeDtypeStruct((b, t, w), F32),
        compiler_params=_cparams(("parallel",)),
        name="ctx_attn",
    )(q, k, v)


def _na_span_base(j, rows):
    return np.clip(NA_QROWS * j - NA_KR // 2, 0, rows - NA_SPAN)


NA_DR_PAD = NA_QROWS


def _na_bias_tables(rpb, rows):
    col = np.arange(GRID_W)
    dcm = np.clip(col[None, :] - col[:, None], -(NA_KC - 1), NA_KC - 1) + (NA_KC - 1)
    onehot = (dcm[None] == np.arange(2 * NA_KC - 1)[:, None, None]).astype(np.float32)
    tab = jnp.einsum('hrd,dqk->hrqk', rpb.astype(F32), jnp.asarray(onehot), precision=HI)
    col_start = np.clip(col - NA_KC // 2, 0, GRID_W - NA_KC)
    col_in = (col[None, :] >= col_start[:, None]) & (col[None, :] < col_start[:, None] + NA_KC)
    tab = jnp.where(jnp.asarray(col_in)[None, None], tab, NEG_INF)
    n_dr = 2 * NA_KR - 1
    n_side = NA_DR_PAD + NA_SPAN - n_dr + 1
    blank_lo = jnp.full((NA_HEADS, NA_DR_PAD, GRID_W, GRID_W), NEG_INF, F32)
    blank_hi = jnp.full((NA_HEADS, n_side, GRID_W, GRID_W), NEG_INF, F32)
    padded = jnp.concatenate([blank_lo, tab, blank_hi], axis=1)
    pair_tab = jnp.concatenate([padded[:, :-1], padded[:, 1:]], axis=-1)
    nblk = rows // NA_QROWS
    mask = np.full((3, NA_QROWS, NA_SPAN), NEG_INF, np.float32)
    for p, j in enumerate((0, 1, nblk - 1)):
        base = _na_span_base(j, rows)
        for ri in range(NA_QROWS):
            r = NA_QROWS * j + ri
            rs = np.clip(r - NA_KR // 2, 0, rows - NA_KR)
            for ki in range(NA_SPAN):
                if rs <= base + ki < rs + NA_KR:
                    mask[p, ri, ki] = 0.0
    row_mask = jnp.asarray(np.repeat(mask, GRID_W, axis=2))
    return pair_tab, row_mask


def _na_attn_kernel(q_ref, k_ref, v_ref, ck_ref, cv_ref, tab_ref, mask_ref, o_ref, *, rows):
    j = pl.program_id(2)
    scale = NA_HEAD_DIM ** -0.5
    base = jnp.clip(NA_QROWS * j - NA_KR // 2, 0, rows - NA_SPAN)
    start = pl.multiple_of(base * GRID_W, GRID_W)
    span = NA_SPAN * GRID_W
    q = q_ref[0]
    kl = k_ref[0, pl.ds(start, span), :].astype(BF16)
    vl = v_ref[0, pl.ds(start, span), :].astype(BF16)
    ck = ck_ref[0].astype(BF16)
    cv = cv_ref[0].astype(BF16)
    first = lax.broadcasted_iota(jnp.int32, q.shape, 1) < NA_HEAD_DIM
    off = base - NA_QROWS * j + (NA_KR - 1) + NA_DR_PAD
    outs = []
    for hh in range(2):
        qm = jnp.where(first if hh == 0 else ~first, q, 0.0).astype(BF16)
        s_raw = _dot_nt(qm, kl) * scale
        blocks = []
        for ri in range(NA_QROWS):
            rws = slice(ri * GRID_W, (ri + 1) * GRID_W)
            pieces = [s_raw[rws, m * LANES:(m + 1) * LANES] + tab_ref[hh, off + 2 * m - ri]
                      for m in range(NA_SPAN // 2)]
            blocks.append(jnp.concatenate(pieces, axis=1) + mask_ref[0, ri:ri + 1, :])
        s_loc = jnp.concatenate(blocks, axis=0)
        s_ctx = _dot_nt(qm, ck) * scale
        m = jnp.maximum(jnp.max(s_loc, axis=-1, keepdims=True), jnp.max(s_ctx, axis=-1, keepdims=True))
        p_loc = jnp.exp(s_loc - m)
        p_ctx = jnp.exp(s_ctx - m)
        den = jnp.sum(p_loc, axis=-1, keepdims=True) + jnp.sum(p_ctx, axis=-1, keepdims=True)
        p_loc = (p_loc / den).astype(BF16)
        p_ctx = (p_ctx / den).astype(BF16)
        outs.append(_dot(p_loc, vl) + _dot(p_ctx, cv))
    o_ref[0] = jnp.where(first, outs[0], outs[1])


def _na_attn(q, k, v, ck, cv, pair_tab, row_mask):
    b, n, w = q.shape
    p = ck.shape[1]
    rows = n // GRID_W
    nblk = rows // NA_QROWS
    qb = NA_QROWS * GRID_W

    def pattern(j):
        return jnp.where(j == 0, 0, jnp.where(j == nblk - 1, 2, 1))

    full = pl.BlockSpec((1, n, LANES), lambda bi, hp, j: (bi, 0, hp))
    ctx = pl.BlockSpec((1, p, LANES), lambda bi, hp, j: (bi, 0, hp))
    blk = pl.BlockSpec((1, qb, LANES), lambda bi, hp, j: (bi, j, hp))
    return pl.pallas_call(
        functools.partial(_na_attn_kernel, rows=rows),
        grid=(b, w // LANES, nblk),
        in_specs=[blk, full, full, ctx, ctx,
                  pl.BlockSpec((2,) + pair_tab.shape[1:], lambda bi, hp, j: (hp, 0, 0, 0)),
                  pl.BlockSpec((1,) + row_mask.shape[1:], lambda bi, hp, j: (pattern(j), 0, 0))],
        out_specs=blk,
        out_shape=jax.ShapeDtypeStruct((b, n, w), F32),
        compiler_params=_cparams(("parallel", "parallel", "arbitrary")),
        name="na_attn",
    )(q, k, v, ck, cv, pair_tab, row_mask)


def _gdn_conv_kernel(x_ref, w_ref, o_ref, *, groups):
    c = pl.program_id(1)
    t = x_ref.shape[1]
    row = lax.broadcasted_iota(jnp.int32, (t, LANES), 0)
    n_qk = 2 * GDN_HEADS
    for i in range(groups):
        lanes = slice(i * LANES, (i + 1) * LANES)
        x = x_ref[0, :, lanes]
        y = jnp.zeros_like(x)
        for jj in range(CONV_K):
            o = jj - CONV_K // 2
            xs = x if o == 0 else pltpu.roll(x, (-o) % t, 0)
            xs = jnp.where((row + o >= 0) & (row + o < t), xs, 0.0)
            y = y + xs * w_ref[jj:jj + 1, lanes]
        y = _silu(y)
        nrm = lax.rsqrt(jnp.sum(y * y, axis=-1, keepdims=True) + EPS)
        o_ref[0, i] = jnp.where(c * groups + i < n_qk, y * nrm, y)


def _gdn_conv(x, conv_w):
    b, t, ch = x.shape
    nc = ch // LANES
    groups = max(1, min(nc, GDN_CONV_BLOCK_ROWS // t))
    return pl.pallas_call(
        functools.partial(_gdn_conv_kernel, groups=groups),
        grid=(b, nc // groups),
        in_specs=[pl.BlockSpec((1, t, groups * LANES), lambda bi, c: (bi, 0, c)),
                  pl.BlockSpec((CONV_K, groups * LANES), lambda bi, c: (0, c))],
        out_specs=pl.BlockSpec((1, groups, t, LANES), lambda bi, c: (bi, c, 0, 0)),
        out_shape=jax.ShapeDtypeStruct((b, nc, t, LANES), F32),
        compiler_params=_cparams(("parallel", "parallel")),
        name="gdn_conv",
    )(x, conv_w)


def _bdot(a, b):
    return jnp.dot(a.astype(BF16), b.astype(BF16), preferred_element_type=F32)


def _split_bf16(x):
    hi = x.astype(BF16)
    return hi, (x - hi.astype(F32)).astype(BF16)


def _dot3(a, b):
    m = a.shape[0]
    ah, al = _split_bf16(a)
    bh, bl = _split_bf16(b)
    top = _dot(jnp.concatenate([ah, al], axis=0), bh)
    return top[:m] + top[m:] + _dot(ah, bl)


def _gdn_chunk_kernel(qkv_ref, ba_ref, alog_ref, dt_ref, s0_ref, o_ref, sfin_ref, s_ref, *, reverse, cb):
    c = pl.program_id(1)
    nh = GDN_HEADS

    @pl.when(c == 0)
    def _():
        s_ref[...] = s0_ref[0, 0]

    ii = lax.broadcasted_iota(jnp.int32, (CHUNK, CHUNK), 0)
    jj = lax.broadcasted_iota(jnp.int32, (CHUNK, CHUNK), 1)
    lag = (jj - ii) if reverse else (ii - jj)
    incl = lag >= 0
    strict = lag > 0
    eye = (ii == jj).astype(F32)
    tri = incl.astype(F32)
    bcol = nh if reverse else 0
    gcol0 = (3 if reverse else 2) * nh
    sub8 = lax.broadcasted_iota(jnp.int32, (8, LANES), 0)
    lane8 = lax.broadcasted_iota(jnp.int32, (8, LANES), 1)
    sel8 = (lane8 == gcol0 + sub8).astype(F32)
    units = [(ci, h) for ci in range(cb) for h in range(nh)]
    gc_alls, beta_alls, grow8s = [], [], []
    for ci in range(cb):
        ba = ba_ref[0, ci * CHUNK:(ci + 1) * CHUNK, :]
        z = ba + dt_ref[...]
        softplus = jnp.maximum(z, 0.0) + jnp.log1p(jnp.exp(-jnp.abs(z)))
        g_all = -jnp.exp(alog_ref[...]) * softplus
        gc_all = jnp.dot(tri, g_all, precision=HI, preferred_element_type=F32)
        gc_alls.append(gc_all)
        beta_alls.append(jax.nn.sigmoid(ba))
        grow8s.append(_dot_nt(sel8, gc_all, precision=HI))
    gcol = [gc_alls[ci][:, gcol0 + h:gcol0 + h + 1] for ci, h in units]
    beta = [beta_alls[ci][:, bcol + h:bcol + h + 1] for ci, h in units]
    rows = [slice(ci * CHUNK, (ci + 1) * CHUNK) for ci, _ in units]
    k = [qkv_ref[0, nh + h, rows[u], :] for u, (_, h) in enumerate(units)]
    kb = [k[u] * beta[u] for u in range(len(units))]
    q = [qkv_ref[0, h, rows[u], :] * (GDN_DK ** -0.5) for u, (_, h) in enumerate(units)]
    kq = [_dot_nt(jnp.concatenate([kb[u], q[u]], axis=0).astype(BF16), k[u].astype(BF16))
          for u in range(len(units))]
    decay = [jnp.where(incl, jnp.exp(jnp.where(incl, gcol[u] - grow8s[ci][h:h + 1, :], 0.0)), 0.0)
             for u, (ci, h) in enumerate(units)]
    intra = [jnp.where(incl, kq[u][CHUNK:] * decay[u], 0.0) for u in range(len(units))]
    pw = [-jnp.where(strict, kq[u][:CHUNK] * decay[u], 0.0) for u in range(len(units))]
    tmat = [eye + p for p in pw]
    pw = [_dot3(p, p) for p in pw]
    for _ in range(4):
        pt = [_dot3(jnp.concatenate([pw[u], tmat[u]], axis=0), pw[u]) for u in range(len(units))]
        pw = [x[:CHUNK] for x in pt]
        tmat = [tmat[u] + pt[u][CHUNK:] for u in range(len(units))]
    tmat = [tmat[u] + _dot3(tmat[u], pw[u]) for u in range(len(units))]
    eg = [jnp.exp(g) for g in gcol]
    uw = [_bdot(tmat[u], jnp.concatenate(
        [qkv_ref[0, 2 * nh + h, rows[u], :] * beta[u], kb[u] * eg[u]], axis=1))
        for u, (_, h) in enumerate(units)]
    g_last = [g[0:1, :] if reverse else g[CHUNK - 1:CHUNK, :] for g in gcol]
    kd = [k[u] * jnp.exp(g_last[u] - gcol[u]) for u in range(len(units))]
    qe = [q[u] * eg[u] for u in range(len(units))]
    s = [s_ref[h] for h in range(nh)]
    for ci in (reversed(range(cb)) if reverse else range(cb)):
        us = [ci * nh + h for h in range(nh)]
        wq = [_bdot(jnp.concatenate([uw[u][:, GDN_DV:], qe[u]], axis=0), s[h]) for h, u in enumerate(us)]
        v_new = [uw[u][:, :GDN_DV] - wq[h][:CHUNK] for h, u in enumerate(us)]
        for h, u in enumerate(us):
            o_ref[0, rows[u], h * GDN_DV:(h + 1) * GDN_DV] = wq[h][CHUNK:] + _bdot(intra[u], v_new[h])
        s = [s[h] * jnp.exp(g_last[u]) + lax.dot_general(
            kd[u].astype(BF16), v_new[h].astype(BF16), (((0,), (0,)), ((), ())), preferred_element_type=F32)
            for h, u in enumerate(us)]
    for h in range(nh):
        s_ref[h] = s[h]

    @pl.when(c == pl.num_programs(1) - 1)
    def _():
        sfin_ref[0] = s_ref[...]


def _gdn_chunks(qkv, ba, alog_row, dt_row, s0, reverse):
    b, _, t, _ = qkv.shape
    cb = GDN_CHUNKS_PER_STEP
    rows = cb * CHUNK
    n = t // rows
    d = 1 if reverse else 0

    def blk(c):
        return n - 1 - c if reverse else c

    return pl.pallas_call(
        functools.partial(_gdn_chunk_kernel, reverse=reverse, cb=cb),
        grid=(b, n),
        in_specs=[pl.BlockSpec((1, 3 * GDN_HEADS, rows, LANES), lambda bi, c: (bi, 0, blk(c), 0)),
                  pl.BlockSpec((1, rows, LANES), lambda bi, c: (bi, blk(c), 0)),
                  pl.BlockSpec((1, LANES), lambda bi, c: (0, 0)),
                  pl.BlockSpec((1, LANES), lambda bi, c: (0, 0)),
                  pl.BlockSpec((1, 1, GDN_HEADS, GDN_DK, GDN_DV), lambda bi, c: (bi, d, 0, 0, 0))],
        out_specs=[pl.BlockSpec((1, rows, GDN_V_WIDTH), lambda bi, c: (bi, blk(c), 0)),
                   pl.BlockSpec((1, GDN_HEADS, GDN_DK, GDN_DV), lambda bi, c: (bi, 0, 0, 0))],
        out_shape=[jax.ShapeDtypeStruct((b, t, GDN_V_WIDTH), F32),
                   jax.ShapeDtypeStruct((b, GDN_HEADS, GDN_DK, GDN_DV), F32)],
        scratch_shapes=[pltpu.VMEM((GDN_HEADS, GDN_DK, GDN_DV), F32)],
        compiler_params=_cparams(("parallel", "arbitrary")),
        name="gdn_bwd" if reverse else "gdn_fwd",
    )(qkv, ba, alog_row, dt_row, s0)


def _pack_bf16_pairs(h):
    half = D_MODEL // 2
    lo = pltpu.bitcast(h[:, :half].astype(BF16).astype(F32), jnp.uint32)
    hi = pltpu.bitcast(h[:, half:].astype(BF16).astype(F32), jnp.uint32)
    return (hi & jnp.uint32(0xFFFF0000)) | (lo >> 16)


def _unpack_bf16_pairs(w):
    lo = pltpu.bitcast(w << 16, F32).astype(BF16)
    hi = pltpu.bitcast(w & jnp.uint32(0xFFFF0000), F32).astype(BF16)
    return lo, hi


def _mix_kernel(x_ref, na_ref, of_ref, ob_ref, z_ref, gate_ref, mod_ref, gnw_ref, gpost_ref, gpre_ref,
                wna_ref, wgdn_ref, wout_ref, x1_ref, h2_ref, h2p_ref):
    mod = mod_ref[0]
    o = of_ref[...] + ob_ref[...]
    parts = []
    for h in range(GDN_HEADS):
        sl = slice(h * GDN_DV, (h + 1) * GDN_DV)
        parts.append(_rms(o[:, sl], gnw_ref[...]) * _silu(z_ref[:, sl]))
    gdn_o = jnp.concatenate(parts, axis=-1)
    a = _dot(na_ref[...].astype(BF16), wna_ref[...])
    b = _dot(gdn_o.astype(BF16), wgdn_ref[...])
    gate = jax.nn.sigmoid(gate_ref[...])
    pre = gate[:, :D_MODEL] * a + gate[:, D_MODEL:] * b
    mix = _dot(pre.astype(BF16), wout_ref[...])
    x1 = x_ref[...] + mod[:, 2 * D_MODEL:3 * D_MODEL] * _rms(mix, gpost_ref[...])
    x1_ref[...] = x1
    h2 = _rms(x1, gpre_ref[...]) * (1.0 + mod[:, 4 * D_MODEL:5 * D_MODEL]) + mod[:, 3 * D_MODEL:4 * D_MODEL]
    h2_ref[...] = h2
    packed = _pack_bf16_pairs(h2)
    for c in range(PACK_ROWS):
        h2p_ref[pl.ds(c, h2.shape[0], stride=PACK_ROWS), :] = packed[:, c * LANES:(c + 1) * LANES]


def _mix(x, na_o, o_f, o_b, z, gate, mods3, gnw, gpost, gpre, wna, wgdn, wout, row_of_tile):
    n = x.shape[0]
    tm = TOK_TILE
    row = lambda w: pl.BlockSpec((tm, w), lambda i: (i, 0))
    const = lambda r, c: pl.BlockSpec((r, c), lambda i: (0, 0))
    return pl.pallas_call(
        _mix_kernel,
        grid=(n // tm,),
        in_specs=[row(D_MODEL), row(NA_WIDTH), row(GDN_V_WIDTH), row(GDN_V_WIDTH),
                  row(GDN_V_WIDTH), row(2 * D_MODEL),
                  pl.BlockSpec((1, 1, 6 * D_MODEL), lambda i: (row_of_tile(i), 0, 0)),
                  const(1, GDN_DV), const(1, D_MODEL), const(1, D_MODEL),
                  const(NA_WIDTH, D_MODEL), const(GDN_V_WIDTH, D_MODEL), const(D_MODEL, D_MODEL)],
        out_specs=[row(D_MODEL), row(D_MODEL), pl.BlockSpec((tm * PACK_ROWS, LANES), lambda i: (i, 0))],
        out_shape=[jax.ShapeDtypeStruct((n, D_MODEL), F32),
                   jax.ShapeDtypeStruct((n, D_MODEL), F32),
                   jax.ShapeDtypeStruct((n * PACK_ROWS, LANES), jnp.uint32)],
        compiler_params=_cparams(("parallel",)),
        name="mix",
    )(x, na_o, o_f, o_b, z, gate, mods3, gnw.reshape(1, GDN_DV), gpost.reshape(1, D_MODEL),
      gpre.reshape(1, D_MODEL), wna, wgdn, wout)


def _router_kernel(h_ref, wh_ref, wl_ref, b_ref, e_ref, wt_ref, cnt_ref, acc_ref):
    i = pl.program_id(0)
    tm = h_ref.shape[0]
    ne = N_EXPERTS
    per = ne // N_GROUPS
    hh, hl = _split_bf16(h_ref[...])
    wh = wh_ref[...]
    logits = _dot_nt(wh, hh) + _dot_nt(wl_ref[...], hh) + _dot_nt(wh, hl)
    scores = jax.nn.sigmoid(logits)
    biased = scores + jnp.concatenate([b_ref[...]] * (tm // LANES), axis=1)

    def first_max(x):
        n = x.shape[0]
        iota = lax.broadcasted_iota(jnp.int32, x.shape, 0).astype(F32)
        m = jnp.max(x, axis=0, keepdims=True)
        idx = jnp.min(jnp.where(x == m, iota, float(n)), axis=0, keepdims=True)
        return m, idx, iota

    gs_rows = []
    for g in range(N_GROUPS):
        bg = biased[g * per:(g + 1) * per]
        m1, i1, iota = first_max(bg)
        m2 = jnp.max(jnp.where(iota == i1, NEG_INF, bg), axis=0, keepdims=True)
        gs_rows.append(m1 + m2)
    gs = jnp.concatenate(gs_rows, axis=0)
    gsel = jnp.zeros(gs.shape, F32)
    for _ in range(TOPK_GROUPS):
        _, gi, iota = first_max(gs)
        hit = iota == gi
        gs = jnp.where(hit, NEG_INF, gs)
        gsel = jnp.where(hit, 1.0, gsel)
    masked = jnp.concatenate(
        [jnp.where(gsel[g:g + 1] > 0.0, biased[g * per:(g + 1) * per], NEG_INF) for g in range(N_GROUPS)], axis=0)
    onehot = jnp.zeros((ne, tm), F32)
    e_rows, w_rows = [], []
    for _ in range(TOP_K):
        _, ei, iota = first_max(masked)
        hit = iota == ei
        masked = jnp.where(hit, NEG_INF, masked)
        w_rows.append(jnp.sum(jnp.where(hit, scores, 0.0), axis=0, keepdims=True))
        e_rows.append(ei)
        onehot = onehot + hit.astype(F32)
    w_out = jnp.concatenate(w_rows, axis=0)
    e_ref[...] = jnp.concatenate(e_rows, axis=0).astype(jnp.int32)
    wt_ref[...] = w_out / jnp.sum(w_out, axis=0, keepdims=True) * ROUTED_SCALE

    @pl.when(i == 0)
    def _():
        acc_ref[...] = jnp.zeros_like(acc_ref)

    acc_ref[...] += sum(onehot[:, c * LANES:(c + 1) * LANES] for c in range(tm // LANES))

    @pl.when(i == pl.num_programs(0) - 1)
    def _():
        cnt_ref[...] = jnp.broadcast_to(jnp.sum(acc_ref[...], axis=1, keepdims=True), cnt_ref.shape)


def _router(h2, w_rt_hi, w_rt_lo, bias_col):
    n = h2.shape[0]
    tm = TOK_TILE
    return pl.pallas_call(
        _router_kernel,
        grid=(n // tm,),
        in_specs=[pl.BlockSpec((tm, D_MODEL), lambda i: (i, 0)),
                  pl.BlockSpec((N_EXPERTS, D_MODEL), lambda i: (0, 0)),
                  pl.BlockSpec((N_EXPERTS, D_MODEL), lambda i: (0, 0)),
                  pl.BlockSpec((N_EXPERTS, LANES), lambda i: (0, 0))],
        out_specs=[pl.BlockSpec((TOP_K, tm), lambda i: (0, i)),
                   pl.BlockSpec((TOP_K, tm), lambda i: (0, i)),
                   pl.BlockSpec((N_EXPERTS, LANES), lambda i: (0, 0))],
        out_shape=[jax.ShapeDtypeStruct((TOP_K, n), jnp.int32),
                   jax.ShapeDtypeStruct((TOP_K, n), F32),
                   jax.ShapeDtypeStruct((N_EXPERTS, LANES), F32)],
        scratch_shapes=[pltpu.VMEM((N_EXPERTS, LANES), F32)],
        compiler_params=_cparams(("arbitrary",)),
        name="router",
    )(h2, w_rt_hi, w_rt_lo, bias_col)


_ST_SLOT, _ST_READY, _ST_PEND_BASE, _ST_PEND_VALID = range(4)


def _moe_kernel(tok_ref, seg_ref, cnt_ref, h2p_hbm, w2d_ref, wg_ref, wu_ref, wd_ref, out_hbm,
                h2p_ref, acc_ref, xs_ref, ye_ref, st_ref, sem):
    e = pl.program_id(0)
    n_exp = pl.num_programs(0)
    half = D_MODEL // 2
    xw, yw = PACK_ROWS, ACC_ROWS
    n_tok = out_hbm.shape[0] // yw
    grp = MOE_ROW_GROUP

    def gather_tile(base, n_valid, slot):
        for r in range(MOE_ROWS):
            tok = jnp.where(r < n_valid, tok_ref[base + r], n_tok)
            xs_ref[slot, xw * r:xw * (r + 1), :] = h2p_ref[pl.ds(pl.multiple_of(tok * xw, xw), xw), :]

    def scatter_tile(base, n_valid, slot):
        for g in range(MOE_ROWS // grp):
            toks = [pl.multiple_of(
                jnp.where(g * grp + j < n_valid, tok_ref[base + g * grp + j], n_tok) * yw, yw)
                for j in range(grp)]
            rows = [acc_ref[pl.ds(toks[j], yw), :] + ye_ref[slot, yw * (g * grp + j):yw * (g * grp + j + 1), :]
                    for j in range(grp)]
            for j in range(grp):
                acc_ref[pl.ds(toks[j], yw), :] = rows[j]

    seg = seg_ref[e]
    cnt = cnt_ref[e]
    n_tiles = (cnt + MOE_ROWS - 1) // MOE_ROWS

    @pl.when(e == 0)
    def _():
        cp = pltpu.make_async_copy(h2p_hbm, h2p_ref.at[pl.ds(0, n_tok * xw)], sem.at[0])
        cp.start()
        acc_ref[...] = jnp.zeros_like(acc_ref)
        ye_ref[...] = jnp.zeros_like(ye_ref)
        h2p_ref[pl.ds(n_tok * xw, grp * xw), :] = jnp.zeros((grp * xw, LANES), jnp.uint32)
        st_ref[_ST_SLOT] = 0
        st_ref[_ST_READY] = -1
        st_ref[_ST_PEND_BASE] = 0
        st_ref[_ST_PEND_VALID] = 0
        cp.wait()

    @pl.when((n_tiles > 0) & (st_ref[_ST_READY] != e))
    def _():
        gather_tile(seg, jnp.minimum(cnt, MOE_ROWS), st_ref[_ST_SLOT])

    def chunk(w_ref, c):
        return jnp.concatenate([w_ref[0, LANES * c:LANES * (c + 1), :],
                                w_ref[0, half + LANES * c:half + LANES * (c + 1), :]], axis=0).astype(BF16)

    lane = lax.broadcasted_iota(jnp.int32, (8, LANES), 1)
    nxt_e = jnp.minimum(e + 1, n_exp - 1)
    nxt_seg = seg_ref[nxt_e]
    nxt_cnt = jnp.where(e + 1 < n_exp, cnt_ref[nxt_e], 0)

    def tile_body(t, carry):
        slot = st_ref[_ST_SLOT]
        base = seg + t * MOE_ROWS
        last = t + 1 == n_tiles
        xk = [jnp.concatenate(_unpack_bf16_pairs(xs_ref[slot, pl.ds(c, MOE_ROWS, stride=xw), :]), axis=1)
              for c in range(xw)]
        g_base = jnp.where(last, nxt_seg, base + MOE_ROWS)
        g_valid = jnp.minimum(MOE_ROWS, jnp.where(last, nxt_cnt, cnt - (t + 1) * MOE_ROWS))
        gather_tile(g_base, g_valid, 1 - slot)
        scatter_tile(st_ref[_ST_PEND_BASE], st_ref[_ST_PEND_VALID], 1 - slot)
        hg = sum(_dot(xk[c], chunk(wg_ref, c)) for c in range(xw))
        hu = sum(_dot(xk[c], chunk(wu_ref, c)) for c in range(xw))
        act = (_silu(hg) * hu).astype(BF16)
        q = base // LANES
        sh = base % LANES
        rot = (LANES - sh) % LANES
        row_a = pltpu.roll(jnp.broadcast_to(w2d_ref[pl.ds(q, 1), :], (8, LANES)), rot, 1)
        row_b = pltpu.roll(jnp.broadcast_to(w2d_ref[pl.ds(q + 1, 1), :], (8, LANES)), rot, 1)
        w_row = jnp.where(lane + sh < LANES, row_a, row_b)[0:1, :]
        w_col = jnp.broadcast_to(w_row, (MOE_ROWS, LANES)).T
        ye = _dot(act, wd_ref[0].astype(BF16))
        for c in range(yw):
            ye_ref[slot, pl.ds(c, MOE_ROWS, stride=yw), :] = ye[:, LANES * c:LANES * (c + 1)] * w_col
        st_ref[_ST_PEND_BASE] = base
        st_ref[_ST_PEND_VALID] = jnp.minimum(MOE_ROWS, cnt - t * MOE_ROWS)
        st_ref[_ST_SLOT] = 1 - slot
        st_ref[_ST_READY] = jnp.where(last, e + 1, e)
        return carry

    lax.fori_loop(0, n_tiles, tile_body, 0)

    @pl.when(e == n_exp - 1)
    def _():
        scatter_tile(st_ref[_ST_PEND_BASE], st_ref[_ST_PEND_VALID], 1 - st_ref[_ST_SLOT])
        cp = pltpu.make_async_copy(acc_ref.at[pl.ds(0, n_tok * yw)], out_hbm, sem.at[1])
        cp.start()
        cp.wait()


def _moe(tok_sorted, seg_start, seg_count, h2p, w2d, wg, wu, wd):
    xw, yw = PACK_ROWS, ACC_ROWS
    n = h2p.shape[0] // xw
    grid_spec = pltpu.PrefetchScalarGridSpec(
        num_scalar_prefetch=3,
        grid=(N_EXPERTS,),
        in_specs=[pl.BlockSpec(memory_space=pl.ANY),
                  pl.BlockSpec(w2d.shape, lambda e, *_: (0, 0)),
                  pl.BlockSpec((1, D_MODEL, EXPERT_DIM), lambda e, *_: (e, 0, 0)),
                  pl.BlockSpec((1, D_MODEL, EXPERT_DIM), lambda e, *_: (e, 0, 0)),
                  pl.BlockSpec((1, EXPERT_DIM, D_MODEL), lambda e, *_: (e, 0, 0))],
        out_specs=pl.BlockSpec(memory_space=pl.ANY),
        scratch_shapes=[pltpu.VMEM(((n + MOE_ROW_GROUP) * xw, LANES), jnp.uint32),
                        pltpu.VMEM(((n + MOE_ROW_GROUP) * yw, LANES), F32),
                        pltpu.VMEM((2, MOE_ROWS * xw, LANES), jnp.uint32),
                        pltpu.VMEM((2, MOE_ROWS * yw, LANES), F32),
                        pltpu.SMEM((4,), jnp.int32),
                        pltpu.SemaphoreType.DMA((2,))],
    )
    return pl.pallas_call(
        _moe_kernel,
        grid_spec=grid_spec,
        out_shape=jax.ShapeDtypeStruct((n * yw, LANES), F32),
        compiler_params=_cparams(("arbitrary",), vmem=60 * 1024 * 1024),
        name="moe",
    )(tok_sorted, seg_start, seg_count, h2p, w2d, wg, wu, wd)


def _moe_routed(top_e, top_w, counts, h2p, wg, wu, wd):
    tok_sorted, w2d, seg_start, seg_count = _moe_dispatch_plan(top_e, top_w, counts, h2p.shape[0] // PACK_ROWS)
    return _moe(tok_sorted, seg_start, seg_count, h2p, w2d, wg, wu, wd)


def _moe_dispatch_plan(top_e, top_w, counts, n):
    flat_e = top_e.reshape(-1)
    flat_t = jnp.arange(n * TOP_K, dtype=jnp.int32) % n
    flat_w = top_w.reshape(-1)
    _, tok_sorted, w_sorted = lax.sort((flat_e, flat_t, flat_w), num_keys=1, is_stable=True)
    total = flat_e.shape[0]
    tok_sorted = jnp.concatenate([tok_sorted, jnp.full((MOE_ROWS,), n, jnp.int32)])
    table_rows = -(-(total // LANES + 2) // 8) * 8
    w2d = jnp.concatenate([w_sorted, jnp.zeros((table_rows * LANES - total,), F32)]).reshape(table_rows, LANES)
    cnt = counts[:, 0].astype(jnp.int32)
    return tok_sorted, w2d, jnp.cumsum(cnt) - cnt, cnt


def _regroup_rows(w):
    half = D_MODEL // 2
    return jnp.concatenate([w[r0 + LANES * c:r0 + LANES * (c + 1)]
                            for c in range(PACK_ROWS) for r0 in (0, half)], axis=0)


def _final_kernel(x1_ref, h2p_ref, r_ref, mod_ref, g_ref, wg_ref, wu_ref, wd_ref, y_ref):
    tm = x1_ref.shape[0]
    kc = 2 * LANES
    mod = mod_ref[0]
    xk = [jnp.concatenate(_unpack_bf16_pairs(h2p_ref[pl.ds(c, tm, stride=PACK_ROWS), :]), axis=1)
          for c in range(PACK_ROWS)]
    hg = sum(_dot(xk[c], wg_ref[kc * c:kc * (c + 1), :]) for c in range(PACK_ROWS))
    hu = sum(_dot(xk[c], wu_ref[kc * c:kc * (c + 1), :]) for c in range(PACK_ROWS))
    shared = _dot((_silu(hg) * hu).astype(BF16), wd_ref[...])
    routed = jnp.concatenate([r_ref[pl.ds(c, tm, stride=ACC_ROWS), :] for c in range(ACC_ROWS)], axis=1)
    ffn = routed + shared
    y_ref[...] = x1_ref[...] + mod[:, 5 * D_MODEL:6 * D_MODEL] * _rms(ffn, g_ref[...])


def _final(x1, h2p, routed, mods3, g, wg, wu, wd, row_of_tile):
    n = x1.shape[0]
    tm = TOK_TILE
    sd = wg.shape[1]
    row = lambda w: pl.BlockSpec((tm, w), lambda i: (i, 0))
    const = lambda r, c: pl.BlockSpec((r, c), lambda i: (0, 0))
    return pl.pallas_call(
        _final_kernel,
        grid=(n // tm,),
        in_specs=[row(D_MODEL), pl.BlockSpec((tm * PACK_ROWS, LANES), lambda i: (i, 0)),
                  pl.BlockSpec((tm * ACC_ROWS, LANES), lambda i: (i, 0)),
                  pl.BlockSpec((1, 1, 6 * D_MODEL), lambda i: (row_of_tile(i), 0, 0)),
                  const(1, D_MODEL), const(D_MODEL, sd), const(D_MODEL, sd), const(sd, D_MODEL)],
        out_specs=row(D_MODEL),
        out_shape=jax.ShapeDtypeStruct((n, D_MODEL), F32),
        compiler_params=_cparams(("parallel",)),
        name="final",
    )(x1, h2p, routed, mods3, g.reshape(1, D_MODEL), wg, wu, wd)


def _trunk(x3, mods3, row_of_tile, attend, s0, wts):
    b, t, _ = x3.shape
    n = b * t
    x = x3.reshape(n, D_MODEL)
    q, k, v, gdn, z, gate, ba = _premix(x, mods3, wts["g_pre_mix"], wts["w_cat"], row_of_tile)
    na_o = attend(q, k, v).reshape(n, NA_WIDTH)
    qkv = _gdn_conv(gdn.reshape(b, t, GDN_CONV_CH), wts["conv_w"])
    ba3 = ba.reshape(b, t, LANES)
    o_f, s_f = _gdn_chunks(qkv, ba3, wts["alog_row"], wts["dt_row"], s0, reverse=False)
    o_b, s_b = _gdn_chunks(qkv, ba3, wts["alog_row"], wts["dt_row"], s0, reverse=True)
    s_fin = jnp.stack([s_f, s_b], axis=1)
    x1, h2, h2p = _mix(x, na_o, o_f.reshape(n, GDN_V_WIDTH), o_b.reshape(n, GDN_V_WIDTH), z, gate, mods3,
                       wts["gdn_norm_w"],
                       wts["g_post_mix"], wts["g_pre_ffn"], wts["w_na_up"], wts["w_gdn_up"],
                       wts["w_out"], row_of_tile)
    top_e, top_w, counts = _router(h2, wts["w_rt_hi"], wts["w_rt_lo"], wts["router_bias_col"])
    routed = _moe_routed(top_e, top_w, counts, h2p, wts["w_exp_gate"], wts["w_exp_up"], wts["w_exp_down"])
    y = _final(x1, h2p, routed, mods3, wts["g_post_ffn"], wts["w_sh_gate"], wts["w_sh_up"],
               wts["w_sh_down"], row_of_tile)
    return y.reshape(b, t, D_MODEL), k, v, s_fin


def kernel(x_prompt, x_sample, cache_na_k, cache_na_v, state_gdn, c, c_ctx, w_ada, b_ada, g_pre_mix,
           g_post_mix, g_pre_ffn, g_post_ffn, w_in, conv_w, gdn_a_log, gdn_dt_bias, gdn_norm_w, na_rpb,
           w_na_up, w_gdn_up, w_out, w_router, router_bias, w_exp_gate, w_exp_up, w_exp_down, w_sh_gate,
           w_sh_up, w_sh_down):
    depth = w_ada.shape[0]
    bp, tp, _ = x_prompt.shape
    bs, ts, _ = x_sample.shape
    y_prompt, y_sample = x_prompt, x_sample
    zero_state = jnp.zeros((bp, 2, GDN_HEADS, GDN_DK, GDN_DV), F32)
    new_k, new_v, new_s = [], [], []
    for l in range(depth):
        cv = jnp.concatenate([c_ctx[None], c, jnp.zeros((8 - 1 - bs, D_MODEL), F32)], axis=0)
        mods3 = _ada(cv, w_ada[l], b_ada[l]).reshape(8, 1, 6 * D_MODEL)
        wl = w_in[l]
        w_cat = jnp.concatenate(
            [wl[:, :S_Z], wl[:, S_A:], wl[:, S_Z:S_A],
             jnp.zeros((D_MODEL, LANES - 4 * GDN_HEADS), F32)], axis=1).astype(BF16)
        pad = jnp.zeros((2 * GDN_HEADS,), F32)
        tail = jnp.zeros((LANES - 4 * GDN_HEADS,), F32)
        w_rt = w_router[l].astype(F32).T
        w_rt_hi = w_rt.astype(BF16)
        wts = dict(
            w_cat=w_cat, g_pre_mix=g_pre_mix[l], g_post_mix=g_post_mix[l], g_pre_ffn=g_pre_ffn[l],
            g_post_ffn=g_post_ffn[l], conv_w=conv_w[l], gdn_norm_w=gdn_norm_w[l],
            alog_row=jnp.concatenate([pad, gdn_a_log[l].reshape(-1), tail]).reshape(1, LANES),
            dt_row=jnp.concatenate([pad, gdn_dt_bias[l].reshape(-1), tail]).reshape(1, LANES),
            w_na_up=w_na_up[l].astype(BF16), w_gdn_up=w_gdn_up[l].astype(BF16),
            w_out=w_out[l].astype(BF16), w_rt_hi=w_rt_hi, w_rt_lo=(w_rt - w_rt_hi.astype(F32)).astype(BF16),
            router_bias_col=jnp.broadcast_to(router_bias[l].astype(F32)[:, None], (N_EXPERTS, LANES)),
            w_exp_gate=w_exp_gate[l], w_exp_up=w_exp_up[l], w_exp_down=w_exp_down[l],
            w_sh_gate=_regroup_rows(w_sh_gate[l]).astype(BF16), w_sh_up=_regroup_rows(w_sh_up[l]).astype(BF16),
            w_sh_down=w_sh_down[l].astype(BF16))

        def ctx_attend(q, k, v):
            return _ctx_attn(q.reshape(bp, tp, NA_WIDTH), k.reshape(bp, tp, NA_WIDTH),
                             v.reshape(bp, tp, NA_WIDTH))

        y_prompt, k_ctx, v_ctx, s_ctx = _trunk(y_prompt, mods3, lambda i: 0, ctx_attend, zero_state, wts)
        new_k.append(k_ctx.reshape(bp, tp, NA_HEADS, NA_HEAD_DIM))
        new_v.append(v_ctx.reshape(bp, tp, NA_HEADS, NA_HEAD_DIM))
        new_s.append(s_ctx)

        pair_tab, row_mask = _na_bias_tables(na_rpb[l], ts // GRID_W)
        ck = cache_na_k[:, l].reshape(bs, -1, NA_WIDTH)
        cvv = cache_na_v[:, l].reshape(bs, -1, NA_WIDTH)

        def na_attend(q, k, v):
            return _na_attn(q.reshape(bs, ts, NA_WIDTH), k.reshape(bs, ts, NA_WIDTH),
                            v.reshape(bs, ts, NA_WIDTH), ck, cvv, pair_tab, row_mask)

        tiles_per_seq = ts // TOK_TILE
        y_sample, _, _, _ = _trunk(y_sample, mods3, lambda i: 1 + i // tiles_per_seq, na_attend,
                                   state_gdn[:, l], wts)
    return (y_prompt, y_sample, jnp.stack(new_k, axis=1), jnp.stack(new_v, axis=1),
            jnp.stack(new_s, axis=1))
```

```python
import functools

import numpy as np
import jax
import jax.numpy as jnp
from jax import lax
from jax.experimental import pallas as pl
from jax.experimental.pallas import tpu as pltpu

F32 = jnp.float32
BF16 = jnp.bfloat16
HI = lax.Precision.HIGHEST

D_MODEL = 1024
GRID_W = 64
NA_HEADS = 8
NA_HEAD_DIM = 64
NA_WIDTH = NA_HEADS * NA_HEAD_DIM
NA_KR = 8
NA_KC = 16
GDN_HEADS = 4
GDN_DK = 128
GDN_DV = 128
GDN_QK_WIDTH = GDN_HEADS * GDN_DK
GDN_V_WIDTH = GDN_HEADS * GDN_DV
GDN_CONV_CH = 2 * GDN_QK_WIDTH + GDN_V_WIDTH
CONV_K = 5
CHUNK = 64
N_EXPERTS = 256
TOP_K = 8
N_GROUPS = 8
TOPK_GROUPS = 4
EXPERT_DIM = 256
ROUTED_SCALE = 2.5
EPS = 1e-6
S_NA = 3 * NA_WIDTH
S_GDN = S_NA + GDN_CONV_CH
S_Z = S_GDN + GDN_V_WIDTH
S_B = S_Z + 2 * GDN_HEADS
S_A = S_B + 2 * GDN_HEADS

LANES = 128
TOK_TILE = 256
NA_QROWS = 8
NA_SPAN = 16
MOE_ROWS = 128
PACK_ROWS = D_MODEL // 2 // LANES
ACC_ROWS = D_MODEL // LANES
MOE_ROW_GROUP = 8
GDN_CHUNKS_PER_STEP = 8
GDN_CONV_BLOCK_ROWS = 4096
VMEM_LIMIT = 56 * 1024 * 1024
NEG_INF = float("-inf")


def _cparams(sem, vmem=VMEM_LIMIT):
    return pltpu.CompilerParams(dimension_semantics=sem, vmem_limit_bytes=vmem)


def _silu(x):
    return x * jax.nn.sigmoid(x)


def _rms(x, g):
    return x * lax.rsqrt(jnp.mean(x * x, axis=-1, keepdims=True) + EPS) * g


def _dot(a, b):
    return jnp.dot(a, b, preferred_element_type=F32)


def _dot_nt(a, b, precision=None):
    return lax.dot_general(a, b, (((1,), (1,)), ((), ())), precision=precision,
                           preferred_element_type=F32)


def _ada_kernel(c_ref, w_ref, b_ref, o_ref):
    o_ref[...] = jnp.dot(_silu(c_ref[...]), w_ref[...], precision=HI,
                         preferred_element_type=F32) + b_ref[...]


def _ada(cv, w_ada, b_ada):
    n = w_ada.shape[1]
    tn = 512
    return pl.pallas_call(
        _ada_kernel,
        grid=(n // tn,),
        in_specs=[pl.BlockSpec((8, D_MODEL), lambda j: (0, 0)),
                  pl.BlockSpec((D_MODEL, tn), lambda j: (0, j)),
                  pl.BlockSpec((1, tn), lambda j: (0, j))],
        out_specs=pl.BlockSpec((8, tn), lambda j: (0, j)),
        out_shape=jax.ShapeDtypeStruct((8, n), F32),
        compiler_params=_cparams(("parallel",)),
        name="ada",
    )(cv, w_ada, b_ada.reshape(1, n))


_PM_WIDTHS = (NA_WIDTH, NA_WIDTH, NA_WIDTH, GDN_CONV_CH, GDN_V_WIDTH, 2 * D_MODEL, LANES)


def _premix_kernel(x_ref, mod_ref, g_ref, w_ref, *o_refs):
    mod = mod_ref[0]
    h = _rms(x_ref[...], g_ref[...]) * (1.0 + mod[:, D_MODEL:2 * D_MODEL]) + mod[:, 0:D_MODEL]
    hb = h.astype(BF16)
    off = 0
    for o_ref, wd in zip(o_refs, _PM_WIDTHS):
        for c0 in range(0, wd, 512):
            c1 = min(c0 + 512, wd)
            o_ref[:, c0:c1] = _dot(hb, w_ref[:, off + c0:off + c1])
        off += wd


def _premix(x, mods3, g, w_cat, row_of_tile):
    n = x.shape[0]
    wtot = w_cat.shape[1]
    tm = TOK_TILE
    return pl.pallas_call(
        _premix_kernel,
        grid=(n // tm,),
        in_specs=[pl.BlockSpec((tm, D_MODEL), lambda i: (i, 0)),
                  pl.BlockSpec((1, 1, 6 * D_MODEL), lambda i: (row_of_tile(i), 0, 0)),
                  pl.BlockSpec((1, D_MODEL), lambda i: (0, 0)),
                  pl.BlockSpec((D_MODEL, wtot), lambda i: (0, 0))],
        out_specs=[pl.BlockSpec((tm, wd), lambda i: (i, 0)) for wd in _PM_WIDTHS],
        out_shape=[jax.ShapeDtypeStruct((n, wd), F32) for wd in _PM_WIDTHS],
        compiler_params=_cparams(("parallel",)),
        name="premix",
    )(x, mods3, g.reshape(1, D_MODEL), w_cat)


def _softmax_rows(s):
    m = jnp.max(s, axis=-1, keepdims=True)
    p = jnp.exp(s - m)
    return p / jnp.sum(p, axis=-1, keepdims=True)


def _ctx_attn_kernel(q_ref, k_ref, v_ref, o_ref):
    scale = NA_HEAD_DIM ** -0.5
    for hp in range(NA_HEADS // 2):
        outs = []
        for h in (2 * hp, 2 * hp + 1):
            sl = slice(h * NA_HEAD_DIM, (h + 1) * NA_HEAD_DIM)
            q = q_ref[0, :, sl].astype(BF16)
            k = k_ref[0, :, sl].astype(BF16)
            v = v_ref[0, :, sl].astype(BF16)
            p = _softmax_rows(_dot_nt(q, k) * scale)
            outs.append(_dot(p.astype(BF16), v))
        o_ref[0, :, hp * LANES:(hp + 1) * LANES] = jnp.concatenate(outs, axis=-1)


def _ctx_attn(q, k, v):
    b, t, w = q.shape
    spec = pl.BlockSpec((1, t, w), lambda i: (i, 0, 0))
    return pl.pallas_call(
        _ctx_attn_kernel,
        grid=(b,),
        in_specs=[spec, spec, spec],
        out_specs=spec,
        out_shape=jax.ShapeDtypeStruct((b, t, w), F32),
        compiler_params=_cparams(("parallel",)),
        name="ctx_attn",
    )(q, k, v)


def _na_span_base(j, rows):
    return np.clip(NA_QROWS * j - NA_KR // 2, 0, rows - NA_SPAN)


NA_DR_PAD = NA_QROWS


def _na_bias_tables(rpb, rows):
    col = np.arange(GRID_W)
    dcm = np.clip(col[None, :] - col[:, None], -(NA_KC - 1), NA_KC - 1) + (NA_KC - 1)
    onehot = (dcm[None] == np.arange(2 * NA_KC - 1)[:, None, None]).astype(np.float32)
    tab = jnp.einsum('hrd,dqk->hrqk', rpb.astype(F32), jnp.asarray(onehot), precision=HI)
    col_start = np.clip(col - NA_KC // 2, 0, GRID_W - NA_KC)
    col_in = (col[None, :] >= col_start[:, None]) & (col[None, :] < col_start[:, None] + NA_KC)
    tab = jnp.where(jnp.asarray(col_in)[None, None], tab, NEG_INF)
    n_dr = 2 * NA_KR - 1
    n_side = NA_DR_PAD + NA_SPAN - n_dr + 1
    blank_lo = jnp.full((NA_HEADS, NA_DR_PAD, GRID_W, GRID_W), NEG_INF, F32)
    blank_hi = jnp.full((NA_HEADS, n_side, GRID_W, GRID_W), NEG_INF, F32)
    padded = jnp.concatenate([blank_lo, tab, blank_hi], axis=1)
    pair_tab = jnp.concatenate([padded[:, :-1], padded[:, 1:]], axis=-1)
    nblk = rows // NA_QROWS
    mask = np.full((3, NA_QROWS, NA_SPAN), NEG_INF, np.float32)
    for p, j in enumerate((0, 1, nblk - 1)):
        base = _na_span_base(j, rows)
        for ri in range(NA_QROWS):
            r = NA_QROWS * j + ri
            rs = np.clip(r - NA_KR // 2, 0, rows - NA_KR)
            for ki in range(NA_SPAN):
                if rs <= base + ki < rs + NA_KR:
                    mask[p, ri, ki] = 0.0
    row_mask = jnp.asarray(np.repeat(mask, GRID_W, axis=2))
    return pair_tab, row_mask


def _na_attn_kernel(q_ref, k_ref, v_ref, ck_ref, cv_ref, tab_ref, mask_ref, o_ref, *, rows):
    j = pl.program_id(2)
    scale = NA_HEAD_DIM ** -0.5
    base = jnp.clip(NA_QROWS * j - NA_KR // 2, 0, rows - NA_SPAN)
    start = pl.multiple_of(base * GRID_W, GRID_W)
    span = NA_SPAN * GRID_W
    q = q_ref[0]
    kl = k_ref[0, pl.ds(start, span), :].astype(BF16)
    vl = v_ref[0, pl.ds(start, span), :].astype(BF16)
    ck = ck_ref[0].astype(BF16)
    cv = cv_ref[0].astype(BF16)
    first = lax.broadcasted_iota(jnp.int32, q.shape, 1) < NA_HEAD_DIM
    off = base - NA_QROWS * j + (NA_KR - 1) + NA_DR_PAD
    outs = []
    for hh in range(2):
        qm = jnp.where(first if hh == 0 else ~first, q, 0.0).astype(BF16)
        s_raw = _dot_nt(qm, kl) * scale
        blocks = []
        for ri in range(NA_QROWS):
            rws = slice(ri * GRID_W, (ri + 1) * GRID_W)
            pieces = [s_raw[rws, m * LANES:(m + 1) * LANES] + tab_ref[hh, off + 2 * m - ri]
                      for m in range(NA_SPAN // 2)]
            blocks.append(jnp.concatenate(pieces, axis=1) + mask_ref[0, ri:ri + 1, :])
        s_loc = jnp.concatenate(blocks, axis=0)
        s_ctx = _dot_nt(qm, ck) * scale
        m = jnp.maximum(jnp.max(s_loc, axis=-1, keepdims=True), jnp.max(s_ctx, axis=-1, keepdims=True))
        p_loc = jnp.exp(s_loc - m)
        p_ctx = jnp.exp(s_ctx - m)
        den = jnp.sum(p_loc, axis=-1, keepdims=True) + jnp.sum(p_ctx, axis=-1, keepdims=True)
        p_loc = (p_loc / den).astype(BF16)
        p_ctx = (p_ctx / den).astype(BF16)
        outs.append(_dot(p_loc, vl) + _dot(p_ctx, cv))
    o_ref[0] = jnp.where(first, outs[0], outs[1])


def _na_attn(q, k, v, ck, cv, pair_tab, row_mask):
    b, n, w = q.shape
    p = ck.shape[1]
    rows = n // GRID_W
    nblk = rows // NA_QROWS
    qb = NA_QROWS * GRID_W

    def pattern(j):
        return jnp.where(j == 0, 0, jnp.where(j == nblk - 1, 2, 1))

    full = pl.BlockSpec((1, n, LANES), lambda bi, hp, j: (bi, 0, hp))
    ctx = pl.BlockSpec((1, p, LANES), lambda bi, hp, j: (bi, 0, hp))
    blk = pl.BlockSpec((1, qb, LANES), lambda bi, hp, j: (bi, j, hp))
    return pl.pallas_call(
        functools.partial(_na_attn_kernel, rows=rows),
        grid=(b, w // LANES, nblk),
        in_specs=[blk, full, full, ctx, ctx,
                  pl.BlockSpec((2,) + pair_tab.shape[1:], lambda bi, hp, j: (hp, 0, 0, 0)),
                  pl.BlockSpec((1,) + row_mask.shape[1:], lambda bi, hp, j: (pattern(j), 0, 0))],
        out_specs=blk,
        out_shape=jax.ShapeDtypeStruct((b, n, w), F32),
        compiler_params=_cparams(("parallel", "parallel", "arbitrary")),
        name="na_attn",
    )(q, k, v, ck, cv, pair_tab, row_mask)


def _gdn_conv_kernel(x_ref, w_ref, o_ref, *, groups):
    c = pl.program_id(1)
    t = x_ref.shape[1]
    row = lax.broadcasted_iota(jnp.int32, (t, LANES), 0)
    n_qk = 2 * GDN_HEADS
    for i in range(groups):
        lanes = slice(i * LANES, (i + 1) * LANES)
        x = x_ref[0, :, lanes]
        y = jnp.zeros_like(x)
        for jj in range(CONV_K):
            o = jj - CONV_K // 2
            xs = x if o == 0 else pltpu.roll(x, (-o) % t, 0)
            xs = jnp.where((row + o >= 0) & (row + o < t), xs, 0.0)
            y = y + xs * w_ref[jj:jj + 1, lanes]
        y = _silu(y)
        nrm = lax.rsqrt(jnp.sum(y * y, axis=-1, keepdims=True) + EPS)
        o_ref[0, i] = jnp.where(c * groups + i < n_qk, y * nrm, y)


def _gdn_conv(x, conv_w):
    b, t, ch = x.shape
    nc = ch // LANES
    groups = max(g for g in range(1, nc + 1)
                 if nc % g == 0 and (g == 1 or g * t <= GDN_CONV_BLOCK_ROWS))
    return pl.pallas_call(
        functools.partial(_gdn_conv_kernel, groups=groups),
        grid=(b, nc // groups),
        in_specs=[pl.BlockSpec((1, t, groups * LANES), lambda bi, c: (bi, 0, c)),
                  pl.BlockSpec((CONV_K, groups * LANES), lambda bi, c: (0, c))],
        out_specs=pl.BlockSpec((1, groups, t, LANES), lambda bi, c: (bi, c, 0, 0)),
        out_shape=jax.ShapeDtypeStruct((b, nc, t, LANES), F32),
        compiler_params=_cparams(("parallel", "parallel")),
        name="gdn_conv",
    )(x, conv_w)


def _bdot(a, b):
    return jnp.dot(a.astype(BF16), b.astype(BF16), preferred_element_type=F32)


def _split_bf16(x):
    hi = x.astype(BF16)
    return hi, (x - hi.astype(F32)).astype(BF16)


def _dot3(a, b):
    m = a.shape[0]
    ah, al = _split_bf16(a)
    bh, bl = _split_bf16(b)
    top = _dot(jnp.concatenate([ah, al], axis=0), bh)
    return top[:m] + top[m:] + _dot(ah, bl)


def _gdn_chunk_kernel(qkv_ref, ba_ref, alog_ref, dt_ref, s0_ref, o_ref, sfin_ref, s_ref, *, reverse, cb):
    c = pl.program_id(1)
    nh = GDN_HEADS

    @pl.when(c == 0)
    def _():
        s_ref[...] = s0_ref[0, 0]

    ii = lax.broadcasted_iota(jnp.int32, (CHUNK, CHUNK), 0)
    jj = lax.broadcasted_iota(jnp.int32, (CHUNK, CHUNK), 1)
    lag = (jj - ii) if reverse else (ii - jj)
    incl = lag >= 0
    strict = lag > 0
    eye = (ii == jj).astype(F32)
    tri = incl.astype(F32)
    bcol = nh if reverse else 0
    gcol0 = (3 if reverse else 2) * nh
    sub8 = lax.broadcasted_iota(jnp.int32, (8, LANES), 0)
    lane8 = lax.broadcasted_iota(jnp.int32, (8, LANES), 1)
    sel8 = (lane8 == gcol0 + sub8).astype(F32)
    units = [(ci, h) for ci in range(cb) for h in range(nh)]
    gc_alls, beta_alls, grow8s = [], [], []
    for ci in range(cb):
        ba = ba_ref[0, ci * CHUNK:(ci + 1) * CHUNK, :]
        z = ba + dt_ref[...]
        softplus = jnp.maximum(z, 0.0) + jnp.log1p(jnp.exp(-jnp.abs(z)))
        g_all = -jnp.exp(alog_ref[...]) * softplus
        gc_all = jnp.dot(tri, g_all, precision=HI, preferred_element_type=F32)
        gc_alls.append(gc_all)
        beta_alls.append(jax.nn.sigmoid(ba))
        grow8s.append(_dot_nt(sel8, gc_all, precision=HI))
    gcol = [gc_alls[ci][:, gcol0 + h:gcol0 + h + 1] for ci, h in units]
    beta = [beta_alls[ci][:, bcol + h:bcol + h + 1] for ci, h in units]
    rows = [slice(ci * CHUNK, (ci + 1) * CHUNK) for ci, _ in units]
    k = [qkv_ref[0, nh + h, rows[u], :] for u, (_, h) in enumerate(units)]
    kb = [k[u] * beta[u] for u in range(len(units))]
    q = [qkv_ref[0, h, rows[u], :] * (GDN_DK ** -0.5) for u, (_, h) in enumerate(units)]
    kq = [_dot_nt(jnp.concatenate([kb[u], q[u]], axis=0).astype(BF16), k[u].astype(BF16))
          for u in range(len(units))]
    decay = [jnp.where(incl, jnp.exp(jnp.where(incl, gcol[u] - grow8s[ci][h:h + 1, :], 0.0)), 0.0)
             for u, (ci, h) in enumerate(units)]
    intra = [jnp.where(incl, kq[u][CHUNK:] * decay[u], 0.0) for u in range(len(units))]
    wide = nh * CHUNK
    bd_mask = (lax.broadcasted_iota(jnp.int32, (wide, wide), 0) // CHUNK
               == lax.broadcasted_iota(jnp.int32, (wide, wide), 1) // CHUNK)

    def block_diag(x):
        return jnp.where(bd_mask, jnp.concatenate([x] * nh, axis=0), jnp.zeros((), x.dtype))

    def dot3_bd(a, b):
        m = a.shape[0]
        ah, al = _split_bf16(a)
        bh, bl = _split_bf16(b)
        top = _dot(jnp.concatenate([ah, al], axis=0), block_diag(bh))
        return top[:m] + top[m:] + _dot(ah, block_diag(bl))

    eye_w = jnp.concatenate([eye] * nh, axis=1)
    pw = [jnp.concatenate([-jnp.where(strict, kq[ci * nh + h][:CHUNK] * decay[ci * nh + h], 0.0)
                           for h in range(nh)], axis=1) for ci in range(cb)]
    tmat = [eye_w + p for p in pw]
    pw = [dot3_bd(p, p) for p in pw]
    for _ in range(4):
        pt = [dot3_bd(jnp.concatenate([pw[ci], tmat[ci]], axis=0), pw[ci]) for ci in range(cb)]
        pw = [x[:CHUNK] for x in pt]
        tmat = [tmat[ci] + pt[ci][CHUNK:] for ci in range(cb)]
    tmat = [tmat[ci] + dot3_bd(tmat[ci], pw[ci]) for ci in range(cb)]
    eg = [jnp.exp(g) for g in gcol]
    uw_all = [_dot(block_diag(tmat[ci].astype(BF16)), jnp.concatenate(
        [jnp.concatenate([qkv_ref[0, 2 * nh + h, rows[ci * nh + h], :] * beta[ci * nh + h],
                          kb[ci * nh + h] * eg[ci * nh + h]], axis=1) for h in range(nh)],
        axis=0).astype(BF16)) for ci in range(cb)]
    uw = [uw_all[ci][h * CHUNK:(h + 1) * CHUNK] for ci, h in units]
    g_last = [g[0:1, :] if reverse else g[CHUNK - 1:CHUNK, :] for g in gcol]
    kd = [k[u] * jnp.exp(g_last[u] - gcol[u]) for u in range(len(units))]
    qe = [q[u] * eg[u] for u in range(len(units))]
    kd_uw = [lax.dot_general(kd[u].astype(BF16), uw[u].astype(BF16), (((0,), (0,)), ((), ())),
                             preferred_element_type=F32) for u in range(len(units))]
    in_uw = [_bdot(intra[u], uw[u]) for u in range(len(units))]
    lhs = [jnp.concatenate([kd_uw[u][:, GDN_DV:], qe[u] - in_uw[u][:, GDN_DV:]], axis=0).astype(BF16)
           for u in range(len(units))]
    s = [s_ref[h] for h in range(nh)]
    for ci in (reversed(range(cb)) if reverse else range(cb)):
        us = [ci * nh + h for h in range(nh)]
        r = [_dot(lhs[u], s[h].astype(BF16)) for h, u in enumerate(us)]
        for h, u in enumerate(us):
            o_ref[0, rows[u], h * GDN_DV:(h + 1) * GDN_DV] = r[h][GDN_DK:] + in_uw[u][:, :GDN_DV]
        s = [s[h] * jnp.exp(g_last[u]) - r[h][:GDN_DK] + kd_uw[u][:, :GDN_DV] for h, u in enumerate(us)]
    for h in range(nh):
        s_ref[h] = s[h]

    @pl.when(c == pl.num_programs(1) - 1)
    def _():
        sfin_ref[0] = s_ref[...]


def _gdn_chunks(qkv, ba, alog_row, dt_row, s0, reverse):
    b, _, t, _ = qkv.shape
    cb = min(GDN_CHUNKS_PER_STEP, t // CHUNK)
    rows = cb * CHUNK
    n = t // rows
    d = 1 if reverse else 0

    def blk(c):
        return n - 1 - c if reverse else c

    return pl.pallas_call(
        functools.partial(_gdn_chunk_kernel, reverse=reverse, cb=cb),
        grid=(b, n),
        in_specs=[pl.BlockSpec((1, 3 * GDN_HEADS, rows, LANES), lambda bi, c: (bi, 0, blk(c), 0)),
                  pl.BlockSpec((1, rows, LANES), lambda bi, c: (bi, blk(c), 0)),
                  pl.BlockSpec((1, LANES), lambda bi, c: (0, 0)),
                  pl.BlockSpec((1, LANES), lambda bi, c: (0, 0)),
                  pl.BlockSpec((1, 1, GDN_HEADS, GDN_DK, GDN_DV), lambda bi, c: (bi, d, 0, 0, 0))],
        out_specs=[pl.BlockSpec((1, rows, GDN_V_WIDTH), lambda bi, c: (bi, blk(c), 0)),
                   pl.BlockSpec((1, GDN_HEADS, GDN_DK, GDN_DV), lambda bi, c: (bi, 0, 0, 0))],
        out_shape=[jax.ShapeDtypeStruct((b, t, GDN_V_WIDTH), F32),
                   jax.ShapeDtypeStruct((b, GDN_HEADS, GDN_DK, GDN_DV), F32)],
        scratch_shapes=[pltpu.VMEM((GDN_HEADS, GDN_DK, GDN_DV), F32)],
        compiler_params=_cparams(("parallel", "arbitrary")),
        name="gdn_bwd" if reverse else "gdn_fwd",
    )(qkv, ba, alog_row, dt_row, s0)


def _pack_bf16_pairs(h):
    half = D_MODEL // 2
    lo = pltpu.bitcast(h[:, :half].astype(BF16).astype(F32), jnp.uint32)
    hi = pltpu.bitcast(h[:, half:].astype(BF16).astype(F32), jnp.uint32)
    return (hi & jnp.uint32(0xFFFF0000)) | (lo >> 16)


def _unpack_bf16_pairs(w):
    lo = pltpu.bitcast(w << 16, F32).astype(BF16)
    hi = pltpu.bitcast(w & jnp.uint32(0xFFFF0000), F32).astype(BF16)
    return lo, hi


def _mix_kernel(x_ref, na_ref, of_ref, ob_ref, z_ref, gate_ref, mod_ref, gnw_ref, gpost_ref, gpre_ref,
                wna_ref, wgdn_ref, wout_ref, x1_ref, h2_ref, h2p_ref):
    mod = mod_ref[0]
    o = of_ref[...] + ob_ref[...]
    parts = []
    for h in range(GDN_HEADS):
        sl = slice(h * GDN_DV, (h + 1) * GDN_DV)
        parts.append(_rms(o[:, sl], gnw_ref[...]) * _silu(z_ref[:, sl]))
    gdn_o = jnp.concatenate(parts, axis=-1)
    a = _dot(na_ref[...].astype(BF16), wna_ref[...])
    b = _dot(gdn_o.astype(BF16), wgdn_ref[...])
    gate = jax.nn.sigmoid(gate_ref[...])
    pre = gate[:, :D_MODEL] * a + gate[:, D_MODEL:] * b
    mix = _dot(pre.astype(BF16), wout_ref[...])
    x1 = x_ref[...] + mod[:, 2 * D_MODEL:3 * D_MODEL] * _rms(mix, gpost_ref[...])
    x1_ref[...] = x1
    h2 = _rms(x1, gpre_ref[...]) * (1.0 + mod[:, 4 * D_MODEL:5 * D_MODEL]) + mod[:, 3 * D_MODEL:4 * D_MODEL]
    h2_ref[...] = h2
    packed = _pack_bf16_pairs(h2)
    for c in range(PACK_ROWS):
        h2p_ref[pl.ds(c, h2.shape[0], stride=PACK_ROWS), :] = packed[:, c * LANES:(c + 1) * LANES]


def _mix(x, na_o, o_f, o_b, z, gate, mods3, gnw, gpost, gpre, wna, wgdn, wout, row_of_tile):
    n = x.shape[0]
    tm = TOK_TILE
    row = lambda w: pl.BlockSpec((tm, w), lambda i: (i, 0))
    const = lambda r, c: pl.BlockSpec((r, c), lambda i: (0, 0))
    return pl.pallas_call(
        _mix_kernel,
        grid=(n // tm,),
        in_specs=[row(D_MODEL), row(NA_WIDTH), row(GDN_V_WIDTH), row(GDN_V_WIDTH),
                  row(GDN_V_WIDTH), row(2 * D_MODEL),
                  pl.BlockSpec((1, 1, 6 * D_MODEL), lambda i: (row_of_tile(i), 0, 0)),
                  const(1, GDN_DV), const(1, D_MODEL), const(1, D_MODEL),
                  const(NA_WIDTH, D_MODEL), const(GDN_V_WIDTH, D_MODEL), const(D_MODEL, D_MODEL)],
        out_specs=[row(D_MODEL), row(D_MODEL), pl.BlockSpec((tm * PACK_ROWS, LANES), lambda i: (i, 0))],
        out_shape=[jax.ShapeDtypeStruct((n, D_MODEL), F32),
                   jax.ShapeDtypeStruct((n, D_MODEL), F32),
                   jax.ShapeDtypeStruct((n * PACK_ROWS, LANES), jnp.uint32)],
        compiler_params=_cparams(("parallel",)),
        name="mix",
    )(x, na_o, o_f, o_b, z, gate, mods3, gnw.reshape(1, GDN_DV), gpost.reshape(1, D_MODEL),
      gpre.reshape(1, D_MODEL), wna, wgdn, wout)


def _router_kernel(h_ref, wh_ref, wl_ref, b_ref, e_ref, wt_ref, cnt_ref, acc_ref):
    i = pl.program_id(0)
    tm = h_ref.shape[0]
    ne = N_EXPERTS
    per = ne // N_GROUPS
    hh, hl = _split_bf16(h_ref[...])
    wh = wh_ref[...]
    logits = _dot_nt(wh, hh) + _dot_nt(wl_ref[...], hh) + _dot_nt(wh, hl)
    scores = jax.nn.sigmoid(logits)
    biased = scores + jnp.concatenate([b_ref[...]] * (tm // LANES), axis=1)

    def first_max(x):
        n = x.shape[0]
        iota = lax.broadcasted_iota(jnp.int32, x.shape, 0).astype(F32)
        m = jnp.max(x, axis=0, keepdims=True)
        idx = jnp.min(jnp.where(x == m, iota, float(n)), axis=0, keepdims=True)
        return m, idx, iota

    gs_rows = []
    for g in range(N_GROUPS):
        bg = biased[g * per:(g + 1) * per]
        m1, i1, iota = first_max(bg)
        m2 = jnp.max(jnp.where(iota == i1, NEG_INF, bg), axis=0, keepdims=True)
        gs_rows.append(m1 + m2)
    gs = jnp.concatenate(gs_rows, axis=0)
    gsel = jnp.zeros(gs.shape, F32)
    for _ in range(TOPK_GROUPS):
        _, gi, iota = first_max(gs)
        hit = iota == gi
        gs = jnp.where(hit, NEG_INF, gs)
        gsel = jnp.where(hit, 1.0, gsel)
    masked = jnp.concatenate(
        [jnp.where(gsel[g:g + 1] > 0.0, biased[g * per:(g + 1) * per], NEG_INF) for g in range(N_GROUPS)], axis=0)
    onehot = jnp.zeros((ne, tm), F32)
    e_rows, w_rows = [], []
    for _ in range(TOP_K):
        _, ei, iota = first_max(masked)
        hit = iota == ei
        masked = jnp.where(hit, NEG_INF, masked)
        w_rows.append(jnp.sum(jnp.where(hit, scores, 0.0), axis=0, keepdims=True))
        e_rows.append(ei)
        onehot = onehot + hit.astype(F32)
    w_out = jnp.concatenate(w_rows, axis=0)
    e_ref[...] = jnp.concatenate(e_rows, axis=0).astype(jnp.int32)
    wt_ref[...] = w_out / jnp.sum(w_out, axis=0, keepdims=True) * ROUTED_SCALE

    @pl.when(i == 0)
    def _():
        acc_ref[...] = jnp.zeros_like(acc_ref)

    acc_ref[...] += sum(onehot[:, c * LANES:(c + 1) * LANES] for c in range(tm // LANES))

    @pl.when(i == pl.num_programs(0) - 1)
    def _():
        cnt_ref[...] = jnp.broadcast_to(jnp.sum(acc_ref[...], axis=1, keepdims=True), cnt_ref.shape)


def _router(h2, w_rt_hi, w_rt_lo, bias_col):
    n = h2.shape[0]
    tm = TOK_TILE
    return pl.pallas_call(
        _router_kernel,
        grid=(n // tm,),
        in_specs=[pl.BlockSpec((tm, D_MODEL), lambda i: (i, 0)),
                  pl.BlockSpec((N_EXPERTS, D_MODEL), lambda i: (0, 0)),
                  pl.BlockSpec((N_EXPERTS, D_MODEL), lambda i: (0, 0)),
                  pl.BlockSpec((N_EXPERTS, LANES), lambda i: (0, 0))],
        out_specs=[pl.BlockSpec((TOP_K, tm), lambda i: (0, i)),
                   pl.BlockSpec((TOP_K, tm), lambda i: (0, i)),
                   pl.BlockSpec((N_EXPERTS, LANES), lambda i: (0, 0))],
        out_shape=[jax.ShapeDtypeStruct((TOP_K, n), jnp.int32),
                   jax.ShapeDtypeStruct((TOP_K, n), F32),
                   jax.ShapeDtypeStruct((N_EXPERTS, LANES), F32)],
        scratch_shapes=[pltpu.VMEM((N_EXPERTS, LANES), F32)],
        compiler_params=_cparams(("arbitrary",)),
        name="router",
    )(h2, w_rt_hi, w_rt_lo, bias_col)


_ST_SLOT, _ST_READY, _ST_PEND_BASE, _ST_PEND_VALID = range(4)


def _moe_kernel(tok_ref, seg_ref, cnt_ref, h2p_hbm, w2d_ref, wg_ref, wu_ref, wd_ref, out_hbm,
                h2p_ref, acc_ref, xs_ref, ye_ref, st_ref, sem):
    e = pl.program_id(0)
    n_exp = pl.num_programs(0)
    half = D_MODEL // 2
    xw, yw = PACK_ROWS, ACC_ROWS
    n_tok = out_hbm.shape[0] // yw
    grp = MOE_ROW_GROUP

    def gather_tile(base, n_valid, slot):
        for r in range(MOE_ROWS):
            tok = jnp.where(r < n_valid, tok_ref[base + r], n_tok)
            xs_ref[slot, xw * r:xw * (r + 1), :] = h2p_ref[pl.ds(pl.multiple_of(tok * xw, xw), xw), :]

    def scatter_tile(base, n_valid, slot):
        for g in range(MOE_ROWS // grp):
            toks = [pl.multiple_of(
                jnp.where(g * grp + j < n_valid, tok_ref[base + g * grp + j], n_tok) * yw, yw)
                for j in range(grp)]
            rows = [acc_ref[pl.ds(toks[j], yw), :] + ye_ref[slot, yw * (g * grp + j):yw * (g * grp + j + 1), :]
                    for j in range(grp)]
            for j in range(grp):
                acc_ref[pl.ds(toks[j], yw), :] = rows[j]

    seg = seg_ref[e]
    cnt = cnt_ref[e]
    n_tiles = (cnt + MOE_ROWS - 1) // MOE_ROWS

    @pl.when(e == 0)
    def _():
        cp = pltpu.make_async_copy(h2p_hbm, h2p_ref.at[pl.ds(0, n_tok * xw)], sem.at[0])
        cp.start()
        acc_ref[...] = jnp.zeros_like(acc_ref)
        ye_ref[...] = jnp.zeros_like(ye_ref)
        h2p_ref[pl.ds(n_tok * xw, grp * xw), :] = jnp.zeros((grp * xw, LANES), jnp.uint32)
        st_ref[_ST_SLOT] = 0
        st_ref[_ST_READY] = -1
        st_ref[_ST_PEND_BASE] = 0
        st_ref[_ST_PEND_VALID] = 0
        cp.wait()

    @pl.when((n_tiles > 0) & (st_ref[_ST_READY] != e))
    def _():
        gather_tile(seg, jnp.minimum(cnt, MOE_ROWS), st_ref[_ST_SLOT])

    def chunk(w_ref, c):
        return jnp.concatenate([w_ref[0, LANES * c:LANES * (c + 1), :],
                                w_ref[0, half + LANES * c:half + LANES * (c + 1), :]], axis=0).astype(BF16)

    lane = lax.broadcasted_iota(jnp.int32, (8, LANES), 1)
    nxt_e = jnp.minimum(e + 1, n_exp - 1)
    nxt_seg = seg_ref[nxt_e]
    nxt_cnt = jnp.where(e + 1 < n_exp, cnt_ref[nxt_e], 0)

    def tile_body(t, carry):
        slot = st_ref[_ST_SLOT]
        base = seg + t * MOE_ROWS
        last = t + 1 == n_tiles
        xk = [jnp.concatenate(_unpack_bf16_pairs(xs_ref[slot, pl.ds(c, MOE_ROWS, stride=xw), :]), axis=1)
              for c in range(xw)]
        g_base = jnp.where(last, nxt_seg, base + MOE_ROWS)
        g_valid = jnp.minimum(MOE_ROWS, jnp.where(last, nxt_cnt, cnt - (t + 1) * MOE_ROWS))
        gather_tile(g_base, g_valid, 1 - slot)
        scatter_tile(st_ref[_ST_PEND_BASE], st_ref[_ST_PEND_VALID], 1 - slot)
        hg = sum(_dot(xk[c], chunk(wg_ref, c)) for c in range(xw))
        hu = sum(_dot(xk[c], chunk(wu_ref, c)) for c in range(xw))
        act = (_silu(hg) * hu).astype(BF16)
        q = base // LANES
        sh = base % LANES
        rot = (LANES - sh) % LANES
        row_a = pltpu.roll(jnp.broadcast_to(w2d_ref[pl.ds(q, 1), :], (8, LANES)), rot, 1)
        row_b = pltpu.roll(jnp.broadcast_to(w2d_ref[pl.ds(q + 1, 1), :], (8, LANES)), rot, 1)
        w_row = jnp.where(lane + sh < LANES, row_a, row_b)[0:1, :]
        w_col = jnp.broadcast_to(w_row, (MOE_ROWS, LANES)).T
        ye = _dot(act, wd_ref[0].astype(BF16))
        for c in range(yw):
            ye_ref[slot, pl.ds(c, MOE_ROWS, stride=yw), :] = ye[:, LANES * c:LANES * (c + 1)] * w_col
        st_ref[_ST_PEND_BASE] = base
        st_ref[_ST_PEND_VALID] = jnp.minimum(MOE_ROWS, cnt - t * MOE_ROWS)
        st_ref[_ST_SLOT] = 1 - slot
        st_ref[_ST_READY] = jnp.where(last, e + 1, e)
        return carry

    lax.fori_loop(0, n_tiles, tile_body, 0)

    @pl.when(e == n_exp - 1)
    def _():
        scatter_tile(st_ref[_ST_PEND_BASE], st_ref[_ST_PEND_VALID], 1 - st_ref[_ST_SLOT])
        cp = pltpu.make_async_copy(acc_ref.at[pl.ds(0, n_tok * yw)], out_hbm, sem.at[1])
        cp.start()
        cp.wait()


def _moe(tok_sorted, seg_start, seg_count, h2p, w2d, wg, wu, wd):
    xw, yw = PACK_ROWS, ACC_ROWS
    n = h2p.shape[0] // xw
    grid_spec = pltpu.PrefetchScalarGridSpec(
        num_scalar_prefetch=3,
        grid=(N_EXPERTS,),
        in_specs=[pl.BlockSpec(memory_space=pl.ANY),
                  pl.BlockSpec(w2d.shape, lambda e, *_: (0, 0)),
                  pl.BlockSpec((1, D_MODEL, EXPERT_DIM), lambda e, *_: (e, 0, 0)),
                  pl.BlockSpec((1, D_MODEL, EXPERT_DIM), lambda e, *_: (e, 0, 0)),
                  pl.BlockSpec((1, EXPERT_DIM, D_MODEL), lambda e, *_: (e, 0, 0))],
        out_specs=pl.BlockSpec(memory_space=pl.ANY),
        scratch_shapes=[pltpu.VMEM(((n + MOE_ROW_GROUP) * xw, LANES), jnp.uint32),
                        pltpu.VMEM(((n + MOE_ROW_GROUP) * yw, LANES), F32),
                        pltpu.VMEM((2, MOE_ROWS * xw, LANES), jnp.uint32),
                        pltpu.VMEM((2, MOE_ROWS * yw, LANES), F32),
                        pltpu.SMEM((4,), jnp.int32),
                        pltpu.SemaphoreType.DMA((2,))],
    )
    return pl.pallas_call(
        _moe_kernel,
        grid_spec=grid_spec,
        out_shape=jax.ShapeDtypeStruct((n * yw, LANES), F32),
        compiler_params=_cparams(("arbitrary",), vmem=60 * 1024 * 1024),
        name="moe",
    )(tok_sorted, seg_start, seg_count, h2p, w2d, wg, wu, wd)


def _moe_routed(top_e, top_w, counts, h2p, wg, wu, wd):
    tok_sorted, w2d, seg_start, seg_count = _moe_dispatch_plan(top_e, top_w, counts, h2p.shape[0] // PACK_ROWS)
    return _moe(tok_sorted, seg_start, seg_count, h2p, w2d, wg, wu, wd)


def _moe_dispatch_plan(top_e, top_w, counts, n):
    flat_e = top_e.reshape(-1)
    flat_t = jnp.arange(n * TOP_K, dtype=jnp.int32) % n
    flat_w = top_w.reshape(-1)
    _, tok_sorted, w_sorted = lax.sort((flat_e, flat_t, flat_w), num_keys=1, is_stable=True)
    total = flat_e.shape[0]
    tok_sorted = jnp.concatenate([tok_sorted, jnp.full((MOE_ROWS,), n, jnp.int32)])
    table_rows = -(-(total // LANES + 2) // 8) * 8
    w2d = jnp.concatenate([w_sorted, jnp.zeros((table_rows * LANES - total,), F32)]).reshape(table_rows, LANES)
    cnt = counts[:, 0].astype(jnp.int32)
    return tok_sorted, w2d, jnp.cumsum(cnt) - cnt, cnt


def _regroup_rows(w):
    half = D_MODEL // 2
    return jnp.concatenate([w[r0 + LANES * c:r0 + LANES * (c + 1)]
                            for c in range(PACK_ROWS) for r0 in (0, half)], axis=0)


def _final_kernel(x1_ref, h2p_ref, r_ref, mod_ref, g_ref, wg_ref, wu_ref, wd_ref, y_ref):
    tm = x1_ref.shape[0]
    kc = 2 * LANES
    mod = mod_ref[0]
    xk = [jnp.concatenate(_unpack_bf16_pairs(h2p_ref[pl.ds(c, tm, stride=PACK_ROWS), :]), axis=1)
          for c in range(PACK_ROWS)]
    hg = sum(_dot(xk[c], wg_ref[kc * c:kc * (c + 1), :]) for c in range(PACK_ROWS))
    hu = sum(_dot(xk[c], wu_ref[kc * c:kc * (c + 1), :]) for c in range(PACK_ROWS))
    shared = _dot((_silu(hg) * hu).astype(BF16), wd_ref[...])
    routed = jnp.concatenate([r_ref[pl.ds(c, tm, stride=ACC_ROWS), :] for c in range(ACC_ROWS)], axis=1)
    ffn = routed + shared
    y_ref[...] = x1_ref[...] + mod[:, 5 * D_MODEL:6 * D_MODEL] * _rms(ffn, g_ref[...])


def _final(x1, h2p, routed, mods3, g, wg, wu, wd, row_of_tile):
    n = x1.shape[0]
    tm = TOK_TILE
    sd = wg.shape[1]
    row = lambda w: pl.BlockSpec((tm, w), lambda i: (i, 0))
    const = lambda r, c: pl.BlockSpec((r, c), lambda i: (0, 0))
    return pl.pallas_call(
        _final_kernel,
        grid=(n // tm,),
        in_specs=[row(D_MODEL), pl.BlockSpec((tm * PACK_ROWS, LANES), lambda i: (i, 0)),
                  pl.BlockSpec((tm * ACC_ROWS, LANES), lambda i: (i, 0)),
                  pl.BlockSpec((1, 1, 6 * D_MODEL), lambda i: (row_of_tile(i), 0, 0)),
                  const(1, D_MODEL), const(D_MODEL, sd), const(D_MODEL, sd), const(sd, D_MODEL)],
        out_specs=row(D_MODEL),
        out_shape=jax.ShapeDtypeStruct((n, D_MODEL), F32),
        compiler_params=_cparams(("parallel",)),
        name="final",
    )(x1, h2p, routed, mods3, g.reshape(1, D_MODEL), wg, wu, wd)


def _trunk(x3, mods3, row_of_tile, attend, s0, wts):
    b, t, _ = x3.shape
    n = b * t
    x = x3.reshape(n, D_MODEL)
    q, k, v, gdn, z, gate, ba = _premix(x, mods3, wts["g_pre_mix"], wts["w_cat"], row_of_tile)
    na_o = attend(q, k, v).reshape(n, NA_WIDTH)
    qkv = _gdn_conv(gdn.reshape(b, t, GDN_CONV_CH), wts["conv_w"])
    ba3 = ba.reshape(b, t, LANES)
    o_f, s_f = _gdn_chunks(qkv, ba3, wts["alog_row"], wts["dt_row"], s0, reverse=False)
    o_b, s_b = _gdn_chunks(qkv, ba3, wts["alog_row"], wts["dt_row"], s0, reverse=True)
    s_fin = jnp.stack([s_f, s_b], axis=1)
    x1, h2, h2p = _mix(x, na_o, o_f.reshape(n, GDN_V_WIDTH), o_b.reshape(n, GDN_V_WIDTH), z, gate, mods3,
                       wts["gdn_norm_w"],
                       wts["g_post_mix"], wts["g_pre_ffn"], wts["w_na_up"], wts["w_gdn_up"],
                       wts["w_out"], row_of_tile)
    top_e, top_w, counts = _router(h2, wts["w_rt_hi"], wts["w_rt_lo"], wts["router_bias_col"])
    routed = _moe_routed(top_e, top_w, counts, h2p, wts["w_exp_gate"], wts["w_exp_up"], wts["w_exp_down"])
    y = _final(x1, h2p, routed, mods3, wts["g_post_ffn"], wts["w_sh_gate"], wts["w_sh_up"],
               wts["w_sh_down"], row_of_tile)
    return y.reshape(b, t, D_MODEL), k, v, s_fin


def kernel(x_prompt, x_sample, cache_na_k, cache_na_v, state_gdn, c, c_ctx, w_ada, b_ada, g_pre_mix,
           g_post_mix, g_pre_ffn, g_post_ffn, w_in, conv_w, gdn_a_log, gdn_dt_bias, gdn_norm_w, na_rpb,
           w_na_up, w_gdn_up, w_out, w_router, router_bias, w_exp_gate, w_exp_up, w_exp_down, w_sh_gate,
           w_sh_up, w_sh_down):
    depth = w_ada.shape[0]
    bp, tp, _ = x_prompt.shape
    bs, ts, _ = x_sample.shape
    y_prompt, y_sample = x_prompt, x_sample
    zero_state = jnp.zeros((bp, 2, GDN_HEADS, GDN_DK, GDN_DV), F32)
    new_k, new_v, new_s = [], [], []
    for l in range(depth):
        cv = jnp.concatenate([c_ctx[None], c, jnp.zeros((8 - 1 - bs, D_MODEL), F32)], axis=0)
        mods3 = _ada(cv, w_ada[l], b_ada[l]).reshape(8, 1, 6 * D_MODEL)
        wl = w_in[l]
        w_cat = jnp.concatenate(
            [wl[:, :S_Z], wl[:, S_A:], wl[:, S_Z:S_A],
             jnp.zeros((D_MODEL, LANES - 4 * GDN_HEADS), F32)], axis=1).astype(BF16)
        pad = jnp.zeros((2 * GDN_HEADS,), F32)
        tail = jnp.zeros((LANES - 4 * GDN_HEADS,), F32)
        w_rt = w_router[l].astype(F32).T
        w_rt_hi = w_rt.astype(BF16)
        wts = dict(
            w_cat=w_cat, g_pre_mix=g_pre_mix[l], g_post_mix=g_post_mix[l], g_pre_ffn=g_pre_ffn[l],
            g_post_ffn=g_post_ffn[l], conv_w=conv_w[l], gdn_norm_w=gdn_norm_w[l],
            alog_row=jnp.concatenate([pad, gdn_a_log[l].reshape(-1), tail]).reshape(1, LANES),
            dt_row=jnp.concatenate([pad, gdn_dt_bias[l].reshape(-1), tail]).reshape(1, LANES),
            w_na_up=w_na_up[l].astype(BF16), w_gdn_up=w_gdn_up[l].astype(BF16),
            w_out=w_out[l].astype(BF16), w_rt_hi=w_rt_hi, w_rt_lo=(w_rt - w_rt_hi.astype(F32)).astype(BF16),
            router_bias_col=jnp.broadcast_to(router_bias[l].astype(F32)[:, None], (N_EXPERTS, LANES)),
            w_exp_gate=w_exp_gate[l], w_exp_up=w_exp_up[l], w_exp_down=w_exp_down[l],
            w_sh_gate=_regroup_rows(w_sh_gate[l]).astype(BF16), w_sh_up=_regroup_rows(w_sh_up[l]).astype(BF16),
            w_sh_down=w_sh_down[l].astype(BF16))

        def ctx_attend(q, k, v):
            return _ctx_attn(q.reshape(bp, tp, NA_WIDTH), k.reshape(bp, tp, NA_WIDTH),
                             v.reshape(bp, tp, NA_WIDTH))

        y_prompt, k_ctx, v_ctx, s_ctx = _trunk(y_prompt, mods3, lambda i: 0, ctx_attend, zero_state, wts)
        new_k.append(k_ctx.reshape(bp, tp, NA_HEADS, NA_HEAD_DIM))
        new_v.append(v_ctx.reshape(bp, tp, NA_HEADS, NA_HEAD_DIM))
        new_s.append(s_ctx)

        pair_tab, row_mask = _na_bias_tables(na_rpb[l], ts // GRID_W)
        ck = cache_na_k[:, l].reshape(bs, -1, NA_WIDTH)
        cvv = cache_na_v[:, l].reshape(bs, -1, NA_WIDTH)

        def na_attend(q, k, v):
            return _na_attn(q.reshape(bs, ts, NA_WIDTH), k.reshape(bs, ts, NA_WIDTH),
                            v.reshape(bs, ts, NA_WIDTH), ck, cvv, pair_tab, row_mask)

        tiles_per_seq = ts // TOK_TILE
        y_sample, _, _, _ = _trunk(y_sample, mods3, lambda i: 1 + i // tiles_per_seq, na_attend,
                                   state_gdn[:, l], wts)
    return (y_prompt, y_sample, jnp.stack(new_k, axis=1), jnp.stack(new_v, axis=1),
            jnp.stack(new_s, axis=1))
```

```python
import functools

import numpy as np
import jax
import jax.numpy as jnp
from jax import lax
from jax.experimental import pallas as pl
from jax.experimental.pallas import tpu as pltpu

F32 = jnp.float32
BF16 = jnp.bfloat16
HI = lax.Precision.HIGHEST

D_MODEL = 1024
GRID_W = 64
NA_HEADS = 8
NA_HEAD_DIM = 64
NA_WIDTH = NA_HEADS * NA_HEAD_DIM
NA_KR = 8
NA_KC = 16
GDN_HEADS = 4
GDN_DK = 128
GDN_DV = 128
GDN_QK_WIDTH = GDN_HEADS * GDN_DK
GDN_V_WIDTH = GDN_HEADS * GDN_DV
GDN_CONV_CH = 2 * GDN_QK_WIDTH + GDN_V_WIDTH
CONV_K = 5
CHUNK = 64
N_EXPERTS = 256
TOP_K = 8
N_GROUPS = 8
TOPK_GROUPS = 4
EXPERT_DIM = 256
ROUTED_SCALE = 2.5
EPS = 1e-6
S_NA = 3 * NA_WIDTH
S_GDN = S_NA + GDN_CONV_CH
S_Z = S_GDN + GDN_V_WIDTH
S_B = S_Z + 2 * GDN_HEADS
S_A = S_B + 2 * GDN_HEADS

LANES = 128
TOK_TILE = 256
NA_QROWS = 8
NA_SPAN = 16
MOE_ROWS = 128
PACK_ROWS = D_MODEL // 2 // LANES
ACC_ROWS = D_MODEL // LANES
MOE_ROW_GROUP = 16
GDN_CHUNKS_PER_STEP = 8
GDN_CONV_BLOCK_ROWS = 4096
VMEM_LIMIT = 56 * 1024 * 1024
NEG_INF = float("-inf")


def _cparams(sem, vmem=VMEM_LIMIT):
    return pltpu.CompilerParams(dimension_semantics=sem, vmem_limit_bytes=vmem)


def _silu(x):
    return x * jax.nn.sigmoid(x)


def _rms(x, g):
    return x * lax.rsqrt(jnp.mean(x * x, axis=-1, keepdims=True) + EPS) * g


def _dot(a, b):
    return jnp.dot(a, b, preferred_element_type=F32)


def _dot_nt(a, b, precision=None):
    return lax.dot_general(a, b, (((1,), (1,)), ((), ())), precision=precision,
                           preferred_element_type=F32)


def _ada_kernel(c_ref, w_ref, b_ref, o_ref):
    o_ref[...] = jnp.dot(_silu(c_ref[...]), w_ref[...], precision=HI,
                         preferred_element_type=F32) + b_ref[...]


def _ada(cv, w_ada, b_ada):
    n = w_ada.shape[1]
    tn = 512
    return pl.pallas_call(
        _ada_kernel,
        grid=(n // tn,),
        in_specs=[pl.BlockSpec((8, D_MODEL), lambda j: (0, 0)),
                  pl.BlockSpec((D_MODEL, tn), lambda j: (0, j)),
                  pl.BlockSpec((1, tn), lambda j: (0, j))],
        out_specs=pl.BlockSpec((8, tn), lambda j: (0, j)),
        out_shape=jax.ShapeDtypeStruct((8, n), F32),
        compiler_params=_cparams(("parallel",)),
        name="ada",
    )(cv, w_ada, b_ada.reshape(1, n))


_PM_WIDTHS = (NA_WIDTH, NA_WIDTH, NA_WIDTH, GDN_CONV_CH, GDN_V_WIDTH, 2 * D_MODEL, LANES)


def _premix_kernel(x_ref, mod_ref, g_ref, w_ref, *o_refs):
    mod = mod_ref[0]
    h = _rms(x_ref[...], g_ref[...]) * (1.0 + mod[:, D_MODEL:2 * D_MODEL]) + mod[:, 0:D_MODEL]
    hb = h.astype(BF16)
    off = 0
    for o_ref, wd in zip(o_refs, _PM_WIDTHS):
        for c0 in range(0, wd, 512):
            c1 = min(c0 + 512, wd)
            o_ref[:, c0:c1] = _dot(hb, w_ref[:, off + c0:off + c1])
        off += wd


def _premix(x, mods3, g, w_cat, row_of_tile):
    n = x.shape[0]
    wtot = w_cat.shape[1]
    tm = TOK_TILE
    return pl.pallas_call(
        _premix_kernel,
        grid=(n // tm,),
        in_specs=[pl.BlockSpec((tm, D_MODEL), lambda i: (i, 0)),
                  pl.BlockSpec((1, 1, 6 * D_MODEL), lambda i: (row_of_tile(i), 0, 0)),
                  pl.BlockSpec((1, D_MODEL), lambda i: (0, 0)),
                  pl.BlockSpec((D_MODEL, wtot), lambda i: (0, 0))],
        out_specs=[pl.BlockSpec((tm, wd), lambda i: (i, 0)) for wd in _PM_WIDTHS],
        out_shape=[jax.ShapeDtypeStruct((n, wd), F32) for wd in _PM_WIDTHS],
        compiler_params=_cparams(("parallel",)),
        name="premix",
    )(x, mods3, g.reshape(1, D_MODEL), w_cat)


def _softmax_rows(s):
    m = jnp.max(s, axis=-1, keepdims=True)
    p = jnp.exp(s - m)
    return p / jnp.sum(p, axis=-1, keepdims=True)


def _ctx_attn_kernel(q_ref, k_ref, v_ref, o_ref):
    scale = NA_HEAD_DIM ** -0.5
    for hp in range(NA_HEADS // 2):
        outs = []
        for h in (2 * hp, 2 * hp + 1):
            sl = slice(h * NA_HEAD_DIM, (h + 1) * NA_HEAD_DIM)
            q = q_ref[0, :, sl].astype(BF16)
            k = k_ref[0, :, sl].astype(BF16)
            v = v_ref[0, :, sl].astype(BF16)
            p = _softmax_rows(_dot_nt(q, k) * scale)
            outs.append(_dot(p.astype(BF16), v))
        o_ref[0, :, hp * LANES:(hp + 1) * LANES] = jnp.concatenate(outs, axis=-1)


def _ctx_attn(q, k, v):
    b, t, w = q.shape
    spec = pl.BlockSpec((1, t, w), lambda i: (i, 0, 0))
    return pl.pallas_call(
        _ctx_attn_kernel,
        grid=(b,),
        in_specs=[spec, spec, spec],
        out_specs=spec,
        out_shape=jax.ShapeDtypeStruct((b, t, w), F32),
        compiler_params=_cparams(("parallel",)),
        name="ctx_attn",
    )(q, k, v)


def _na_span_base(j, rows):
    return np.clip(NA_QROWS * j - NA_KR // 2, 0, rows - NA_SPAN)


NA_DR_PAD = NA_QROWS


def _na_bias_tables(rpb, rows):
    col = np.arange(GRID_W)
    dcm = np.clip(col[None, :] - col[:, None], -(NA_KC - 1), NA_KC - 1) + (NA_KC - 1)
    onehot = (dcm[None] == np.arange(2 * NA_KC - 1)[:, None, None]).astype(np.float32)
    tab = jnp.einsum('hrd,dqk->hrqk', rpb.astype(F32), jnp.asarray(onehot), precision=HI)
    col_start = np.clip(col - NA_KC // 2, 0, GRID_W - NA_KC)
    col_in = (col[None, :] >= col_start[:, None]) & (col[None, :] < col_start[:, None] + NA_KC)
    tab = jnp.where(jnp.asarray(col_in)[None, None], tab, NEG_INF)
    n_dr = 2 * NA_KR - 1
    n_side = NA_DR_PAD + NA_SPAN - n_dr + 1
    blank_lo = jnp.full((NA_HEADS, NA_DR_PAD, GRID_W, GRID_W), NEG_INF, F32)
    blank_hi = jnp.full((NA_HEADS, n_side, GRID_W, GRID_W), NEG_INF, F32)
    padded = jnp.concatenate([blank_lo, tab, blank_hi], axis=1)
    pair_tab = jnp.concatenate([padded[:, :-1], padded[:, 1:]], axis=-1)
    nblk = rows // NA_QROWS
    mask = np.full((3, NA_QROWS, NA_SPAN), NEG_INF, np.float32)
    for p, j in enumerate((0, 1, nblk - 1)):
        base = _na_span_base(j, rows)
        for ri in range(NA_QROWS):
            r = NA_QROWS * j + ri
            rs = np.clip(r - NA_KR // 2, 0, rows - NA_KR)
            for ki in range(NA_SPAN):
                if rs <= base + ki < rs + NA_KR:
                    mask[p, ri, ki] = 0.0
    row_mask = jnp.asarray(np.repeat(mask, GRID_W, axis=2))
    return pair_tab, row_mask


def _na_attn_kernel(q_ref, k_ref, v_ref, ck_ref, cv_ref, tab_ref, mask_ref, o_ref, *, rows):
    j = pl.program_id(2)
    scale = NA_HEAD_DIM ** -0.5
    base = jnp.clip(NA_QROWS * j - NA_KR // 2, 0, rows - NA_SPAN)
    start = pl.multiple_of(base * GRID_W, GRID_W)
    span = NA_SPAN * GRID_W
    q = q_ref[0]
    kl = k_ref[0, pl.ds(start, span), :].astype(BF16)
    vl = v_ref[0, pl.ds(start, span), :].astype(BF16)
    ck = ck_ref[0].astype(BF16)
    cv = cv_ref[0].astype(BF16)
    first = lax.broadcasted_iota(jnp.int32, q.shape, 1) < NA_HEAD_DIM
    off = base - NA_QROWS * j + (NA_KR - 1) + NA_DR_PAD
    outs = []
    for hh in range(2):
        qm = jnp.where(first if hh == 0 else ~first, q, 0.0).astype(BF16)
        s_raw = _dot_nt(qm, kl) * scale
        blocks = []
        for ri in range(NA_QROWS):
            rws = slice(ri * GRID_W, (ri + 1) * GRID_W)
            pieces = [s_raw[rws, m * LANES:(m + 1) * LANES] + tab_ref[hh, off + 2 * m - ri]
                      for m in range(NA_SPAN // 2)]
            blocks.append(jnp.concatenate(pieces, axis=1) + mask_ref[0, ri:ri + 1, :])
        s_loc = jnp.concatenate(blocks, axis=0)
        s_ctx = _dot_nt(qm, ck) * scale
        m = jnp.maximum(jnp.max(s_loc, axis=-1, keepdims=True), jnp.max(s_ctx, axis=-1, keepdims=True))
        p_loc = jnp.exp(s_loc - m)
        p_ctx = jnp.exp(s_ctx - m)
        den = jnp.sum(p_loc, axis=-1, keepdims=True) + jnp.sum(p_ctx, axis=-1, keepdims=True)
        p_loc = (p_loc / den).astype(BF16)
        p_ctx = (p_ctx / den).astype(BF16)
        outs.append(_dot(p_loc, vl) + _dot(p_ctx, cv))
    o_ref[0] = jnp.where(first, outs[0], outs[1])


def _na_attn(q, k, v, ck, cv, pair_tab, row_mask):
    b, n, w = q.shape
    p = ck.shape[1]
    rows = n // GRID_W
    nblk = rows // NA_QROWS
    qb = NA_QROWS * GRID_W

    def pattern(j):
        return jnp.where(j == 0, 0, jnp.where(j == nblk - 1, 2, 1))

    full = pl.BlockSpec((1, n, LANES), lambda bi, hp, j: (bi, 0, hp))
    ctx = pl.BlockSpec((1, p, LANES), lambda bi, hp, j: (bi, 0, hp))
    blk = pl.BlockSpec((1, qb, LANES), lambda bi, hp, j: (bi, j, hp))
    return pl.pallas_call(
        functools.partial(_na_attn_kernel, rows=rows),
        grid=(b, w // LANES, nblk),
        in_specs=[blk, full, full, ctx, ctx,
                  pl.BlockSpec((2,) + pair_tab.shape[1:], lambda bi, hp, j: (hp, 0, 0, 0)),
                  pl.BlockSpec((1,) + row_mask.shape[1:], lambda bi, hp, j: (pattern(j), 0, 0))],
        out_specs=blk,
        out_shape=jax.ShapeDtypeStruct((b, n, w), F32),
        compiler_params=_cparams(("parallel", "parallel", "arbitrary")),
        name="na_attn",
    )(q, k, v, ck, cv, pair_tab, row_mask)


def _gdn_conv_kernel(x_ref, w_ref, o_ref, *, groups):
    c = pl.program_id(1)
    t = x_ref.shape[1]
    row = lax.broadcasted_iota(jnp.int32, (t, LANES), 0)
    n_qk = 2 * GDN_HEADS
    for i in range(groups):
        lanes = slice(i * LANES, (i + 1) * LANES)
        x = x_ref[0, :, lanes]
        y = jnp.zeros_like(x)
        for jj in range(CONV_K):
            o = jj - CONV_K // 2
            xs = x if o == 0 else pltpu.roll(x, (-o) % t, 0)
            xs = jnp.where((row + o >= 0) & (row + o < t), xs, 0.0)
            y = y + xs * w_ref[jj:jj + 1, lanes]
        y = _silu(y)
        nrm = lax.rsqrt(jnp.sum(y * y, axis=-1, keepdims=True) + EPS)
        o_ref[0, i] = jnp.where(c * groups + i < n_qk, y * nrm, y)


def _gdn_conv(x, conv_w):
    b, t, ch = x.shape
    nc = ch // LANES
    groups = max(g for g in range(1, nc + 1)
                 if nc % g == 0 and (g == 1 or g * t <= GDN_CONV_BLOCK_ROWS))
    return pl.pallas_call(
        functools.partial(_gdn_conv_kernel, groups=groups),
        grid=(b, nc // groups),
        in_specs=[pl.BlockSpec((1, t, groups * LANES), lambda bi, c: (bi, 0, c)),
                  pl.BlockSpec((CONV_K, groups * LANES), lambda bi, c: (0, c))],
        out_specs=pl.BlockSpec((1, groups, t, LANES), lambda bi, c: (bi, c, 0, 0)),
        out_shape=jax.ShapeDtypeStruct((b, nc, t, LANES), F32),
        compiler_params=_cparams(("parallel", "parallel")),
        name="gdn_conv",
    )(x, conv_w)


def _bdot(a, b):
    return jnp.dot(a.astype(BF16), b.astype(BF16), preferred_element_type=F32)


def _split_bf16(x):
    hi = x.astype(BF16)
    return hi, (x - hi.astype(F32)).astype(BF16)


def _dot3(a, b):
    m = a.shape[0]
    ah, al = _split_bf16(a)
    bh, bl = _split_bf16(b)
    top = _dot(jnp.concatenate([ah, al], axis=0), bh)
    return top[:m] + top[m:] + _dot(ah, bl)


def _gdn_chunk_kernel(qkv_ref, ba_ref, alog_ref, dt_ref, s0_ref, o_ref, sfin_ref, s_ref, *, reverse, cb):
    c = pl.program_id(1)
    nh = GDN_HEADS

    @pl.when(c == 0)
    def _():
        s_ref[...] = s0_ref[0, 0]

    ii = lax.broadcasted_iota(jnp.int32, (CHUNK, CHUNK), 0)
    jj = lax.broadcasted_iota(jnp.int32, (CHUNK, CHUNK), 1)
    lag = (jj - ii) if reverse else (ii - jj)
    incl = lag >= 0
    strict = lag > 0
    eye = (ii == jj).astype(F32)
    tri = incl.astype(F32)
    bcol = nh if reverse else 0
    gcol0 = (3 if reverse else 2) * nh
    sub8 = lax.broadcasted_iota(jnp.int32, (8, LANES), 0)
    lane8 = lax.broadcasted_iota(jnp.int32, (8, LANES), 1)
    sel8 = (lane8 == gcol0 + sub8).astype(F32)
    units = [(ci, h) for ci in range(cb) for h in range(nh)]
    gc_alls, beta_alls, grow8s = [], [], []
    for ci in range(cb):
        ba = ba_ref[0, ci * CHUNK:(ci + 1) * CHUNK, :]
        z = ba + dt_ref[...]
        softplus = jnp.maximum(z, 0.0) + jnp.log1p(jnp.exp(-jnp.abs(z)))
        g_all = -jnp.exp(alog_ref[...]) * softplus
        gc_all = jnp.dot(tri, g_all, precision=HI, preferred_element_type=F32)
        gc_alls.append(gc_all)
        beta_alls.append(jax.nn.sigmoid(ba))
        grow8s.append(_dot_nt(sel8, gc_all, precision=HI))
    gcol = [gc_alls[ci][:, gcol0 + h:gcol0 + h + 1] for ci, h in units]
    beta = [beta_alls[ci][:, bcol + h:bcol + h + 1] for ci, h in units]
    rows = [slice(ci * CHUNK, (ci + 1) * CHUNK) for ci, _ in units]
    k = [qkv_ref[0, nh + h, rows[u], :] for u, (_, h) in enumerate(units)]
    kb = [k[u] * beta[u] for u in range(len(units))]
    q = [qkv_ref[0, h, rows[u], :] * (GDN_DK ** -0.5) for u, (_, h) in enumerate(units)]
    kq = [_dot_nt(jnp.concatenate([kb[u], q[u]], axis=0).astype(BF16), k[u].astype(BF16))
          for u in range(len(units))]
    decay = [jnp.where(incl, jnp.exp(jnp.where(incl, gcol[u] - grow8s[ci][h:h + 1, :], 0.0)), 0.0)
             for u, (ci, h) in enumerate(units)]
    intra = [jnp.where(incl, kq[u][CHUNK:] * decay[u], 0.0) for u in range(len(units))]
    wide = nh * CHUNK
    bd_mask = (lax.broadcasted_iota(jnp.int32, (wide, wide), 0) // CHUNK
               == lax.broadcasted_iota(jnp.int32, (wide, wide), 1) // CHUNK)

    def block_diag(x):
        return jnp.where(bd_mask, jnp.concatenate([x] * nh, axis=0), jnp.zeros((), x.dtype))

    def dot3_bd(a, b):
        m = a.shape[0]
        ah, al = _split_bf16(a)
        bh, bl = _split_bf16(b)
        top = _dot(jnp.concatenate([ah, al], axis=0), block_diag(bh))
        return top[:m] + top[m:] + _dot(ah, block_diag(bl))

    eye_w = jnp.concatenate([eye] * nh, axis=1)
    pw = [jnp.concatenate([-jnp.where(strict, kq[ci * nh + h][:CHUNK] * decay[ci * nh + h], 0.0)
                           for h in range(nh)], axis=1) for ci in range(cb)]
    tmat = [eye_w + p for p in pw]
    pw = [dot3_bd(p, p) for p in pw]
    for _ in range(4):
        pt = [dot3_bd(jnp.concatenate([pw[ci], tmat[ci]], axis=0), pw[ci]) for ci in range(cb)]
        pw = [x[:CHUNK] for x in pt]
        tmat = [tmat[ci] + pt[ci][CHUNK:] for ci in range(cb)]
    tmat = [tmat[ci] + dot3_bd(tmat[ci], pw[ci]) for ci in range(cb)]
    eg = [jnp.exp(g) for g in gcol]
    uw_all = [_dot(block_diag(tmat[ci].astype(BF16)), jnp.concatenate(
        [jnp.concatenate([qkv_ref[0, 2 * nh + h, rows[ci * nh + h], :] * beta[ci * nh + h],
                          kb[ci * nh + h] * eg[ci * nh + h]], axis=1) for h in range(nh)],
        axis=0).astype(BF16)) for ci in range(cb)]
    uw = [uw_all[ci][h * CHUNK:(h + 1) * CHUNK] for ci, h in units]
    g_last = [g[0:1, :] if reverse else g[CHUNK - 1:CHUNK, :] for g in gcol]
    kd = [k[u] * jnp.exp(g_last[u] - gcol[u]) for u in range(len(units))]
    qe = [q[u] * eg[u] for u in range(len(units))]
    kd_uw = [lax.dot_general(kd[u].astype(BF16), uw[u].astype(BF16), (((0,), (0,)), ((), ())),
                             preferred_element_type=F32) for u in range(len(units))]
    in_uw = [_bdot(intra[u], uw[u]) for u in range(len(units))]
    lhs = [jnp.concatenate([kd_uw[u][:, GDN_DV:], qe[u] - in_uw[u][:, GDN_DV:]], axis=0).astype(BF16)
           for u in range(len(units))]
    s = [s_ref[h] for h in range(nh)]
    for ci in (reversed(range(cb)) if reverse else range(cb)):
        us = [ci * nh + h for h in range(nh)]
        r = [_dot(lhs[u], s[h].astype(BF16)) for h, u in enumerate(us)]
        for h, u in enumerate(us):
            o_ref[0, rows[u], h * GDN_DV:(h + 1) * GDN_DV] = r[h][GDN_DK:] + in_uw[u][:, :GDN_DV]
        s = [s[h] * jnp.exp(g_last[u]) - r[h][:GDN_DK] + kd_uw[u][:, :GDN_DV] for h, u in enumerate(us)]
    for h in range(nh):
        s_ref[h] = s[h]

    @pl.when(c == pl.num_programs(1) - 1)
    def _():
        sfin_ref[0] = s_ref[...]


def _gdn_chunks(qkv, ba, alog_row, dt_row, s0, reverse):
    b, _, t, _ = qkv.shape
    cb = min(GDN_CHUNKS_PER_STEP, t // CHUNK)
    rows = cb * CHUNK
    n = t // rows
    d = 1 if reverse else 0

    def blk(c):
        return n - 1 - c if reverse else c

    return pl.pallas_call(
        functools.partial(_gdn_chunk_kernel, reverse=reverse, cb=cb),
        grid=(b, n),
        in_specs=[pl.BlockSpec((1, 3 * GDN_HEADS, rows, LANES), lambda bi, c: (bi, 0, blk(c), 0)),
                  pl.BlockSpec((1, rows, LANES), lambda bi, c: (bi, blk(c), 0)),
                  pl.BlockSpec((1, LANES), lambda bi, c: (0, 0)),
                  pl.BlockSpec((1, LANES), lambda bi, c: (0, 0)),
                  pl.BlockSpec((1, 1, GDN_HEADS, GDN_DK, GDN_DV), lambda bi, c: (bi, d, 0, 0, 0))],
        out_specs=[pl.BlockSpec((1, rows, GDN_V_WIDTH), lambda bi, c: (bi, blk(c), 0)),
                   pl.BlockSpec((1, GDN_HEADS, GDN_DK, GDN_DV), lambda bi, c: (bi, 0, 0, 0))],
        out_shape=[jax.ShapeDtypeStruct((b, t, GDN_V_WIDTH), F32),
                   jax.ShapeDtypeStruct((b, GDN_HEADS, GDN_DK, GDN_DV), F32)],
        scratch_shapes=[pltpu.VMEM((GDN_HEADS, GDN_DK, GDN_DV), F32)],
        compiler_params=_cparams(("parallel", "arbitrary")),
        name="gdn_bwd" if reverse else "gdn_fwd",
    )(qkv, ba, alog_row, dt_row, s0)


def _pack_bf16_pairs(h):
    half = D_MODEL // 2
    lo = pltpu.bitcast(h[:, :half].astype(BF16).astype(F32), jnp.uint32)
    hi = pltpu.bitcast(h[:, half:].astype(BF16).astype(F32), jnp.uint32)
    return (hi & jnp.uint32(0xFFFF0000)) | (lo >> 16)


def _unpack_bf16_pairs(w):
    lo = pltpu.bitcast(w << 16, F32).astype(BF16)
    hi = pltpu.bitcast(w & jnp.uint32(0xFFFF0000), F32).astype(BF16)
    return lo, hi


def _mix_kernel(x_ref, na_ref, of_ref, ob_ref, z_ref, gate_ref, mod_ref, gnw_ref, gpost_ref, gpre_ref,
                wna_ref, wgdn_ref, wout_ref, x1_ref, h2_ref, h2p_ref):
    mod = mod_ref[0]
    o = of_ref[...] + ob_ref[...]
    parts = []
    for h in range(GDN_HEADS):
        sl = slice(h * GDN_DV, (h + 1) * GDN_DV)
        parts.append(_rms(o[:, sl], gnw_ref[...]) * _silu(z_ref[:, sl]))
    gdn_o = jnp.concatenate(parts, axis=-1)
    a = _dot(na_ref[...].astype(BF16), wna_ref[...])
    b = _dot(gdn_o.astype(BF16), wgdn_ref[...])
    gate = jax.nn.sigmoid(gate_ref[...])
    pre = gate[:, :D_MODEL] * a + gate[:, D_MODEL:] * b
    mix = _dot(pre.astype(BF16), wout_ref[...])
    x1 = x_ref[...] + mod[:, 2 * D_MODEL:3 * D_MODEL] * _rms(mix, gpost_ref[...])
    x1_ref[...] = x1
    h2 = _rms(x1, gpre_ref[...]) * (1.0 + mod[:, 4 * D_MODEL:5 * D_MODEL]) + mod[:, 3 * D_MODEL:4 * D_MODEL]
    h2_ref[...] = h2
    packed = _pack_bf16_pairs(h2)
    for c in range(PACK_ROWS):
        h2p_ref[pl.ds(c, h2.shape[0], stride=PACK_ROWS), :] = packed[:, c * LANES:(c + 1) * LANES]


def _mix(x, na_o, o_f, o_b, z, gate, mods3, gnw, gpost, gpre, wna, wgdn, wout, row_of_tile):
    n = x.shape[0]
    tm = TOK_TILE
    row = lambda w: pl.BlockSpec((tm, w), lambda i: (i, 0))
    const = lambda r, c: pl.BlockSpec((r, c), lambda i: (0, 0))
    return pl.pallas_call(
        _mix_kernel,
        grid=(n // tm,),
        in_specs=[row(D_MODEL), row(NA_WIDTH), row(GDN_V_WIDTH), row(GDN_V_WIDTH),
                  row(GDN_V_WIDTH), row(2 * D_MODEL),
                  pl.BlockSpec((1, 1, 6 * D_MODEL), lambda i: (row_of_tile(i), 0, 0)),
                  const(1, GDN_DV), const(1, D_MODEL), const(1, D_MODEL),
                  const(NA_WIDTH, D_MODEL), const(GDN_V_WIDTH, D_MODEL), const(D_MODEL, D_MODEL)],
        out_specs=[row(D_MODEL), row(D_MODEL), pl.BlockSpec((tm * PACK_ROWS, LANES), lambda i: (i, 0))],
        out_shape=[jax.ShapeDtypeStruct((n, D_MODEL), F32),
                   jax.ShapeDtypeStruct((n, D_MODEL), F32),
                   jax.ShapeDtypeStruct((n * PACK_ROWS, LANES), jnp.uint32)],
        compiler_params=_cparams(("parallel",)),
        name="mix",
    )(x, na_o, o_f, o_b, z, gate, mods3, gnw.reshape(1, GDN_DV), gpost.reshape(1, D_MODEL),
      gpre.reshape(1, D_MODEL), wna, wgdn, wout)


def _router_kernel(h_ref, wh_ref, wl_ref, b_ref, e_ref, wt_ref, cnt_ref, acc_ref):
    i = pl.program_id(0)
    tm = h_ref.shape[0]
    ne = N_EXPERTS
    per = ne // N_GROUPS
    hh, hl = _split_bf16(h_ref[...])
    wh = wh_ref[...]
    logits = _dot_nt(wh, hh) + _dot_nt(wl_ref[...], hh) + _dot_nt(wh, hl)
    scores = jax.nn.sigmoid(logits)
    biased = scores + jnp.concatenate([b_ref[...]] * (tm // LANES), axis=1)

    def first_max(x):
        n = x.shape[0]
        iota = lax.broadcasted_iota(jnp.int32, x.shape, 0).astype(F32)
        m = jnp.max(x, axis=0, keepdims=True)
        idx = jnp.min(jnp.where(x == m, iota, float(n)), axis=0, keepdims=True)
        return m, idx, iota

    gs_rows = []
    for g in range(N_GROUPS):
        bg = biased[g * per:(g + 1) * per]
        m1, i1, iota = first_max(bg)
        m2 = jnp.max(jnp.where(iota == i1, NEG_INF, bg), axis=0, keepdims=True)
        gs_rows.append(m1 + m2)
    gs = jnp.concatenate(gs_rows, axis=0)
    gsel = jnp.zeros(gs.shape, F32)
    for _ in range(TOPK_GROUPS):
        _, gi, iota = first_max(gs)
        hit = iota == gi
        gs = jnp.where(hit, NEG_INF, gs)
        gsel = jnp.where(hit, 1.0, gsel)
    masked = jnp.concatenate(
        [jnp.where(gsel[g:g + 1] > 0.0, biased[g * per:(g + 1) * per], NEG_INF) for g in range(N_GROUPS)], axis=0)
    onehot = jnp.zeros((ne, tm), F32)
    e_rows, w_rows = [], []
    for _ in range(TOP_K):
        _, ei, iota = first_max(masked)
        hit = iota == ei
        masked = jnp.where(hit, NEG_INF, masked)
        w_rows.append(jnp.sum(jnp.where(hit, scores, 0.0), axis=0, keepdims=True))
        e_rows.append(ei)
        onehot = onehot + hit.astype(F32)
    w_out = jnp.concatenate(w_rows, axis=0)
    e_ref[...] = jnp.concatenate(e_rows, axis=0).astype(jnp.int32)
    wt_ref[...] = w_out / jnp.sum(w_out, axis=0, keepdims=True) * ROUTED_SCALE

    @pl.when(i == 0)
    def _():
        acc_ref[...] = jnp.zeros_like(acc_ref)

    acc_ref[...] += sum(onehot[:, c * LANES:(c + 1) * LANES] for c in range(tm // LANES))

    @pl.when(i == pl.num_programs(0) - 1)
    def _():
        cnt_ref[...] = jnp.broadcast_to(jnp.sum(acc_ref[...], axis=1, keepdims=True), cnt_ref.shape)


def _router(h2, w_rt_hi, w_rt_lo, bias_col):
    n = h2.shape[0]
    tm = TOK_TILE
    return pl.pallas_call(
        _router_kernel,
        grid=(n // tm,),
        in_specs=[pl.BlockSpec((tm, D_MODEL), lambda i: (i, 0)),
                  pl.BlockSpec((N_EXPERTS, D_MODEL), lambda i: (0, 0)),
                  pl.BlockSpec((N_EXPERTS, D_MODEL), lambda i: (0, 0)),
                  pl.BlockSpec((N_EXPERTS, LANES), lambda i: (0, 0))],
        out_specs=[pl.BlockSpec((TOP_K, tm), lambda i: (0, i)),
                   pl.BlockSpec((TOP_K, tm), lambda i: (0, i)),
                   pl.BlockSpec((N_EXPERTS, LANES), lambda i: (0, 0))],
        out_shape=[jax.ShapeDtypeStruct((TOP_K, n), jnp.int32),
                   jax.ShapeDtypeStruct((TOP_K, n), F32),
                   jax.ShapeDtypeStruct((N_EXPERTS, LANES), F32)],
        scratch_shapes=[pltpu.VMEM((N_EXPERTS, LANES), F32)],
        compiler_params=_cparams(("arbitrary",)),
        name="router",
    )(h2, w_rt_hi, w_rt_lo, bias_col)


_ST_SLOT, _ST_READY, _ST_PEND_BASE = range(3)


def _moe_kernel(tok_ref, seg_ref, cnt_ref, h2p_hbm, w2d_ref, wg_ref, wu_ref, wd_ref, out_hbm,
                h2p_ref, acc_ref, xs_ref, ye_ref, st_ref, sem):
    e = pl.program_id(0)
    n_exp = pl.num_programs(0)
    half = D_MODEL // 2
    xw, yw = PACK_ROWS, ACC_ROWS
    n_tok = out_hbm.shape[0] // yw
    grp = MOE_ROW_GROUP

    n_groups = MOE_ROWS // grp

    def gather_tile(base, slot, groups=(0, n_groups)):
        for r in range(groups[0] * grp, groups[1] * grp):
            tok = tok_ref[base + r]
            xs_ref[slot, xw * r:xw * (r + 1), :] = h2p_ref[pl.ds(pl.multiple_of(tok * xw, xw), xw), :]

    def scatter_tile(base, slot, groups=(0, n_groups)):
        for g in range(*groups):
            toks = [pl.multiple_of(tok_ref[base + g * grp + j] * yw, yw) for j in range(grp)]
            rows = [acc_ref[pl.ds(toks[j], yw), :] + ye_ref[slot, yw * (g * grp + j):yw * (g * grp + j + 1), :]
                    for j in range(grp)]
            for j in reversed(range(grp)):
                acc_ref[pl.ds(toks[j], yw), :] = rows[j]

    seg = seg_ref[e]
    cnt = cnt_ref[e]
    n_tiles = (cnt + MOE_ROWS - 1) // MOE_ROWS

    @pl.when(e == 0)
    def _():
        cp = pltpu.make_async_copy(h2p_hbm, h2p_ref.at[pl.ds(0, n_tok * xw)], sem.at[0])
        cp.start()
        acc_ref[...] = jnp.zeros_like(acc_ref)
        ye_ref[...] = jnp.zeros_like(ye_ref)
        h2p_ref[pl.ds(n_tok * xw, grp * xw), :] = jnp.zeros((grp * xw, LANES), jnp.uint32)
        st_ref[_ST_SLOT] = 0
        st_ref[_ST_READY] = -1
        st_ref[_ST_PEND_BASE] = 0
        cp.wait()

    @pl.when((n_tiles > 0) & (st_ref[_ST_READY] != e))
    def _():
        gather_tile(seg, st_ref[_ST_SLOT])

    def chunk(w_ref, c):
        return jnp.concatenate([w_ref[0, LANES * c:LANES * (c + 1), :],
                                w_ref[0, half + LANES * c:half + LANES * (c + 1), :]], axis=0).astype(BF16)

    lane = lax.broadcasted_iota(jnp.int32, (8, LANES), 1)
    nxt_seg = seg_ref[jnp.minimum(e + 1, n_exp - 1)]

    def tile_body(t, carry):
        slot = st_ref[_ST_SLOT]
        base = seg + t * MOE_ROWS
        last = t + 1 == n_tiles
        xk = [jnp.concatenate(_unpack_bf16_pairs(xs_ref[slot, pl.ds(c, MOE_ROWS, stride=xw), :]), axis=1)
              for c in range(xw)]
        g_base = jnp.where(last, nxt_seg, base + MOE_ROWS)
        p_base = st_ref[_ST_PEND_BASE]
        cuts = [n_groups * i // (xw + 1) for i in range(xw + 2)]

        def move_rows(i):
            gather_tile(g_base, 1 - slot, (cuts[i], cuts[i + 1]))
            scatter_tile(p_base, 1 - slot, (cuts[i], cuts[i + 1]))

        hg = hu = 0.0
        for c in range(xw):
            move_rows(c)
            hg = hg + _dot(xk[c], chunk(wg_ref, c))
            hu = hu + _dot(xk[c], chunk(wu_ref, c))
        act = (_silu(hg) * hu).astype(BF16)
        move_rows(xw)
        q = base // LANES
        sh = base % LANES
        rot = (LANES - sh) % LANES
        row_a = pltpu.roll(jnp.broadcast_to(w2d_ref[pl.ds(q, 1), :], (8, LANES)), rot, 1)
        row_b = pltpu.roll(jnp.broadcast_to(w2d_ref[pl.ds(q + 1, 1), :], (8, LANES)), rot, 1)
        w_row = jnp.where(lane + sh < LANES, row_a, row_b)
        w_row = jnp.where(lane < cnt - t * MOE_ROWS, w_row, 0.0)[0:1, :]
        w_col = jnp.broadcast_to(w_row, (MOE_ROWS, LANES)).T
        ye = _dot(act, wd_ref[0].astype(BF16))
        for c in range(yw):
            ye_ref[slot, pl.ds(c, MOE_ROWS, stride=yw), :] = ye[:, LANES * c:LANES * (c + 1)] * w_col
        st_ref[_ST_PEND_BASE] = base
        st_ref[_ST_SLOT] = 1 - slot
        st_ref[_ST_READY] = jnp.where(last, e + 1, e)
        return carry

    lax.fori_loop(0, n_tiles, tile_body, 0)

    @pl.when(e == n_exp - 1)
    def _():
        scatter_tile(st_ref[_ST_PEND_BASE], 1 - st_ref[_ST_SLOT])
        cp = pltpu.make_async_copy(acc_ref.at[pl.ds(0, n_tok * yw)], out_hbm, sem.at[1])
        cp.start()
        cp.wait()


def _moe(tok_sorted, seg_start, seg_count, h2p, w2d, wg, wu, wd):
    xw, yw = PACK_ROWS, ACC_ROWS
    n = h2p.shape[0] // xw
    grid_spec = pltpu.PrefetchScalarGridSpec(
        num_scalar_prefetch=3,
        grid=(N_EXPERTS,),
        in_specs=[pl.BlockSpec(memory_space=pl.ANY),
                  pl.BlockSpec(w2d.shape, lambda e, *_: (0, 0)),
                  pl.BlockSpec((1, D_MODEL, EXPERT_DIM), lambda e, *_: (e, 0, 0)),
                  pl.BlockSpec((1, D_MODEL, EXPERT_DIM), lambda e, *_: (e, 0, 0)),
                  pl.BlockSpec((1, EXPERT_DIM, D_MODEL), lambda e, *_: (e, 0, 0))],
        out_specs=pl.BlockSpec(memory_space=pl.ANY),
        scratch_shapes=[pltpu.VMEM(((n + MOE_ROW_GROUP) * xw, LANES), jnp.uint32),
                        pltpu.VMEM(((n + MOE_ROW_GROUP) * yw, LANES), F32),
                        pltpu.VMEM((2, MOE_ROWS * xw, LANES), jnp.uint32),
                        pltpu.VMEM((2, MOE_ROWS * yw, LANES), F32),
                        pltpu.SMEM((3,), jnp.int32),
                        pltpu.SemaphoreType.DMA((2,))],
    )
    return pl.pallas_call(
        _moe_kernel,
        grid_spec=grid_spec,
        out_shape=jax.ShapeDtypeStruct((n * yw, LANES), F32),
        compiler_params=_cparams(("arbitrary",), vmem=60 * 1024 * 1024),
        name="moe",
    )(tok_sorted, seg_start, seg_count, h2p, w2d, wg, wu, wd)


def _moe_routed(top_e, top_w, counts, h2p, wg, wu, wd):
    tok_sorted, w2d, seg_start, seg_count = _moe_dispatch_plan(top_e, top_w, counts, h2p.shape[0] // PACK_ROWS)
    return _moe(tok_sorted, seg_start, seg_count, h2p, w2d, wg, wu, wd)


def _moe_dispatch_plan(top_e, top_w, counts, n):
    flat_e = top_e.reshape(-1)
    flat_t = jnp.arange(n * TOP_K, dtype=jnp.int32) % n
    flat_w = top_w.reshape(-1)
    _, tok_sorted, w_sorted = lax.sort((flat_e, flat_t, flat_w), num_keys=1, is_stable=True)
    total = flat_e.shape[0]
    tok_sorted = jnp.concatenate([tok_sorted, jnp.full((MOE_ROWS,), n, jnp.int32)])
    table_rows = -(-(total // LANES + 2) // 8) * 8
    w2d = jnp.concatenate([w_sorted, jnp.zeros((table_rows * LANES - total,), F32)]).reshape(table_rows, LANES)
    cnt = counts[:, 0].astype(jnp.int32)
    return tok_sorted, w2d, jnp.cumsum(cnt) - cnt, cnt


def _regroup_rows(w):
    half = D_MODEL // 2
    return jnp.concatenate([w[r0 + LANES * c:r0 + LANES * (c + 1)]
                            for c in range(PACK_ROWS) for r0 in (0, half)], axis=0)


def _final_kernel(x1_ref, h2p_ref, r_ref, mod_ref, g_ref, wg_ref, wu_ref, wd_ref, y_ref):
    tm = x1_ref.shape[0]
    kc = 2 * LANES
    mod = mod_ref[0]
    xk = [jnp.concatenate(_unpack_bf16_pairs(h2p_ref[pl.ds(c, tm, stride=PACK_ROWS), :]), axis=1)
          for c in range(PACK_ROWS)]
    hg = sum(_dot(xk[c], wg_ref[kc * c:kc * (c + 1), :]) for c in range(PACK_ROWS))
    hu = sum(_dot(xk[c], wu_ref[kc * c:kc * (c + 1), :]) for c in range(PACK_ROWS))
    shared = _dot((_silu(hg) * hu).astype(BF16), wd_ref[...])
    routed = jnp.concatenate([r_ref[pl.ds(c, tm, stride=ACC_ROWS), :] for c in range(ACC_ROWS)], axis=1)
    ffn = routed + shared
    y_ref[...] = x1_ref[...] + mod[:, 5 * D_MODEL:6 * D_MODEL] * _rms(ffn, g_ref[...])


def _final(x1, h2p, routed, mods3, g, wg, wu, wd, row_of_tile):
    n = x1.shape[0]
    tm = TOK_TILE
    sd = wg.shape[1]
    row = lambda w: pl.BlockSpec((tm, w), lambda i: (i, 0))
    const = lambda r, c: pl.BlockSpec((r, c), lambda i: (0, 0))
    return pl.pallas_call(
        _final_kernel,
        grid=(n // tm,),
        in_specs=[row(D_MODEL), pl.BlockSpec((tm * PACK_ROWS, LANES), lambda i: (i, 0)),
                  pl.BlockSpec((tm * ACC_ROWS, LANES), lambda i: (i, 0)),
                  pl.BlockSpec((1, 1, 6 * D_MODEL), lambda i: (row_of_tile(i), 0, 0)),
                  const(1, D_MODEL), const(D_MODEL, sd), const(D_MODEL, sd), const(sd, D_MODEL)],
        out_specs=row(D_MODEL),
        out_shape=jax.ShapeDtypeStruct((n, D_MODEL), F32),
        compiler_params=_cparams(("parallel",)),
        name="final",
    )(x1, h2p, routed, mods3, g.reshape(1, D_MODEL), wg, wu, wd)


def _trunk(x3, mods3, row_of_tile, attend, s0, wts):
    b, t, _ = x3.shape
    n = b * t
    x = x3.reshape(n, D_MODEL)
    q, k, v, gdn, z, gate, ba = _premix(x, mods3, wts["g_pre_mix"], wts["w_cat"], row_of_tile)
    na_o = attend(q, k, v).reshape(n, NA_WIDTH)
    qkv = _gdn_conv(gdn.reshape(b, t, GDN_CONV_CH), wts["conv_w"])
    ba3 = ba.reshape(b, t, LANES)
    o_f, s_f = _gdn_chunks(qkv, ba3, wts["alog_row"], wts["dt_row"], s0, reverse=False)
    o_b, s_b = _gdn_chunks(qkv, ba3, wts["alog_row"], wts["dt_row"], s0, reverse=True)
    s_fin = jnp.stack([s_f, s_b], axis=1)
    x1, h2, h2p = _mix(x, na_o, o_f.reshape(n, GDN_V_WIDTH), o_b.reshape(n, GDN_V_WIDTH), z, gate, mods3,
                       wts["gdn_norm_w"],
                       wts["g_post_mix"], wts["g_pre_ffn"], wts["w_na_up"], wts["w_gdn_up"],
                       wts["w_out"], row_of_tile)
    top_e, top_w, counts = _router(h2, wts["w_rt_hi"], wts["w_rt_lo"], wts["router_bias_col"])
    routed = _moe_routed(top_e, top_w, counts, h2p, wts["w_exp_gate"], wts["w_exp_up"], wts["w_exp_down"])
    y = _final(x1, h2p, routed, mods3, wts["g_post_ffn"], wts["w_sh_gate"], wts["w_sh_up"],
               wts["w_sh_down"], row_of_tile)
    return y.reshape(b, t, D_MODEL), k, v, s_fin


def kernel(x_prompt, x_sample, cache_na_k, cache_na_v, state_gdn, c, c_ctx, w_ada, b_ada, g_pre_mix,
           g_post_mix, g_pre_ffn, g_post_ffn, w_in, conv_w, gdn_a_log, gdn_dt_bias, gdn_norm_w, na_rpb,
           w_na_up, w_gdn_up, w_out, w_router, router_bias, w_exp_gate, w_exp_up, w_exp_down, w_sh_gate,
           w_sh_up, w_sh_down):
    depth = w_ada.shape[0]
    bp, tp, _ = x_prompt.shape
    bs, ts, _ = x_sample.shape
    y_prompt, y_sample = x_prompt, x_sample
    zero_state = jnp.zeros((bp, 2, GDN_HEADS, GDN_DK, GDN_DV), F32)
    new_k, new_v, new_s = [], [], []
    for l in range(depth):
        cv = jnp.concatenate([c_ctx[None], c, jnp.zeros((8 - 1 - bs, D_MODEL), F32)], axis=0)
        mods3 = _ada(cv, w_ada[l], b_ada[l]).reshape(8, 1, 6 * D_MODEL)
        wl = w_in[l]
        w_cat = jnp.concatenate(
            [wl[:, :S_Z], wl[:, S_A:], wl[:, S_Z:S_A],
             jnp.zeros((D_MODEL, LANES - 4 * GDN_HEADS), F32)], axis=1).astype(BF16)
        pad = jnp.zeros((2 * GDN_HEADS,), F32)
        tail = jnp.zeros((LANES - 4 * GDN_HEADS,), F32)
        w_rt = w_router[l].astype(F32).T
        w_rt_hi = w_rt.astype(BF16)
        wts = dict(
            w_cat=w_cat, g_pre_mix=g_pre_mix[l], g_post_mix=g_post_mix[l], g_pre_ffn=g_pre_ffn[l],
            g_post_ffn=g_post_ffn[l], conv_w=conv_w[l], gdn_norm_w=gdn_norm_w[l],
            alog_row=jnp.concatenate([pad, gdn_a_log[l].reshape(-1), tail]).reshape(1, LANES),
            dt_row=jnp.concatenate([pad, gdn_dt_bias[l].reshape(-1), tail]).reshape(1, LANES),
            w_na_up=w_na_up[l].astype(BF16), w_gdn_up=w_gdn_up[l].astype(BF16),
            w_out=w_out[l].astype(BF16), w_rt_hi=w_rt_hi, w_rt_lo=(w_rt - w_rt_hi.astype(F32)).astype(BF16),
            router_bias_col=jnp.broadcast_to(router_bias[l].astype(F32)[:, None], (N_EXPERTS, LANES)),
            w_exp_gate=w_exp_gate[l], w_exp_up=w_exp_up[l], w_exp_down=w_exp_down[l],
            w_sh_gate=_regroup_rows(w_sh_gate[l]).astype(BF16), w_sh_up=_regroup_rows(w_sh_up[l]).astype(BF16),
            w_sh_down=w_sh_down[l].astype(BF16))

        def ctx_attend(q, k, v):
            return _ctx_attn(q.reshape(bp, tp, NA_WIDTH), k.reshape(bp, tp, NA_WIDTH),
                             v.reshape(bp, tp, NA_WIDTH))

        y_prompt, k_ctx, v_ctx, s_ctx = _trunk(y_prompt, mods3, lambda i: 0, ctx_attend, zero_state, wts)
        new_k.append(k_ctx.reshape(bp, tp, NA_HEADS, NA_HEAD_DIM))
        new_v.append(v_ctx.reshape(bp, tp, NA_HEADS, NA_HEAD_DIM))
        new_s.append(s_ctx)

        pair_tab, row_mask = _na_bias_tables(na_rpb[l], ts // GRID_W)
        ck = cache_na_k[:, l].reshape(bs, -1, NA_WIDTH)
        cvv = cache_na_v[:, l].reshape(bs, -1, NA_WIDTH)

        def na_attend(q, k, v):
            return _na_attn(q.reshape(bs, ts, NA_WIDTH), k.reshape(bs, ts, NA_WIDTH),
                            v.reshape(bs, ts, NA_WIDTH), ck, cvv, pair_tab, row_mask)

        tiles_per_seq = ts // TOK_TILE
        y_sample, _, _, _ = _trunk(y_sample, mods3, lambda i: 1 + i // tiles_per_seq, na_attend,
                                   state_gdn[:, l], wts)
    return (y_prompt, y_sample, jnp.stack(new_k, axis=1), jnp.stack(new_v, axis=1),
            jnp.stack(new_s, axis=1))
```

```python
import functools

import numpy as np
import jax
import jax.numpy as jnp
from jax import lax
from jax.experimental import pallas as pl
from jax.experimental.pallas import tpu as pltpu

F32 = jnp.float32
BF16 = jnp.bfloat16
HI = lax.Precision.HIGHEST

D_MODEL = 1024
GRID_W = 64
NA_HEADS = 8
NA_HEAD_DIM = 64
NA_WIDTH = NA_HEADS * NA_HEAD_DIM
NA_KR = 8
NA_KC = 16
GDN_HEADS = 4
GDN_DK = 128
GDN_DV = 128
GDN_QK_WIDTH = GDN_HEADS * GDN_DK
GDN_V_WIDTH = GDN_HEADS * GDN_DV
GDN_CONV_CH = 2 * GDN_QK_WIDTH + GDN_V_WIDTH
CONV_K = 5
CHUNK = 64
N_EXPERTS = 256
TOP_K = 8
N_GROUPS = 8
TOPK_GROUPS = 4
EXPERT_DIM = 256
ROUTED_SCALE = 2.5
EPS = 1e-6
S_NA = 3 * NA_WIDTH
S_GDN = S_NA + GDN_CONV_CH
S_Z = S_GDN + GDN_V_WIDTH
S_B = S_Z + 2 * GDN_HEADS
S_A = S_B + 2 * GDN_HEADS

LANES = 128
TOK_TILE = 256
NA_QROWS = 8
NA_SPAN = 16
MOE_ROWS = 128
PACK_ROWS = D_MODEL // 2 // LANES
ACC_ROWS = D_MODEL // LANES
MOE_ROW_GROUP = 16
GDN_CHUNKS_PER_STEP = 8
GDN_CONV_BLOCK_ROWS = 4096
VMEM_LIMIT = 56 * 1024 * 1024
NEG_INF = float("-inf")


def _cparams(sem, vmem=VMEM_LIMIT):
    return pltpu.CompilerParams(dimension_semantics=sem, vmem_limit_bytes=vmem)


def _silu(x):
    return x * jax.nn.sigmoid(x)


def _rms(x, g):
    return x * lax.rsqrt(jnp.mean(x * x, axis=-1, keepdims=True) + EPS) * g


def _dot(a, b):
    return jnp.dot(a, b, preferred_element_type=F32)


def _dot_nt(a, b, precision=None):
    return lax.dot_general(a, b, (((1,), (1,)), ((), ())), precision=precision,
                           preferred_element_type=F32)


def _ada_kernel(c_ref, w_ref, b_ref, o_ref):
    o_ref[...] = jnp.dot(_silu(c_ref[...]), w_ref[...], precision=HI,
                         preferred_element_type=F32) + b_ref[...]


def _ada(cv, w_ada, b_ada):
    n = w_ada.shape[1]
    tn = 512
    return pl.pallas_call(
        _ada_kernel,
        grid=(n // tn,),
        in_specs=[pl.BlockSpec((8, D_MODEL), lambda j: (0, 0)),
                  pl.BlockSpec((D_MODEL, tn), lambda j: (0, j)),
                  pl.BlockSpec((1, tn), lambda j: (0, j))],
        out_specs=pl.BlockSpec((8, tn), lambda j: (0, j)),
        out_shape=jax.ShapeDtypeStruct((8, n), F32),
        compiler_params=_cparams(("parallel",)),
        name="ada",
    )(cv, w_ada, b_ada.reshape(1, n))


_PM_WIDTHS = (NA_WIDTH, NA_WIDTH, NA_WIDTH, GDN_CONV_CH, GDN_V_WIDTH, 2 * D_MODEL, LANES)


def _premix_kernel(x_ref, mod_ref, g_ref, w_ref, *o_refs):
    mod = mod_ref[0]
    h = _rms(x_ref[...], g_ref[...]) * (1.0 + mod[:, D_MODEL:2 * D_MODEL]) + mod[:, 0:D_MODEL]
    hb = h.astype(BF16)
    off = 0
    for o_ref, wd in zip(o_refs, _PM_WIDTHS):
        for c0 in range(0, wd, 512):
            c1 = min(c0 + 512, wd)
            o_ref[:, c0:c1] = _dot(hb, w_ref[:, off + c0:off + c1])
        off += wd


def _premix(x, mods3, g, w_cat, row_of_tile):
    n = x.shape[0]
    wtot = w_cat.shape[1]
    tm = TOK_TILE
    return pl.pallas_call(
        _premix_kernel,
        grid=(n // tm,),
        in_specs=[pl.BlockSpec((tm, D_MODEL), lambda i: (i, 0)),
                  pl.BlockSpec((1, 1, 6 * D_MODEL), lambda i: (row_of_tile(i), 0, 0)),
                  pl.BlockSpec((1, D_MODEL), lambda i: (0, 0)),
                  pl.BlockSpec((D_MODEL, wtot), lambda i: (0, 0))],
        out_specs=[pl.BlockSpec((tm, wd), lambda i: (i, 0)) for wd in _PM_WIDTHS],
        out_shape=[jax.ShapeDtypeStruct((n, wd), F32) for wd in _PM_WIDTHS],
        compiler_params=_cparams(("parallel",)),
        name="premix",
    )(x, mods3, g.reshape(1, D_MODEL), w_cat)


def _softmax_rows(s):
    m = jnp.max(s, axis=-1, keepdims=True)
    p = jnp.exp(s - m)
    return p / jnp.sum(p, axis=-1, keepdims=True)


def _ctx_attn_kernel(q_ref, k_ref, v_ref, o_ref):
    scale = NA_HEAD_DIM ** -0.5
    first = lax.broadcasted_iota(jnp.int32, (q_ref.shape[1], LANES), 1) < NA_HEAD_DIM
    for hp in range(NA_WIDTH // LANES):
        lanes = slice(hp * LANES, (hp + 1) * LANES)
        q = q_ref[0, :, lanes]
        k = k_ref[0, :, lanes].astype(BF16)
        v = v_ref[0, :, lanes].astype(BF16)
        outs = []
        for hh in range(2):
            qm = jnp.where(first if hh == 0 else ~first, q, 0.0).astype(BF16)
            p = _softmax_rows(_dot_nt(qm, k) * scale)
            outs.append(_dot(p.astype(BF16), v))
        o_ref[0, :, lanes] = jnp.where(first, outs[0], outs[1])


def _ctx_attn(q, k, v):
    b, t, w = q.shape
    spec = pl.BlockSpec((1, t, w), lambda i: (i, 0, 0))
    return pl.pallas_call(
        _ctx_attn_kernel,
        grid=(b,),
        in_specs=[spec, spec, spec],
        out_specs=spec,
        out_shape=jax.ShapeDtypeStruct((b, t, w), F32),
        compiler_params=_cparams(("parallel",)),
        name="ctx_attn",
    )(q, k, v)


def _na_span_base(j, rows):
    return np.clip(NA_QROWS * j - NA_KR // 2, 0, rows - NA_SPAN)


NA_DR_PAD = NA_SPAN - NA_KR


def _na_bias_tables(rpb, rows):
    col = np.arange(GRID_W)
    dcm = np.clip(col[None, :] - col[:, None], -(NA_KC - 1), NA_KC - 1) + (NA_KC - 1)
    onehot = (dcm[None] == np.arange(2 * NA_KC - 1)[:, None, None]).astype(np.float32)
    tab = jnp.einsum('hrd,dqk->hrqk', rpb.astype(F32), jnp.asarray(onehot), precision=HI)
    col_start = np.clip(col - NA_KC // 2, 0, GRID_W - NA_KC)
    col_in = (col[None, :] >= col_start[:, None]) & (col[None, :] < col_start[:, None] + NA_KC)
    tab = jnp.where(jnp.asarray(col_in)[None, None], tab, NEG_INF)
    n_dr = 2 * NA_KR - 1
    n_side = NA_SPAN - NA_KR + 1
    blank_lo = jnp.full((NA_HEADS, NA_DR_PAD, GRID_W, GRID_W), NEG_INF, F32)
    blank_hi = jnp.full((NA_HEADS, n_side, GRID_W, GRID_W), NEG_INF, F32)
    padded = jnp.concatenate([blank_lo, tab, blank_hi], axis=1)
    pair_tab = jnp.concatenate([padded[:, :-1], padded[:, 1:]], axis=-1)
    nblk = rows // NA_QROWS
    mask = np.full((3, NA_QROWS, NA_SPAN), NEG_INF, np.float32)
    for p, j in enumerate((0, 1, nblk - 1)):
        base = _na_span_base(j, rows)
        for ri in range(NA_QROWS):
            r = NA_QROWS * j + ri
            rs = np.clip(r - NA_KR // 2, 0, rows - NA_KR)
            for ki in range(NA_SPAN):
                if rs <= base + ki < rs + NA_KR:
                    mask[p, ri, ki] = 0.0
    row_mask = jnp.asarray(np.repeat(mask, GRID_W, axis=2))
    return pair_tab, row_mask


def _na_attn_kernel(q_ref, k_ref, v_ref, ck_ref, cv_ref, tab_ref, mask_ref, o_ref, *, rows):
    j = pl.program_id(2)
    scale = NA_HEAD_DIM ** -0.5
    base = jnp.clip(NA_QROWS * j - NA_KR // 2, 0, rows - NA_SPAN)
    start = pl.multiple_of(base * GRID_W, GRID_W)
    span = NA_SPAN * GRID_W
    q = q_ref[0]
    kl = k_ref[0, pl.ds(start, span), :].astype(BF16)
    vl = v_ref[0, pl.ds(start, span), :].astype(BF16)
    ck = ck_ref[0].astype(BF16)
    cv = cv_ref[0].astype(BF16)
    first = lax.broadcasted_iota(jnp.int32, q.shape, 1) < NA_HEAD_DIM
    off = base - NA_QROWS * j + (NA_KR - 1) + NA_DR_PAD
    outs = []
    for hh in range(2):
        qm = jnp.where(first if hh == 0 else ~first, q, 0.0).astype(BF16)
        s_raw = _dot_nt(qm, kl) * scale
        blocks = []
        for ri in range(NA_QROWS):
            rws = slice(ri * GRID_W, (ri + 1) * GRID_W)
            pieces = [s_raw[rws, m * LANES:(m + 1) * LANES] + tab_ref[hh, off + 2 * m - ri]
                      for m in range(NA_SPAN // 2)]
            blocks.append(jnp.concatenate(pieces, axis=1) + mask_ref[0, ri:ri + 1, :])
        s_loc = jnp.concatenate(blocks, axis=0)
        s_ctx = _dot_nt(qm, ck) * scale
        m = jnp.maximum(jnp.max(s_loc, axis=-1, keepdims=True), jnp.max(s_ctx, axis=-1, keepdims=True))
        p_loc = jnp.exp(s_loc - m)
        p_ctx = jnp.exp(s_ctx - m)
        den = jnp.sum(p_loc, axis=-1, keepdims=True) + jnp.sum(p_ctx, axis=-1, keepdims=True)
        p_loc = (p_loc / den).astype(BF16)
        p_ctx = (p_ctx / den).astype(BF16)
        outs.append(_dot(p_loc, vl) + _dot(p_ctx, cv))
    o_ref[0] = jnp.where(first, outs[0], outs[1])


def _na_attn(q, k, v, ck, cv, pair_tab, row_mask):
    b, n, w = q.shape
    p = ck.shape[1]
    rows = n // GRID_W
    nblk = rows // NA_QROWS
    qb = NA_QROWS * GRID_W

    def pattern(j):
        return jnp.where(j == 0, 0, jnp.where(j == nblk - 1, 2, 1))

    full = pl.BlockSpec((1, n, LANES), lambda bi, hp, j: (bi, 0, hp))
    ctx = pl.BlockSpec((1, p, LANES), lambda bi, hp, j: (bi, 0, hp))
    blk = pl.BlockSpec((1, qb, LANES), lambda bi, hp, j: (bi, j, hp))
    return pl.pallas_call(
        functools.partial(_na_attn_kernel, rows=rows),
        grid=(b, w // LANES, nblk),
        in_specs=[blk, full, full, ctx, ctx,
                  pl.BlockSpec((2,) + pair_tab.shape[1:], lambda bi, hp, j: (hp, 0, 0, 0)),
                  pl.BlockSpec((1,) + row_mask.shape[1:], lambda bi, hp, j: (pattern(j), 0, 0))],
        out_specs=blk,
        out_shape=jax.ShapeDtypeStruct((b, n, w), F32),
        compiler_params=_cparams(("parallel", "parallel", "arbitrary")),
        name="na_attn",
    )(q, k, v, ck, cv, pair_tab, row_mask)


def _gdn_conv_kernel(x_ref, w_ref, o_ref, *, groups):
    c = pl.program_id(1)
    t = x_ref.shape[1]
    row = lax.broadcasted_iota(jnp.int32, (t, LANES), 0)
    n_qk = 2 * GDN_HEADS
    for i in range(groups):
        lanes = slice(i * LANES, (i + 1) * LANES)
        x = x_ref[0, :, lanes]
        y = jnp.zeros_like(x)
        for jj in range(CONV_K):
            o = jj - CONV_K // 2
            xs = x if o == 0 else pltpu.roll(x, (-o) % t, 0)
            xs = jnp.where((row + o >= 0) & (row + o < t), xs, 0.0)
            y = y + xs * w_ref[jj:jj + 1, lanes]
        y = _silu(y)
        nrm = lax.rsqrt(jnp.sum(y * y, axis=-1, keepdims=True) + EPS)
        o_ref[0, i] = jnp.where(c * groups + i < n_qk, y * nrm, y)


def _gdn_conv(x, conv_w):
    b, t, ch = x.shape
    nc = ch // LANES
    groups = max(g for g in range(1, nc + 1)
                 if nc % g == 0 and (g == 1 or g * t <= GDN_CONV_BLOCK_ROWS))
    return pl.pallas_call(
        functools.partial(_gdn_conv_kernel, groups=groups),
        grid=(b, nc // groups),
        in_specs=[pl.BlockSpec((1, t, groups * LANES), lambda bi, c: (bi, 0, c)),
                  pl.BlockSpec((CONV_K, groups * LANES), lambda bi, c: (0, c))],
        out_specs=pl.BlockSpec((1, groups, t, LANES), lambda bi, c: (bi, c, 0, 0)),
        out_shape=jax.ShapeDtypeStruct((b, nc, t, LANES), F32),
        compiler_params=_cparams(("parallel", "parallel")),
        name="gdn_conv",
    )(x, conv_w)


def _bdot(a, b):
    return jnp.dot(a.astype(BF16), b.astype(BF16), preferred_element_type=F32)


def _split_bf16(x):
    hi = x.astype(BF16)
    return hi, (x - hi.astype(F32)).astype(BF16)


def _dot3(a, b):
    m = a.shape[0]
    ah, al = _split_bf16(a)
    bh, bl = _split_bf16(b)
    top = _dot(jnp.concatenate([ah, al], axis=0), bh)
    return top[:m] + top[m:] + _dot(ah, bl)


def _gdn_chunk_kernel(qkv_ref, ba_ref, alog_ref, dt_ref, s0_ref, o_ref, sfin_ref, s_ref, *, reverse, cb, nb):
    c = pl.program_id(1)
    nh = GDN_HEADS

    @pl.when(c == 0)
    def _():
        s_ref[...] = s0_ref[:, 0]

    ii = lax.broadcasted_iota(jnp.int32, (CHUNK, CHUNK), 0)
    jj = lax.broadcasted_iota(jnp.int32, (CHUNK, CHUNK), 1)
    lag = (jj - ii) if reverse else (ii - jj)
    incl = lag >= 0
    strict = lag > 0
    eye = (ii == jj).astype(F32)
    tri = incl.astype(F32)
    bcol = nh if reverse else 0
    gcol0 = (3 if reverse else 2) * nh
    sub8 = lax.broadcasted_iota(jnp.int32, (8, LANES), 0)
    lane8 = lax.broadcasted_iota(jnp.int32, (8, LANES), 1)
    sel8 = (lane8 == gcol0 + sub8).astype(F32)
    nch = nb * cb
    units = [(ci, h) for ci in range(nch) for h in range(nh)]
    gc_alls, beta_alls, grow8s = [], [], []
    for ci in range(nch):
        ba = ba_ref[ci // cb, (ci % cb) * CHUNK:(ci % cb + 1) * CHUNK, :]
        z = ba + dt_ref[...]
        softplus = jnp.maximum(z, 0.0) + jnp.log1p(jnp.exp(-jnp.abs(z)))
        g_all = -jnp.exp(alog_ref[...]) * softplus
        gc_all = jnp.dot(tri, g_all, precision=HI, preferred_element_type=F32)
        gc_alls.append(gc_all)
        beta_alls.append(jax.nn.sigmoid(ba))
        grow8s.append(_dot_nt(sel8, gc_all, precision=HI))
    gcol = [gc_alls[ci][:, gcol0 + h:gcol0 + h + 1] for ci, h in units]
    beta = [beta_alls[ci][:, bcol + h:bcol + h + 1] for ci, h in units]
    rows = [slice((ci % cb) * CHUNK, (ci % cb + 1) * CHUNK) for ci, _ in units]
    seq = [ci // cb for ci, _ in units]
    k = [qkv_ref[seq[u], nh + h, rows[u], :] for u, (_, h) in enumerate(units)]
    kb = [k[u] * beta[u] for u in range(len(units))]
    q = [qkv_ref[seq[u], h, rows[u], :] * (GDN_DK ** -0.5) for u, (_, h) in enumerate(units)]
    kq = [_dot_nt(jnp.concatenate([kb[u], q[u]], axis=0).astype(BF16), k[u].astype(BF16))
          for u in range(len(units))]
    decay = [jnp.where(incl, jnp.exp(jnp.where(incl, gcol[u] - grow8s[ci][h:h + 1, :], 0.0)), 0.0)
             for u, (ci, h) in enumerate(units)]
    intra = [jnp.where(incl, kq[u][CHUNK:] * decay[u], 0.0) for u in range(len(units))]
    wide = nh * CHUNK
    bd_mask = (lax.broadcasted_iota(jnp.int32, (wide, wide), 0) // CHUNK
               == lax.broadcasted_iota(jnp.int32, (wide, wide), 1) // CHUNK)

    def block_diag(x):
        return jnp.where(bd_mask, jnp.concatenate([x] * nh, axis=0), jnp.zeros((), x.dtype))

    def dot3_bd(a, b):
        m = a.shape[0]
        ah, al = _split_bf16(a)
        bh, bl = _split_bf16(b)
        top = _dot(jnp.concatenate([ah, al], axis=0), block_diag(bh))
        return top[:m] + top[m:] + _dot(ah, block_diag(bl))

    eye_w = jnp.concatenate([eye] * nh, axis=1)
    pw = [jnp.concatenate([-jnp.where(strict, kq[ci * nh + h][:CHUNK] * decay[ci * nh + h], 0.0)
                           for h in range(nh)], axis=1) for ci in range(nch)]
    tmat = [eye_w + p for p in pw]
    pw = [dot3_bd(p, p) for p in pw]
    for _ in range(4):
        pt = [dot3_bd(jnp.concatenate([pw[ci], tmat[ci]], axis=0), pw[ci]) for ci in range(nch)]
        pw = [x[:CHUNK] for x in pt]
        tmat = [tmat[ci] + pt[ci][CHUNK:] for ci in range(nch)]
    tmat = [tmat[ci] + dot3_bd(tmat[ci], pw[ci]) for ci in range(nch)]
    eg = [jnp.exp(g) for g in gcol]
    uw_all = [_dot(block_diag(tmat[ci].astype(BF16)), jnp.concatenate(
        [jnp.concatenate([qkv_ref[ci // cb, 2 * nh + h, rows[ci * nh + h], :] * beta[ci * nh + h],
                          kb[ci * nh + h] * eg[ci * nh + h]], axis=1) for h in range(nh)],
        axis=0).astype(BF16)) for ci in range(nch)]
    uw = [uw_all[ci][h * CHUNK:(h + 1) * CHUNK] for ci, h in units]
    g_last = [g[0:1, :] if reverse else g[CHUNK - 1:CHUNK, :] for g in gcol]
    kd = [k[u] * jnp.exp(g_last[u] - gcol[u]) for u in range(len(units))]
    qe = [q[u] * eg[u] for u in range(len(units))]
    kd_uw = [lax.dot_general(kd[u].astype(BF16), uw[u].astype(BF16), (((0,), (0,)), ((), ())),
                             preferred_element_type=F32) for u in range(len(units))]
    in_uw = [_bdot(intra[u], uw[u]) for u in range(len(units))]
    lhs = [jnp.concatenate([kd_uw[u][:, GDN_DV:], qe[u] - in_uw[u][:, GDN_DV:]], axis=0).astype(BF16)
           for u in range(len(units))]
    s = [s_ref[bi, h] for bi in range(nb) for h in range(nh)]
    for cl in (reversed(range(cb)) if reverse else range(cb)):
        us = [(bi * cb + cl) * nh + h for bi in range(nb) for h in range(nh)]
        r = [_dot(lhs[u], s[i].astype(BF16)) for i, u in enumerate(us)]
        for i, u in enumerate(us):
            h = i % nh
            o_ref[i // nh, rows[u], h * GDN_DV:(h + 1) * GDN_DV] = r[i][GDN_DK:] + in_uw[u][:, :GDN_DV]
        s = [s[i] * jnp.exp(g_last[u]) - r[i][:GDN_DK] + kd_uw[u][:, :GDN_DV] for i, u in enumerate(us)]
    for i in range(nb * nh):
        s_ref[i // nh, i % nh] = s[i]

    @pl.when(c == pl.num_programs(1) - 1)
    def _():
        sfin_ref[...] = s_ref[...]


def _gdn_chunks(qkv, ba, alog_row, dt_row, s0, reverse):
    b, _, t, _ = qkv.shape
    cb = min(GDN_CHUNKS_PER_STEP, t // CHUNK)
    nb = max(g for g in range(1, b + 1) if b % g == 0 and g * cb <= GDN_CHUNKS_PER_STEP)
    rows = cb * CHUNK
    n = t // rows
    d = 1 if reverse else 0

    def blk(c):
        return n - 1 - c if reverse else c

    return pl.pallas_call(
        functools.partial(_gdn_chunk_kernel, reverse=reverse, cb=cb, nb=nb),
        grid=(b // nb, n),
        in_specs=[pl.BlockSpec((nb, 3 * GDN_HEADS, rows, LANES), lambda bi, c: (bi, 0, blk(c), 0)),
                  pl.BlockSpec((nb, rows, LANES), lambda bi, c: (bi, blk(c), 0)),
                  pl.BlockSpec((1, LANES), lambda bi, c: (0, 0)),
                  pl.BlockSpec((1, LANES), lambda bi, c: (0, 0)),
                  pl.BlockSpec((nb, 1, GDN_HEADS, GDN_DK, GDN_DV), lambda bi, c: (bi, d, 0, 0, 0))],
        out_specs=[pl.BlockSpec((nb, rows, GDN_V_WIDTH), lambda bi, c: (bi, blk(c), 0)),
                   pl.BlockSpec((nb, GDN_HEADS, GDN_DK, GDN_DV), lambda bi, c: (bi, 0, 0, 0))],
        out_shape=[jax.ShapeDtypeStruct((b, t, GDN_V_WIDTH), F32),
                   jax.ShapeDtypeStruct((b, GDN_HEADS, GDN_DK, GDN_DV), F32)],
        scratch_shapes=[pltpu.VMEM((nb, GDN_HEADS, GDN_DK, GDN_DV), F32)],
        compiler_params=_cparams(("parallel", "arbitrary")),
        name="gdn_bwd" if reverse else "gdn_fwd",
    )(qkv, ba, alog_row, dt_row, s0)


def _pack_bf16_pairs(h):
    half = D_MODEL // 2
    lo = pltpu.bitcast(h[:, :half].astype(BF16).astype(F32), jnp.uint32)
    hi = pltpu.bitcast(h[:, half:].astype(BF16).astype(F32), jnp.uint32)
    return (hi & jnp.uint32(0xFFFF0000)) | (lo >> 16)


def _unpack_bf16_pairs(w):
    lo = pltpu.bitcast(w << 16, F32).astype(BF16)
    hi = pltpu.bitcast(w & jnp.uint32(0xFFFF0000), F32).astype(BF16)
    return lo, hi


def _mix_kernel(x_ref, na_ref, of_ref, ob_ref, z_ref, gate_ref, mod_ref, gnw_ref, gpost_ref, gpre_ref,
                wna_ref, wgdn_ref, wout_ref, x1_ref, h2_ref, h2p_ref):
    mod = mod_ref[0]
    o = of_ref[...] + ob_ref[...]
    parts = []
    for h in range(GDN_HEADS):
        sl = slice(h * GDN_DV, (h + 1) * GDN_DV)
        parts.append(_rms(o[:, sl], gnw_ref[...]) * _silu(z_ref[:, sl]))
    gdn_o = jnp.concatenate(parts, axis=-1)
    a = _dot(na_ref[...].astype(BF16), wna_ref[...])
    b = _dot(gdn_o.astype(BF16), wgdn_ref[...])
    gate = jax.nn.sigmoid(gate_ref[...])
    pre = gate[:, :D_MODEL] * a + gate[:, D_MODEL:] * b
    mix = _dot(pre.astype(BF16), wout_ref[...])
    x1 = x_ref[...] + mod[:, 2 * D_MODEL:3 * D_MODEL] * _rms(mix, gpost_ref[...])
    x1_ref[...] = x1
    h2 = _rms(x1, gpre_ref[...]) * (1.0 + mod[:, 4 * D_MODEL:5 * D_MODEL]) + mod[:, 3 * D_MODEL:4 * D_MODEL]
    h2_ref[...] = h2
    packed = _pack_bf16_pairs(h2)
    for c in range(PACK_ROWS):
        h2p_ref[pl.ds(c, h2.shape[0], stride=PACK_ROWS), :] = packed[:, c * LANES:(c + 1) * LANES]


def _mix(x, na_o, o_f, o_b, z, gate, mods3, gnw, gpost, gpre, wna, wgdn, wout, row_of_tile):
    n = x.shape[0]
    tm = TOK_TILE
    row = lambda w: pl.BlockSpec((tm, w), lambda i: (i, 0))
    const = lambda r, c: pl.BlockSpec((r, c), lambda i: (0, 0))
    return pl.pallas_call(
        _mix_kernel,
        grid=(n // tm,),
        in_specs=[row(D_MODEL), row(NA_WIDTH), row(GDN_V_WIDTH), row(GDN_V_WIDTH),
                  row(GDN_V_WIDTH), row(2 * D_MODEL),
                  pl.BlockSpec((1, 1, 6 * D_MODEL), lambda i: (row_of_tile(i), 0, 0)),
                  const(1, GDN_DV), const(1, D_MODEL), const(1, D_MODEL),
                  const(NA_WIDTH, D_MODEL), const(GDN_V_WIDTH, D_MODEL), const(D_MODEL, D_MODEL)],
        out_specs=[row(D_MODEL), row(D_MODEL), pl.BlockSpec((tm * PACK_ROWS, LANES), lambda i: (i, 0))],
        out_shape=[jax.ShapeDtypeStruct((n, D_MODEL), F32),
                   jax.ShapeDtypeStruct((n, D_MODEL), F32),
                   jax.ShapeDtypeStruct((n * PACK_ROWS, LANES), jnp.uint32)],
        compiler_params=_cparams(("parallel",)),
        name="mix",
    )(x, na_o, o_f, o_b, z, gate, mods3, gnw.reshape(1, GDN_DV), gpost.reshape(1, D_MODEL),
      gpre.reshape(1, D_MODEL), wna, wgdn, wout)


def _router_kernel(h_ref, wh_ref, wl_ref, b_ref, e_ref, wt_ref, cnt_ref, acc_ref):
    i = pl.program_id(0)
    tm = h_ref.shape[0]
    ne = N_EXPERTS
    per = ne // N_GROUPS
    hh, hl = _split_bf16(h_ref[...])
    wh = wh_ref[...]
    logits = _dot_nt(wh, hh) + _dot_nt(wl_ref[...], hh) + _dot_nt(wh, hl)
    scores = jax.nn.sigmoid(logits)
    biased = scores + jnp.concatenate([b_ref[...]] * (tm // LANES), axis=1)

    def first_max(x):
        n = x.shape[0]
        iota = lax.broadcasted_iota(jnp.int32, x.shape, 0).astype(F32)
        m = jnp.max(x, axis=0, keepdims=True)
        idx = jnp.min(jnp.where(x == m, iota, float(n)), axis=0, keepdims=True)
        return m, idx, iota

    gs_rows = []
    for g in range(N_GROUPS):
        bg = biased[g * per:(g + 1) * per]
        m1, i1, iota = first_max(bg)
        m2 = jnp.max(jnp.where(iota == i1, NEG_INF, bg), axis=0, keepdims=True)
        gs_rows.append(m1 + m2)
    gs = jnp.concatenate(gs_rows, axis=0)
    gsel = jnp.zeros(gs.shape, F32)
    for _ in range(TOPK_GROUPS):
        _, gi, iota = first_max(gs)
        hit = iota == gi
        gs = jnp.where(hit, NEG_INF, gs)
        gsel = jnp.where(hit, 1.0, gsel)
    masked = jnp.concatenate(
        [jnp.where(gsel[g:g + 1] > 0.0, biased[g * per:(g + 1) * per], NEG_INF) for g in range(N_GROUPS)], axis=0)
    onehot = jnp.zeros((ne, tm), F32)
    e_rows, w_rows = [], []
    for _ in range(TOP_K):
        _, ei, iota = first_max(masked)
        hit = iota == ei
        masked = jnp.where(hit, NEG_INF, masked)
        w_rows.append(jnp.sum(jnp.where(hit, scores, 0.0), axis=0, keepdims=True))
        e_rows.append(ei)
        onehot = onehot + hit.astype(F32)
    w_out = jnp.concatenate(w_rows, axis=0)
    e_ref[...] = jnp.concatenate(e_rows, axis=0).astype(jnp.int32)
    wt_ref[...] = w_out / jnp.sum(w_out, axis=0, keepdims=True) * ROUTED_SCALE

    @pl.when(i == 0)
    def _():
        acc_ref[...] = jnp.zeros_like(acc_ref)

    acc_ref[...] += sum(onehot[:, c * LANES:(c + 1) * LANES] for c in range(tm // LANES))

    @pl.when(i == pl.num_programs(0) - 1)
    def _():
        cnt_ref[...] = jnp.broadcast_to(jnp.sum(acc_ref[...], axis=1, keepdims=True), cnt_ref.shape)


def _router(h2, w_rt_hi, w_rt_lo, bias_col):
    n = h2.shape[0]
    tm = TOK_TILE
    return pl.pallas_call(
        _router_kernel,
        grid=(n // tm,),
        in_specs=[pl.BlockSpec((tm, D_MODEL), lambda i: (i, 0)),
                  pl.BlockSpec((N_EXPERTS, D_MODEL), lambda i: (0, 0)),
                  pl.BlockSpec((N_EXPERTS, D_MODEL), lambda i: (0, 0)),
                  pl.BlockSpec((N_EXPERTS, LANES), lambda i: (0, 0))],
        out_specs=[pl.BlockSpec((TOP_K, tm), lambda i: (0, i)),
                   pl.BlockSpec((TOP_K, tm), lambda i: (0, i)),
                   pl.BlockSpec((N_EXPERTS, LANES), lambda i: (0, 0))],
        out_shape=[jax.ShapeDtypeStruct((TOP_K, n), jnp.int32),
                   jax.ShapeDtypeStruct((TOP_K, n), F32),
                   jax.ShapeDtypeStruct((N_EXPERTS, LANES), F32)],
        scratch_shapes=[pltpu.VMEM((N_EXPERTS, LANES), F32)],
        compiler_params=_cparams(("arbitrary",)),
        name="router",
    )(h2, w_rt_hi, w_rt_lo, bias_col)


_ST_SLOT, _ST_READY, _ST_PEND_BASE = range(3)


def _moe_kernel(tok_ref, seg_ref, cnt_ref, h2p_hbm, w2d_ref, wg_ref, wu_ref, wd_ref, out_hbm,
                h2p_ref, acc_ref, xs_ref, ye_ref, st_ref, sem):
    e = pl.program_id(0)
    n_exp = pl.num_programs(0)
    half = D_MODEL // 2
    xw, yw = PACK_ROWS, ACC_ROWS
    n_tok = out_hbm.shape[0] // yw
    grp = MOE_ROW_GROUP

    n_groups = MOE_ROWS // grp

    def gather_tile(base, slot, groups=(0, n_groups)):
        for r in range(groups[0] * grp, groups[1] * grp):
            tok = tok_ref[base + r]
            xs_ref[slot, xw * r:xw * (r + 1), :] = h2p_ref[pl.ds(pl.multiple_of(tok * xw, xw), xw), :]

    def scatter_tile(base, slot, groups=(0, n_groups)):
        for g in range(*groups):
            toks = [pl.multiple_of(tok_ref[base + g * grp + j] * yw, yw) for j in range(grp)]
            rows = [acc_ref[pl.ds(toks[j], yw), :] + ye_ref[slot, yw * (g * grp + j):yw * (g * grp + j + 1), :]
                    for j in range(grp)]
            for j in reversed(range(grp)):
                acc_ref[pl.ds(toks[j], yw), :] = rows[j]

    seg = seg_ref[e]
    cnt = cnt_ref[e]
    n_tiles = (cnt + MOE_ROWS - 1) // MOE_ROWS

    @pl.when(e == 0)
    def _():
        cp = pltpu.make_async_copy(h2p_hbm, h2p_ref.at[pl.ds(0, n_tok * xw)], sem.at[0])
        cp.start()
        acc_ref[...] = jnp.zeros_like(acc_ref)
        ye_ref[...] = jnp.zeros_like(ye_ref)
        h2p_ref[pl.ds(n_tok * xw, grp * xw), :] = jnp.zeros((grp * xw, LANES), jnp.uint32)
        st_ref[_ST_SLOT] = 0
        st_ref[_ST_READY] = -1
        st_ref[_ST_PEND_BASE] = 0
        cp.wait()

    @pl.when((n_tiles > 0) & (st_ref[_ST_READY] != e))
    def _():
        gather_tile(seg, st_ref[_ST_SLOT])

    def chunk(w_ref, c):
        return jnp.concatenate([w_ref[0, LANES * c:LANES * (c + 1), :],
                                w_ref[0, half + LANES * c:half + LANES * (c + 1), :]], axis=0).astype(BF16)

    lane = lax.broadcasted_iota(jnp.int32, (8, LANES), 1)
    nxt_seg = seg_ref[jnp.minimum(e + 1, n_exp - 1)]

    def tile_body(t, carry):
        slot = st_ref[_ST_SLOT]
        base = seg + t * MOE_ROWS
        last = t + 1 == n_tiles
        xk = [jnp.concatenate(_unpack_bf16_pairs(xs_ref[slot, pl.ds(c, MOE_ROWS, stride=xw), :]), axis=1)
              for c in range(xw)]
        g_base = jnp.where(last, nxt_seg, base + MOE_ROWS)
        p_base = st_ref[_ST_PEND_BASE]
        cuts = [n_groups * i // (xw + 1) for i in range(xw + 2)]

        def move_rows(i):
            gather_tile(g_base, 1 - slot, (cuts[i], cuts[i + 1]))
            scatter_tile(p_base, 1 - slot, (cuts[i], cuts[i + 1]))

        hg = hu = 0.0
        for c in range(xw):
            move_rows(c)
            hg = hg + _dot(xk[c], chunk(wg_ref, c))
            hu = hu + _dot(xk[c], chunk(wu_ref, c))
        act = (_silu(hg) * hu).astype(BF16)
        move_rows(xw)
        q = base // LANES
        sh = base % LANES
        rot = (LANES - sh) % LANES
        row_a = pltpu.roll(jnp.broadcast_to(w2d_ref[pl.ds(q, 1), :], (8, LANES)), rot, 1)
        row_b = pltpu.roll(jnp.broadcast_to(w2d_ref[pl.ds(q + 1, 1), :], (8, LANES)), rot, 1)
        w_row = jnp.where(lane + sh < LANES, row_a, row_b)
        w_row = jnp.where(lane < cnt - t * MOE_ROWS, w_row, 0.0)[0:1, :]
        w_col = jnp.broadcast_to(w_row, (LANES, LANES)).T[:MOE_ROWS]
        ye = _dot(act, wd_ref[0].astype(BF16))
        for c in range(yw):
            ye_ref[slot, pl.ds(c, MOE_ROWS, stride=yw), :] = ye[:, LANES * c:LANES * (c + 1)] * w_col
        st_ref[_ST_PEND_BASE] = base
        st_ref[_ST_SLOT] = 1 - slot
        st_ref[_ST_READY] = jnp.where(last, e + 1, e)
        return carry

    lax.fori_loop(0, n_tiles, tile_body, 0)

    @pl.when(e == n_exp - 1)
    def _():
        scatter_tile(st_ref[_ST_PEND_BASE], 1 - st_ref[_ST_SLOT])
        cp = pltpu.make_async_copy(acc_ref.at[pl.ds(0, n_tok * yw)], out_hbm, sem.at[1])
        cp.start()
        cp.wait()


def _moe(tok_sorted, seg_start, seg_count, h2p, w2d, wg, wu, wd):
    xw, yw = PACK_ROWS, ACC_ROWS
    n = h2p.shape[0] // xw
    grid_spec = pltpu.PrefetchScalarGridSpec(
        num_scalar_prefetch=3,
        grid=(N_EXPERTS,),
        in_specs=[pl.BlockSpec(memory_space=pl.ANY),
                  pl.BlockSpec(w2d.shape, lambda e, *_: (0, 0)),
                  pl.BlockSpec((1, D_MODEL, EXPERT_DIM), lambda e, *_: (e, 0, 0)),
                  pl.BlockSpec((1, D_MODEL, EXPERT_DIM), lambda e, *_: (e, 0, 0)),
                  pl.BlockSpec((1, EXPERT_DIM, D_MODEL), lambda e, *_: (e, 0, 0))],
        out_specs=pl.BlockSpec(memory_space=pl.ANY),
        scratch_shapes=[pltpu.VMEM(((n + MOE_ROW_GROUP) * xw, LANES), jnp.uint32),
                        pltpu.VMEM(((n + MOE_ROW_GROUP) * yw, LANES), F32),
                        pltpu.VMEM((2, MOE_ROWS * xw, LANES), jnp.uint32),
                        pltpu.VMEM((2, MOE_ROWS * yw, LANES), F32),
                        pltpu.SMEM((3,), jnp.int32),
                        pltpu.SemaphoreType.DMA((2,))],
    )
    return pl.pallas_call(
        _moe_kernel,
        grid_spec=grid_spec,
        out_shape=jax.ShapeDtypeStruct((n * yw, LANES), F32),
        compiler_params=_cparams(("arbitrary",), vmem=60 * 1024 * 1024),
        name="moe",
    )(tok_sorted, seg_start, seg_count, h2p, w2d, wg, wu, wd)


def _moe_routed(top_e, top_w, counts, h2p, wg, wu, wd):
    tok_sorted, w2d, seg_start, seg_count = _moe_dispatch_plan(top_e, top_w, counts, h2p.shape[0] // PACK_ROWS)
    return _moe(tok_sorted, seg_start, seg_count, h2p, w2d, wg, wu, wd)


def _moe_dispatch_plan(top_e, top_w, counts, n):
    flat_e = top_e.reshape(-1)
    flat_t = jnp.arange(n * TOP_K, dtype=jnp.int32) % n
    flat_w = top_w.reshape(-1)
    _, tok_sorted, w_sorted = lax.sort((flat_e, flat_t, flat_w), num_keys=1, is_stable=True)
    total = flat_e.shape[0]
    tok_sorted = jnp.concatenate([tok_sorted, jnp.full((MOE_ROWS,), n, jnp.int32)])
    table_rows = -(-(total // LANES + 2) // 8) * 8
    w2d = jnp.concatenate([w_sorted, jnp.zeros((table_rows * LANES - total,), F32)]).reshape(table_rows, LANES)
    cnt = counts[:, 0].astype(jnp.int32)
    return tok_sorted, w2d, jnp.cumsum(cnt) - cnt, cnt


def _regroup_rows(w):
    half = D_MODEL // 2
    return jnp.concatenate([w[r0 + LANES * c:r0 + LANES * (c + 1)]
                            for c in range(PACK_ROWS) for r0 in (0, half)], axis=0)


def _final_kernel(x1_ref, h2p_ref, r_ref, mod_ref, g_ref, wg_ref, wu_ref, wd_ref, y_ref):
    tm = x1_ref.shape[0]
    kc = 2 * LANES
    mod = mod_ref[0]
    xk = [jnp.concatenate(_unpack_bf16_pairs(h2p_ref[pl.ds(c, tm, stride=PACK_ROWS), :]), axis=1)
          for c in range(PACK_ROWS)]
    hg = sum(_dot(xk[c], wg_ref[kc * c:kc * (c + 1), :]) for c in range(PACK_ROWS))
    hu = sum(_dot(xk[c], wu_ref[kc * c:kc * (c + 1), :]) for c in range(PACK_ROWS))
    shared = _dot((_silu(hg) * hu).astype(BF16), wd_ref[...])
    routed = jnp.concatenate([r_ref[pl.ds(c, tm, stride=ACC_ROWS), :] for c in range(ACC_ROWS)], axis=1)
    ffn = routed + shared
    y_ref[...] = x1_ref[...] + mod[:, 5 * D_MODEL:6 * D_MODEL] * _rms(ffn, g_ref[...])


def _final(x1, h2p, routed, mods3, g, wg, wu, wd, row_of_tile):
    n = x1.shape[0]
    tm = TOK_TILE
    sd = wg.shape[1]
    row = lambda w: pl.BlockSpec((tm, w), lambda i: (i, 0))
    const = lambda r, c: pl.BlockSpec((r, c), lambda i: (0, 0))
    return pl.pallas_call(
        _final_kernel,
        grid=(n // tm,),
        in_specs=[row(D_MODEL), pl.BlockSpec((tm * PACK_ROWS, LANES), lambda i: (i, 0)),
                  pl.BlockSpec((tm * ACC_ROWS, LANES), lambda i: (i, 0)),
                  pl.BlockSpec((1, 1, 6 * D_MODEL), lambda i: (row_of_tile(i), 0, 0)),
                  const(1, D_MODEL), const(D_MODEL, sd), const(D_MODEL, sd), const(sd, D_MODEL)],
        out_specs=row(D_MODEL),
        out_shape=jax.ShapeDtypeStruct((n, D_MODEL), F32),
        compiler_params=_cparams(("parallel",)),
        name="final",
    )(x1, h2p, routed, mods3, g.reshape(1, D_MODEL), wg, wu, wd)


def _trunk(x3, mods3, row_of_tile, attend, s0, wts):
    b, t, _ = x3.shape
    n = b * t
    x = x3.reshape(n, D_MODEL)
    q, k, v, gdn, z, gate, ba = _premix(x, mods3, wts["g_pre_mix"], wts["w_cat"], row_of_tile)
    na_o = attend(q, k, v).reshape(n, NA_WIDTH)
    qkv = _gdn_conv(gdn.reshape(b, t, GDN_CONV_CH), wts["conv_w"])
    ba3 = ba.reshape(b, t, LANES)
    o_f, s_f = _gdn_chunks(qkv, ba3, wts["alog_row"], wts["dt_row"], s0, reverse=False)
    o_b, s_b = _gdn_chunks(qkv, ba3, wts["alog_row"], wts["dt_row"], s0, reverse=True)
    s_fin = jnp.stack([s_f, s_b], axis=1)
    x1, h2, h2p = _mix(x, na_o, o_f.reshape(n, GDN_V_WIDTH), o_b.reshape(n, GDN_V_WIDTH), z, gate, mods3,
                       wts["gdn_norm_w"],
                       wts["g_post_mix"], wts["g_pre_ffn"], wts["w_na_up"], wts["w_gdn_up"],
                       wts["w_out"], row_of_tile)
    top_e, top_w, counts = _router(h2, wts["w_rt_hi"], wts["w_rt_lo"], wts["router_bias_col"])
    routed = _moe_routed(top_e, top_w, counts, h2p, wts["w_exp_gate"], wts["w_exp_up"], wts["w_exp_down"])
    y = _final(x1, h2p, routed, mods3, wts["g_post_ffn"], wts["w_sh_gate"], wts["w_sh_up"],
               wts["w_sh_down"], row_of_tile)
    return y.reshape(b, t, D_MODEL), k, v, s_fin


def kernel(x_prompt, x_sample, cache_na_k, cache_na_v, state_gdn, c, c_ctx, w_ada, b_ada, g_pre_mix,
           g_post_mix, g_pre_ffn, g_post_ffn, w_in, conv_w, gdn_a_log, gdn_dt_bias, gdn_norm_w, na_rpb,
           w_na_up, w_gdn_up, w_out, w_router, router_bias, w_exp_gate, w_exp_up, w_exp_down, w_sh_gate,
           w_sh_up, w_sh_down):
    depth = w_ada.shape[0]
    bp, tp, _ = x_prompt.shape
    bs, ts, _ = x_sample.shape
    y_prompt, y_sample = x_prompt, x_sample
    zero_state = jnp.zeros((bp, 2, GDN_HEADS, GDN_DK, GDN_DV), F32)
    new_k, new_v, new_s = [], [], []
    for l in range(depth):
        cv = jnp.concatenate([c_ctx[None], c, jnp.zeros((8 - 1 - bs, D_MODEL), F32)], axis=0)
        mods3 = _ada(cv, w_ada[l], b_ada[l]).reshape(8, 1, 6 * D_MODEL)
        wl = w_in[l]
        w_cat = jnp.concatenate(
            [wl[:, :S_Z], wl[:, S_A:], wl[:, S_Z:S_A],
             jnp.zeros((D_MODEL, LANES - 4 * GDN_HEADS), F32)], axis=1).astype(BF16)
        pad = jnp.zeros((2 * GDN_HEADS,), F32)
        tail = jnp.zeros((LANES - 4 * GDN_HEADS,), F32)
        w_rt = w_router[l].astype(F32).T
        w_rt_hi = w_rt.astype(BF16)
        wts = dict(
            w_cat=w_cat, g_pre_mix=g_pre_mix[l], g_post_mix=g_post_mix[l], g_pre_ffn=g_pre_ffn[l],
            g_post_ffn=g_post_ffn[l], conv_w=conv_w[l], gdn_norm_w=gdn_norm_w[l],
            alog_row=jnp.concatenate([pad, gdn_a_log[l].reshape(-1), tail]).reshape(1, LANES),
            dt_row=jnp.concatenate([pad, gdn_dt_bias[l].reshape(-1), tail]).reshape(1, LANES),
            w_na_up=w_na_up[l].astype(BF16), w_gdn_up=w_gdn_up[l].astype(BF16),
            w_out=w_out[l].astype(BF16), w_rt_hi=w_rt_hi, w_rt_lo=(w_rt - w_rt_hi.astype(F32)).astype(BF16),
            router_bias_col=jnp.broadcast_to(router_bias[l].astype(F32)[:, None], (N_EXPERTS, LANES)),
            w_exp_gate=w_exp_gate[l], w_exp_up=w_exp_up[l], w_exp_down=w_exp_down[l],
            w_sh_gate=_regroup_rows(w_sh_gate[l]).astype(BF16), w_sh_up=_regroup_rows(w_sh_up[l]).astype(BF16),
            w_sh_down=w_sh_down[l].astype(BF16))

        def ctx_attend(q, k, v):
            return _ctx_attn(q.reshape(bp, tp, NA_WIDTH), k.reshape(bp, tp, NA_WIDTH),
                             v.reshape(bp, tp, NA_WIDTH))

        y_prompt, k_ctx, v_ctx, s_ctx = _trunk(y_prompt, mods3, lambda i: 0, ctx_attend, zero_state, wts)
        new_k.append(k_ctx.reshape(bp, tp, NA_HEADS, NA_HEAD_DIM))
        new_v.append(v_ctx.reshape(bp, tp, NA_HEADS, NA_HEAD_DIM))
        new_s.append(s_ctx)

        pair_tab, row_mask = _na_bias_tables(na_rpb[l], ts // GRID_W)
        ck = cache_na_k[:, l].reshape(bs, -1, NA_WIDTH)
        cvv = cache_na_v[:, l].reshape(bs, -1, NA_WIDTH)

        def na_attend(q, k, v):
            return _na_attn(q.reshape(bs, ts, NA_WIDTH), k.reshape(bs, ts, NA_WIDTH),
                            v.reshape(bs, ts, NA_WIDTH), ck, cvv, pair_tab, row_mask)

        tiles_per_seq = ts // TOK_TILE
        y_sample, _, _, _ = _trunk(y_sample, mods3, lambda i: 1 + i // tiles_per_seq, na_attend,
                                   state_gdn[:, l], wts)
    return (y_prompt, y_sample, jnp.stack(new_k, axis=1), jnp.stack(new_v, axis=1),
            jnp.stack(new_s, axis=1))
```

```python
import functools

import numpy as np
import jax
import jax.numpy as jnp
from jax import lax
from jax.experimental import pallas as pl
from jax.experimental.pallas import tpu as pltpu

F32 = jnp.float32
BF16 = jnp.bfloat16
HI = lax.Precision.HIGHEST

D_MODEL = 1024
GRID_W = 64
NA_HEADS = 8
NA_HEAD_DIM = 64
NA_WIDTH = NA_HEADS * NA_HEAD_DIM
NA_KR = 8
NA_KC = 16
GDN_HEADS = 4
GDN_DK = 128
GDN_DV = 128
GDN_QK_WIDTH = GDN_HEADS * GDN_DK
GDN_V_WIDTH = GDN_HEADS * GDN_DV
GDN_CONV_CH = 2 * GDN_QK_WIDTH + GDN_V_WIDTH
CONV_K = 5
CHUNK = 64
N_EXPERTS = 256
TOP_K = 8
N_GROUPS = 8
TOPK_GROUPS = 4
EXPERT_DIM = 256
ROUTED_SCALE = 2.5
EPS = 1e-6
S_NA = 3 * NA_WIDTH
S_GDN = S_NA + GDN_CONV_CH
S_Z = S_GDN + GDN_V_WIDTH
S_B = S_Z + 2 * GDN_HEADS
S_A = S_B + 2 * GDN_HEADS

LANES = 128
TOK_TILE = 256
NA_QROWS = 8
NA_SPAN = 16
MOE_ROWS = 128
PACK_ROWS = D_MODEL // 2 // LANES
ACC_ROWS = D_MODEL // LANES
MOE_ROW_GROUP = 16
GDN_CHUNKS_PER_STEP = 8
GDN_CONV_BLOCK_ROWS = 4096
VMEM_LIMIT = 56 * 1024 * 1024
NEG_INF = float("-inf")


def _cparams(sem, vmem=VMEM_LIMIT):
    return pltpu.CompilerParams(dimension_semantics=sem, vmem_limit_bytes=vmem)


def _silu(x):
    return x * jax.nn.sigmoid(x)


def _rms(x, g):
    return x * lax.rsqrt(jnp.mean(x * x, axis=-1, keepdims=True) + EPS) * g


def _dot(a, b):
    return jnp.dot(a, b, preferred_element_type=F32)


def _dot_nt(a, b, precision=None):
    return lax.dot_general(a, b, (((1,), (1,)), ((), ())), precision=precision,
                           preferred_element_type=F32)


def _ada_kernel(c_ref, w_ref, b_ref, o_ref):
    o_ref[...] = jnp.dot(_silu(c_ref[...]), w_ref[...], precision=HI,
                         preferred_element_type=F32) + b_ref[...]


def _ada(cv, w_ada, b_ada):
    n = w_ada.shape[1]
    tn = 512
    return pl.pallas_call(
        _ada_kernel,
        grid=(n // tn,),
        in_specs=[pl.BlockSpec((8, D_MODEL), lambda j: (0, 0)),
                  pl.BlockSpec((D_MODEL, tn), lambda j: (0, j)),
                  pl.BlockSpec((1, tn), lambda j: (0, j))],
        out_specs=pl.BlockSpec((8, tn), lambda j: (0, j)),
        out_shape=jax.ShapeDtypeStruct((8, n), F32),
        compiler_params=_cparams(("parallel",)),
        name="ada",
    )(cv, w_ada, b_ada.reshape(1, n))


_PM_WIDTHS = (NA_WIDTH, NA_WIDTH, NA_WIDTH, GDN_CONV_CH, GDN_V_WIDTH, 2 * D_MODEL, LANES)


def _premix_kernel(x_ref, mod_ref, g_ref, w_ref, *o_refs):
    mod = mod_ref[0]
    h = _rms(x_ref[...], g_ref[...]) * (1.0 + mod[:, D_MODEL:2 * D_MODEL]) + mod[:, 0:D_MODEL]
    hb = h.astype(BF16)
    off = 0
    for o_ref, wd in zip(o_refs, _PM_WIDTHS):
        for c0 in range(0, wd, 512):
            c1 = min(c0 + 512, wd)
            o_ref[:, c0:c1] = _dot(hb, w_ref[:, off + c0:off + c1])
        off += wd


def _premix(x, mods3, g, w_cat, row_of_tile):
    n = x.shape[0]
    wtot = w_cat.shape[1]
    tm = TOK_TILE
    return pl.pallas_call(
        _premix_kernel,
        grid=(n // tm,),
        in_specs=[pl.BlockSpec((tm, D_MODEL), lambda i: (i, 0)),
                  pl.BlockSpec((1, 1, 6 * D_MODEL), lambda i: (row_of_tile(i), 0, 0)),
                  pl.BlockSpec((1, D_MODEL), lambda i: (0, 0)),
                  pl.BlockSpec((D_MODEL, wtot), lambda i: (0, 0))],
        out_specs=[pl.BlockSpec((tm, wd), lambda i: (i, 0)) for wd in _PM_WIDTHS],
        out_shape=[jax.ShapeDtypeStruct((n, wd), F32) for wd in _PM_WIDTHS],
        compiler_params=_cparams(("parallel",)),
        name="premix",
    )(x, mods3, g.reshape(1, D_MODEL), w_cat)


def _softmax_rows(s):
    m = jnp.max(s, axis=-1, keepdims=True)
    p = jnp.exp(s - m)
    return p / jnp.sum(p, axis=-1, keepdims=True)


def _ctx_attn_kernel(q_ref, k_ref, v_ref, o_ref):
    scale = NA_HEAD_DIM ** -0.5
    first = lax.broadcasted_iota(jnp.int32, (q_ref.shape[1], LANES), 1) < NA_HEAD_DIM
    for hp in range(NA_WIDTH // LANES):
        lanes = slice(hp * LANES, (hp + 1) * LANES)
        q = q_ref[0, :, lanes]
        k = k_ref[0, :, lanes].astype(BF16)
        v = v_ref[0, :, lanes].astype(BF16)
        outs = []
        for hh in range(2):
            qm = jnp.where(first if hh == 0 else ~first, q, 0.0).astype(BF16)
            p = _softmax_rows(_dot_nt(qm, k) * scale)
            outs.append(_dot(p.astype(BF16), v))
        o_ref[0, :, lanes] = jnp.where(first, outs[0], outs[1])


def _ctx_attn(q, k, v):
    b, t, w = q.shape
    spec = pl.BlockSpec((1, t, w), lambda i: (i, 0, 0))
    return pl.pallas_call(
        _ctx_attn_kernel,
        grid=(b,),
        in_specs=[spec, spec, spec],
        out_specs=spec,
        out_shape=jax.ShapeDtypeStruct((b, t, w), F32),
        compiler_params=_cparams(("parallel",)),
        name="ctx_attn",
    )(q, k, v)


def _na_span_base(j, rows):
    return np.clip(NA_QROWS * j - NA_KR // 2, 0, rows - NA_SPAN)


NA_DR_PAD = NA_SPAN - NA_KR


def _na_bias_tables(rpb, rows):
    col = np.arange(GRID_W)
    dcm = np.clip(col[None, :] - col[:, None], -(NA_KC - 1), NA_KC - 1) + (NA_KC - 1)
    onehot = (dcm[None] == np.arange(2 * NA_KC - 1)[:, None, None]).astype(np.float32)
    tab = jnp.einsum('hrd,dqk->hrqk', rpb.astype(F32), jnp.asarray(onehot), precision=HI)
    col_start = np.clip(col - NA_KC // 2, 0, GRID_W - NA_KC)
    col_in = (col[None, :] >= col_start[:, None]) & (col[None, :] < col_start[:, None] + NA_KC)
    tab = jnp.where(jnp.asarray(col_in)[None, None], tab, NEG_INF)
    n_dr = 2 * NA_KR - 1
    n_side = NA_SPAN - NA_KR + 1
    blank_lo = jnp.full((NA_HEADS, NA_DR_PAD, GRID_W, GRID_W), NEG_INF, F32)
    blank_hi = jnp.full((NA_HEADS, n_side, GRID_W, GRID_W), NEG_INF, F32)
    padded = jnp.concatenate([blank_lo, tab, blank_hi], axis=1)
    pair_tab = jnp.concatenate([padded[:, :-1], padded[:, 1:]], axis=-1)
    nblk = rows // NA_QROWS
    mask = np.full((3, NA_QROWS, NA_SPAN), NEG_INF, np.float32)
    for p, j in enumerate((0, 1, nblk - 1)):
        base = _na_span_base(j, rows)
        for ri in range(NA_QROWS):
            r = NA_QROWS * j + ri
            rs = np.clip(r - NA_KR // 2, 0, rows - NA_KR)
            for ki in range(NA_SPAN):
                if rs <= base + ki < rs + NA_KR:
                    mask[p, ri, ki] = 0.0
    row_mask = jnp.asarray(np.repeat(mask, GRID_W, axis=2))
    return pair_tab, row_mask


def _na_attn_kernel(q_ref, k_ref, v_ref, ck_ref, cv_ref, tab_ref, mask_ref, o_ref, *, rows):
    j = pl.program_id(2)
    scale = NA_HEAD_DIM ** -0.5
    base = jnp.clip(NA_QROWS * j - NA_KR // 2, 0, rows - NA_SPAN)
    start = pl.multiple_of(base * GRID_W, GRID_W)
    span = NA_SPAN * GRID_W
    q = q_ref[0]
    kl = k_ref[0, pl.ds(start, span), :].astype(BF16)
    vl = v_ref[0, pl.ds(start, span), :].astype(BF16)
    ck = ck_ref[0].astype(BF16)
    cv = cv_ref[0].astype(BF16)
    first = lax.broadcasted_iota(jnp.int32, q.shape, 1) < NA_HEAD_DIM
    off = base - NA_QROWS * j + (NA_KR - 1) + NA_DR_PAD
    outs = []
    for hh in range(2):
        qm = jnp.where(first if hh == 0 else ~first, q, 0.0).astype(BF16)
        s_raw = _dot_nt(qm, kl) * scale
        blocks = []
        for ri in range(NA_QROWS):
            rws = slice(ri * GRID_W, (ri + 1) * GRID_W)
            pieces = [s_raw[rws, m * LANES:(m + 1) * LANES] + tab_ref[hh, off + 2 * m - ri]
                      for m in range(NA_SPAN // 2)]
            blocks.append(jnp.concatenate(pieces, axis=1) + mask_ref[0, ri:ri + 1, :])
        s_loc = jnp.concatenate(blocks, axis=0)
        s_ctx = _dot_nt(qm, ck) * scale
        m = jnp.maximum(jnp.max(s_loc, axis=-1, keepdims=True), jnp.max(s_ctx, axis=-1, keepdims=True))
        p_loc = jnp.exp(s_loc - m)
        p_ctx = jnp.exp(s_ctx - m)
        den = jnp.sum(p_loc, axis=-1, keepdims=True) + jnp.sum(p_ctx, axis=-1, keepdims=True)
        p_loc = (p_loc / den).astype(BF16)
        p_ctx = (p_ctx / den).astype(BF16)
        outs.append(_dot(p_loc, vl) + _dot(p_ctx, cv))
    o_ref[0] = jnp.where(first, outs[0], outs[1])


def _na_attn(q, k, v, ck, cv, pair_tab, row_mask):
    b, n, w = q.shape
    p = ck.shape[1]
    rows = n // GRID_W
    nblk = rows // NA_QROWS
    qb = NA_QROWS * GRID_W

    def pattern(j):
        return jnp.where(j == 0, 0, jnp.where(j == nblk - 1, 2, 1))

    full = pl.BlockSpec((1, n, LANES), lambda bi, hp, j: (bi, 0, hp))
    ctx = pl.BlockSpec((1, p, LANES), lambda bi, hp, j: (bi, 0, hp))
    blk = pl.BlockSpec((1, qb, LANES), lambda bi, hp, j: (bi, j, hp))
    return pl.pallas_call(
        functools.partial(_na_attn_kernel, rows=rows),
        grid=(b, w // LANES, nblk),
        in_specs=[blk, full, full, ctx, ctx,
                  pl.BlockSpec((2,) + pair_tab.shape[1:], lambda bi, hp, j: (hp, 0, 0, 0)),
                  pl.BlockSpec((1,) + row_mask.shape[1:], lambda bi, hp, j: (pattern(j), 0, 0))],
        out_specs=blk,
        out_shape=jax.ShapeDtypeStruct((b, n, w), F32),
        compiler_params=_cparams(("parallel", "parallel", "arbitrary")),
        name="na_attn",
    )(q, k, v, ck, cv, pair_tab, row_mask)


def _gdn_conv_kernel(x_ref, w_ref, o_ref, *, groups):
    c = pl.program_id(1)
    t = x_ref.shape[1]
    row = lax.broadcasted_iota(jnp.int32, (t, LANES), 0)
    n_qk = 2 * GDN_HEADS
    for i in range(groups):
        lanes = slice(i * LANES, (i + 1) * LANES)
        x = x_ref[0, :, lanes]
        y = jnp.zeros_like(x)
        for jj in range(CONV_K):
            o = jj - CONV_K // 2
            xs = x if o == 0 else pltpu.roll(x, (-o) % t, 0)
            xs = jnp.where((row + o >= 0) & (row + o < t), xs, 0.0)
            y = y + xs * w_ref[jj:jj + 1, lanes]
        y = _silu(y)
        nrm = lax.rsqrt(jnp.sum(y * y, axis=-1, keepdims=True) + EPS)
        o_ref[0, i] = jnp.where(c * groups + i < n_qk, y * nrm, y)


def _gdn_conv(x, conv_w):
    b, t, ch = x.shape
    nc = ch // LANES
    groups = max(g for g in range(1, nc + 1)
                 if nc % g == 0 and (g == 1 or g * t <= GDN_CONV_BLOCK_ROWS))
    return pl.pallas_call(
        functools.partial(_gdn_conv_kernel, groups=groups),
        grid=(b, nc // groups),
        in_specs=[pl.BlockSpec((1, t, groups * LANES), lambda bi, c: (bi, 0, c)),
                  pl.BlockSpec((CONV_K, groups * LANES), lambda bi, c: (0, c))],
        out_specs=pl.BlockSpec((1, groups, t, LANES), lambda bi, c: (bi, c, 0, 0)),
        out_shape=jax.ShapeDtypeStruct((b, nc, t, LANES), F32),
        compiler_params=_cparams(("parallel", "parallel")),
        name="gdn_conv",
    )(x, conv_w)


def _bdot(a, b):
    return jnp.dot(a.astype(BF16), b.astype(BF16), preferred_element_type=F32)


def _split_bf16(x):
    hi = x.astype(BF16)
    return hi, (x - hi.astype(F32)).astype(BF16)


def _dot3(a, b):
    m = a.shape[0]
    ah, al = _split_bf16(a)
    bh, bl = _split_bf16(b)
    top = _dot(jnp.concatenate([ah, al], axis=0), bh)
    return top[:m] + top[m:] + _dot(ah, bl)


def _gdn_chunk_kernel(qkv_ref, ba_ref, alog_ref, dt_ref, s0_ref, o_ref, sfin_ref, s_ref, *, reverse, cb, nb):
    c = pl.program_id(1)
    nh = GDN_HEADS

    @pl.when(c == 0)
    def _():
        s_ref[...] = s0_ref[:, 0]

    ii = lax.broadcasted_iota(jnp.int32, (CHUNK, CHUNK), 0)
    jj = lax.broadcasted_iota(jnp.int32, (CHUNK, CHUNK), 1)
    lag = (jj - ii) if reverse else (ii - jj)
    incl = lag >= 0
    strict = lag > 0
    eye = (ii == jj).astype(F32)
    tri = incl.astype(F32)
    bcol = nh if reverse else 0
    gcol0 = (3 if reverse else 2) * nh
    sub8 = lax.broadcasted_iota(jnp.int32, (8, LANES), 0)
    lane8 = lax.broadcasted_iota(jnp.int32, (8, LANES), 1)
    sel8 = (lane8 == gcol0 + sub8).astype(F32)
    nch = nb * cb
    units = [(ci, h) for ci in range(nch) for h in range(nh)]
    gc_alls, beta_alls, grow8s = [], [], []
    for ci in range(nch):
        ba = ba_ref[ci // cb, (ci % cb) * CHUNK:(ci % cb + 1) * CHUNK, :]
        z = ba + dt_ref[...]
        softplus = jnp.maximum(z, 0.0) + jnp.log1p(jnp.exp(-jnp.abs(z)))
        g_all = -jnp.exp(alog_ref[...]) * softplus
        gc_all = jnp.dot(tri, g_all, precision=HI, preferred_element_type=F32)
        gc_alls.append(gc_all)
        beta_alls.append(jax.nn.sigmoid(ba))
        grow8s.append(_dot_nt(sel8, gc_all, precision=HI))
    gcol = [gc_alls[ci][:, gcol0 + h:gcol0 + h + 1] for ci, h in units]
    beta = [beta_alls[ci][:, bcol + h:bcol + h + 1] for ci, h in units]
    rows = [slice((ci % cb) * CHUNK, (ci % cb + 1) * CHUNK) for ci, _ in units]
    seq = [ci // cb for ci, _ in units]
    k = [qkv_ref[seq[u], nh + h, rows[u], :] for u, (_, h) in enumerate(units)]
    kb = [k[u] * beta[u] for u in range(len(units))]
    q = [qkv_ref[seq[u], h, rows[u], :] * (GDN_DK ** -0.5) for u, (_, h) in enumerate(units)]
    kq = [_dot_nt(jnp.concatenate([kb[u], q[u]], axis=0).astype(BF16), k[u].astype(BF16))
          for u in range(len(units))]
    decay = [jnp.where(incl, jnp.exp(jnp.where(incl, gcol[u] - grow8s[ci][h:h + 1, :], 0.0)), 0.0)
             for u, (ci, h) in enumerate(units)]
    intra = [jnp.where(incl, kq[u][CHUNK:] * decay[u], 0.0) for u in range(len(units))]
    wide = nh * CHUNK
    bd_mask = (lax.broadcasted_iota(jnp.int32, (wide, wide), 0) // CHUNK
               == lax.broadcasted_iota(jnp.int32, (wide, wide), 1) // CHUNK)

    def block_diag(x):
        return jnp.where(bd_mask, jnp.concatenate([x] * nh, axis=0), jnp.zeros((), x.dtype))

    def dot3_bd(a, b):
        m = a.shape[0]
        ah, al = _split_bf16(a)
        bh, bl = _split_bf16(b)
        top = _dot(jnp.concatenate([ah, al], axis=0), block_diag(bh))
        return top[:m] + top[m:] + _dot(ah, block_diag(bl))

    eye_w = jnp.concatenate([eye] * nh, axis=1)
    pw = [jnp.concatenate([-jnp.where(strict, kq[ci * nh + h][:CHUNK] * decay[ci * nh + h], 0.0)
                           for h in range(nh)], axis=1) for ci in range(nch)]
    tmat = [eye_w + p for p in pw]
    pw = [dot3_bd(p, p) for p in pw]
    for _ in range(4):
        pt = [dot3_bd(jnp.concatenate([pw[ci], tmat[ci]], axis=0), pw[ci]) for ci in range(nch)]
        pw = [x[:CHUNK] for x in pt]
        tmat = [tmat[ci] + pt[ci][CHUNK:] for ci in range(nch)]
    tmat = [tmat[ci] + dot3_bd(tmat[ci], pw[ci]) for ci in range(nch)]
    eg = [jnp.exp(g) for g in gcol]
    uw_all = [_dot(block_diag(tmat[ci].astype(BF16)), jnp.concatenate(
        [jnp.concatenate([qkv_ref[ci // cb, 2 * nh + h, rows[ci * nh + h], :] * beta[ci * nh + h],
                          kb[ci * nh + h] * eg[ci * nh + h]], axis=1) for h in range(nh)],
        axis=0).astype(BF16)) for ci in range(nch)]
    uw = [uw_all[ci][h * CHUNK:(h + 1) * CHUNK] for ci, h in units]
    g_last = [g[0:1, :] if reverse else g[CHUNK - 1:CHUNK, :] for g in gcol]
    kd = [k[u] * jnp.exp(g_last[u] - gcol[u]) for u in range(len(units))]
    qe = [q[u] * eg[u] for u in range(len(units))]
    kd_uw = [lax.dot_general(kd[u].astype(BF16), uw[u].astype(BF16), (((0,), (0,)), ((), ())),
                             preferred_element_type=F32) for u in range(len(units))]
    in_uw = [_bdot(intra[u], uw[u]) for u in range(len(units))]
    lhs = [jnp.concatenate([kd_uw[u][:, GDN_DV:], qe[u] - in_uw[u][:, GDN_DV:]], axis=0).astype(BF16)
           for u in range(len(units))]
    s = [s_ref[bi, h] for bi in range(nb) for h in range(nh)]
    for cl in (reversed(range(cb)) if reverse else range(cb)):
        us = [(bi * cb + cl) * nh + h for bi in range(nb) for h in range(nh)]
        r = [_dot(lhs[u], s[i].astype(BF16)) for i, u in enumerate(us)]
        for i, u in enumerate(us):
            h = i % nh
            o_ref[i // nh, rows[u], h * GDN_DV:(h + 1) * GDN_DV] = r[i][GDN_DK:] + in_uw[u][:, :GDN_DV]
        s = [s[i] * jnp.exp(g_last[u]) - r[i][:GDN_DK] + kd_uw[u][:, :GDN_DV] for i, u in enumerate(us)]
    for i in range(nb * nh):
        s_ref[i // nh, i % nh] = s[i]

    @pl.when(c == pl.num_programs(1) - 1)
    def _():
        sfin_ref[...] = s_ref[...]


def _gdn_chunks(qkv, ba, alog_row, dt_row, s0, reverse):
    b, _, t, _ = qkv.shape
    cb = min(GDN_CHUNKS_PER_STEP, t // CHUNK)
    nb = max(g for g in range(1, b + 1) if b % g == 0 and g * cb <= GDN_CHUNKS_PER_STEP)
    rows = cb * CHUNK
    n = t // rows
    d = 1 if reverse else 0

    def blk(c):
        return n - 1 - c if reverse else c

    return pl.pallas_call(
        functools.partial(_gdn_chunk_kernel, reverse=reverse, cb=cb, nb=nb),
        grid=(b // nb, n),
        in_specs=[pl.BlockSpec((nb, 3 * GDN_HEADS, rows, LANES), lambda bi, c: (bi, 0, blk(c), 0)),
                  pl.BlockSpec((nb, rows, LANES), lambda bi, c: (bi, blk(c), 0)),
                  pl.BlockSpec((1, LANES), lambda bi, c: (0, 0)),
                  pl.BlockSpec((1, LANES), lambda bi, c: (0, 0)),
                  pl.BlockSpec((nb, 1, GDN_HEADS, GDN_DK, GDN_DV), lambda bi, c: (bi, d, 0, 0, 0))],
        out_specs=[pl.BlockSpec((nb, rows, GDN_V_WIDTH), lambda bi, c: (bi, blk(c), 0)),
                   pl.BlockSpec((nb, GDN_HEADS, GDN_DK, GDN_DV), lambda bi, c: (bi, 0, 0, 0))],
        out_shape=[jax.ShapeDtypeStruct((b, t, GDN_V_WIDTH), F32),
                   jax.ShapeDtypeStruct((b, GDN_HEADS, GDN_DK, GDN_DV), F32)],
        scratch_shapes=[pltpu.VMEM((nb, GDN_HEADS, GDN_DK, GDN_DV), F32)],
        compiler_params=_cparams(("parallel", "arbitrary")),
        name="gdn_bwd" if reverse else "gdn_fwd",
    )(qkv, ba, alog_row, dt_row, s0)


def _pack_bf16_pairs(h):
    half = D_MODEL // 2
    lo = pltpu.bitcast(h[:, :half].astype(BF16).astype(F32), jnp.uint32)
    hi = pltpu.bitcast(h[:, half:].astype(BF16).astype(F32), jnp.uint32)
    return (hi & jnp.uint32(0xFFFF0000)) | (lo >> 16)


def _unpack_bf16_pairs(w):
    lo = pltpu.bitcast(w << 16, F32).astype(BF16)
    hi = pltpu.bitcast(w & jnp.uint32(0xFFFF0000), F32).astype(BF16)
    return lo, hi


def _mix_kernel(x_ref, na_ref, of_ref, ob_ref, z_ref, gate_ref, mod_ref, gnw_ref, gpost_ref, gpre_ref,
                wna_ref, wgdn_ref, wout_ref, wrh_ref, wrl_ref, rb_ref,
                x1_ref, h2p_ref, e_ref, wt_ref, cnt_ref, cnt_acc_ref):
    mod = mod_ref[0]
    o = of_ref[...] + ob_ref[...]
    parts = []
    for h in range(GDN_HEADS):
        sl = slice(h * GDN_DV, (h + 1) * GDN_DV)
        parts.append(_rms(o[:, sl], gnw_ref[...]) * _silu(z_ref[:, sl]))
    gdn_o = jnp.concatenate(parts, axis=-1)
    a = _dot(na_ref[...].astype(BF16), wna_ref[...])
    b = _dot(gdn_o.astype(BF16), wgdn_ref[...])
    gate = jax.nn.sigmoid(gate_ref[...])
    pre = gate[:, :D_MODEL] * a + gate[:, D_MODEL:] * b
    mix = _dot(pre.astype(BF16), wout_ref[...])
    x1 = x_ref[...] + mod[:, 2 * D_MODEL:3 * D_MODEL] * _rms(mix, gpost_ref[...])
    x1_ref[...] = x1
    h2 = _rms(x1, gpre_ref[...]) * (1.0 + mod[:, 4 * D_MODEL:5 * D_MODEL]) + mod[:, 3 * D_MODEL:4 * D_MODEL]
    _route(h2, wrh_ref, wrl_ref, rb_ref, e_ref, wt_ref, cnt_ref, cnt_acc_ref)
    packed = _pack_bf16_pairs(h2)
    for c in range(PACK_ROWS):
        h2p_ref[pl.ds(c, h2.shape[0], stride=PACK_ROWS), :] = packed[:, c * LANES:(c + 1) * LANES]


def _mix(x, na_o, o_f, o_b, z, gate, mods3, gnw, gpost, gpre, wna, wgdn, wout, w_rt_hi, w_rt_lo, bias_col,
         row_of_tile):
    n = x.shape[0]
    tm = TOK_TILE
    row = lambda w: pl.BlockSpec((tm, w), lambda i: (i, 0))
    const = lambda r, c: pl.BlockSpec((r, c), lambda i: (0, 0))
    return pl.pallas_call(
        _mix_kernel,
        grid=(n // tm,),
        in_specs=[row(D_MODEL), row(NA_WIDTH), row(GDN_V_WIDTH), row(GDN_V_WIDTH),
                  row(GDN_V_WIDTH), row(2 * D_MODEL),
                  pl.BlockSpec((1, 1, 6 * D_MODEL), lambda i: (row_of_tile(i), 0, 0)),
                  const(1, GDN_DV), const(1, D_MODEL), const(1, D_MODEL),
                  const(NA_WIDTH, D_MODEL), const(GDN_V_WIDTH, D_MODEL), const(D_MODEL, D_MODEL),
                  const(N_EXPERTS, D_MODEL), const(N_EXPERTS, D_MODEL), const(N_EXPERTS, LANES)],
        out_specs=[row(D_MODEL), pl.BlockSpec((tm * PACK_ROWS, LANES), lambda i: (i, 0)),
                   pl.BlockSpec((TOP_K, tm), lambda i: (0, i)),
                   pl.BlockSpec((TOP_K, tm), lambda i: (0, i)),
                   const(N_EXPERTS, LANES)],
        out_shape=[jax.ShapeDtypeStruct((n, D_MODEL), F32),
                   jax.ShapeDtypeStruct((n * PACK_ROWS, LANES), jnp.uint32),
                   jax.ShapeDtypeStruct((TOP_K, n), jnp.int32),
                   jax.ShapeDtypeStruct((TOP_K, n), F32),
                   jax.ShapeDtypeStruct((N_EXPERTS, LANES), F32)],
        scratch_shapes=[pltpu.VMEM((N_EXPERTS, LANES), F32)],
        compiler_params=_cparams(("arbitrary",)),
        name="mix",
    )(x, na_o, o_f, o_b, z, gate, mods3, gnw.reshape(1, GDN_DV), gpost.reshape(1, D_MODEL),
      gpre.reshape(1, D_MODEL), wna, wgdn, wout, w_rt_hi, w_rt_lo, bias_col)


def _route(h, wh_ref, wl_ref, b_ref, e_ref, wt_ref, cnt_ref, acc_ref):
    i = pl.program_id(0)
    tm = h.shape[0]
    ne = N_EXPERTS
    per = ne // N_GROUPS
    hh, hl = _split_bf16(h)
    wh = wh_ref[...]
    logits = _dot_nt(wh, hh) + _dot_nt(wl_ref[...], hh) + _dot_nt(wh, hl)
    scores = jax.nn.sigmoid(logits)
    biased = scores + jnp.concatenate([b_ref[...]] * (tm // LANES), axis=1)

    def first_max(x):
        n = x.shape[0]
        iota = lax.broadcasted_iota(jnp.int32, x.shape, 0).astype(F32)
        m = jnp.max(x, axis=0, keepdims=True)
        idx = jnp.min(jnp.where(x == m, iota, float(n)), axis=0, keepdims=True)
        return m, idx, iota

    gs_rows = []
    for g in range(N_GROUPS):
        bg = biased[g * per:(g + 1) * per]
        m1, i1, iota = first_max(bg)
        m2 = jnp.max(jnp.where(iota == i1, NEG_INF, bg), axis=0, keepdims=True)
        gs_rows.append(m1 + m2)
    gs = jnp.concatenate(gs_rows, axis=0)
    gsel = jnp.zeros(gs.shape, F32)
    for _ in range(TOPK_GROUPS):
        _, gi, iota = first_max(gs)
        hit = iota == gi
        gs = jnp.where(hit, NEG_INF, gs)
        gsel = jnp.where(hit, 1.0, gsel)
    masked = jnp.concatenate(
        [jnp.where(gsel[g:g + 1] > 0.0, biased[g * per:(g + 1) * per], NEG_INF) for g in range(N_GROUPS)], axis=0)
    onehot = jnp.zeros((ne, tm), F32)
    e_rows, w_rows = [], []
    for _ in range(TOP_K):
        _, ei, iota = first_max(masked)
        hit = iota == ei
        masked = jnp.where(hit, NEG_INF, masked)
        w_rows.append(jnp.sum(jnp.where(hit, scores, 0.0), axis=0, keepdims=True))
        e_rows.append(ei)
        onehot = onehot + hit.astype(F32)
    w_out = jnp.concatenate(w_rows, axis=0)
    e_ref[...] = jnp.concatenate(e_rows, axis=0).astype(jnp.int32)
    wt_ref[...] = w_out / jnp.sum(w_out, axis=0, keepdims=True) * ROUTED_SCALE

    @pl.when(i == 0)
    def _():
        acc_ref[...] = jnp.zeros_like(acc_ref)

    acc_ref[...] += sum(onehot[:, c * LANES:(c + 1) * LANES] for c in range(tm // LANES))

    @pl.when(i == pl.num_programs(0) - 1)
    def _():
        cnt_ref[...] = jnp.broadcast_to(jnp.sum(acc_ref[...], axis=1, keepdims=True), cnt_ref.shape)


_ST_SLOT, _ST_READY, _ST_PEND_BASE = range(3)


def _moe_kernel(tok_ref, seg_ref, cnt_ref, h2p_hbm, w2d_ref, wg_ref, wu_ref, wd_ref, out_hbm,
                h2p_ref, acc_ref, xs_ref, ye_ref, st_ref, sem):
    e = pl.program_id(0)
    n_exp = pl.num_programs(0)
    half = D_MODEL // 2
    xw, yw = PACK_ROWS, ACC_ROWS
    n_tok = out_hbm.shape[0] // yw
    grp = MOE_ROW_GROUP

    n_groups = MOE_ROWS // grp

    def gather_tile(base, slot, groups=(0, n_groups)):
        for r in range(groups[0] * grp, groups[1] * grp):
            tok = tok_ref[base + r]
            xs_ref[slot, xw * r:xw * (r + 1), :] = h2p_ref[pl.ds(pl.multiple_of(tok * xw, xw), xw), :]

    def scatter_tile(base, slot, groups=(0, n_groups)):
        for g in range(*groups):
            toks = [pl.multiple_of(tok_ref[base + g * grp + j] * yw, yw) for j in range(grp)]
            rows = [acc_ref[pl.ds(toks[j], yw), :] + ye_ref[slot, yw * (g * grp + j):yw * (g * grp + j + 1), :]
                    for j in range(grp)]
            for j in reversed(range(grp)):
                acc_ref[pl.ds(toks[j], yw), :] = rows[j]

    seg = seg_ref[e]
    cnt = cnt_ref[e]
    n_tiles = (cnt + MOE_ROWS - 1) // MOE_ROWS

    @pl.when(e == 0)
    def _():
        cp = pltpu.make_async_copy(h2p_hbm, h2p_ref.at[pl.ds(0, n_tok * xw)], sem.at[0])
        cp.start()
        acc_ref[...] = jnp.zeros_like(acc_ref)
        ye_ref[...] = jnp.zeros_like(ye_ref)
        h2p_ref[pl.ds(n_tok * xw, grp * xw), :] = jnp.zeros((grp * xw, LANES), jnp.uint32)
        st_ref[_ST_SLOT] = 0
        st_ref[_ST_READY] = -1
        st_ref[_ST_PEND_BASE] = 0
        cp.wait()

    @pl.when((n_tiles > 0) & (st_ref[_ST_READY] != e))
    def _():
        gather_tile(seg, st_ref[_ST_SLOT])

    def chunk(w_ref, c):
        return jnp.concatenate([w_ref[0, LANES * c:LANES * (c + 1), :],
                                w_ref[0, half + LANES * c:half + LANES * (c + 1), :]], axis=0).astype(BF16)

    lane = lax.broadcasted_iota(jnp.int32, (8, LANES), 1)
    nxt_seg = seg_ref[jnp.minimum(e + 1, n_exp - 1)]

    def tile_body(t, carry):
        slot = st_ref[_ST_SLOT]
        base = seg + t * MOE_ROWS
        last = t + 1 == n_tiles
        xk = [jnp.concatenate(_unpack_bf16_pairs(xs_ref[slot, pl.ds(c, MOE_ROWS, stride=xw), :]), axis=1)
              for c in range(xw)]
        g_base = jnp.where(last, nxt_seg, base + MOE_ROWS)
        p_base = st_ref[_ST_PEND_BASE]
        cuts = [n_groups * i // (xw + 1) for i in range(xw + 2)]

        def move_rows(i):
            gather_tile(g_base, 1 - slot, (cuts[i], cuts[i + 1]))
            scatter_tile(p_base, 1 - slot, (cuts[i], cuts[i + 1]))

        hg = hu = 0.0
        for c in range(xw):
            move_rows(c)
            hg = hg + _dot(xk[c], chunk(wg_ref, c))
            hu = hu + _dot(xk[c], chunk(wu_ref, c))
        act = (_silu(hg) * hu).astype(BF16)
        move_rows(xw)
        q = base // LANES
        sh = base % LANES
        rot = (LANES - sh) % LANES
        row_a = pltpu.roll(jnp.broadcast_to(w2d_ref[pl.ds(q, 1), :], (8, LANES)), rot, 1)
        row_b = pltpu.roll(jnp.broadcast_to(w2d_ref[pl.ds(q + 1, 1), :], (8, LANES)), rot, 1)
        w_row = jnp.where(lane + sh < LANES, row_a, row_b)
        w_row = jnp.where(lane < cnt - t * MOE_ROWS, w_row, 0.0)[0:1, :]
        w_col = jnp.broadcast_to(w_row, (LANES, LANES)).T[:MOE_ROWS]
        ye = _dot(act, wd_ref[0].astype(BF16))
        for c in range(yw):
            ye_ref[slot, pl.ds(c, MOE_ROWS, stride=yw), :] = ye[:, LANES * c:LANES * (c + 1)] * w_col
        st_ref[_ST_PEND_BASE] = base
        st_ref[_ST_SLOT] = 1 - slot
        st_ref[_ST_READY] = jnp.where(last, e + 1, e)
        return carry

    lax.fori_loop(0, n_tiles, tile_body, 0)

    @pl.when(e == n_exp - 1)
    def _():
        scatter_tile(st_ref[_ST_PEND_BASE], 1 - st_ref[_ST_SLOT])
        cp = pltpu.make_async_copy(acc_ref.at[pl.ds(0, n_tok * yw)], out_hbm, sem.at[1])
        cp.start()
        cp.wait()


def _moe(tok_sorted, seg_start, seg_count, h2p, w2d, wg, wu, wd):
    xw, yw = PACK_ROWS, ACC_ROWS
    n = h2p.shape[0] // xw
    grid_spec = pltpu.PrefetchScalarGridSpec(
        num_scalar_prefetch=3,
        grid=(N_EXPERTS,),
        in_specs=[pl.BlockSpec(memory_space=pl.ANY),
                  pl.BlockSpec(w2d.shape, lambda e, *_: (0, 0)),
                  pl.BlockSpec((1, D_MODEL, EXPERT_DIM), lambda e, *_: (e, 0, 0)),
                  pl.BlockSpec((1, D_MODEL, EXPERT_DIM), lambda e, *_: (e, 0, 0)),
                  pl.BlockSpec((1, EXPERT_DIM, D_MODEL), lambda e, *_: (e, 0, 0))],
        out_specs=pl.BlockSpec(memory_space=pl.ANY),
        scratch_shapes=[pltpu.VMEM(((n + MOE_ROW_GROUP) * xw, LANES), jnp.uint32),
                        pltpu.VMEM(((n + MOE_ROW_GROUP) * yw, LANES), F32),
                        pltpu.VMEM((2, MOE_ROWS * xw, LANES), jnp.uint32),
                        pltpu.VMEM((2, MOE_ROWS * yw, LANES), F32),
                        pltpu.SMEM((3,), jnp.int32),
                        pltpu.SemaphoreType.DMA((2,))],
    )
    return pl.pallas_call(
        _moe_kernel,
        grid_spec=grid_spec,
        out_shape=jax.ShapeDtypeStruct((n * yw, LANES), F32),
        compiler_params=_cparams(("arbitrary",), vmem=60 * 1024 * 1024),
        name="moe",
    )(tok_sorted, seg_start, seg_count, h2p, w2d, wg, wu, wd)


def _moe_routed(top_e, top_w, counts, h2p, wg, wu, wd):
    tok_sorted, w2d, seg_start, seg_count = _moe_dispatch_plan(top_e, top_w, counts, h2p.shape[0] // PACK_ROWS)
    return _moe(tok_sorted, seg_start, seg_count, h2p, w2d, wg, wu, wd)


def _moe_dispatch_plan(top_e, top_w, counts, n):
    total = n * TOP_K
    assert N_EXPERTS * total < 2 ** 31
    key = top_e.reshape(-1) * total + jnp.arange(total, dtype=jnp.int32)
    key_sorted, w_sorted = lax.sort((key, top_w.reshape(-1)), num_keys=1, is_stable=False)
    tok_sorted = (key_sorted % total) % n
    tok_sorted = jnp.concatenate([tok_sorted, jnp.full((MOE_ROWS,), n, jnp.int32)])
    table_rows = -(-(total // LANES + 2) // 8) * 8
    w2d = jnp.concatenate([w_sorted, jnp.zeros((table_rows * LANES - total,), F32)]).reshape(table_rows, LANES)
    cnt = counts[:, 0].astype(jnp.int32)
    return tok_sorted, w2d, jnp.cumsum(cnt) - cnt, cnt


def _regroup_rows(w):
    half = D_MODEL // 2
    return jnp.concatenate([w[r0 + LANES * c:r0 + LANES * (c + 1)]
                            for c in range(PACK_ROWS) for r0 in (0, half)], axis=0)


def _final_kernel(x1_ref, h2p_ref, r_ref, mod_ref, g_ref, wg_ref, wu_ref, wd_ref, y_ref):
    tm = x1_ref.shape[0]
    kc = 2 * LANES
    mod = mod_ref[0]
    xk = [jnp.concatenate(_unpack_bf16_pairs(h2p_ref[pl.ds(c, tm, stride=PACK_ROWS), :]), axis=1)
          for c in range(PACK_ROWS)]
    hg = sum(_dot(xk[c], wg_ref[kc * c:kc * (c + 1), :]) for c in range(PACK_ROWS))
    hu = sum(_dot(xk[c], wu_ref[kc * c:kc * (c + 1), :]) for c in range(PACK_ROWS))
    shared = _dot((_silu(hg) * hu).astype(BF16), wd_ref[...])
    routed = jnp.concatenate([r_ref[pl.ds(c, tm, stride=ACC_ROWS), :] for c in range(ACC_ROWS)], axis=1)
    ffn = routed + shared
    y_ref[...] = x1_ref[...] + mod[:, 5 * D_MODEL:6 * D_MODEL] * _rms(ffn, g_ref[...])


def _final(x1, h2p, routed, mods3, g, wg, wu, wd, row_of_tile):
    n = x1.shape[0]
    tm = TOK_TILE
    sd = wg.shape[1]
    row = lambda w: pl.BlockSpec((tm, w), lambda i: (i, 0))
    const = lambda r, c: pl.BlockSpec((r, c), lambda i: (0, 0))
    return pl.pallas_call(
        _final_kernel,
        grid=(n // tm,),
        in_specs=[row(D_MODEL), pl.BlockSpec((tm * PACK_ROWS, LANES), lambda i: (i, 0)),
                  pl.BlockSpec((tm * ACC_ROWS, LANES), lambda i: (i, 0)),
                  pl.BlockSpec((1, 1, 6 * D_MODEL), lambda i: (row_of_tile(i), 0, 0)),
                  const(1, D_MODEL), const(D_MODEL, sd), const(D_MODEL, sd), const(sd, D_MODEL)],
        out_specs=row(D_MODEL),
        out_shape=jax.ShapeDtypeStruct((n, D_MODEL), F32),
        compiler_params=_cparams(("parallel",)),
        name="final",
    )(x1, h2p, routed, mods3, g.reshape(1, D_MODEL), wg, wu, wd)


def _trunk(x3, mods3, row_of_tile, attend, s0, wts):
    b, t, _ = x3.shape
    n = b * t
    x = x3.reshape(n, D_MODEL)
    q, k, v, gdn, z, gate, ba = _premix(x, mods3, wts["g_pre_mix"], wts["w_cat"], row_of_tile)
    na_o = attend(q, k, v).reshape(n, NA_WIDTH)
    qkv = _gdn_conv(gdn.reshape(b, t, GDN_CONV_CH), wts["conv_w"])
    ba3 = ba.reshape(b, t, LANES)
    o_f, s_f = _gdn_chunks(qkv, ba3, wts["alog_row"], wts["dt_row"], s0, reverse=False)
    o_b, s_b = _gdn_chunks(qkv, ba3, wts["alog_row"], wts["dt_row"], s0, reverse=True)
    s_fin = jnp.stack([s_f, s_b], axis=1)
    x1, h2p, top_e, top_w, counts = _mix(
        x, na_o, o_f.reshape(n, GDN_V_WIDTH), o_b.reshape(n, GDN_V_WIDTH), z, gate, mods3, wts["gdn_norm_w"],
        wts["g_post_mix"], wts["g_pre_ffn"], wts["w_na_up"], wts["w_gdn_up"], wts["w_out"],
        wts["w_rt_hi"], wts["w_rt_lo"], wts["router_bias_col"], row_of_tile)
    routed = _moe_routed(top_e, top_w, counts, h2p, wts["w_exp_gate"], wts["w_exp_up"], wts["w_exp_down"])
    y = _final(x1, h2p, routed, mods3, wts["g_post_ffn"], wts["w_sh_gate"], wts["w_sh_up"],
               wts["w_sh_down"], row_of_tile)
    return y.reshape(b, t, D_MODEL), k, v, s_fin


def kernel(x_prompt, x_sample, cache_na_k, cache_na_v, state_gdn, c, c_ctx, w_ada, b_ada, g_pre_mix,
           g_post_mix, g_pre_ffn, g_post_ffn, w_in, conv_w, gdn_a_log, gdn_dt_bias, gdn_norm_w, na_rpb,
           w_na_up, w_gdn_up, w_out, w_router, router_bias, w_exp_gate, w_exp_up, w_exp_down, w_sh_gate,
           w_sh_up, w_sh_down):
    depth = w_ada.shape[0]
    bp, tp, _ = x_prompt.shape
    bs, ts, _ = x_sample.shape
    y_prompt, y_sample = x_prompt, x_sample
    zero_state = jnp.zeros((bp, 2, GDN_HEADS, GDN_DK, GDN_DV), F32)
    new_k, new_v, new_s = [], [], []
    for l in range(depth):
        cv = jnp.concatenate([c_ctx[None], c, jnp.zeros((8 - 1 - bs, D_MODEL), F32)], axis=0)
        mods3 = _ada(cv, w_ada[l], b_ada[l]).reshape(8, 1, 6 * D_MODEL)
        wl = w_in[l]
        w_cat = jnp.concatenate(
            [wl[:, :S_Z], wl[:, S_A:], wl[:, S_Z:S_A],
             jnp.zeros((D_MODEL, LANES - 4 * GDN_HEADS), F32)], axis=1).astype(BF16)
        pad = jnp.zeros((2 * GDN_HEADS,), F32)
        tail = jnp.zeros((LANES - 4 * GDN_HEADS,), F32)
        w_rt = w_router[l].astype(F32).T
        w_rt_hi = w_rt.astype(BF16)
        wts = dict(
            w_cat=w_cat, g_pre_mix=g_pre_mix[l], g_post_mix=g_post_mix[l], g_pre_ffn=g_pre_ffn[l],
            g_post_ffn=g_post_ffn[l], conv_w=conv_w[l], gdn_norm_w=gdn_norm_w[l],
            alog_row=jnp.concatenate([pad, gdn_a_log[l].reshape(-1), tail]).reshape(1, LANES),
            dt_row=jnp.concatenate([pad, gdn_dt_bias[l].reshape(-1), tail]).reshape(1, LANES),
            w_na_up=w_na_up[l].astype(BF16), w_gdn_up=w_gdn_up[l].astype(BF16),
            w_out=w_out[l].astype(BF16), w_rt_hi=w_rt_hi, w_rt_lo=(w_rt - w_rt_hi.astype(F32)).astype(BF16),
            router_bias_col=jnp.broadcast_to(router_bias[l].astype(F32)[:, None], (N_EXPERTS, LANES)),
            w_exp_gate=w_exp_gate[l], w_exp_up=w_exp_up[l], w_exp_down=w_exp_down[l],
            w_sh_gate=_regroup_rows(w_sh_gate[l]).astype(BF16), w_sh_up=_regroup_rows(w_sh_up[l]).astype(BF16),
            w_sh_down=w_sh_down[l].astype(BF16))

        def ctx_attend(q, k, v):
            return _ctx_attn(q.reshape(bp, tp, NA_WIDTH), k.reshape(bp, tp, NA_WIDTH),
                             v.reshape(bp, tp, NA_WIDTH))

        y_prompt, k_ctx, v_ctx, s_ctx = _trunk(y_prompt, mods3, lambda i: 0, ctx_attend, zero_state, wts)
        new_k.append(k_ctx.reshape(bp, tp, NA_HEADS, NA_HEAD_DIM))
        new_v.append(v_ctx.reshape(bp, tp, NA_HEADS, NA_HEAD_DIM))
        new_s.append(s_ctx)

        pair_tab, row_mask = _na_bias_tables(na_rpb[l], ts // GRID_W)
        ck = cache_na_k[:, l].reshape(bs, -1, NA_WIDTH)
        cvv = cache_na_v[:, l].reshape(bs, -1, NA_WIDTH)

        def na_attend(q, k, v):
            return _na_attn(q.reshape(bs, ts, NA_WIDTH), k.reshape(bs, ts, NA_WIDTH),
                            v.reshape(bs, ts, NA_WIDTH), ck, cvv, pair_tab, row_mask)

        tiles_per_seq = ts // TOK_TILE
        y_sample, _, _, _ = _trunk(y_sample, mods3, lambda i: 1 + i // tiles_per_seq, na_attend,
                                   state_gdn[:, l], wts)
    return (y_prompt, y_sample, jnp.stack(new_k, axis=1), jnp.stack(new_v, axis=1),
            jnp.stack(new_s, axis=1))
```

```python
import functools

import numpy as np
import jax
import jax.numpy as jnp
from jax import lax
from jax.experimental import pallas as pl
from jax.experimental.pallas import tpu as pltpu

F32 = jnp.float32
BF16 = jnp.bfloat16
HI = lax.Precision.HIGHEST

D_MODEL = 1024
GRID_W = 64
NA_HEADS = 8
NA_HEAD_DIM = 64
NA_WIDTH = NA_HEADS * NA_HEAD_DIM
NA_KR = 8
NA_KC = 16
GDN_HEADS = 4
GDN_DK = 128
GDN_DV = 128
GDN_QK_WIDTH = GDN_HEADS * GDN_DK
GDN_V_WIDTH = GDN_HEADS * GDN_DV
GDN_CONV_CH = 2 * GDN_QK_WIDTH + GDN_V_WIDTH
CONV_K = 5
CHUNK = 64
N_EXPERTS = 256
TOP_K = 8
N_GROUPS = 8
TOPK_GROUPS = 4
EXPERT_DIM = 256
ROUTED_SCALE = 2.5
EPS = 1e-6
S_NA = 3 * NA_WIDTH
S_GDN = S_NA + GDN_CONV_CH
S_Z = S_GDN + GDN_V_WIDTH
S_B = S_Z + 2 * GDN_HEADS
S_A = S_B + 2 * GDN_HEADS

LANES = 128
SUBLANES = 8
COL_TILE = 512
TOK_TILE = 256
NA_QROWS = 8
NA_SPAN = 16
MOE_ROWS = 128
PACK_ROWS = D_MODEL // 2 // LANES
ACC_ROWS = D_MODEL // LANES
MOE_ROW_GROUP = 16
GDN_CHUNKS_PER_STEP = 8
GDN_CONV_BLOCK_ROWS = 4096
VMEM_LIMIT = 56 * 1024 * 1024
MOE_VMEM_LIMIT = 60 * 1024 * 1024
assert 2 * NA_HEAD_DIM == LANES
NEG_INF = float("-inf")


def _cparams(sem, vmem=VMEM_LIMIT):
    return pltpu.CompilerParams(dimension_semantics=sem, vmem_limit_bytes=vmem)


def _silu(x):
    return x * jax.nn.sigmoid(x)


def _rms(x, g):
    return x * lax.rsqrt(jnp.mean(x * x, axis=-1, keepdims=True) + EPS) * g


def _dot(a, b):
    return jnp.dot(a, b, preferred_element_type=F32)


def _dot_nt(a, b, precision=None):
    return lax.dot_general(a, b, (((1,), (1,)), ((), ())), precision=precision,
                           preferred_element_type=F32)


def _ada_kernel(c_ref, w_ref, b_ref, o_ref):
    o_ref[...] = jnp.dot(_silu(c_ref[...]), w_ref[...], precision=HI,
                         preferred_element_type=F32) + b_ref[...]


def _ada(cv, w_ada, b_ada):
    n = w_ada.shape[1]
    tn = COL_TILE
    return pl.pallas_call(
        _ada_kernel,
        grid=(n // tn,),
        in_specs=[pl.BlockSpec((SUBLANES, D_MODEL), lambda j: (0, 0)),
                  pl.BlockSpec((D_MODEL, tn), lambda j: (0, j)),
                  pl.BlockSpec((1, tn), lambda j: (0, j))],
        out_specs=pl.BlockSpec((SUBLANES, tn), lambda j: (0, j)),
        out_shape=jax.ShapeDtypeStruct((SUBLANES, n), F32),
        compiler_params=_cparams(("parallel",)),
        name="ada",
    )(cv, w_ada, b_ada.reshape(1, n))


_PM_WIDTHS = (NA_WIDTH, NA_WIDTH, NA_WIDTH, GDN_CONV_CH, GDN_V_WIDTH, 2 * D_MODEL, LANES)


def _premix_kernel(x_ref, mod_ref, g_ref, w_ref, *o_refs):
    mod = mod_ref[0]
    h = _rms(x_ref[...], g_ref[...]) * (1.0 + mod[:, D_MODEL:2 * D_MODEL]) + mod[:, 0:D_MODEL]
    hb = h.astype(BF16)
    off = 0
    for o_ref, wd in zip(o_refs, _PM_WIDTHS):
        for c0 in range(0, wd, COL_TILE):
            c1 = min(c0 + COL_TILE, wd)
            o_ref[:, c0:c1] = _dot(hb, w_ref[:, off + c0:off + c1])
        off += wd


def _premix(x, mods3, g, w_cat, row_of_tile):
    n = x.shape[0]
    wtot = w_cat.shape[1]
    tm = TOK_TILE
    return pl.pallas_call(
        _premix_kernel,
        grid=(n // tm,),
        in_specs=[pl.BlockSpec((tm, D_MODEL), lambda i: (i, 0)),
                  pl.BlockSpec((1, 1, 6 * D_MODEL), lambda i: (row_of_tile(i), 0, 0)),
                  pl.BlockSpec((1, D_MODEL), lambda i: (0, 0)),
                  pl.BlockSpec((D_MODEL, wtot), lambda i: (0, 0))],
        out_specs=[pl.BlockSpec((tm, wd), lambda i: (i, 0)) for wd in _PM_WIDTHS],
        out_shape=[jax.ShapeDtypeStruct((n, wd), F32) for wd in _PM_WIDTHS],
        compiler_params=_cparams(("parallel",)),
        name="premix",
    )(x, mods3, g.reshape(1, D_MODEL), w_cat)


def _softmax_rows(s):
    m = jnp.max(s, axis=-1, keepdims=True)
    p = jnp.exp(s - m)
    return p / jnp.sum(p, axis=-1, keepdims=True)


def _ctx_attn_kernel(q_ref, k_ref, v_ref, o_ref):
    scale = NA_HEAD_DIM ** -0.5
    first = lax.broadcasted_iota(jnp.int32, (q_ref.shape[1], LANES), 1) < NA_HEAD_DIM
    for hp in range(NA_WIDTH // LANES):
        lanes = slice(hp * LANES, (hp + 1) * LANES)
        q = q_ref[0, :, lanes]
        k = k_ref[0, :, lanes].astype(BF16)
        v = v_ref[0, :, lanes].astype(BF16)
        outs = []
        for hh in range(2):
            qm = jnp.where(first if hh == 0 else ~first, q, 0.0).astype(BF16)
            p = _softmax_rows(_dot_nt(qm, k) * scale)
            outs.append(_dot(p.astype(BF16), v))
        o_ref[0, :, lanes] = jnp.where(first, outs[0], outs[1])


def _ctx_attn(q, k, v):
    b, t, w = q.shape
    spec = pl.BlockSpec((1, t, w), lambda i: (i, 0, 0))
    return pl.pallas_call(
        _ctx_attn_kernel,
        grid=(b,),
        in_specs=[spec, spec, spec],
        out_specs=spec,
        out_shape=jax.ShapeDtypeStruct((b, t, w), F32),
        compiler_params=_cparams(("parallel",)),
        name="ctx_attn",
    )(q, k, v)


def _na_span_base(j, rows):
    return np.clip(NA_QROWS * j - NA_KR // 2, 0, rows - NA_SPAN)


NA_DR_PAD = NA_SPAN - NA_KR


def _na_bias_tables(rpb, rows):
    col = np.arange(GRID_W)
    dcm = np.clip(col[None, :] - col[:, None], -(NA_KC - 1), NA_KC - 1) + (NA_KC - 1)
    onehot = (dcm[None] == np.arange(2 * NA_KC - 1)[:, None, None]).astype(np.float32)
    tab = jnp.einsum('hrd,dqk->hrqk', rpb.astype(F32), jnp.asarray(onehot), precision=HI)
    col_start = np.clip(col - NA_KC // 2, 0, GRID_W - NA_KC)
    col_in = (col[None, :] >= col_start[:, None]) & (col[None, :] < col_start[:, None] + NA_KC)
    tab = jnp.where(jnp.asarray(col_in)[None, None], tab, NEG_INF)
    n_side = NA_SPAN - NA_KR + 1
    blank_lo = jnp.full((NA_HEADS, NA_DR_PAD, GRID_W, GRID_W), NEG_INF, F32)
    blank_hi = jnp.full((NA_HEADS, n_side, GRID_W, GRID_W), NEG_INF, F32)
    padded = jnp.concatenate([blank_lo, tab, blank_hi], axis=1)
    pair_tab = jnp.concatenate([padded[:, :-1], padded[:, 1:]], axis=-1)
    nblk = rows // NA_QROWS
    mask = np.full((3, NA_QROWS, NA_SPAN), NEG_INF, np.float32)
    for p, j in enumerate((0, 1, nblk - 1)):
        base = _na_span_base(j, rows)
        for ri in range(NA_QROWS):
            r = NA_QROWS * j + ri
            rs = np.clip(r - NA_KR // 2, 0, rows - NA_KR)
            for ki in range(NA_SPAN):
                if rs <= base + ki < rs + NA_KR:
                    mask[p, ri, ki] = 0.0
    row_mask = jnp.asarray(np.repeat(mask, GRID_W, axis=2))
    return pair_tab, row_mask


def _na_attn_kernel(q_ref, k_ref, v_ref, ck_ref, cv_ref, tab_ref, mask_ref, o_ref, *, rows):
    j = pl.program_id(2)
    scale = NA_HEAD_DIM ** -0.5
    base = jnp.clip(NA_QROWS * j - NA_KR // 2, 0, rows - NA_SPAN)
    start = pl.multiple_of(base * GRID_W, GRID_W)
    span = NA_SPAN * GRID_W
    q = q_ref[0]
    kl = k_ref[0, pl.ds(start, span), :].astype(BF16)
    vl = v_ref[0, pl.ds(start, span), :].astype(BF16)
    ck = ck_ref[0].astype(BF16)
    cv = cv_ref[0].astype(BF16)
    first = lax.broadcasted_iota(jnp.int32, q.shape, 1) < NA_HEAD_DIM
    off = base - NA_QROWS * j + (NA_KR - 1) + NA_DR_PAD
    outs = []
    for hh in range(2):
        qm = jnp.where(first if hh == 0 else ~first, q, 0.0).astype(BF16)
        s_raw = _dot_nt(qm, kl) * scale
        blocks = []
        for ri in range(NA_QROWS):
            rws = slice(ri * GRID_W, (ri + 1) * GRID_W)
            pieces = [s_raw[rws, m * LANES:(m + 1) * LANES] + tab_ref[hh, off + 2 * m - ri]
                      for m in range(NA_SPAN // 2)]
            blocks.append(jnp.concatenate(pieces, axis=1) + mask_ref[0, ri:ri + 1, :])
        s_loc = jnp.concatenate(blocks, axis=0)
        s_ctx = _dot_nt(qm, ck) * scale
        m = jnp.maximum(jnp.max(s_loc, axis=-1, keepdims=True), jnp.max(s_ctx, axis=-1, keepdims=True))
        p_loc = jnp.exp(s_loc - m)
        p_ctx = jnp.exp(s_ctx - m)
        den = jnp.sum(p_loc, axis=-1, keepdims=True) + jnp.sum(p_ctx, axis=-1, keepdims=True)
        p_loc = (p_loc / den).astype(BF16)
        p_ctx = (p_ctx / den).astype(BF16)
        outs.append(_dot(p_loc, vl) + _dot(p_ctx, cv))
    o_ref[0] = jnp.where(first, outs[0], outs[1])


def _na_attn(q, k, v, ck, cv, pair_tab, row_mask):
    b, n, w = q.shape
    p = ck.shape[1]
    rows = n // GRID_W
    nblk = rows // NA_QROWS
    qb = NA_QROWS * GRID_W

    def pattern(j):
        return jnp.where(j == 0, 0, jnp.where(j == nblk - 1, 2, 1))

    full = pl.BlockSpec((1, n, LANES), lambda bi, hp, j: (bi, 0, hp))
    ctx = pl.BlockSpec((1, p, LANES), lambda bi, hp, j: (bi, 0, hp))
    blk = pl.BlockSpec((1, qb, LANES), lambda bi, hp, j: (bi, j, hp))
    return pl.pallas_call(
        functools.partial(_na_attn_kernel, rows=rows),
        grid=(b, w // LANES, nblk),
        in_specs=[blk, full, full, ctx, ctx,
                  pl.BlockSpec((2,) + pair_tab.shape[1:], lambda bi, hp, j: (hp, 0, 0, 0)),
                  pl.BlockSpec((1,) + row_mask.shape[1:], lambda bi, hp, j: (pattern(j), 0, 0))],
        out_specs=blk,
        out_shape=jax.ShapeDtypeStruct((b, n, w), F32),
        compiler_params=_cparams(("parallel", "parallel", "arbitrary")),
        name="na_attn",
    )(q, k, v, ck, cv, pair_tab, row_mask)


def _gdn_conv_kernel(x_ref, w_ref, o_ref, *, groups):
    c = pl.program_id(1)
    t = x_ref.shape[1]
    row = lax.broadcasted_iota(jnp.int32, (t, LANES), 0)
    n_qk = 2 * GDN_HEADS
    for i in range(groups):
        lanes = slice(i * LANES, (i + 1) * LANES)
        x = x_ref[0, :, lanes]
        y = jnp.zeros_like(x)
        for jj in range(CONV_K):
            o = jj - CONV_K // 2
            xs = x if o == 0 else pltpu.roll(x, (-o) % t, 0)
            xs = jnp.where((row + o >= 0) & (row + o < t), xs, 0.0)
            y = y + xs * w_ref[jj:jj + 1, lanes]
        y = _silu(y)
        nrm = lax.rsqrt(jnp.sum(y * y, axis=-1, keepdims=True) + EPS)
        o_ref[0, i] = jnp.where(c * groups + i < n_qk, y * nrm, y)


def _gdn_conv(x, conv_w):
    b, t, ch = x.shape
    nc = ch // LANES
    groups = max(g for g in range(1, nc + 1)
                 if nc % g == 0 and (g == 1 or g * t <= GDN_CONV_BLOCK_ROWS))
    return pl.pallas_call(
        functools.partial(_gdn_conv_kernel, groups=groups),
        grid=(b, nc // groups),
        in_specs=[pl.BlockSpec((1, t, groups * LANES), lambda bi, c: (bi, 0, c)),
                  pl.BlockSpec((CONV_K, groups * LANES), lambda bi, c: (0, c))],
        out_specs=pl.BlockSpec((1, groups, t, LANES), lambda bi, c: (bi, c, 0, 0)),
        out_shape=jax.ShapeDtypeStruct((b, nc, t, LANES), F32),
        compiler_params=_cparams(("parallel", "parallel")),
        name="gdn_conv",
    )(x, conv_w)


def _bdot(a, b):
    return jnp.dot(a.astype(BF16), b.astype(BF16), preferred_element_type=F32)


def _split_bf16(x):
    hi = x.astype(BF16)
    return hi, (x - hi.astype(F32)).astype(BF16)


def _dot3(a, b):
    m = a.shape[0]
    ah, al = _split_bf16(a)
    bh, bl = _split_bf16(b)
    top = _dot(jnp.concatenate([ah, al], axis=0), bh)
    return top[:m] + top[m:] + _dot(ah, bl)


def _gdn_chunk_kernel(qkv_ref, ba_ref, alog_ref, dt_ref, s0_ref, o_ref, sfin_ref, s_ref, *, reverse, cb, nb):
    c = pl.program_id(1)
    nh = GDN_HEADS

    @pl.when(c == 0)
    def _():
        s_ref[...] = s0_ref[:, 0]

    ii = lax.broadcasted_iota(jnp.int32, (CHUNK, CHUNK), 0)
    jj = lax.broadcasted_iota(jnp.int32, (CHUNK, CHUNK), 1)
    lag = (jj - ii) if reverse else (ii - jj)
    incl = lag >= 0
    strict = lag > 0
    eye = (ii == jj).astype(F32)
    tri = incl.astype(F32)
    bcol = nh if reverse else 0
    gcol0 = (3 if reverse else 2) * nh
    sub8 = lax.broadcasted_iota(jnp.int32, (8, LANES), 0)
    lane8 = lax.broadcasted_iota(jnp.int32, (8, LANES), 1)
    sel8 = (lane8 == gcol0 + sub8).astype(F32)
    nch = nb * cb
    units = [(ci, h) for ci in range(nch) for h in range(nh)]
    gc_alls, beta_alls, grow8s = [], [], []
    for ci in range(nch):
        ba = ba_ref[ci // cb, (ci % cb) * CHUNK:(ci % cb + 1) * CHUNK, :]
        z = ba + dt_ref[...]
        softplus = jnp.maximum(z, 0.0) + jnp.log1p(jnp.exp(-jnp.abs(z)))
        g_all = -jnp.exp(alog_ref[...]) * softplus
        gc_all = jnp.dot(tri, g_all, precision=HI, preferred_element_type=F32)
        gc_alls.append(gc_all)
        beta_alls.append(jax.nn.sigmoid(ba))
        grow8s.append(_dot_nt(sel8, gc_all, precision=HI))
    gcol = [gc_alls[ci][:, gcol0 + h:gcol0 + h + 1] for ci, h in units]
    beta = [beta_alls[ci][:, bcol + h:bcol + h + 1] for ci, h in units]
    rows = [slice((ci % cb) * CHUNK, (ci % cb + 1) * CHUNK) for ci, _ in units]
    seq = [ci // cb for ci, _ in units]
    k = [qkv_ref[seq[u], nh + h, rows[u], :] for u, (_, h) in enumerate(units)]
    kb = [k[u] * beta[u] for u in range(len(units))]
    q = [qkv_ref[seq[u], h, rows[u], :] * (GDN_DK ** -0.5) for u, (_, h) in enumerate(units)]
    kq = [_dot_nt(jnp.concatenate([kb[u], q[u]], axis=0).astype(BF16), k[u].astype(BF16))
          for u in range(len(units))]
    decay = [jnp.where(incl, jnp.exp(jnp.where(incl, gcol[u] - grow8s[ci][h:h + 1, :], 0.0)), 0.0)
             for u, (ci, h) in enumerate(units)]
    intra = [jnp.where(incl, kq[u][CHUNK:] * decay[u], 0.0) for u in range(len(units))]
    wide = nh * CHUNK
    bd_mask = (lax.broadcasted_iota(jnp.int32, (wide, wide), 0) // CHUNK
               == lax.broadcasted_iota(jnp.int32, (wide, wide), 1) // CHUNK)

    def block_diag(x):
        return jnp.where(bd_mask, jnp.concatenate([x] * nh, axis=0), jnp.zeros((), x.dtype))

    def dot3_bd(a, b):
        m = a.shape[0]
        ah, al = _split_bf16(a)
        bh, bl = _split_bf16(b)
        top = _dot(jnp.concatenate([ah, al], axis=0), block_diag(bh))
        return top[:m] + top[m:] + _dot(ah, block_diag(bl))

    eye_w = jnp.concatenate([eye] * nh, axis=1)
    pw = [jnp.concatenate([-jnp.where(strict, kq[ci * nh + h][:CHUNK] * decay[ci * nh + h], 0.0)
                           for h in range(nh)], axis=1) for ci in range(nch)]
    tmat = [eye_w + p for p in pw]
    pw = [dot3_bd(p, p) for p in pw]
    for _ in range(4):
        pt = [dot3_bd(jnp.concatenate([pw[ci], tmat[ci]], axis=0), pw[ci]) for ci in range(nch)]
        pw = [x[:CHUNK] for x in pt]
        tmat = [tmat[ci] + pt[ci][CHUNK:] for ci in range(nch)]
    tmat = [tmat[ci] + dot3_bd(tmat[ci], pw[ci]) for ci in range(nch)]
    eg = [jnp.exp(g) for g in gcol]
    uw_all = [_dot(block_diag(tmat[ci].astype(BF16)), jnp.concatenate(
        [jnp.concatenate([qkv_ref[ci // cb, 2 * nh + h, rows[ci * nh + h], :] * beta[ci * nh + h],
                          kb[ci * nh + h] * eg[ci * nh + h]], axis=1) for h in range(nh)],
        axis=0).astype(BF16)) for ci in range(nch)]
    uw = [uw_all[ci][h * CHUNK:(h + 1) * CHUNK] for ci, h in units]
    g_last = [g[0:1, :] if reverse else g[CHUNK - 1:CHUNK, :] for g in gcol]
    kd = [k[u] * jnp.exp(g_last[u] - gcol[u]) for u in range(len(units))]
    qe = [q[u] * eg[u] for u in range(len(units))]
    kd_uw = [lax.dot_general(kd[u].astype(BF16), uw[u].astype(BF16), (((0,), (0,)), ((), ())),
                             preferred_element_type=F32) for u in range(len(units))]
    in_uw = [_bdot(intra[u], uw[u]) for u in range(len(units))]
    lhs = [jnp.concatenate([kd_uw[u][:, GDN_DV:], qe[u] - in_uw[u][:, GDN_DV:]], axis=0).astype(BF16)
           for u in range(len(units))]
    s = [s_ref[bi, h] for bi in range(nb) for h in range(nh)]
    for cl in (reversed(range(cb)) if reverse else range(cb)):
        us = [(bi * cb + cl) * nh + h for bi in range(nb) for h in range(nh)]
        r = [_dot(lhs[u], s[i].astype(BF16)) for i, u in enumerate(us)]
        for i, u in enumerate(us):
            h = i % nh
            o_ref[i // nh, rows[u], h * GDN_DV:(h + 1) * GDN_DV] = r[i][GDN_DK:] + in_uw[u][:, :GDN_DV]
        s = [s[i] * jnp.exp(g_last[u]) - r[i][:GDN_DK] + kd_uw[u][:, :GDN_DV] for i, u in enumerate(us)]
    for i in range(nb * nh):
        s_ref[i // nh, i % nh] = s[i]

    @pl.when(c == pl.num_programs(1) - 1)
    def _():
        sfin_ref[...] = s_ref[...]


def _gdn_chunks(qkv, ba, alog_row, dt_row, s0, reverse):
    b, _, t, _ = qkv.shape
    cb = min(GDN_CHUNKS_PER_STEP, t // CHUNK)
    nb = max(g for g in range(1, b + 1) if b % g == 0 and g * cb <= GDN_CHUNKS_PER_STEP)
    rows = cb * CHUNK
    n = t // rows
    d = 1 if reverse else 0

    def blk(c):
        return n - 1 - c if reverse else c

    return pl.pallas_call(
        functools.partial(_gdn_chunk_kernel, reverse=reverse, cb=cb, nb=nb),
        grid=(b // nb, n),
        in_specs=[pl.BlockSpec((nb, 3 * GDN_HEADS, rows, LANES), lambda bi, c: (bi, 0, blk(c), 0)),
                  pl.BlockSpec((nb, rows, LANES), lambda bi, c: (bi, blk(c), 0)),
                  pl.BlockSpec((1, LANES), lambda bi, c: (0, 0)),
                  pl.BlockSpec((1, LANES), lambda bi, c: (0, 0)),
                  pl.BlockSpec((nb, 1, GDN_HEADS, GDN_DK, GDN_DV), lambda bi, c: (bi, d, 0, 0, 0))],
        out_specs=[pl.BlockSpec((nb, rows, GDN_V_WIDTH), lambda bi, c: (bi, blk(c), 0)),
                   pl.BlockSpec((nb, GDN_HEADS, GDN_DK, GDN_DV), lambda bi, c: (bi, 0, 0, 0))],
        out_shape=[jax.ShapeDtypeStruct((b, t, GDN_V_WIDTH), F32),
                   jax.ShapeDtypeStruct((b, GDN_HEADS, GDN_DK, GDN_DV), F32)],
        scratch_shapes=[pltpu.VMEM((nb, GDN_HEADS, GDN_DK, GDN_DV), F32)],
        compiler_params=_cparams(("parallel", "arbitrary")),
        name="gdn_bwd" if reverse else "gdn_fwd",
    )(qkv, ba, alog_row, dt_row, s0)


def _pack_bf16_pairs(h):
    half = D_MODEL // 2
    lo = pltpu.bitcast(h[:, :half].astype(BF16).astype(F32), jnp.uint32)
    hi = pltpu.bitcast(h[:, half:].astype(BF16).astype(F32), jnp.uint32)
    return (hi & jnp.uint32(0xFFFF0000)) | (lo >> 16)


def _unpack_bf16_pairs(w):
    lo = pltpu.bitcast(w << 16, F32).astype(BF16)
    hi = pltpu.bitcast(w & jnp.uint32(0xFFFF0000), F32).astype(BF16)
    return lo, hi


def _mix_kernel(x_ref, na_ref, of_ref, ob_ref, z_ref, gate_ref, mod_ref, gnw_ref, gpost_ref, gpre_ref,
                wna_ref, wgdn_ref, wout_ref, wrh_ref, wrl_ref, rb_ref,
                x1_ref, h2p_ref, e_ref, wt_ref, cnt_ref, cnt_acc_ref):
    mod = mod_ref[0]
    o = of_ref[...] + ob_ref[...]
    parts = []
    for h in range(GDN_HEADS):
        sl = slice(h * GDN_DV, (h + 1) * GDN_DV)
        parts.append(_rms(o[:, sl], gnw_ref[...]) * _silu(z_ref[:, sl]))
    gdn_o = jnp.concatenate(parts, axis=-1)
    a = _dot(na_ref[...].astype(BF16), wna_ref[...])
    b = _dot(gdn_o.astype(BF16), wgdn_ref[...])
    gate = jax.nn.sigmoid(gate_ref[...])
    pre = gate[:, :D_MODEL] * a + gate[:, D_MODEL:] * b
    mix = _dot(pre.astype(BF16), wout_ref[...])
    x1 = x_ref[...] + mod[:, 2 * D_MODEL:3 * D_MODEL] * _rms(mix, gpost_ref[...])
    x1_ref[...] = x1
    h2 = _rms(x1, gpre_ref[...]) * (1.0 + mod[:, 4 * D_MODEL:5 * D_MODEL]) + mod[:, 3 * D_MODEL:4 * D_MODEL]
    _route(h2, wrh_ref, wrl_ref, rb_ref, e_ref, wt_ref, cnt_ref, cnt_acc_ref)
    packed = _pack_bf16_pairs(h2)
    for c in range(PACK_ROWS):
        h2p_ref[pl.ds(c, h2.shape[0], stride=PACK_ROWS), :] = packed[:, c * LANES:(c + 1) * LANES]


def _mix(x, na_o, o_f, o_b, z, gate, mods3, gnw, gpost, gpre, wna, wgdn, wout, w_rt_hi, w_rt_lo, bias_col,
         row_of_tile):
    n = x.shape[0]
    tm = TOK_TILE
    row = lambda w: pl.BlockSpec((tm, w), lambda i: (i, 0))
    const = lambda r, c: pl.BlockSpec((r, c), lambda i: (0, 0))
    return pl.pallas_call(
        _mix_kernel,
        grid=(n // tm,),
        in_specs=[row(D_MODEL), row(NA_WIDTH), row(GDN_V_WIDTH), row(GDN_V_WIDTH),
                  row(GDN_V_WIDTH), row(2 * D_MODEL),
                  pl.BlockSpec((1, 1, 6 * D_MODEL), lambda i: (row_of_tile(i), 0, 0)),
                  const(1, GDN_DV), const(1, D_MODEL), const(1, D_MODEL),
                  const(NA_WIDTH, D_MODEL), const(GDN_V_WIDTH, D_MODEL), const(D_MODEL, D_MODEL),
                  const(N_EXPERTS, D_MODEL), const(N_EXPERTS, D_MODEL), const(N_EXPERTS, LANES)],
        out_specs=[row(D_MODEL), pl.BlockSpec((tm * PACK_ROWS, LANES), lambda i: (i, 0)),
                   pl.BlockSpec((TOP_K, tm), lambda i: (0, i)),
                   pl.BlockSpec((TOP_K, tm), lambda i: (0, i)),
                   const(N_EXPERTS, LANES)],
        out_shape=[jax.ShapeDtypeStruct((n, D_MODEL), F32),
                   jax.ShapeDtypeStruct((n * PACK_ROWS, LANES), jnp.uint32),
                   jax.ShapeDtypeStruct((TOP_K, n), jnp.int32),
                   jax.ShapeDtypeStruct((TOP_K, n), F32),
                   jax.ShapeDtypeStruct((N_EXPERTS, LANES), F32)],
        scratch_shapes=[pltpu.VMEM((N_EXPERTS, LANES), F32)],
        compiler_params=_cparams(("arbitrary",)),
        name="mix",
    )(x, na_o, o_f, o_b, z, gate, mods3, gnw.reshape(1, GDN_DV), gpost.reshape(1, D_MODEL),
      gpre.reshape(1, D_MODEL), wna, wgdn, wout, w_rt_hi, w_rt_lo, bias_col)


def _route(h, wh_ref, wl_ref, b_ref, e_ref, wt_ref, cnt_ref, acc_ref):
    i = pl.program_id(0)
    tm = h.shape[0]
    ne = N_EXPERTS
    per = ne // N_GROUPS
    hh, hl = _split_bf16(h)
    wh = wh_ref[...]
    logits = _dot_nt(wh, hh) + _dot_nt(wl_ref[...], hh) + _dot_nt(wh, hl)
    scores = jax.nn.sigmoid(logits)
    biased = scores + jnp.concatenate([b_ref[...]] * (tm // LANES), axis=1)

    def first_max(x):
        n = x.shape[0]
        iota = lax.broadcasted_iota(jnp.int32, x.shape, 0).astype(F32)
        m = jnp.max(x, axis=0, keepdims=True)
        idx = jnp.min(jnp.where(x == m, iota, float(n)), axis=0, keepdims=True)
        return m, idx, iota

    gs_rows = []
    for g in range(N_GROUPS):
        bg = biased[g * per:(g + 1) * per]
        m1, i1, iota = first_max(bg)
        m2 = jnp.max(jnp.where(iota == i1, NEG_INF, bg), axis=0, keepdims=True)
        gs_rows.append(m1 + m2)
    gs = jnp.concatenate(gs_rows, axis=0)
    gsel = jnp.zeros(gs.shape, F32)
    for _ in range(TOPK_GROUPS):
        _, gi, iota = first_max(gs)
        hit = iota == gi
        gs = jnp.where(hit, NEG_INF, gs)
        gsel = jnp.where(hit, 1.0, gsel)
    masked = jnp.concatenate(
        [jnp.where(gsel[g:g + 1] > 0.0, biased[g * per:(g + 1) * per], NEG_INF) for g in range(N_GROUPS)], axis=0)
    onehot = jnp.zeros((ne, tm), F32)
    e_rows, w_rows = [], []
    for _ in range(TOP_K):
        _, ei, iota = first_max(masked)
        hit = iota == ei
        masked = jnp.where(hit, NEG_INF, masked)
        w_rows.append(jnp.sum(jnp.where(hit, scores, 0.0), axis=0, keepdims=True))
        e_rows.append(ei)
        onehot = onehot + hit.astype(F32)
    w_out = jnp.concatenate(w_rows, axis=0)
    e_ref[...] = jnp.concatenate(e_rows, axis=0).astype(jnp.int32)
    wt_ref[...] = w_out / jnp.sum(w_out, axis=0, keepdims=True) * ROUTED_SCALE

    @pl.when(i == 0)
    def _():
        acc_ref[...] = jnp.zeros_like(acc_ref)

    acc_ref[...] += sum(onehot[:, c * LANES:(c + 1) * LANES] for c in range(tm // LANES))

    @pl.when(i == pl.num_programs(0) - 1)
    def _():
        cnt_ref[...] = jnp.broadcast_to(jnp.sum(acc_ref[...], axis=1, keepdims=True), cnt_ref.shape)


_ST_SLOT, _ST_READY, _ST_PEND_BASE = range(3)


def _moe_kernel(tok_ref, seg_ref, cnt_ref, h2p_hbm, w2d_ref, wg_ref, wu_ref, wd_ref, out_hbm,
                h2p_ref, acc_ref, xs_ref, ye_ref, st_ref, sem):
    e = pl.program_id(0)
    n_exp = pl.num_programs(0)
    half = D_MODEL // 2
    xw, yw = PACK_ROWS, ACC_ROWS
    n_tok = out_hbm.shape[0] // yw
    grp = MOE_ROW_GROUP

    n_groups = MOE_ROWS // grp

    def gather_tile(base, slot, groups=(0, n_groups)):
        for r in range(groups[0] * grp, groups[1] * grp):
            tok = tok_ref[base + r]
            xs_ref[slot, xw * r:xw * (r + 1), :] = h2p_ref[pl.ds(pl.multiple_of(tok * xw, xw), xw), :]

    def scatter_tile(base, slot, groups=(0, n_groups)):
        for g in range(*groups):
            toks = [pl.multiple_of(tok_ref[base + g * grp + j] * yw, yw) for j in range(grp)]
            rows = [acc_ref[pl.ds(toks[j], yw), :] + ye_ref[slot, yw * (g * grp + j):yw * (g * grp + j + 1), :]
                    for j in range(grp)]
            for j in reversed(range(grp)):
                acc_ref[pl.ds(toks[j], yw), :] = rows[j]

    seg = seg_ref[e]
    cnt = cnt_ref[e]
    n_tiles = (cnt + MOE_ROWS - 1) // MOE_ROWS

    @pl.when(e == 0)
    def _():
        cp = pltpu.make_async_copy(h2p_hbm, h2p_ref.at[pl.ds(0, n_tok * xw)], sem.at[0])
        cp.start()
        acc_ref[...] = jnp.zeros_like(acc_ref)
        ye_ref[...] = jnp.zeros_like(ye_ref)
        h2p_ref[pl.ds(n_tok * xw, grp * xw), :] = jnp.zeros((grp * xw, LANES), jnp.uint32)
        st_ref[_ST_SLOT] = 0
        st_ref[_ST_READY] = -1
        st_ref[_ST_PEND_BASE] = 0
        cp.wait()

    @pl.when((n_tiles > 0) & (st_ref[_ST_READY] != e))
    def _():
        gather_tile(seg, st_ref[_ST_SLOT])

    def chunk(w_ref, c):
        return jnp.concatenate([w_ref[0, LANES * c:LANES * (c + 1), :],
                                w_ref[0, half + LANES * c:half + LANES * (c + 1), :]], axis=0).astype(BF16)

    lane = lax.broadcasted_iota(jnp.int32, (8, LANES), 1)
    nxt_seg = seg_ref[jnp.minimum(e + 1, n_exp - 1)]

    def tile_body(t, carry):
        slot = st_ref[_ST_SLOT]
        base = seg + t * MOE_ROWS
        last = t + 1 == n_tiles
        xk = [jnp.concatenate(_unpack_bf16_pairs(xs_ref[slot, pl.ds(c, MOE_ROWS, stride=xw), :]), axis=1)
              for c in range(xw)]
        g_base = jnp.where(last, nxt_seg, base + MOE_ROWS)
        p_base = st_ref[_ST_PEND_BASE]
        cuts = [n_groups * i // (xw + 1) for i in range(xw + 2)]

        def move_rows(i):
            gather_tile(g_base, 1 - slot, (cuts[i], cuts[i + 1]))
            scatter_tile(p_base, 1 - slot, (cuts[i], cuts[i + 1]))

        hg = hu = 0.0
        for c in range(xw):
            move_rows(c)
            hg = hg + _dot(xk[c], chunk(wg_ref, c))
            hu = hu + _dot(xk[c], chunk(wu_ref, c))
        act = (_silu(hg) * hu).astype(BF16)
        move_rows(xw)
        q = base // LANES
        sh = base % LANES
        rot = (LANES - sh) % LANES
        row_a = pltpu.roll(jnp.broadcast_to(w2d_ref[pl.ds(q, 1), :], (8, LANES)), rot, 1)
        row_b = pltpu.roll(jnp.broadcast_to(w2d_ref[pl.ds(q + 1, 1), :], (8, LANES)), rot, 1)
        w_row = jnp.where(lane + sh < LANES, row_a, row_b)
        w_row = jnp.where(lane < cnt - t * MOE_ROWS, w_row, 0.0)[0:1, :]
        w_col = jnp.broadcast_to(w_row, (LANES, LANES)).T[:MOE_ROWS]
        ye = _dot(act, wd_ref[0].astype(BF16))
        for c in range(yw):
            ye_ref[slot, pl.ds(c, MOE_ROWS, stride=yw), :] = ye[:, LANES * c:LANES * (c + 1)] * w_col
        st_ref[_ST_PEND_BASE] = base
        st_ref[_ST_SLOT] = 1 - slot
        st_ref[_ST_READY] = jnp.where(last, e + 1, e)
        return carry

    lax.fori_loop(0, n_tiles, tile_body, 0)

    @pl.when(e == n_exp - 1)
    def _():
        scatter_tile(st_ref[_ST_PEND_BASE], 1 - st_ref[_ST_SLOT])
        cp = pltpu.make_async_copy(acc_ref.at[pl.ds(0, n_tok * yw)], out_hbm, sem.at[1])
        cp.start()
        cp.wait()


def _moe(tok_sorted, seg_start, seg_count, h2p, w2d, wg, wu, wd):
    xw, yw = PACK_ROWS, ACC_ROWS
    n = h2p.shape[0] // xw
    grid_spec = pltpu.PrefetchScalarGridSpec(
        num_scalar_prefetch=3,
        grid=(N_EXPERTS,),
        in_specs=[pl.BlockSpec(memory_space=pl.ANY),
                  pl.BlockSpec(w2d.shape, lambda e, *_: (0, 0)),
                  pl.BlockSpec((1, D_MODEL, EXPERT_DIM), lambda e, *_: (e, 0, 0)),
                  pl.BlockSpec((1, D_MODEL, EXPERT_DIM), lambda e, *_: (e, 0, 0)),
                  pl.BlockSpec((1, EXPERT_DIM, D_MODEL), lambda e, *_: (e, 0, 0))],
        out_specs=pl.BlockSpec(memory_space=pl.ANY),
        scratch_shapes=[pltpu.VMEM(((n + MOE_ROW_GROUP) * xw, LANES), jnp.uint32),
                        pltpu.VMEM(((n + MOE_ROW_GROUP) * yw, LANES), F32),
                        pltpu.VMEM((2, MOE_ROWS * xw, LANES), jnp.uint32),
                        pltpu.VMEM((2, MOE_ROWS * yw, LANES), F32),
                        pltpu.SMEM((3,), jnp.int32),
                        pltpu.SemaphoreType.DMA((2,))],
    )
    return pl.pallas_call(
        _moe_kernel,
        grid_spec=grid_spec,
        out_shape=jax.ShapeDtypeStruct((n * yw, LANES), F32),
        compiler_params=_cparams(("arbitrary",), vmem=MOE_VMEM_LIMIT),
        name="moe",
    )(tok_sorted, seg_start, seg_count, h2p, w2d, wg, wu, wd)


def _moe_routed(top_e, top_w, counts, h2p, wg, wu, wd):
    tok_sorted, w2d, seg_start, seg_count = _moe_dispatch_plan(top_e, top_w, counts, h2p.shape[0] // PACK_ROWS)
    return _moe(tok_sorted, seg_start, seg_count, h2p, w2d, wg, wu, wd)


def _moe_dispatch_plan(top_e, top_w, counts, n):
    total = n * TOP_K
    assert N_EXPERTS * total < 2 ** 31
    key = top_e.reshape(-1) * total + jnp.arange(total, dtype=jnp.int32)
    key_sorted, w_sorted = lax.sort((key, top_w.reshape(-1)), num_keys=1, is_stable=False)
    tok_sorted = (key_sorted % total) % n
    tok_sorted = jnp.concatenate([tok_sorted, jnp.full((MOE_ROWS,), n, jnp.int32)])
    table_rows = -(-(total // LANES + 2) // 8) * 8
    w2d = jnp.concatenate([w_sorted, jnp.zeros((table_rows * LANES - total,), F32)]).reshape(table_rows, LANES)
    cnt = counts[:, 0].astype(jnp.int32)
    return tok_sorted, w2d, jnp.cumsum(cnt) - cnt, cnt


def _regroup_rows(w):
    half = D_MODEL // 2
    return jnp.concatenate([w[r0 + LANES * c:r0 + LANES * (c + 1)]
                            for c in range(PACK_ROWS) for r0 in (0, half)], axis=0)


def _final_kernel(x1_ref, h2p_ref, r_ref, mod_ref, g_ref, wg_ref, wu_ref, wd_ref, y_ref):
    tm = x1_ref.shape[0]
    kc = 2 * LANES
    mod = mod_ref[0]
    xk = [jnp.concatenate(_unpack_bf16_pairs(h2p_ref[pl.ds(c, tm, stride=PACK_ROWS), :]), axis=1)
          for c in range(PACK_ROWS)]
    hg = sum(_dot(xk[c], wg_ref[kc * c:kc * (c + 1), :]) for c in range(PACK_ROWS))
    hu = sum(_dot(xk[c], wu_ref[kc * c:kc * (c + 1), :]) for c in range(PACK_ROWS))
    shared = _dot((_silu(hg) * hu).astype(BF16), wd_ref[...])
    routed = jnp.concatenate([r_ref[pl.ds(c, tm, stride=ACC_ROWS), :] for c in range(ACC_ROWS)], axis=1)
    ffn = routed + shared
    y_ref[...] = x1_ref[...] + mod[:, 5 * D_MODEL:6 * D_MODEL] * _rms(ffn, g_ref[...])


def _final(x1, h2p, routed, mods3, g, wg, wu, wd, row_of_tile):
    n = x1.shape[0]
    tm = TOK_TILE
    sd = wg.shape[1]
    row = lambda w: pl.BlockSpec((tm, w), lambda i: (i, 0))
    const = lambda r, c: pl.BlockSpec((r, c), lambda i: (0, 0))
    return pl.pallas_call(
        _final_kernel,
        grid=(n // tm,),
        in_specs=[row(D_MODEL), pl.BlockSpec((tm * PACK_ROWS, LANES), lambda i: (i, 0)),
                  pl.BlockSpec((tm * ACC_ROWS, LANES), lambda i: (i, 0)),
                  pl.BlockSpec((1, 1, 6 * D_MODEL), lambda i: (row_of_tile(i), 0, 0)),
                  const(1, D_MODEL), const(D_MODEL, sd), const(D_MODEL, sd), const(sd, D_MODEL)],
        out_specs=row(D_MODEL),
        out_shape=jax.ShapeDtypeStruct((n, D_MODEL), F32),
        compiler_params=_cparams(("parallel",)),
        name="final",
    )(x1, h2p, routed, mods3, g.reshape(1, D_MODEL), wg, wu, wd)


def _trunk(x3, mods3, row_of_tile, attend, s0, wts):
    b, t, _ = x3.shape
    n = b * t
    x = x3.reshape(n, D_MODEL)
    q, k, v, gdn, z, gate, ba = _premix(x, mods3, wts["g_pre_mix"], wts["w_cat"], row_of_tile)
    na_o = attend(q, k, v).reshape(n, NA_WIDTH)
    qkv = _gdn_conv(gdn.reshape(b, t, GDN_CONV_CH), wts["conv_w"])
    ba3 = ba.reshape(b, t, LANES)
    o_f, s_f = _gdn_chunks(qkv, ba3, wts["alog_row"], wts["dt_row"], s0, reverse=False)
    o_b, s_b = _gdn_chunks(qkv, ba3, wts["alog_row"], wts["dt_row"], s0, reverse=True)
    s_fin = jnp.stack([s_f, s_b], axis=1)
    x1, h2p, top_e, top_w, counts = _mix(
        x, na_o, o_f.reshape(n, GDN_V_WIDTH), o_b.reshape(n, GDN_V_WIDTH), z, gate, mods3, wts["gdn_norm_w"],
        wts["g_post_mix"], wts["g_pre_ffn"], wts["w_na_up"], wts["w_gdn_up"], wts["w_out"],
        wts["w_rt_hi"], wts["w_rt_lo"], wts["router_bias_col"], row_of_tile)
    routed = _moe_routed(top_e, top_w, counts, h2p, wts["w_exp_gate"], wts["w_exp_up"], wts["w_exp_down"])
    y = _final(x1, h2p, routed, mods3, wts["g_post_ffn"], wts["w_sh_gate"], wts["w_sh_up"],
               wts["w_sh_down"], row_of_tile)
    return y.reshape(b, t, D_MODEL), k, v, s_fin


def kernel(x_prompt, x_sample, cache_na_k, cache_na_v, state_gdn, c, c_ctx, w_ada, b_ada, g_pre_mix,
           g_post_mix, g_pre_ffn, g_post_ffn, w_in, conv_w, gdn_a_log, gdn_dt_bias, gdn_norm_w, na_rpb,
           w_na_up, w_gdn_up, w_out, w_router, router_bias, w_exp_gate, w_exp_up, w_exp_down, w_sh_gate,
           w_sh_up, w_sh_down):
    depth = w_ada.shape[0]
    bp, tp, _ = x_prompt.shape
    bs, ts, _ = x_sample.shape
    y_prompt, y_sample = x_prompt, x_sample
    zero_state = jnp.zeros((bp, 2, GDN_HEADS, GDN_DK, GDN_DV), F32)
    new_k, new_v, new_s = [], [], []
    for l in range(depth):
        cv = jnp.concatenate([c_ctx[None], c, jnp.zeros((SUBLANES - 1 - bs, D_MODEL), F32)], axis=0)
        mods3 = _ada(cv, w_ada[l], b_ada[l]).reshape(SUBLANES, 1, 6 * D_MODEL)
        wl = w_in[l]
        w_cat = jnp.concatenate(
            [wl[:, :S_Z], wl[:, S_A:], wl[:, S_Z:S_A],
             jnp.zeros((D_MODEL, LANES - 4 * GDN_HEADS), F32)], axis=1).astype(BF16)
        pad = jnp.zeros((2 * GDN_HEADS,), F32)
        tail = jnp.zeros((LANES - 4 * GDN_HEADS,), F32)
        w_rt = w_router[l].astype(F32).T
        w_rt_hi = w_rt.astype(BF16)
        wts = dict(
            w_cat=w_cat, g_pre_mix=g_pre_mix[l], g_post_mix=g_post_mix[l], g_pre_ffn=g_pre_ffn[l],
            g_post_ffn=g_post_ffn[l], conv_w=conv_w[l], gdn_norm_w=gdn_norm_w[l],
            alog_row=jnp.concatenate([pad, gdn_a_log[l].reshape(-1), tail]).reshape(1, LANES),
            dt_row=jnp.concatenate([pad, gdn_dt_bias[l].reshape(-1), tail]).reshape(1, LANES),
            w_na_up=w_na_up[l].astype(BF16), w_gdn_up=w_gdn_up[l].astype(BF16),
            w_out=w_out[l].astype(BF16), w_rt_hi=w_rt_hi, w_rt_lo=(w_rt - w_rt_hi.astype(F32)).astype(BF16),
            router_bias_col=jnp.broadcast_to(router_bias[l].astype(F32)[:, None], (N_EXPERTS, LANES)),
            w_exp_gate=w_exp_gate[l], w_exp_up=w_exp_up[l], w_exp_down=w_exp_down[l],
            w_sh_gate=_regroup_rows(w_sh_gate[l]).astype(BF16), w_sh_up=_regroup_rows(w_sh_up[l]).astype(BF16),
            w_sh_down=w_sh_down[l].astype(BF16))

        def ctx_attend(q, k, v):
            return _ctx_attn(q.reshape(bp, tp, NA_WIDTH), k.reshape(bp, tp, NA_WIDTH),
                             v.reshape(bp, tp, NA_WIDTH))

        y_prompt, k_ctx, v_ctx, s_ctx = _trunk(y_prompt, mods3, lambda i: 0, ctx_attend, zero_state, wts)
        new_k.append(k_ctx.reshape(bp, tp, NA_HEADS, NA_HEAD_DIM))
        new_v.append(v_ctx.reshape(bp, tp, NA_HEADS, NA_HEAD_DIM))
        new_s.append(s_ctx)

        pair_tab, row_mask = _na_bias_tables(na_rpb[l], ts // GRID_W)
        ck = cache_na_k[:, l].reshape(bs, -1, NA_WIDTH)
        cvv = cache_na_v[:, l].reshape(bs, -1, NA_WIDTH)

        def na_attend(q, k, v):
            return _na_attn(q.reshape(bs, ts, NA_WIDTH), k.reshape(bs, ts, NA_WIDTH),
                            v.reshape(bs, ts, NA_WIDTH), ck, cvv, pair_tab, row_mask)

        tiles_per_seq = ts // TOK_TILE
        y_sample, _, _, _ = _trunk(y_sample, mods3, lambda i: 1 + i // tiles_per_seq, na_attend,
                                   state_gdn[:, l], wts)
    return (y_prompt, y_sample, jnp.stack(new_k, axis=1), jnp.stack(new_v, axis=1),
            jnp.stack(new_s, axis=1))
```

```python
import functools

import numpy as np
import jax
import jax.numpy as jnp
from jax import lax
from jax.experimental import pallas as pl
from jax.experimental.pallas import tpu as pltpu

F32 = jnp.float32
BF16 = jnp.bfloat16
HI = lax.Precision.HIGHEST

D_MODEL = 1024
GRID_W = 64
NA_HEADS = 8
NA_HEAD_DIM = 64
NA_WIDTH = NA_HEADS * NA_HEAD_DIM
NA_KR = 8
NA_KC = 16
GDN_HEADS = 4
GDN_DK = 128
GDN_DV = 128
GDN_QK_WIDTH = GDN_HEADS * GDN_DK
GDN_V_WIDTH = GDN_HEADS * GDN_DV
GDN_CONV_CH = 2 * GDN_QK_WIDTH + GDN_V_WIDTH
CONV_K = 5
CHUNK = 64
N_EXPERTS = 256
TOP_K = 8
N_GROUPS = 8
TOPK_GROUPS = 4
EXPERT_DIM = 256
ROUTED_SCALE = 2.5
EPS = 1e-6
S_NA = 3 * NA_WIDTH
S_GDN = S_NA + GDN_CONV_CH
S_Z = S_GDN + GDN_V_WIDTH
S_B = S_Z + 2 * GDN_HEADS
S_A = S_B + 2 * GDN_HEADS

LANES = 128
SUBLANES = 8
COL_TILE = 512
TOK_TILE = 256
FINAL_TILE = 512
NA_QROWS = 8
NA_SPAN = 16
MOE_ROWS = 128
PACK_ROWS = D_MODEL // 2 // LANES
ACC_ROWS = D_MODEL // LANES
MOE_ROW_GROUP = 16
GDN_CHUNKS_PER_STEP = 8
GDN_CONV_BLOCK_ROWS = 4096
VMEM_LIMIT = 56 * 1024 * 1024
MOE_VMEM_LIMIT = 60 * 1024 * 1024
assert 2 * NA_HEAD_DIM == LANES
NEG_INF = float("-inf")


def _cparams(sem, vmem=VMEM_LIMIT):
    return pltpu.CompilerParams(dimension_semantics=sem, vmem_limit_bytes=vmem)


def _silu(x):
    return x * jax.nn.sigmoid(x)


def _rms(x, g):
    return x * lax.rsqrt(jnp.mean(x * x, axis=-1, keepdims=True) + EPS) * g


def _dot(a, b):
    return jnp.dot(a, b, preferred_element_type=F32)


def _dot_nt(a, b, precision=None):
    return lax.dot_general(a, b, (((1,), (1,)), ((), ())), precision=precision,
                           preferred_element_type=F32)


def _ada_kernel(c_ref, w_ref, b_ref, o_ref):
    o_ref[...] = jnp.dot(_silu(c_ref[...]), w_ref[...], precision=HI,
                         preferred_element_type=F32) + b_ref[...]


def _ada(cv, w_ada, b_ada):
    n = w_ada.shape[1]
    tn = COL_TILE
    return pl.pallas_call(
        _ada_kernel,
        grid=(n // tn,),
        in_specs=[pl.BlockSpec((SUBLANES, D_MODEL), lambda j: (0, 0)),
                  pl.BlockSpec((D_MODEL, tn), lambda j: (0, j)),
                  pl.BlockSpec((1, tn), lambda j: (0, j))],
        out_specs=pl.BlockSpec((SUBLANES, tn), lambda j: (0, j)),
        out_shape=jax.ShapeDtypeStruct((SUBLANES, n), F32),
        compiler_params=_cparams(("parallel",)),
        name="ada",
    )(cv, w_ada, b_ada.reshape(1, n))


_PM_WIDTHS = (NA_WIDTH, NA_WIDTH, NA_WIDTH, GDN_CONV_CH, GDN_V_WIDTH, 2 * D_MODEL, LANES)


def _premix_kernel(x_ref, mod_ref, g_ref, w_ref, *o_refs):
    mod = mod_ref[0]
    h = _rms(x_ref[...], g_ref[...]) * (1.0 + mod[:, D_MODEL:2 * D_MODEL]) + mod[:, 0:D_MODEL]
    hb = h.astype(BF16)
    off = 0
    for o_ref, wd in zip(o_refs, _PM_WIDTHS):
        for c0 in range(0, wd, COL_TILE):
            c1 = min(c0 + COL_TILE, wd)
            o_ref[:, c0:c1] = _dot(hb, w_ref[:, off + c0:off + c1])
        off += wd


def _premix(x, mods3, g, w_cat, row_of_token):
    n = x.shape[0]
    wtot = w_cat.shape[1]
    tm = TOK_TILE
    return pl.pallas_call(
        _premix_kernel,
        grid=(n // tm,),
        in_specs=[pl.BlockSpec((tm, D_MODEL), lambda i: (i, 0)),
                  pl.BlockSpec((1, 1, 6 * D_MODEL), lambda i: (row_of_token(i * tm), 0, 0)),
                  pl.BlockSpec((1, D_MODEL), lambda i: (0, 0)),
                  pl.BlockSpec((D_MODEL, wtot), lambda i: (0, 0))],
        out_specs=[pl.BlockSpec((tm, wd), lambda i: (i, 0)) for wd in _PM_WIDTHS],
        out_shape=[jax.ShapeDtypeStruct((n, wd), F32) for wd in _PM_WIDTHS],
        compiler_params=_cparams(("parallel",)),
        name="premix",
    )(x, mods3, g.reshape(1, D_MODEL), w_cat)


def _softmax_rows(s):
    m = jnp.max(s, axis=-1, keepdims=True)
    p = jnp.exp(s - m)
    return p / jnp.sum(p, axis=-1, keepdims=True)


def _ctx_attn_kernel(q_ref, k_ref, v_ref, o_ref):
    scale = NA_HEAD_DIM ** -0.5
    first = lax.broadcasted_iota(jnp.int32, (q_ref.shape[1], LANES), 1) < NA_HEAD_DIM
    for hp in range(NA_WIDTH // LANES):
        lanes = slice(hp * LANES, (hp + 1) * LANES)
        q = q_ref[0, :, lanes]
        k = k_ref[0, :, lanes].astype(BF16)
        v = v_ref[0, :, lanes].astype(BF16)
        outs = []
        for hh in range(2):
            qm = jnp.where(first if hh == 0 else ~first, q, 0.0).astype(BF16)
            p = _softmax_rows(_dot_nt(qm, k) * scale)
            outs.append(_dot(p.astype(BF16), v))
        o_ref[0, :, lanes] = jnp.where(first, outs[0], outs[1])


def _ctx_attn(q, k, v):
    b, t, w = q.shape
    spec = pl.BlockSpec((1, t, w), lambda i: (i, 0, 0))
    return pl.pallas_call(
        _ctx_attn_kernel,
        grid=(b,),
        in_specs=[spec, spec, spec],
        out_specs=spec,
        out_shape=jax.ShapeDtypeStruct((b, t, w), F32),
        compiler_params=_cparams(("parallel",)),
        name="ctx_attn",
    )(q, k, v)


def _na_span_base(j, rows):
    return np.clip(NA_QROWS * j - NA_KR // 2, 0, rows - NA_SPAN)


NA_DR_PAD = NA_SPAN - NA_KR


def _na_bias_tables(rpb, rows):
    col = np.arange(GRID_W)
    dcm = np.clip(col[None, :] - col[:, None], -(NA_KC - 1), NA_KC - 1) + (NA_KC - 1)
    onehot = (dcm[None] == np.arange(2 * NA_KC - 1)[:, None, None]).astype(np.float32)
    tab = jnp.einsum('hrd,dqk->hrqk', rpb.astype(F32), jnp.asarray(onehot), precision=HI)
    col_start = np.clip(col - NA_KC // 2, 0, GRID_W - NA_KC)
    col_in = (col[None, :] >= col_start[:, None]) & (col[None, :] < col_start[:, None] + NA_KC)
    tab = jnp.where(jnp.asarray(col_in)[None, None], tab, NEG_INF)
    n_side = NA_SPAN - NA_KR + 1
    blank_lo = jnp.full((NA_HEADS, NA_DR_PAD, GRID_W, GRID_W), NEG_INF, F32)
    blank_hi = jnp.full((NA_HEADS, n_side, GRID_W, GRID_W), NEG_INF, F32)
    padded = jnp.concatenate([blank_lo, tab, blank_hi], axis=1)
    pair_tab = jnp.concatenate([padded[:, :-1], padded[:, 1:]], axis=-1)
    nblk = rows // NA_QROWS
    mask = np.full((3, NA_QROWS, NA_SPAN), NEG_INF, np.float32)
    for p, j in enumerate((0, 1, nblk - 1)):
        base = _na_span_base(j, rows)
        for ri in range(NA_QROWS):
            r = NA_QROWS * j + ri
            rs = np.clip(r - NA_KR // 2, 0, rows - NA_KR)
            for ki in range(NA_SPAN):
                if rs <= base + ki < rs + NA_KR:
                    mask[p, ri, ki] = 0.0
    row_mask = jnp.asarray(np.repeat(mask, GRID_W, axis=2))
    return pair_tab, row_mask


def _na_attn_kernel(q_ref, k_ref, v_ref, ck_ref, cv_ref, tab_ref, mask_ref, o_ref, *, rows):
    j = pl.program_id(2)
    scale = NA_HEAD_DIM ** -0.5
    base = jnp.clip(NA_QROWS * j - NA_KR // 2, 0, rows - NA_SPAN)
    start = pl.multiple_of(base * GRID_W, GRID_W)
    span = NA_SPAN * GRID_W
    q = q_ref[0]
    kl = k_ref[0, pl.ds(start, span), :].astype(BF16)
    vl = v_ref[0, pl.ds(start, span), :].astype(BF16)
    ck = ck_ref[0].astype(BF16)
    cv = cv_ref[0].astype(BF16)
    first = lax.broadcasted_iota(jnp.int32, q.shape, 1) < NA_HEAD_DIM
    off = base - NA_QROWS * j + (NA_KR - 1) + NA_DR_PAD
    outs = []
    for hh in range(2):
        qm = jnp.where(first if hh == 0 else ~first, q, 0.0).astype(BF16)
        s_raw = _dot_nt(qm, kl) * scale
        blocks = []
        for ri in range(NA_QROWS):
            rws = slice(ri * GRID_W, (ri + 1) * GRID_W)
            pieces = [s_raw[rws, m * LANES:(m + 1) * LANES] + tab_ref[hh, off + 2 * m - ri]
                      for m in range(NA_SPAN // 2)]
            blocks.append(jnp.concatenate(pieces, axis=1) + mask_ref[0, ri:ri + 1, :])
        s_loc = jnp.concatenate(blocks, axis=0)
        s_ctx = _dot_nt(qm, ck) * scale
        m = jnp.maximum(jnp.max(s_loc, axis=-1, keepdims=True), jnp.max(s_ctx, axis=-1, keepdims=True))
        p_loc = jnp.exp(s_loc - m)
        p_ctx = jnp.exp(s_ctx - m)
        den = jnp.sum(p_loc, axis=-1, keepdims=True) + jnp.sum(p_ctx, axis=-1, keepdims=True)
        p_loc = (p_loc / den).astype(BF16)
        p_ctx = (p_ctx / den).astype(BF16)
        outs.append(_dot(p_loc, vl) + _dot(p_ctx, cv))
    o_ref[0] = jnp.where(first, outs[0], outs[1])


def _na_attn(q, k, v, ck, cv, pair_tab, row_mask):
    b, n, w = q.shape
    p = ck.shape[1]
    rows = n // GRID_W
    nblk = rows // NA_QROWS
    qb = NA_QROWS * GRID_W

    def pattern(j):
        return jnp.where(j == 0, 0, jnp.where(j == nblk - 1, 2, 1))

    full = pl.BlockSpec((1, n, LANES), lambda bi, hp, j: (bi, 0, hp))
    ctx = pl.BlockSpec((1, p, LANES), lambda bi, hp, j: (bi, 0, hp))
    blk = pl.BlockSpec((1, qb, LANES), lambda bi, hp, j: (bi, j, hp))
    return pl.pallas_call(
        functools.partial(_na_attn_kernel, rows=rows),
        grid=(b, w // LANES, nblk),
        in_specs=[blk, full, full, ctx, ctx,
                  pl.BlockSpec((2,) + pair_tab.shape[1:], lambda bi, hp, j: (hp, 0, 0, 0)),
                  pl.BlockSpec((1,) + row_mask.shape[1:], lambda bi, hp, j: (pattern(j), 0, 0))],
        out_specs=blk,
        out_shape=jax.ShapeDtypeStruct((b, n, w), F32),
        compiler_params=_cparams(("parallel", "parallel", "arbitrary")),
        name="na_attn",
    )(q, k, v, ck, cv, pair_tab, row_mask)


def _gdn_conv_kernel(x_ref, w_ref, o_ref, *, groups):
    c = pl.program_id(1)
    t = x_ref.shape[1]
    row = lax.broadcasted_iota(jnp.int32, (t, LANES), 0)
    n_qk = 2 * GDN_HEADS
    for i in range(groups):
        lanes = slice(i * LANES, (i + 1) * LANES)
        x = x_ref[0, :, lanes]
        y = jnp.zeros_like(x)
        for jj in range(CONV_K):
            o = jj - CONV_K // 2
            xs = x if o == 0 else pltpu.roll(x, (-o) % t, 0)
            xs = jnp.where((row + o >= 0) & (row + o < t), xs, 0.0)
            y = y + xs * w_ref[jj:jj + 1, lanes]
        y = _silu(y)
        nrm = lax.rsqrt(jnp.sum(y * y, axis=-1, keepdims=True) + EPS)
        o_ref[0, i] = jnp.where(c * groups + i < n_qk, y * nrm, y)


def _gdn_conv(x, conv_w):
    b, t, ch = x.shape
    nc = ch // LANES
    groups = max(g for g in range(1, nc + 1)
                 if nc % g == 0 and (g == 1 or g * t <= GDN_CONV_BLOCK_ROWS))
    return pl.pallas_call(
        functools.partial(_gdn_conv_kernel, groups=groups),
        grid=(b, nc // groups),
        in_specs=[pl.BlockSpec((1, t, groups * LANES), lambda bi, c: (bi, 0, c)),
                  pl.BlockSpec((CONV_K, groups * LANES), lambda bi, c: (0, c))],
        out_specs=pl.BlockSpec((1, groups, t, LANES), lambda bi, c: (bi, c, 0, 0)),
        out_shape=jax.ShapeDtypeStruct((b, nc, t, LANES), F32),
        compiler_params=_cparams(("parallel", "parallel")),
        name="gdn_conv",
    )(x, conv_w)


def _bdot(a, b):
    return jnp.dot(a.astype(BF16), b.astype(BF16), preferred_element_type=F32)


def _split_bf16(x):
    hi = x.astype(BF16)
    return hi, (x - hi.astype(F32)).astype(BF16)


def _dot3(a, b):
    m = a.shape[0]
    ah, al = _split_bf16(a)
    bh, bl = _split_bf16(b)
    top = _dot(jnp.concatenate([ah, al], axis=0), bh)
    return top[:m] + top[m:] + _dot(ah, bl)


def _gdn_chunk_kernel(qkv_ref, ba_ref, alog_ref, dt_ref, s0_ref, o_ref, sfin_ref, s_ref, *, reverse, cb, nb):
    c = pl.program_id(1)
    nh = GDN_HEADS

    @pl.when(c == 0)
    def _():
        s_ref[...] = s0_ref[:, 0]

    ii = lax.broadcasted_iota(jnp.int32, (CHUNK, CHUNK), 0)
    jj = lax.broadcasted_iota(jnp.int32, (CHUNK, CHUNK), 1)
    lag = (jj - ii) if reverse else (ii - jj)
    incl = lag >= 0
    strict = lag > 0
    eye = (ii == jj).astype(F32)
    tri = incl.astype(F32)
    bcol = nh if reverse else 0
    gcol0 = (3 if reverse else 2) * nh
    sub8 = lax.broadcasted_iota(jnp.int32, (8, LANES), 0)
    lane8 = lax.broadcasted_iota(jnp.int32, (8, LANES), 1)
    sel8 = (lane8 == gcol0 + sub8).astype(F32)
    nch = nb * cb
    units = [(ci, h) for ci in range(nch) for h in range(nh)]
    gc_alls, beta_alls, grow8s = [], [], []
    for ci in range(nch):
        ba = ba_ref[ci // cb, (ci % cb) * CHUNK:(ci % cb + 1) * CHUNK, :]
        z = ba + dt_ref[...]
        softplus = jnp.maximum(z, 0.0) + jnp.log1p(jnp.exp(-jnp.abs(z)))
        g_all = -jnp.exp(alog_ref[...]) * softplus
        gc_all = jnp.dot(tri, g_all, precision=HI, preferred_element_type=F32)
        gc_alls.append(gc_all)
        beta_alls.append(jax.nn.sigmoid(ba))
        grow8s.append(_dot_nt(sel8, gc_all, precision=HI))
    gcol = [gc_alls[ci][:, gcol0 + h:gcol0 + h + 1] for ci, h in units]
    beta = [beta_alls[ci][:, bcol + h:bcol + h + 1] for ci, h in units]
    rows = [slice((ci % cb) * CHUNK, (ci % cb + 1) * CHUNK) for ci, _ in units]
    seq = [ci // cb for ci, _ in units]
    k = [qkv_ref[seq[u], nh + h, rows[u], :] for u, (_, h) in enumerate(units)]
    kb = [k[u] * beta[u] for u in range(len(units))]
    q = [qkv_ref[seq[u], h, rows[u], :] * (GDN_DK ** -0.5) for u, (_, h) in enumerate(units)]
    kq = [_dot_nt(jnp.concatenate([kb[u], q[u]], axis=0).astype(BF16), k[u].astype(BF16))
          for u in range(len(units))]
    decay = [jnp.where(incl, jnp.exp(jnp.where(incl, gcol[u] - grow8s[ci][h:h + 1, :], 0.0)), 0.0)
             for u, (ci, h) in enumerate(units)]
    intra = [jnp.where(incl, kq[u][CHUNK:] * decay[u], 0.0) for u in range(len(units))]
    wide = nh * CHUNK
    bd_mask = (lax.broadcasted_iota(jnp.int32, (wide, wide), 0) // CHUNK
               == lax.broadcasted_iota(jnp.int32, (wide, wide), 1) // CHUNK)

    def block_diag(x):
        return jnp.where(bd_mask, jnp.concatenate([x] * nh, axis=0), jnp.zeros((), x.dtype))

    def dot3_bd(a, b):
        m = a.shape[0]
        ah, al = _split_bf16(a)
        bh, bl = _split_bf16(b)
        top = _dot(jnp.concatenate([ah, al], axis=0), block_diag(bh))
        return top[:m] + top[m:] + _dot(ah, block_diag(bl))

    eye_w = jnp.concatenate([eye] * nh, axis=1)
    pw = [jnp.concatenate([-jnp.where(strict, kq[ci * nh + h][:CHUNK] * decay[ci * nh + h], 0.0)
                           for h in range(nh)], axis=1) for ci in range(nch)]
    tmat = [eye_w + p for p in pw]
    pw = [dot3_bd(p, p) for p in pw]
    for _ in range(4):
        pt = [dot3_bd(jnp.concatenate([pw[ci], tmat[ci]], axis=0), pw[ci]) for ci in range(nch)]
        pw = [x[:CHUNK] for x in pt]
        tmat = [tmat[ci] + pt[ci][CHUNK:] for ci in range(nch)]
    tmat = [tmat[ci] + dot3_bd(tmat[ci], pw[ci]) for ci in range(nch)]
    eg = [jnp.exp(g) for g in gcol]
    uw_all = [_dot(block_diag(tmat[ci].astype(BF16)), jnp.concatenate(
        [jnp.concatenate([qkv_ref[ci // cb, 2 * nh + h, rows[ci * nh + h], :] * beta[ci * nh + h],
                          kb[ci * nh + h] * eg[ci * nh + h]], axis=1) for h in range(nh)],
        axis=0).astype(BF16)) for ci in range(nch)]
    uw = [uw_all[ci][h * CHUNK:(h + 1) * CHUNK] for ci, h in units]
    g_last = [g[0:1, :] if reverse else g[CHUNK - 1:CHUNK, :] for g in gcol]
    kd = [k[u] * jnp.exp(g_last[u] - gcol[u]) for u in range(len(units))]
    qe = [q[u] * eg[u] for u in range(len(units))]
    kd_uw = [lax.dot_general(kd[u].astype(BF16), uw[u].astype(BF16), (((0,), (0,)), ((), ())),
                             preferred_element_type=F32) for u in range(len(units))]
    in_uw = [_bdot(intra[u], uw[u]) for u in range(len(units))]
    lhs = [jnp.concatenate([kd_uw[u][:, GDN_DV:], qe[u] - in_uw[u][:, GDN_DV:]], axis=0).astype(BF16)
           for u in range(len(units))]
    s = [s_ref[bi, h] for bi in range(nb) for h in range(nh)]
    for cl in (reversed(range(cb)) if reverse else range(cb)):
        us = [(bi * cb + cl) * nh + h for bi in range(nb) for h in range(nh)]
        r = [_dot(lhs[u], s[i].astype(BF16)) for i, u in enumerate(us)]
        for i, u in enumerate(us):
            h = i % nh
            o_ref[i // nh, rows[u], h * GDN_DV:(h + 1) * GDN_DV] = r[i][GDN_DK:] + in_uw[u][:, :GDN_DV]
        s = [s[i] * jnp.exp(g_last[u]) - r[i][:GDN_DK] + kd_uw[u][:, :GDN_DV] for i, u in enumerate(us)]
    for i in range(nb * nh):
        s_ref[i // nh, i % nh] = s[i]

    @pl.when(c == pl.num_programs(1) - 1)
    def _():
        sfin_ref[...] = s_ref[...]


def _gdn_chunks(qkv, ba, alog_row, dt_row, s0, reverse):
    b, _, t, _ = qkv.shape
    cb = min(GDN_CHUNKS_PER_STEP, t // CHUNK)
    nb = max(g for g in range(1, b + 1) if b % g == 0 and g * cb <= GDN_CHUNKS_PER_STEP)
    rows = cb * CHUNK
    n = t // rows
    d = 1 if reverse else 0

    def blk(c):
        return n - 1 - c if reverse else c

    return pl.pallas_call(
        functools.partial(_gdn_chunk_kernel, reverse=reverse, cb=cb, nb=nb),
        grid=(b // nb, n),
        in_specs=[pl.BlockSpec((nb, 3 * GDN_HEADS, rows, LANES), lambda bi, c: (bi, 0, blk(c), 0)),
                  pl.BlockSpec((nb, rows, LANES), lambda bi, c: (bi, blk(c), 0)),
                  pl.BlockSpec((1, LANES), lambda bi, c: (0, 0)),
                  pl.BlockSpec((1, LANES), lambda bi, c: (0, 0)),
                  pl.BlockSpec((nb, 1, GDN_HEADS, GDN_DK, GDN_DV), lambda bi, c: (bi, d, 0, 0, 0))],
        out_specs=[pl.BlockSpec((nb, rows, GDN_V_WIDTH), lambda bi, c: (bi, blk(c), 0)),
                   pl.BlockSpec((nb, GDN_HEADS, GDN_DK, GDN_DV), lambda bi, c: (bi, 0, 0, 0))],
        out_shape=[jax.ShapeDtypeStruct((b, t, GDN_V_WIDTH), F32),
                   jax.ShapeDtypeStruct((b, GDN_HEADS, GDN_DK, GDN_DV), F32)],
        scratch_shapes=[pltpu.VMEM((nb, GDN_HEADS, GDN_DK, GDN_DV), F32)],
        compiler_params=_cparams(("parallel", "arbitrary")),
        name="gdn_bwd" if reverse else "gdn_fwd",
    )(qkv, ba, alog_row, dt_row, s0)


def _pack_bf16_pairs(h):
    half = D_MODEL // 2
    lo = pltpu.bitcast(h[:, :half].astype(BF16).astype(F32), jnp.uint32)
    hi = pltpu.bitcast(h[:, half:].astype(BF16).astype(F32), jnp.uint32)
    return (hi & jnp.uint32(0xFFFF0000)) | (lo >> 16)


def _unpack_bf16_pairs(w):
    lo = pltpu.bitcast(w << 16, F32).astype(BF16)
    hi = pltpu.bitcast(w & jnp.uint32(0xFFFF0000), F32).astype(BF16)
    return lo, hi


def _mix_kernel(x_ref, na_ref, of_ref, ob_ref, z_ref, gate_ref, mod_ref, gnw_ref, gpost_ref, gpre_ref,
                wna_ref, wgdn_ref, wout_ref, wrh_ref, wrl_ref, rb_ref,
                x1_ref, h2p_ref, e_ref, wt_ref, cnt_ref, cnt_acc_ref):
    mod = mod_ref[0]
    o = of_ref[...] + ob_ref[...]
    parts = []
    for h in range(GDN_HEADS):
        sl = slice(h * GDN_DV, (h + 1) * GDN_DV)
        parts.append(_rms(o[:, sl], gnw_ref[...]) * _silu(z_ref[:, sl]))
    gdn_o = jnp.concatenate(parts, axis=-1)
    a = _dot(na_ref[...].astype(BF16), wna_ref[...])
    b = _dot(gdn_o.astype(BF16), wgdn_ref[...])
    gate = jax.nn.sigmoid(gate_ref[...])
    pre = gate[:, :D_MODEL] * a + gate[:, D_MODEL:] * b
    mix = _dot(pre.astype(BF16), wout_ref[...])
    x1 = x_ref[...] + mod[:, 2 * D_MODEL:3 * D_MODEL] * _rms(mix, gpost_ref[...])
    x1_ref[...] = x1
    h2 = _rms(x1, gpre_ref[...]) * (1.0 + mod[:, 4 * D_MODEL:5 * D_MODEL]) + mod[:, 3 * D_MODEL:4 * D_MODEL]
    _route(h2, wrh_ref, wrl_ref, rb_ref, e_ref, wt_ref, cnt_ref, cnt_acc_ref)
    packed = _pack_bf16_pairs(h2)
    for c in range(PACK_ROWS):
        h2p_ref[pl.ds(c, h2.shape[0], stride=PACK_ROWS), :] = packed[:, c * LANES:(c + 1) * LANES]


def _mix(x, na_o, o_f, o_b, z, gate, mods3, gnw, gpost, gpre, wna, wgdn, wout, w_rt_hi, w_rt_lo, bias_col,
         row_of_token):
    n = x.shape[0]
    tm = TOK_TILE
    row = lambda w: pl.BlockSpec((tm, w), lambda i: (i, 0))
    const = lambda r, c: pl.BlockSpec((r, c), lambda i: (0, 0))
    return pl.pallas_call(
        _mix_kernel,
        grid=(n // tm,),
        in_specs=[row(D_MODEL), row(NA_WIDTH), row(GDN_V_WIDTH), row(GDN_V_WIDTH),
                  row(GDN_V_WIDTH), row(2 * D_MODEL),
                  pl.BlockSpec((1, 1, 6 * D_MODEL), lambda i: (row_of_token(i * tm), 0, 0)),
                  const(1, GDN_DV), const(1, D_MODEL), const(1, D_MODEL),
                  const(NA_WIDTH, D_MODEL), const(GDN_V_WIDTH, D_MODEL), const(D_MODEL, D_MODEL),
                  const(N_EXPERTS, D_MODEL), const(N_EXPERTS, D_MODEL), const(N_EXPERTS, LANES)],
        out_specs=[row(D_MODEL), pl.BlockSpec((tm * PACK_ROWS, LANES), lambda i: (i, 0)),
                   pl.BlockSpec((TOP_K, tm), lambda i: (0, i)),
                   pl.BlockSpec((TOP_K, tm), lambda i: (0, i)),
                   const(N_EXPERTS, LANES)],
        out_shape=[jax.ShapeDtypeStruct((n, D_MODEL), F32),
                   jax.ShapeDtypeStruct((n * PACK_ROWS, LANES), jnp.uint32),
                   jax.ShapeDtypeStruct((TOP_K, n), jnp.int32),
                   jax.ShapeDtypeStruct((TOP_K, n), F32),
                   jax.ShapeDtypeStruct((N_EXPERTS, LANES), F32)],
        scratch_shapes=[pltpu.VMEM((N_EXPERTS, LANES), F32)],
        compiler_params=_cparams(("arbitrary",)),
        name="mix",
    )(x, na_o, o_f, o_b, z, gate, mods3, gnw.reshape(1, GDN_DV), gpost.reshape(1, D_MODEL),
      gpre.reshape(1, D_MODEL), wna, wgdn, wout, w_rt_hi, w_rt_lo, bias_col)


def _route(h, wh_ref, wl_ref, b_ref, e_ref, wt_ref, cnt_ref, acc_ref):
    i = pl.program_id(0)
    tm = h.shape[0]
    ne = N_EXPERTS
    per = ne // N_GROUPS
    hh, hl = _split_bf16(h)
    wh = wh_ref[...]
    logits = _dot_nt(wh, hh) + _dot_nt(wl_ref[...], hh) + _dot_nt(wh, hl)
    scores = jax.nn.sigmoid(logits)
    biased = scores + jnp.concatenate([b_ref[...]] * (tm // LANES), axis=1)

    def first_max(x):
        n = x.shape[0]
        iota = lax.broadcasted_iota(jnp.int32, x.shape, 0).astype(F32)
        m = jnp.max(x, axis=0, keepdims=True)
        idx = jnp.min(jnp.where(x == m, iota, float(n)), axis=0, keepdims=True)
        return m, idx, iota

    gs_rows = []
    for g in range(N_GROUPS):
        bg = biased[g * per:(g + 1) * per]
        m1, i1, iota = first_max(bg)
        m2 = jnp.max(jnp.where(iota == i1, NEG_INF, bg), axis=0, keepdims=True)
        gs_rows.append(m1 + m2)
    gs = jnp.concatenate(gs_rows, axis=0)
    gsel = jnp.zeros(gs.shape, F32)
    for _ in range(TOPK_GROUPS):
        _, gi, iota = first_max(gs)
        hit = iota == gi
        gs = jnp.where(hit, NEG_INF, gs)
        gsel = jnp.where(hit, 1.0, gsel)
    masked = jnp.concatenate(
        [jnp.where(gsel[g:g + 1] > 0.0, biased[g * per:(g + 1) * per], NEG_INF) for g in range(N_GROUPS)], axis=0)
    onehot = jnp.zeros((ne, tm), F32)
    e_rows, w_rows = [], []
    for _ in range(TOP_K):
        _, ei, iota = first_max(masked)
        hit = iota == ei
        masked = jnp.where(hit, NEG_INF, masked)
        w_rows.append(jnp.sum(jnp.where(hit, scores, 0.0), axis=0, keepdims=True))
        e_rows.append(ei)
        onehot = onehot + hit.astype(F32)
    w_out = jnp.concatenate(w_rows, axis=0)
    e_ref[...] = jnp.concatenate(e_rows, axis=0).astype(jnp.int32)
    wt_ref[...] = w_out / jnp.sum(w_out, axis=0, keepdims=True) * ROUTED_SCALE

    @pl.when(i == 0)
    def _():
        acc_ref[...] = jnp.zeros_like(acc_ref)

    acc_ref[...] += sum(onehot[:, c * LANES:(c + 1) * LANES] for c in range(tm // LANES))

    @pl.when(i == pl.num_programs(0) - 1)
    def _():
        cnt_ref[...] = jnp.broadcast_to(jnp.sum(acc_ref[...], axis=1, keepdims=True), cnt_ref.shape)


_ST_SLOT, _ST_READY, _ST_PEND_BASE = range(3)


def _moe_kernel(tok_ref, seg_ref, cnt_ref, h2p_hbm, w2d_ref, wg_ref, wu_ref, wd_ref, out_hbm,
                h2p_ref, acc_ref, xs_ref, ye_ref, st_ref, sem):
    e = pl.program_id(0)
    n_exp = pl.num_programs(0)
    half = D_MODEL // 2
    xw, yw = PACK_ROWS, ACC_ROWS
    n_tok = out_hbm.shape[0] // yw
    grp = MOE_ROW_GROUP

    n_groups = MOE_ROWS // grp

    def gather_tile(base, slot, groups=(0, n_groups)):
        for r in range(groups[0] * grp, groups[1] * grp):
            tok = tok_ref[base + r]
            xs_ref[slot, xw * r:xw * (r + 1), :] = h2p_ref[pl.ds(pl.multiple_of(tok * xw, xw), xw), :]

    def scatter_tile(base, slot, groups=(0, n_groups)):
        for g in range(*groups):
            toks = [pl.multiple_of(tok_ref[base + g * grp + j] * yw, yw) for j in range(grp)]
            rows = [acc_ref[pl.ds(toks[j], yw), :] + ye_ref[slot, yw * (g * grp + j):yw * (g * grp + j + 1), :]
                    for j in range(grp)]
            for j in reversed(range(grp)):
                acc_ref[pl.ds(toks[j], yw), :] = rows[j]

    seg = seg_ref[e]
    cnt = cnt_ref[e]
    n_tiles = (cnt + MOE_ROWS - 1) // MOE_ROWS

    @pl.when(e == 0)
    def _():
        cp = pltpu.make_async_copy(h2p_hbm, h2p_ref.at[pl.ds(0, n_tok * xw)], sem.at[0])
        cp.start()
        acc_ref[...] = jnp.zeros_like(acc_ref)
        ye_ref[...] = jnp.zeros_like(ye_ref)
        h2p_ref[pl.ds(n_tok * xw, grp * xw), :] = jnp.zeros((grp * xw, LANES), jnp.uint32)
        st_ref[_ST_SLOT] = 0
        st_ref[_ST_READY] = -1
        st_ref[_ST_PEND_BASE] = 0
        cp.wait()

    @pl.when((n_tiles > 0) & (st_ref[_ST_READY] != e))
    def _():
        gather_tile(seg, st_ref[_ST_SLOT])

    def chunk(w_ref, c):
        return jnp.concatenate([w_ref[0, LANES * c:LANES * (c + 1), :],
                                w_ref[0, half + LANES * c:half + LANES * (c + 1), :]], axis=0).astype(BF16)

    lane = lax.broadcasted_iota(jnp.int32, (8, LANES), 1)
    nxt_seg = seg_ref[jnp.minimum(e + 1, n_exp - 1)]

    def tile_body(t, carry):
        slot = st_ref[_ST_SLOT]
        base = seg + t * MOE_ROWS
        last = t + 1 == n_tiles
        xk = [jnp.concatenate(_unpack_bf16_pairs(xs_ref[slot, pl.ds(c, MOE_ROWS, stride=xw), :]), axis=1)
              for c in range(xw)]
        g_base = jnp.where(last, nxt_seg, base + MOE_ROWS)
        p_base = st_ref[_ST_PEND_BASE]
        cuts = [n_groups * i // (xw + 1) for i in range(xw + 2)]

        def move_rows(i):
            gather_tile(g_base, 1 - slot, (cuts[i], cuts[i + 1]))
            scatter_tile(p_base, 1 - slot, (cuts[i], cuts[i + 1]))

        hg = hu = 0.0
        for c in range(xw):
            move_rows(c)
            hg = hg + _dot(xk[c], chunk(wg_ref, c))
            hu = hu + _dot(xk[c], chunk(wu_ref, c))
        act = (_silu(hg) * hu).astype(BF16)
        move_rows(xw)
        q = base // LANES
        sh = base % LANES
        rot = (LANES - sh) % LANES
        row_a = pltpu.roll(jnp.broadcast_to(w2d_ref[pl.ds(q, 1), :], (8, LANES)), rot, 1)
        row_b = pltpu.roll(jnp.broadcast_to(w2d_ref[pl.ds(q + 1, 1), :], (8, LANES)), rot, 1)
        w_row = jnp.where(lane + sh < LANES, row_a, row_b)
        w_row = jnp.where(lane < cnt - t * MOE_ROWS, w_row, 0.0)[0:1, :]
        w_col = jnp.broadcast_to(w_row, (LANES, LANES)).T[:MOE_ROWS]
        ye = _dot(act, wd_ref[0].astype(BF16))
        for c in range(yw):
            ye_ref[slot, pl.ds(c, MOE_ROWS, stride=yw), :] = ye[:, LANES * c:LANES * (c + 1)] * w_col
        st_ref[_ST_PEND_BASE] = base
        st_ref[_ST_SLOT] = 1 - slot
        st_ref[_ST_READY] = jnp.where(last, e + 1, e)
        return carry

    lax.fori_loop(0, n_tiles, tile_body, 0)

    @pl.when(e == n_exp - 1)
    def _():
        scatter_tile(st_ref[_ST_PEND_BASE], 1 - st_ref[_ST_SLOT])
        cp = pltpu.make_async_copy(acc_ref.at[pl.ds(0, n_tok * yw)], out_hbm, sem.at[1])
        cp.start()
        cp.wait()


def _moe(tok_sorted, seg_start, seg_count, h2p, w2d, wg, wu, wd):
    xw, yw = PACK_ROWS, ACC_ROWS
    n = h2p.shape[0] // xw
    grid_spec = pltpu.PrefetchScalarGridSpec(
        num_scalar_prefetch=3,
        grid=(N_EXPERTS,),
        in_specs=[pl.BlockSpec(memory_space=pl.ANY),
                  pl.BlockSpec(w2d.shape, lambda e, *_: (0, 0)),
                  pl.BlockSpec((1, D_MODEL, EXPERT_DIM), lambda e, *_: (e, 0, 0)),
                  pl.BlockSpec((1, D_MODEL, EXPERT_DIM), lambda e, *_: (e, 0, 0)),
                  pl.BlockSpec((1, EXPERT_DIM, D_MODEL), lambda e, *_: (e, 0, 0))],
        out_specs=pl.BlockSpec(memory_space=pl.ANY),
        scratch_shapes=[pltpu.VMEM(((n + MOE_ROW_GROUP) * xw, LANES), jnp.uint32),
                        pltpu.VMEM(((n + MOE_ROW_GROUP) * yw, LANES), F32),
                        pltpu.VMEM((2, MOE_ROWS * xw, LANES), jnp.uint32),
                        pltpu.VMEM((2, MOE_ROWS * yw, LANES), F32),
                        pltpu.SMEM((3,), jnp.int32),
                        pltpu.SemaphoreType.DMA((2,))],
    )
    return pl.pallas_call(
        _moe_kernel,
        grid_spec=grid_spec,
        out_shape=jax.ShapeDtypeStruct((n * yw, LANES), F32),
        compiler_params=_cparams(("arbitrary",), vmem=MOE_VMEM_LIMIT),
        name="moe",
    )(tok_sorted, seg_start, seg_count, h2p, w2d, wg, wu, wd)


def _moe_routed(top_e, top_w, counts, h2p, wg, wu, wd):
    tok_sorted, w2d, seg_start, seg_count = _moe_dispatch_plan(top_e, top_w, counts, h2p.shape[0] // PACK_ROWS)
    return _moe(tok_sorted, seg_start, seg_count, h2p, w2d, wg, wu, wd)


def _moe_dispatch_plan(top_e, top_w, counts, n):
    total = n * TOP_K
    assert N_EXPERTS * total < 2 ** 31
    key = top_e.reshape(-1) * total + jnp.arange(total, dtype=jnp.int32)
    key_sorted, w_sorted = lax.sort((key, top_w.reshape(-1)), num_keys=1, is_stable=False)
    tok_sorted = (key_sorted % total) % n
    tok_sorted = jnp.concatenate([tok_sorted, jnp.full((MOE_ROWS,), n, jnp.int32)])
    table_rows = -(-(total // LANES + 2) // 8) * 8
    w2d = jnp.concatenate([w_sorted, jnp.zeros((table_rows * LANES - total,), F32)]).reshape(table_rows, LANES)
    cnt = counts[:, 0].astype(jnp.int32)
    return tok_sorted, w2d, jnp.cumsum(cnt) - cnt, cnt


def _regroup_rows(w):
    half = D_MODEL // 2
    return jnp.concatenate([w[r0 + LANES * c:r0 + LANES * (c + 1)]
                            for c in range(PACK_ROWS) for r0 in (0, half)], axis=0)


def _final_kernel(x1_ref, h2p_ref, r_ref, mod_ref, g_ref, wg_ref, wu_ref, wd_ref, y_ref):
    tm = x1_ref.shape[0]
    kc = 2 * LANES
    mod = mod_ref[0]
    xk = [jnp.concatenate(_unpack_bf16_pairs(h2p_ref[pl.ds(c, tm, stride=PACK_ROWS), :]), axis=1)
          for c in range(PACK_ROWS)]
    hg = sum(_dot(xk[c], wg_ref[kc * c:kc * (c + 1), :]) for c in range(PACK_ROWS))
    hu = sum(_dot(xk[c], wu_ref[kc * c:kc * (c + 1), :]) for c in range(PACK_ROWS))
    shared = _dot((_silu(hg) * hu).astype(BF16), wd_ref[...])
    routed = jnp.concatenate([r_ref[pl.ds(c, tm, stride=ACC_ROWS), :] for c in range(ACC_ROWS)], axis=1)
    ffn = routed + shared
    y_ref[...] = x1_ref[...] + mod[:, 5 * D_MODEL:6 * D_MODEL] * _rms(ffn, g_ref[...])


def _final(x1, h2p, routed, mods3, g, wg, wu, wd, row_of_token):
    n = x1.shape[0]
    tm = FINAL_TILE
    sd = wg.shape[1]
    row = lambda w: pl.BlockSpec((tm, w), lambda i: (i, 0))
    const = lambda r, c: pl.BlockSpec((r, c), lambda i: (0, 0))
    return pl.pallas_call(
        _final_kernel,
        grid=(n // tm,),
        in_specs=[row(D_MODEL), pl.BlockSpec((tm * PACK_ROWS, LANES), lambda i: (i, 0)),
                  pl.BlockSpec((tm * ACC_ROWS, LANES), lambda i: (i, 0)),
                  pl.BlockSpec((1, 1, 6 * D_MODEL), lambda i: (row_of_token(i * tm), 0, 0)),
                  const(1, D_MODEL), const(D_MODEL, sd), const(D_MODEL, sd), const(sd, D_MODEL)],
        out_specs=row(D_MODEL),
        out_shape=jax.ShapeDtypeStruct((n, D_MODEL), F32),
        compiler_params=_cparams(("parallel",)),
        name="final",
    )(x1, h2p, routed, mods3, g.reshape(1, D_MODEL), wg, wu, wd)


def _trunk(x3, mods3, row_of_token, attend, s0, wts):
    b, t, _ = x3.shape
    n = b * t
    x = x3.reshape(n, D_MODEL)
    q, k, v, gdn, z, gate, ba = _premix(x, mods3, wts["g_pre_mix"], wts["w_cat"], row_of_token)
    na_o = attend(q, k, v).reshape(n, NA_WIDTH)
    qkv = _gdn_conv(gdn.reshape(b, t, GDN_CONV_CH), wts["conv_w"])
    ba3 = ba.reshape(b, t, LANES)
    o_f, s_f = _gdn_chunks(qkv, ba3, wts["alog_row"], wts["dt_row"], s0, reverse=False)
    o_b, s_b = _gdn_chunks(qkv, ba3, wts["alog_row"], wts["dt_row"], s0, reverse=True)
    s_fin = jnp.stack([s_f, s_b], axis=1)
    x1, h2p, top_e, top_w, counts = _mix(
        x, na_o, o_f.reshape(n, GDN_V_WIDTH), o_b.reshape(n, GDN_V_WIDTH), z, gate, mods3, wts["gdn_norm_w"],
        wts["g_post_mix"], wts["g_pre_ffn"], wts["w_na_up"], wts["w_gdn_up"], wts["w_out"],
        wts["w_rt_hi"], wts["w_rt_lo"], wts["router_bias_col"], row_of_token)
    routed = _moe_routed(top_e, top_w, counts, h2p, wts["w_exp_gate"], wts["w_exp_up"], wts["w_exp_down"])
    y = _final(x1, h2p, routed, mods3, wts["g_post_ffn"], wts["w_sh_gate"], wts["w_sh_up"],
               wts["w_sh_down"], row_of_token)
    return y.reshape(b, t, D_MODEL), k, v, s_fin


def kernel(x_prompt, x_sample, cache_na_k, cache_na_v, state_gdn, c, c_ctx, w_ada, b_ada, g_pre_mix,
           g_post_mix, g_pre_ffn, g_post_ffn, w_in, conv_w, gdn_a_log, gdn_dt_bias, gdn_norm_w, na_rpb,
           w_na_up, w_gdn_up, w_out, w_router, router_bias, w_exp_gate, w_exp_up, w_exp_down, w_sh_gate,
           w_sh_up, w_sh_down):
    depth = w_ada.shape[0]
    bp, tp, _ = x_prompt.shape
    bs, ts, _ = x_sample.shape
    y_prompt, y_sample = x_prompt, x_sample
    zero_state = jnp.zeros((bp, 2, GDN_HEADS, GDN_DK, GDN_DV), F32)
    new_k, new_v, new_s = [], [], []
    for l in range(depth):
        cv = jnp.concatenate([c_ctx[None], c, jnp.zeros((SUBLANES - 1 - bs, D_MODEL), F32)], axis=0)
        mods3 = _ada(cv, w_ada[l], b_ada[l]).reshape(SUBLANES, 1, 6 * D_MODEL)
        wl = w_in[l]
        w_cat = jnp.concatenate(
            [wl[:, :S_Z], wl[:, S_A:], wl[:, S_Z:S_A],
             jnp.zeros((D_MODEL, LANES - 4 * GDN_HEADS), F32)], axis=1).astype(BF16)
        pad = jnp.zeros((2 * GDN_HEADS,), F32)
        tail = jnp.zeros((LANES - 4 * GDN_HEADS,), F32)
        w_rt = w_router[l].astype(F32).T
        w_rt_hi = w_rt.astype(BF16)
        wts = dict(
            w_cat=w_cat, g_pre_mix=g_pre_mix[l], g_post_mix=g_post_mix[l], g_pre_ffn=g_pre_ffn[l],
            g_post_ffn=g_post_ffn[l], conv_w=conv_w[l], gdn_norm_w=gdn_norm_w[l],
            alog_row=jnp.concatenate([pad, gdn_a_log[l].reshape(-1), tail]).reshape(1, LANES),
            dt_row=jnp.concatenate([pad, gdn_dt_bias[l].reshape(-1), tail]).reshape(1, LANES),
            w_na_up=w_na_up[l].astype(BF16), w_gdn_up=w_gdn_up[l].astype(BF16),
            w_out=w_out[l].astype(BF16), w_rt_hi=w_rt_hi, w_rt_lo=(w_rt - w_rt_hi.astype(F32)).astype(BF16),
            router_bias_col=jnp.broadcast_to(router_bias[l].astype(F32)[:, None], (N_EXPERTS, LANES)),
            w_exp_gate=w_exp_gate[l], w_exp_up=w_exp_up[l], w_exp_down=w_exp_down[l],
            w_sh_gate=_regroup_rows(w_sh_gate[l]).astype(BF16), w_sh_up=_regroup_rows(w_sh_up[l]).astype(BF16),
            w_sh_down=w_sh_down[l].astype(BF16))

        def ctx_attend(q, k, v):
            return _ctx_attn(q.reshape(bp, tp, NA_WIDTH), k.reshape(bp, tp, NA_WIDTH),
                             v.reshape(bp, tp, NA_WIDTH))

        y_prompt, k_ctx, v_ctx, s_ctx = _trunk(y_prompt, mods3, lambda t0: 0, ctx_attend, zero_state, wts)
        new_k.append(k_ctx.reshape(bp, tp, NA_HEADS, NA_HEAD_DIM))
        new_v.append(v_ctx.reshape(bp, tp, NA_HEADS, NA_HEAD_DIM))
        new_s.append(s_ctx)

        pair_tab, row_mask = _na_bias_tables(na_rpb[l], ts // GRID_W)
        ck = cache_na_k[:, l].reshape(bs, -1, NA_WIDTH)
        cvv = cache_na_v[:, l].reshape(bs, -1, NA_WIDTH)

        def na_attend(q, k, v):
            return _na_attn(q.reshape(bs, ts, NA_WIDTH), k.reshape(bs, ts, NA_WIDTH),
                            v.reshape(bs, ts, NA_WIDTH), ck, cvv, pair_tab, row_mask)

        y_sample, _, _, _ = _trunk(y_sample, mods3, lambda t0: 1 + t0 // ts, na_attend,
                                   state_gdn[:, l], wts)
    return (y_prompt, y_sample, jnp.stack(new_k, axis=1), jnp.stack(new_v, axis=1),
            jnp.stack(new_s, axis=1))
```

```python
import functools

import numpy as np
import jax
import jax.numpy as jnp
from jax import lax
from jax.experimental import pallas as pl
from jax.experimental.pallas import tpu as pltpu

F32 = jnp.float32
BF16 = jnp.bfloat16
HI = lax.Precision.HIGHEST

D_MODEL = 1024
GRID_W = 64
NA_HEADS = 8
NA_HEAD_DIM = 64
NA_WIDTH = NA_HEADS * NA_HEAD_DIM
NA_KR = 8
NA_KC = 16
GDN_HEADS = 4
GDN_DK = 128
GDN_DV = 128
GDN_QK_WIDTH = GDN_HEADS * GDN_DK
GDN_V_WIDTH = GDN_HEADS * GDN_DV
GDN_CONV_CH = 2 * GDN_QK_WIDTH + GDN_V_WIDTH
CONV_K = 5
CHUNK = 64
N_EXPERTS = 256
TOP_K = 8
N_GROUPS = 8
TOPK_GROUPS = 4
EXPERT_DIM = 256
ROUTED_SCALE = 2.5
EPS = 1e-6
S_NA = 3 * NA_WIDTH
S_GDN = S_NA + GDN_CONV_CH
S_Z = S_GDN + GDN_V_WIDTH
S_B = S_Z + 2 * GDN_HEADS
S_A = S_B + 2 * GDN_HEADS

LANES = 128
SUBLANES = 8
COL_TILE = 512
TOK_TILE = 256
WIDE_TILE = 512
NA_QROWS = 8
NA_SPAN = 16
MOE_ROWS = 128
PACK_ROWS = D_MODEL // 2 // LANES
ACC_ROWS = D_MODEL // LANES
MOE_ROW_GROUP = 16
GDN_CHUNKS_PER_STEP = 8
GDN_CONV_BLOCK_ROWS = 4096
VMEM_LIMIT = 56 * 1024 * 1024
MOE_VMEM_LIMIT = 60 * 1024 * 1024
assert 2 * NA_HEAD_DIM == LANES
NEG_INF = float("-inf")


def _cparams(sem, vmem=VMEM_LIMIT):
    return pltpu.CompilerParams(dimension_semantics=sem, vmem_limit_bytes=vmem)


def _silu(x):
    return x * jax.nn.sigmoid(x)


def _rms(x, g):
    return x * lax.rsqrt(jnp.mean(x * x, axis=-1, keepdims=True) + EPS) * g


def _dot(a, b):
    return jnp.dot(a, b, preferred_element_type=F32)


def _dot_nt(a, b, precision=None):
    return lax.dot_general(a, b, (((1,), (1,)), ((), ())), precision=precision,
                           preferred_element_type=F32)


def _ada_kernel(c_ref, w_ref, b_ref, o_ref):
    o_ref[...] = jnp.dot(_silu(c_ref[...]), w_ref[...], precision=HI,
                         preferred_element_type=F32) + b_ref[...]


def _ada(cv, w_ada, b_ada):
    n = w_ada.shape[1]
    tn = COL_TILE
    return pl.pallas_call(
        _ada_kernel,
        grid=(n // tn,),
        in_specs=[pl.BlockSpec((SUBLANES, D_MODEL), lambda j: (0, 0)),
                  pl.BlockSpec((D_MODEL, tn), lambda j: (0, j)),
                  pl.BlockSpec((1, tn), lambda j: (0, j))],
        out_specs=pl.BlockSpec((SUBLANES, tn), lambda j: (0, j)),
        out_shape=jax.ShapeDtypeStruct((SUBLANES, n), F32),
        compiler_params=_cparams(("parallel",)),
        name="ada",
    )(cv, w_ada, b_ada.reshape(1, n))


_PM_WIDTHS = (NA_WIDTH, NA_WIDTH, NA_WIDTH, GDN_CONV_CH, GDN_V_WIDTH, 2 * D_MODEL, LANES)


def _premix_kernel(x_ref, mod_ref, g_ref, w_ref, *o_refs):
    mod = mod_ref[0]
    h = _rms(x_ref[...], g_ref[...]) * (1.0 + mod[:, D_MODEL:2 * D_MODEL]) + mod[:, 0:D_MODEL]
    hb = h.astype(BF16)
    off = 0
    for o_ref, wd in zip(o_refs, _PM_WIDTHS):
        for c0 in range(0, wd, COL_TILE):
            c1 = min(c0 + COL_TILE, wd)
            o_ref[:, c0:c1] = _dot(hb, w_ref[:, off + c0:off + c1])
        off += wd


def _premix(x, mods3, g, w_cat, row_of_token):
    n = x.shape[0]
    wtot = w_cat.shape[1]
    tm = TOK_TILE
    return pl.pallas_call(
        _premix_kernel,
        grid=(n // tm,),
        in_specs=[pl.BlockSpec((tm, D_MODEL), lambda i: (i, 0)),
                  pl.BlockSpec((1, 1, 6 * D_MODEL), lambda i: (row_of_token(i * tm), 0, 0)),
                  pl.BlockSpec((1, D_MODEL), lambda i: (0, 0)),
                  pl.BlockSpec((D_MODEL, wtot), lambda i: (0, 0))],
        out_specs=[pl.BlockSpec((tm, wd), lambda i: (i, 0)) for wd in _PM_WIDTHS],
        out_shape=[jax.ShapeDtypeStruct((n, wd), F32) for wd in _PM_WIDTHS],
        compiler_params=_cparams(("parallel",)),
        name="premix",
    )(x, mods3, g.reshape(1, D_MODEL), w_cat)


def _softmax_rows(s):
    m = jnp.max(s, axis=-1, keepdims=True)
    p = jnp.exp(s - m)
    return p / jnp.sum(p, axis=-1, keepdims=True)


def _ctx_attn_kernel(q_ref, k_ref, v_ref, o_ref):
    scale = NA_HEAD_DIM ** -0.5
    first = lax.broadcasted_iota(jnp.int32, (q_ref.shape[1], LANES), 1) < NA_HEAD_DIM
    for hp in range(NA_WIDTH // LANES):
        lanes = slice(hp * LANES, (hp + 1) * LANES)
        q = q_ref[0, :, lanes]
        k = k_ref[0, :, lanes].astype(BF16)
        v = v_ref[0, :, lanes].astype(BF16)
        outs = []
        for hh in range(2):
            qm = jnp.where(first if hh == 0 else ~first, q, 0.0).astype(BF16)
            p = _softmax_rows(_dot_nt(qm, k) * scale)
            outs.append(_dot(p.astype(BF16), v))
        o_ref[0, :, lanes] = jnp.where(first, outs[0], outs[1])


def _ctx_attn(q, k, v):
    b, t, w = q.shape
    spec = pl.BlockSpec((1, t, w), lambda i: (i, 0, 0))
    return pl.pallas_call(
        _ctx_attn_kernel,
        grid=(b,),
        in_specs=[spec, spec, spec],
        out_specs=spec,
        out_shape=jax.ShapeDtypeStruct((b, t, w), F32),
        compiler_params=_cparams(("parallel",)),
        name="ctx_attn",
    )(q, k, v)


def _na_span_base(j, rows):
    return np.clip(NA_QROWS * j - NA_KR // 2, 0, rows - NA_SPAN)


NA_DR_PAD = NA_SPAN - NA_KR


def _na_bias_tables(rpb, rows):
    col = np.arange(GRID_W)
    dcm = np.clip(col[None, :] - col[:, None], -(NA_KC - 1), NA_KC - 1) + (NA_KC - 1)
    onehot = (dcm[None] == np.arange(2 * NA_KC - 1)[:, None, None]).astype(np.float32)
    tab = jnp.einsum('hrd,dqk->hrqk', rpb.astype(F32), jnp.asarray(onehot), precision=HI)
    col_start = np.clip(col - NA_KC // 2, 0, GRID_W - NA_KC)
    col_in = (col[None, :] >= col_start[:, None]) & (col[None, :] < col_start[:, None] + NA_KC)
    tab = jnp.where(jnp.asarray(col_in)[None, None], tab, NEG_INF)
    n_side = NA_SPAN - NA_KR + 1
    blank_lo = jnp.full((NA_HEADS, NA_DR_PAD, GRID_W, GRID_W), NEG_INF, F32)
    blank_hi = jnp.full((NA_HEADS, n_side, GRID_W, GRID_W), NEG_INF, F32)
    padded = jnp.concatenate([blank_lo, tab, blank_hi], axis=1)
    pair_tab = jnp.concatenate([padded[:, :-1], padded[:, 1:]], axis=-1)
    nblk = rows // NA_QROWS
    mask = np.full((3, NA_QROWS, NA_SPAN), NEG_INF, np.float32)
    for p, j in enumerate((0, 1, nblk - 1)):
        base = _na_span_base(j, rows)
        for ri in range(NA_QROWS):
            r = NA_QROWS * j + ri
            rs = np.clip(r - NA_KR // 2, 0, rows - NA_KR)
            for ki in range(NA_SPAN):
                if rs <= base + ki < rs + NA_KR:
                    mask[p, ri, ki] = 0.0
    row_mask = jnp.asarray(np.repeat(mask, GRID_W, axis=2))
    return pair_tab, row_mask


def _na_attn_kernel(q_ref, k_ref, v_ref, ck_ref, cv_ref, tab_ref, mask_ref, o_ref, *, rows):
    j = pl.program_id(2)
    scale = NA_HEAD_DIM ** -0.5
    base = jnp.clip(NA_QROWS * j - NA_KR // 2, 0, rows - NA_SPAN)
    start = pl.multiple_of(base * GRID_W, GRID_W)
    span = NA_SPAN * GRID_W
    q = q_ref[0]
    kl = k_ref[0, pl.ds(start, span), :].astype(BF16)
    vl = v_ref[0, pl.ds(start, span), :].astype(BF16)
    ck = ck_ref[0].astype(BF16)
    cv = cv_ref[0].astype(BF16)
    first = lax.broadcasted_iota(jnp.int32, q.shape, 1) < NA_HEAD_DIM
    off = base - NA_QROWS * j + (NA_KR - 1) + NA_DR_PAD
    outs = []
    for hh in range(2):
        qm = jnp.where(first if hh == 0 else ~first, q, 0.0).astype(BF16)
        s_raw = _dot_nt(qm, kl) * scale
        blocks = []
        for ri in range(NA_QROWS):
            rws = slice(ri * GRID_W, (ri + 1) * GRID_W)
            pieces = [s_raw[rws, m * LANES:(m + 1) * LANES] + tab_ref[hh, off + 2 * m - ri]
                      for m in range(NA_SPAN // 2)]
            blocks.append(jnp.concatenate(pieces, axis=1) + mask_ref[0, ri:ri + 1, :])
        s_loc = jnp.concatenate(blocks, axis=0)
        s_ctx = _dot_nt(qm, ck) * scale
        m = jnp.maximum(jnp.max(s_loc, axis=-1, keepdims=True), jnp.max(s_ctx, axis=-1, keepdims=True))
        p_loc = jnp.exp(s_loc - m)
        p_ctx = jnp.exp(s_ctx - m)
        den = jnp.sum(p_loc, axis=-1, keepdims=True) + jnp.sum(p_ctx, axis=-1, keepdims=True)
        p_loc = (p_loc / den).astype(BF16)
        p_ctx = (p_ctx / den).astype(BF16)
        outs.append(_dot(p_loc, vl) + _dot(p_ctx, cv))
    o_ref[0] = jnp.where(first, outs[0], outs[1])


def _na_attn(q, k, v, ck, cv, pair_tab, row_mask):
    b, n, w = q.shape
    p = ck.shape[1]
    rows = n // GRID_W
    nblk = rows // NA_QROWS
    qb = NA_QROWS * GRID_W

    def pattern(j):
        return jnp.where(j == 0, 0, jnp.where(j == nblk - 1, 2, 1))

    full = pl.BlockSpec((1, n, LANES), lambda bi, hp, j: (bi, 0, hp))
    ctx = pl.BlockSpec((1, p, LANES), lambda bi, hp, j: (bi, 0, hp))
    blk = pl.BlockSpec((1, qb, LANES), lambda bi, hp, j: (bi, j, hp))
    return pl.pallas_call(
        functools.partial(_na_attn_kernel, rows=rows),
        grid=(b, w // LANES, nblk),
        in_specs=[blk, full, full, ctx, ctx,
                  pl.BlockSpec((2,) + pair_tab.shape[1:], lambda bi, hp, j: (hp, 0, 0, 0)),
                  pl.BlockSpec((1,) + row_mask.shape[1:], lambda bi, hp, j: (pattern(j), 0, 0))],
        out_specs=blk,
        out_shape=jax.ShapeDtypeStruct((b, n, w), F32),
        compiler_params=_cparams(("parallel", "parallel", "arbitrary")),
        name="na_attn",
    )(q, k, v, ck, cv, pair_tab, row_mask)


def _gdn_conv_kernel(x_ref, w_ref, o_ref, *, groups):
    c = pl.program_id(1)
    t = x_ref.shape[1]
    row = lax.broadcasted_iota(jnp.int32, (t, LANES), 0)
    n_qk = 2 * GDN_HEADS
    for i in range(groups):
        lanes = slice(i * LANES, (i + 1) * LANES)
        x = x_ref[0, :, lanes]
        y = jnp.zeros_like(x)
        for jj in range(CONV_K):
            o = jj - CONV_K // 2
            xs = x if o == 0 else pltpu.roll(x, (-o) % t, 0)
            xs = jnp.where((row + o >= 0) & (row + o < t), xs, 0.0)
            y = y + xs * w_ref[jj:jj + 1, lanes]
        y = _silu(y)
        nrm = lax.rsqrt(jnp.sum(y * y, axis=-1, keepdims=True) + EPS)
        o_ref[0, i] = jnp.where(c * groups + i < n_qk, y * nrm, y)


def _gdn_conv(x, conv_w):
    b, t, ch = x.shape
    nc = ch // LANES
    groups = max(g for g in range(1, nc + 1)
                 if nc % g == 0 and (g == 1 or g * t <= GDN_CONV_BLOCK_ROWS))
    return pl.pallas_call(
        functools.partial(_gdn_conv_kernel, groups=groups),
        grid=(b, nc // groups),
        in_specs=[pl.BlockSpec((1, t, groups * LANES), lambda bi, c: (bi, 0, c)),
                  pl.BlockSpec((CONV_K, groups * LANES), lambda bi, c: (0, c))],
        out_specs=pl.BlockSpec((1, groups, t, LANES), lambda bi, c: (bi, c, 0, 0)),
        out_shape=jax.ShapeDtypeStruct((b, nc, t, LANES), F32),
        compiler_params=_cparams(("parallel", "parallel")),
        name="gdn_conv",
    )(x, conv_w)


def _bdot(a, b):
    return jnp.dot(a.astype(BF16), b.astype(BF16), preferred_element_type=F32)


def _split_bf16(x):
    hi = x.astype(BF16)
    return hi, (x - hi.astype(F32)).astype(BF16)


def _dot3(a, b):
    m = a.shape[0]
    ah, al = _split_bf16(a)
    bh, bl = _split_bf16(b)
    top = _dot(jnp.concatenate([ah, al], axis=0), bh)
    return top[:m] + top[m:] + _dot(ah, bl)


def _gdn_chunk_kernel(qkv_ref, ba_ref, alog_ref, dt_ref, s0_ref, o_ref, sfin_ref, s_ref, *, reverse, cb, nb):
    c = pl.program_id(1)
    nh = GDN_HEADS

    @pl.when(c == 0)
    def _():
        s_ref[...] = s0_ref[:, 0]

    ii = lax.broadcasted_iota(jnp.int32, (CHUNK, CHUNK), 0)
    jj = lax.broadcasted_iota(jnp.int32, (CHUNK, CHUNK), 1)
    lag = (jj - ii) if reverse else (ii - jj)
    incl = lag >= 0
    strict = lag > 0
    eye = (ii == jj).astype(F32)
    tri = incl.astype(F32)
    bcol = nh if reverse else 0
    gcol0 = (3 if reverse else 2) * nh
    sub8 = lax.broadcasted_iota(jnp.int32, (8, LANES), 0)
    lane8 = lax.broadcasted_iota(jnp.int32, (8, LANES), 1)
    sel8 = (lane8 == gcol0 + sub8).astype(F32)
    nch = nb * cb
    units = [(ci, h) for ci in range(nch) for h in range(nh)]
    gc_alls, beta_alls, grow8s = [], [], []
    for ci in range(nch):
        ba = ba_ref[ci // cb, (ci % cb) * CHUNK:(ci % cb + 1) * CHUNK, :]
        z = ba + dt_ref[...]
        softplus = jnp.maximum(z, 0.0) + jnp.log1p(jnp.exp(-jnp.abs(z)))
        g_all = -jnp.exp(alog_ref[...]) * softplus
        gc_all = jnp.dot(tri, g_all, precision=HI, preferred_element_type=F32)
        gc_alls.append(gc_all)
        beta_alls.append(jax.nn.sigmoid(ba))
        grow8s.append(_dot_nt(sel8, gc_all, precision=HI))
    gcol = [gc_alls[ci][:, gcol0 + h:gcol0 + h + 1] for ci, h in units]
    beta = [beta_alls[ci][:, bcol + h:bcol + h + 1] for ci, h in units]
    rows = [slice((ci % cb) * CHUNK, (ci % cb + 1) * CHUNK) for ci, _ in units]
    seq = [ci // cb for ci, _ in units]
    k = [qkv_ref[seq[u], nh + h, rows[u], :] for u, (_, h) in enumerate(units)]
    kb = [k[u] * beta[u] for u in range(len(units))]
    q = [qkv_ref[seq[u], h, rows[u], :] * (GDN_DK ** -0.5) for u, (_, h) in enumerate(units)]
    kq = [_dot_nt(jnp.concatenate([kb[u], q[u]], axis=0).astype(BF16), k[u].astype(BF16))
          for u in range(len(units))]
    decay = [jnp.where(incl, jnp.exp(jnp.where(incl, gcol[u] - grow8s[ci][h:h + 1, :], 0.0)), 0.0)
             for u, (ci, h) in enumerate(units)]
    intra = [jnp.where(incl, kq[u][CHUNK:] * decay[u], 0.0) for u in range(len(units))]
    wide = nh * CHUNK
    bd_mask = (lax.broadcasted_iota(jnp.int32, (wide, wide), 0) // CHUNK
               == lax.broadcasted_iota(jnp.int32, (wide, wide), 1) // CHUNK)

    def block_diag(x):
        return jnp.where(bd_mask, jnp.concatenate([x] * nh, axis=0), jnp.zeros((), x.dtype))

    def dot3_bd(a, b):
        m = a.shape[0]
        ah, al = _split_bf16(a)
        bh, bl = _split_bf16(b)
        top = _dot(jnp.concatenate([ah, al], axis=0), block_diag(bh))
        return top[:m] + top[m:] + _dot(ah, block_diag(bl))

    eye_w = jnp.concatenate([eye] * nh, axis=1)
    pw = [jnp.concatenate([-jnp.where(strict, kq[ci * nh + h][:CHUNK] * decay[ci * nh + h], 0.0)
                           for h in range(nh)], axis=1) for ci in range(nch)]
    tmat = [eye_w + p for p in pw]
    pw = [dot3_bd(p, p) for p in pw]
    for _ in range(4):
        pt = [dot3_bd(jnp.concatenate([pw[ci], tmat[ci]], axis=0), pw[ci]) for ci in range(nch)]
        pw = [x[:CHUNK] for x in pt]
        tmat = [tmat[ci] + pt[ci][CHUNK:] for ci in range(nch)]
    tmat = [tmat[ci] + dot3_bd(tmat[ci], pw[ci]) for ci in range(nch)]
    eg = [jnp.exp(g) for g in gcol]
    uw_all = [_dot(block_diag(tmat[ci].astype(BF16)), jnp.concatenate(
        [jnp.concatenate([qkv_ref[ci // cb, 2 * nh + h, rows[ci * nh + h], :] * beta[ci * nh + h],
                          kb[ci * nh + h] * eg[ci * nh + h]], axis=1) for h in range(nh)],
        axis=0).astype(BF16)) for ci in range(nch)]
    uw = [uw_all[ci][h * CHUNK:(h + 1) * CHUNK] for ci, h in units]
    g_last = [g[0:1, :] if reverse else g[CHUNK - 1:CHUNK, :] for g in gcol]
    kd = [k[u] * jnp.exp(g_last[u] - gcol[u]) for u in range(len(units))]
    qe = [q[u] * eg[u] for u in range(len(units))]
    kd_uw = [lax.dot_general(kd[u].astype(BF16), uw[u].astype(BF16), (((0,), (0,)), ((), ())),
                             preferred_element_type=F32) for u in range(len(units))]
    in_uw = [_bdot(intra[u], uw[u]) for u in range(len(units))]
    lhs = [jnp.concatenate([kd_uw[u][:, GDN_DV:], qe[u] - in_uw[u][:, GDN_DV:]], axis=0).astype(BF16)
           for u in range(len(units))]
    s = [s_ref[bi, h] for bi in range(nb) for h in range(nh)]
    for cl in (reversed(range(cb)) if reverse else range(cb)):
        us = [(bi * cb + cl) * nh + h for bi in range(nb) for h in range(nh)]
        r = [_dot(lhs[u], s[i].astype(BF16)) for i, u in enumerate(us)]
        for i, u in enumerate(us):
            h = i % nh
            o_ref[i // nh, rows[u], h * GDN_DV:(h + 1) * GDN_DV] = r[i][GDN_DK:] + in_uw[u][:, :GDN_DV]
        s = [s[i] * jnp.exp(g_last[u]) - r[i][:GDN_DK] + kd_uw[u][:, :GDN_DV] for i, u in enumerate(us)]
    for i in range(nb * nh):
        s_ref[i // nh, i % nh] = s[i]

    @pl.when(c == pl.num_programs(1) - 1)
    def _():
        sfin_ref[...] = s_ref[...]


def _gdn_chunks(qkv, ba, alog_row, dt_row, s0, reverse):
    b, _, t, _ = qkv.shape
    cb = min(GDN_CHUNKS_PER_STEP, t // CHUNK)
    nb = max(g for g in range(1, b + 1) if b % g == 0 and g * cb <= GDN_CHUNKS_PER_STEP)
    rows = cb * CHUNK
    n = t // rows
    d = 1 if reverse else 0

    def blk(c):
        return n - 1 - c if reverse else c

    return pl.pallas_call(
        functools.partial(_gdn_chunk_kernel, reverse=reverse, cb=cb, nb=nb),
        grid=(b // nb, n),
        in_specs=[pl.BlockSpec((nb, 3 * GDN_HEADS, rows, LANES), lambda bi, c: (bi, 0, blk(c), 0)),
                  pl.BlockSpec((nb, rows, LANES), lambda bi, c: (bi, blk(c), 0)),
                  pl.BlockSpec((1, LANES), lambda bi, c: (0, 0)),
                  pl.BlockSpec((1, LANES), lambda bi, c: (0, 0)),
                  pl.BlockSpec((nb, 1, GDN_HEADS, GDN_DK, GDN_DV), lambda bi, c: (bi, d, 0, 0, 0))],
        out_specs=[pl.BlockSpec((nb, rows, GDN_V_WIDTH), lambda bi, c: (bi, blk(c), 0)),
                   pl.BlockSpec((nb, GDN_HEADS, GDN_DK, GDN_DV), lambda bi, c: (bi, 0, 0, 0))],
        out_shape=[jax.ShapeDtypeStruct((b, t, GDN_V_WIDTH), F32),
                   jax.ShapeDtypeStruct((b, GDN_HEADS, GDN_DK, GDN_DV), F32)],
        scratch_shapes=[pltpu.VMEM((nb, GDN_HEADS, GDN_DK, GDN_DV), F32)],
        compiler_params=_cparams(("parallel", "arbitrary")),
        name="gdn_bwd" if reverse else "gdn_fwd",
    )(qkv, ba, alog_row, dt_row, s0)


def _pack_bf16_pairs(h):
    half = D_MODEL // 2
    lo = pltpu.bitcast(h[:, :half].astype(BF16).astype(F32), jnp.uint32)
    hi = pltpu.bitcast(h[:, half:].astype(BF16).astype(F32), jnp.uint32)
    return (hi & jnp.uint32(0xFFFF0000)) | (lo >> 16)


def _unpack_bf16_pairs(w):
    lo = pltpu.bitcast(w << 16, F32).astype(BF16)
    hi = pltpu.bitcast(w & jnp.uint32(0xFFFF0000), F32).astype(BF16)
    return lo, hi


def _mix_kernel(x_ref, na_ref, of_ref, ob_ref, z_ref, gate_ref, mod_ref, gnw_ref, gpost_ref, gpre_ref,
                wna_ref, wgdn_ref, wout_ref, wrh_ref, wrl_ref, rb_ref,
                x1_ref, h2p_ref, e_ref, wt_ref, cnt_ref, cnt_acc_ref):
    mod = mod_ref[0]
    o = of_ref[...] + ob_ref[...]
    parts = []
    for h in range(GDN_HEADS):
        sl = slice(h * GDN_DV, (h + 1) * GDN_DV)
        parts.append(_rms(o[:, sl], gnw_ref[...]) * _silu(z_ref[:, sl]))
    gdn_o = jnp.concatenate(parts, axis=-1)
    a = _dot(na_ref[...].astype(BF16), wna_ref[...])
    b = _dot(gdn_o.astype(BF16), wgdn_ref[...])
    gate = jax.nn.sigmoid(gate_ref[...])
    pre = gate[:, :D_MODEL] * a + gate[:, D_MODEL:] * b
    mix = _dot(pre.astype(BF16), wout_ref[...])
    x1 = x_ref[...] + mod[:, 2 * D_MODEL:3 * D_MODEL] * _rms(mix, gpost_ref[...])
    x1_ref[...] = x1
    h2 = _rms(x1, gpre_ref[...]) * (1.0 + mod[:, 4 * D_MODEL:5 * D_MODEL]) + mod[:, 3 * D_MODEL:4 * D_MODEL]
    _route(h2, wrh_ref, wrl_ref, rb_ref, e_ref, wt_ref, cnt_ref, cnt_acc_ref)
    packed = _pack_bf16_pairs(h2)
    for c in range(PACK_ROWS):
        h2p_ref[pl.ds(c, h2.shape[0], stride=PACK_ROWS), :] = packed[:, c * LANES:(c + 1) * LANES]


def _mix(x, na_o, o_f, o_b, z, gate, mods3, gnw, gpost, gpre, wna, wgdn, wout, w_rt_hi, w_rt_lo, bias_col,
         row_of_token):
    n = x.shape[0]
    tm = WIDE_TILE
    row = lambda w: pl.BlockSpec((tm, w), lambda i: (i, 0))
    const = lambda r, c: pl.BlockSpec((r, c), lambda i: (0, 0))
    return pl.pallas_call(
        _mix_kernel,
        grid=(n // tm,),
        in_specs=[row(D_MODEL), row(NA_WIDTH), row(GDN_V_WIDTH), row(GDN_V_WIDTH),
                  row(GDN_V_WIDTH), row(2 * D_MODEL),
                  pl.BlockSpec((1, 1, 6 * D_MODEL), lambda i: (row_of_token(i * tm), 0, 0)),
                  const(1, GDN_DV), const(1, D_MODEL), const(1, D_MODEL),
                  const(NA_WIDTH, D_MODEL), const(GDN_V_WIDTH, D_MODEL), const(D_MODEL, D_MODEL),
                  const(N_EXPERTS, D_MODEL), const(N_EXPERTS, D_MODEL), const(N_EXPERTS, LANES)],
        out_specs=[row(D_MODEL), pl.BlockSpec((tm * PACK_ROWS, LANES), lambda i: (i, 0)),
                   pl.BlockSpec((TOP_K, tm), lambda i: (0, i)),
                   pl.BlockSpec((TOP_K, tm), lambda i: (0, i)),
                   const(N_EXPERTS, LANES)],
        out_shape=[jax.ShapeDtypeStruct((n, D_MODEL), F32),
                   jax.ShapeDtypeStruct((n * PACK_ROWS, LANES), jnp.uint32),
                   jax.ShapeDtypeStruct((TOP_K, n), jnp.int32),
                   jax.ShapeDtypeStruct((TOP_K, n), F32),
                   jax.ShapeDtypeStruct((N_EXPERTS, LANES), F32)],
        scratch_shapes=[pltpu.VMEM((N_EXPERTS, LANES), F32)],
        compiler_params=_cparams(("arbitrary",)),
        name="mix",
    )(x, na_o, o_f, o_b, z, gate, mods3, gnw.reshape(1, GDN_DV), gpost.reshape(1, D_MODEL),
      gpre.reshape(1, D_MODEL), wna, wgdn, wout, w_rt_hi, w_rt_lo, bias_col)


def _route(h, wh_ref, wl_ref, b_ref, e_ref, wt_ref, cnt_ref, acc_ref):
    i = pl.program_id(0)
    tm = h.shape[0]
    ne = N_EXPERTS
    per = ne // N_GROUPS
    hh, hl = _split_bf16(h)
    wh = wh_ref[...]
    logits = _dot_nt(wh, hh) + _dot_nt(wl_ref[...], hh) + _dot_nt(wh, hl)
    scores = jax.nn.sigmoid(logits)
    biased = scores + jnp.concatenate([b_ref[...]] * (tm // LANES), axis=1)

    def first_max(x):
        n = x.shape[0]
        iota = lax.broadcasted_iota(jnp.int32, x.shape, 0).astype(F32)
        m = jnp.max(x, axis=0, keepdims=True)
        idx = jnp.min(jnp.where(x == m, iota, float(n)), axis=0, keepdims=True)
        return m, idx, iota

    gs_rows = []
    for g in range(N_GROUPS):
        bg = biased[g * per:(g + 1) * per]
        m1, i1, iota = first_max(bg)
        m2 = jnp.max(jnp.where(iota == i1, NEG_INF, bg), axis=0, keepdims=True)
        gs_rows.append(m1 + m2)
    gs = jnp.concatenate(gs_rows, axis=0)
    gsel = jnp.zeros(gs.shape, F32)
    for _ in range(TOPK_GROUPS):
        _, gi, iota = first_max(gs)
        hit = iota == gi
        gs = jnp.where(hit, NEG_INF, gs)
        gsel = jnp.where(hit, 1.0, gsel)
    masked = jnp.concatenate(
        [jnp.where(gsel[g:g + 1] > 0.0, biased[g * per:(g + 1) * per], NEG_INF) for g in range(N_GROUPS)], axis=0)
    onehot = jnp.zeros((ne, tm), F32)
    e_rows, w_rows = [], []
    for _ in range(TOP_K):
        _, ei, iota = first_max(masked)
        hit = iota == ei
        masked = jnp.where(hit, NEG_INF, masked)
        w_rows.append(jnp.sum(jnp.where(hit, scores, 0.0), axis=0, keepdims=True))
        e_rows.append(ei)
        onehot = onehot + hit.astype(F32)
    w_out = jnp.concatenate(w_rows, axis=0)
    e_ref[...] = jnp.concatenate(e_rows, axis=0).astype(jnp.int32)
    wt_ref[...] = w_out / jnp.sum(w_out, axis=0, keepdims=True) * ROUTED_SCALE

    @pl.when(i == 0)
    def _():
        acc_ref[...] = jnp.zeros_like(acc_ref)

    acc_ref[...] += sum(onehot[:, c * LANES:(c + 1) * LANES] for c in range(tm // LANES))

    @pl.when(i == pl.num_programs(0) - 1)
    def _():
        cnt_ref[...] = jnp.broadcast_to(jnp.sum(acc_ref[...], axis=1, keepdims=True), cnt_ref.shape)


_ST_SLOT, _ST_READY, _ST_PEND_BASE = range(3)


def _moe_kernel(tok_ref, seg_ref, cnt_ref, h2p_hbm, w2d_ref, wg_ref, wu_ref, wd_ref, out_hbm,
                h2p_ref, acc_ref, xs_ref, ye_ref, st_ref, sem):
    e = pl.program_id(0)
    n_exp = pl.num_programs(0)
    half = D_MODEL // 2
    xw, yw = PACK_ROWS, ACC_ROWS
    n_tok = out_hbm.shape[0] // yw
    grp = MOE_ROW_GROUP

    n_groups = MOE_ROWS // grp

    def gather_tile(base, slot, groups=(0, n_groups)):
        for r in range(groups[0] * grp, groups[1] * grp):
            tok = tok_ref[base + r]
            xs_ref[slot, xw * r:xw * (r + 1), :] = h2p_ref[pl.ds(pl.multiple_of(tok * xw, xw), xw), :]

    def scatter_tile(base, slot, groups=(0, n_groups)):
        for g in range(*groups):
            toks = [pl.multiple_of(tok_ref[base + g * grp + j] * yw, yw) for j in range(grp)]
            rows = [acc_ref[pl.ds(toks[j], yw), :] + ye_ref[slot, yw * (g * grp + j):yw * (g * grp + j + 1), :]
                    for j in range(grp)]
            for j in reversed(range(grp)):
                acc_ref[pl.ds(toks[j], yw), :] = rows[j]

    seg = seg_ref[e]
    cnt = cnt_ref[e]
    n_tiles = (cnt + MOE_ROWS - 1) // MOE_ROWS

    @pl.when(e == 0)
    def _():
        cp = pltpu.make_async_copy(h2p_hbm, h2p_ref.at[pl.ds(0, n_tok * xw)], sem.at[0])
        cp.start()
        acc_ref[...] = jnp.zeros_like(acc_ref)
        ye_ref[...] = jnp.zeros_like(ye_ref)
        h2p_ref[pl.ds(n_tok * xw, grp * xw), :] = jnp.zeros((grp * xw, LANES), jnp.uint32)
        st_ref[_ST_SLOT] = 0
        st_ref[_ST_READY] = -1
        st_ref[_ST_PEND_BASE] = 0
        cp.wait()

    @pl.when((n_tiles > 0) & (st_ref[_ST_READY] != e))
    def _():
        gather_tile(seg, st_ref[_ST_SLOT])

    def chunk(w_ref, c):
        return jnp.concatenate([w_ref[0, LANES * c:LANES * (c + 1), :],
                                w_ref[0, half + LANES * c:half + LANES * (c + 1), :]], axis=0).astype(BF16)

    lane = lax.broadcasted_iota(jnp.int32, (8, LANES), 1)
    nxt_seg = seg_ref[jnp.minimum(e + 1, n_exp - 1)]

    def tile_body(t, carry):
        slot = st_ref[_ST_SLOT]
        base = seg + t * MOE_ROWS
        last = t + 1 == n_tiles
        xk = [jnp.concatenate(_unpack_bf16_pairs(xs_ref[slot, pl.ds(c, MOE_ROWS, stride=xw), :]), axis=1)
              for c in range(xw)]
        g_base = jnp.where(last, nxt_seg, base + MOE_ROWS)
        p_base = st_ref[_ST_PEND_BASE]
        cuts = [n_groups * i // (xw + 1) for i in range(xw + 2)]

        def move_rows(i):
            gather_tile(g_base, 1 - slot, (cuts[i], cuts[i + 1]))
            scatter_tile(p_base, 1 - slot, (cuts[i], cuts[i + 1]))

        hg = hu = 0.0
        for c in range(xw):
            move_rows(c)
            hg = hg + _dot(xk[c], chunk(wg_ref, c))
            hu = hu + _dot(xk[c], chunk(wu_ref, c))
        act = (_silu(hg) * hu).astype(BF16)
        move_rows(xw)
        q = base // LANES
        sh = base % LANES
        rot = (LANES - sh) % LANES
        row_a = pltpu.roll(jnp.broadcast_to(w2d_ref[pl.ds(q, 1), :], (8, LANES)), rot, 1)
        row_b = pltpu.roll(jnp.broadcast_to(w2d_ref[pl.ds(q + 1, 1), :], (8, LANES)), rot, 1)
        w_row = jnp.where(lane + sh < LANES, row_a, row_b)
        w_row = jnp.where(lane < cnt - t * MOE_ROWS, w_row, 0.0)[0:1, :]
        w_col = jnp.broadcast_to(w_row, (LANES, LANES)).T[:MOE_ROWS]
        ye = _dot(act, wd_ref[0].astype(BF16))
        for c in range(yw):
            ye_ref[slot, pl.ds(c, MOE_ROWS, stride=yw), :] = ye[:, LANES * c:LANES * (c + 1)] * w_col
        st_ref[_ST_PEND_BASE] = base
        st_ref[_ST_SLOT] = 1 - slot
        st_ref[_ST_READY] = jnp.where(last, e + 1, e)
        return carry

    lax.fori_loop(0, n_tiles, tile_body, 0)

    @pl.when(e == n_exp - 1)
    def _():
        scatter_tile(st_ref[_ST_PEND_BASE], 1 - st_ref[_ST_SLOT])
        cp = pltpu.make_async_copy(acc_ref.at[pl.ds(0, n_tok * yw)], out_hbm, sem.at[1])
        cp.start()
        cp.wait()


def _moe(tok_sorted, seg_start, seg_count, h2p, w2d, wg, wu, wd):
    xw, yw = PACK_ROWS, ACC_ROWS
    n = h2p.shape[0] // xw
    grid_spec = pltpu.PrefetchScalarGridSpec(
        num_scalar_prefetch=3,
        grid=(N_EXPERTS,),
        in_specs=[pl.BlockSpec(memory_space=pl.ANY),
                  pl.BlockSpec(w2d.shape, lambda e, *_: (0, 0)),
                  pl.BlockSpec((1, D_MODEL, EXPERT_DIM), lambda e, *_: (e, 0, 0)),
                  pl.BlockSpec((1, D_MODEL, EXPERT_DIM), lambda e, *_: (e, 0, 0)),
                  pl.BlockSpec((1, EXPERT_DIM, D_MODEL), lambda e, *_: (e, 0, 0))],
        out_specs=pl.BlockSpec(memory_space=pl.ANY),
        scratch_shapes=[pltpu.VMEM(((n + MOE_ROW_GROUP) * xw, LANES), jnp.uint32),
                        pltpu.VMEM(((n + MOE_ROW_GROUP) * yw, LANES), F32),
                        pltpu.VMEM((2, MOE_ROWS * xw, LANES), jnp.uint32),
                        pltpu.VMEM((2, MOE_ROWS * yw, LANES), F32),
                        pltpu.SMEM((3,), jnp.int32),
                        pltpu.SemaphoreType.DMA((2,))],
    )
    return pl.pallas_call(
        _moe_kernel,
        grid_spec=grid_spec,
        out_shape=jax.ShapeDtypeStruct((n * yw, LANES), F32),
        compiler_params=_cparams(("arbitrary",), vmem=MOE_VMEM_LIMIT),
        name="moe",
    )(tok_sorted, seg_start, seg_count, h2p, w2d, wg, wu, wd)


def _moe_routed(top_e, top_w, counts, h2p, wg, wu, wd):
    tok_sorted, w2d, seg_start, seg_count = _moe_dispatch_plan(top_e, top_w, counts, h2p.shape[0] // PACK_ROWS)
    return _moe(tok_sorted, seg_start, seg_count, h2p, w2d, wg, wu, wd)


def _moe_dispatch_plan(top_e, top_w, counts, n):
    total = n * TOP_K
    assert N_EXPERTS * total < 2 ** 31
    key = top_e.reshape(-1) * total + jnp.arange(total, dtype=jnp.int32)
    key_sorted, w_sorted = lax.sort((key, top_w.reshape(-1)), num_keys=1, is_stable=False)
    tok_sorted = (key_sorted % total) % n
    tok_sorted = jnp.concatenate([tok_sorted, jnp.full((MOE_ROWS,), n, jnp.int32)])
    table_rows = -(-(total // LANES + 2) // 8) * 8
    w2d = jnp.concatenate([w_sorted, jnp.zeros((table_rows * LANES - total,), F32)]).reshape(table_rows, LANES)
    cnt = counts[:, 0].astype(jnp.int32)
    return tok_sorted, w2d, jnp.cumsum(cnt) - cnt, cnt


def _regroup_rows(w):
    half = D_MODEL // 2
    return jnp.concatenate([w[r0 + LANES * c:r0 + LANES * (c + 1)]
                            for c in range(PACK_ROWS) for r0 in (0, half)], axis=0)


def _final_kernel(x1_ref, h2p_ref, r_ref, mod_ref, g_ref, wg_ref, wu_ref, wd_ref, y_ref):
    tm = x1_ref.shape[0]
    kc = 2 * LANES
    mod = mod_ref[0]
    xk = [jnp.concatenate(_unpack_bf16_pairs(h2p_ref[pl.ds(c, tm, stride=PACK_ROWS), :]), axis=1)
          for c in range(PACK_ROWS)]
    hg = sum(_dot(xk[c], wg_ref[kc * c:kc * (c + 1), :]) for c in range(PACK_ROWS))
    hu = sum(_dot(xk[c], wu_ref[kc * c:kc * (c + 1), :]) for c in range(PACK_ROWS))
    shared = _dot((_silu(hg) * hu).astype(BF16), wd_ref[...])
    routed = jnp.concatenate([r_ref[pl.ds(c, tm, stride=ACC_ROWS), :] for c in range(ACC_ROWS)], axis=1)
    ffn = routed + shared
    y_ref[...] = x1_ref[...] + mod[:, 5 * D_MODEL:6 * D_MODEL] * _rms(ffn, g_ref[...])


def _final(x1, h2p, routed, mods3, g, wg, wu, wd, row_of_token):
    n = x1.shape[0]
    tm = WIDE_TILE
    sd = wg.shape[1]
    row = lambda w: pl.BlockSpec((tm, w), lambda i: (i, 0))
    const = lambda r, c: pl.BlockSpec((r, c), lambda i: (0, 0))
    return pl.pallas_call(
        _final_kernel,
        grid=(n // tm,),
        in_specs=[row(D_MODEL), pl.BlockSpec((tm * PACK_ROWS, LANES), lambda i: (i, 0)),
                  pl.BlockSpec((tm * ACC_ROWS, LANES), lambda i: (i, 0)),
                  pl.BlockSpec((1, 1, 6 * D_MODEL), lambda i: (row_of_token(i * tm), 0, 0)),
                  const(1, D_MODEL), const(D_MODEL, sd), const(D_MODEL, sd), const(sd, D_MODEL)],
        out_specs=row(D_MODEL),
        out_shape=jax.ShapeDtypeStruct((n, D_MODEL), F32),
        compiler_params=_cparams(("parallel",)),
        name="final",
    )(x1, h2p, routed, mods3, g.reshape(1, D_MODEL), wg, wu, wd)


def _trunk(x3, mods3, row_of_token, attend, s0, wts):
    b, t, _ = x3.shape
    n = b * t
    x = x3.reshape(n, D_MODEL)
    q, k, v, gdn, z, gate, ba = _premix(x, mods3, wts["g_pre_mix"], wts["w_cat"], row_of_token)
    na_o = attend(q, k, v).reshape(n, NA_WIDTH)
    qkv = _gdn_conv(gdn.reshape(b, t, GDN_CONV_CH), wts["conv_w"])
    ba3 = ba.reshape(b, t, LANES)
    o_f, s_f = _gdn_chunks(qkv, ba3, wts["alog_row"], wts["dt_row"], s0, reverse=False)
    o_b, s_b = _gdn_chunks(qkv, ba3, wts["alog_row"], wts["dt_row"], s0, reverse=True)
    s_fin = jnp.stack([s_f, s_b], axis=1)
    x1, h2p, top_e, top_w, counts = _mix(
        x, na_o, o_f.reshape(n, GDN_V_WIDTH), o_b.reshape(n, GDN_V_WIDTH), z, gate, mods3, wts["gdn_norm_w"],
        wts["g_post_mix"], wts["g_pre_ffn"], wts["w_na_up"], wts["w_gdn_up"], wts["w_out"],
        wts["w_rt_hi"], wts["w_rt_lo"], wts["router_bias_col"], row_of_token)
    routed = _moe_routed(top_e, top_w, counts, h2p, wts["w_exp_gate"], wts["w_exp_up"], wts["w_exp_down"])
    y = _final(x1, h2p, routed, mods3, wts["g_post_ffn"], wts["w_sh_gate"], wts["w_sh_up"],
               wts["w_sh_down"], row_of_token)
    return y.reshape(b, t, D_MODEL), k, v, s_fin


def kernel(x_prompt, x_sample, cache_na_k, cache_na_v, state_gdn, c, c_ctx, w_ada, b_ada, g_pre_mix,
           g_post_mix, g_pre_ffn, g_post_ffn, w_in, conv_w, gdn_a_log, gdn_dt_bias, gdn_norm_w, na_rpb,
           w_na_up, w_gdn_up, w_out, w_router, router_bias, w_exp_gate, w_exp_up, w_exp_down, w_sh_gate,
           w_sh_up, w_sh_down):
    depth = w_ada.shape[0]
    bp, tp, _ = x_prompt.shape
    bs, ts, _ = x_sample.shape
    y_prompt, y_sample = x_prompt, x_sample
    zero_state = jnp.zeros((bp, 2, GDN_HEADS, GDN_DK, GDN_DV), F32)
    new_k, new_v, new_s = [], [], []
    for l in range(depth):
        cv = jnp.concatenate([c_ctx[None], c, jnp.zeros((SUBLANES - 1 - bs, D_MODEL), F32)], axis=0)
        mods3 = _ada(cv, w_ada[l], b_ada[l]).reshape(SUBLANES, 1, 6 * D_MODEL)
        wl = w_in[l]
        w_cat = jnp.concatenate(
            [wl[:, :S_Z], wl[:, S_A:], wl[:, S_Z:S_A],
             jnp.zeros((D_MODEL, LANES - 4 * GDN_HEADS), F32)], axis=1).astype(BF16)
        pad = jnp.zeros((2 * GDN_HEADS,), F32)
        tail = jnp.zeros((LANES - 4 * GDN_HEADS,), F32)
        w_rt = w_router[l].astype(F32).T
        w_rt_hi = w_rt.astype(BF16)
        wts = dict(
            w_cat=w_cat, g_pre_mix=g_pre_mix[l], g_post_mix=g_post_mix[l], g_pre_ffn=g_pre_ffn[l],
            g_post_ffn=g_post_ffn[l], conv_w=conv_w[l], gdn_norm_w=gdn_norm_w[l],
            alog_row=jnp.concatenate([pad, gdn_a_log[l].reshape(-1), tail]).reshape(1, LANES),
            dt_row=jnp.concatenate([pad, gdn_dt_bias[l].reshape(-1), tail]).reshape(1, LANES),
            w_na_up=w_na_up[l].astype(BF16), w_gdn_up=w_gdn_up[l].astype(BF16),
            w_out=w_out[l].astype(BF16), w_rt_hi=w_rt_hi, w_rt_lo=(w_rt - w_rt_hi.astype(F32)).astype(BF16),
            router_bias_col=jnp.broadcast_to(router_bias[l].astype(F32)[:, None], (N_EXPERTS, LANES)),
            w_exp_gate=w_exp_gate[l], w_exp_up=w_exp_up[l], w_exp_down=w_exp_down[l],
            w_sh_gate=_regroup_rows(w_sh_gate[l]).astype(BF16), w_sh_up=_regroup_rows(w_sh_up[l]).astype(BF16),
            w_sh_down=w_sh_down[l].astype(BF16))

        def ctx_attend(q, k, v):
            return _ctx_attn(q.reshape(bp, tp, NA_WIDTH), k.reshape(bp, tp, NA_WIDTH),
                             v.reshape(bp, tp, NA_WIDTH))

        y_prompt, k_ctx, v_ctx, s_ctx = _trunk(y_prompt, mods3, lambda t0: 0, ctx_attend, zero_state, wts)
        new_k.append(k_ctx.reshape(bp, tp, NA_HEADS, NA_HEAD_DIM))
        new_v.append(v_ctx.reshape(bp, tp, NA_HEADS, NA_HEAD_DIM))
        new_s.append(s_ctx)

        pair_tab, row_mask = _na_bias_tables(na_rpb[l], ts // GRID_W)
        ck = cache_na_k[:, l].reshape(bs, -1, NA_WIDTH)
        cvv = cache_na_v[:, l].reshape(bs, -1, NA_WIDTH)

        def na_attend(q, k, v):
            return _na_attn(q.reshape(bs, ts, NA_WIDTH), k.reshape(bs, ts, NA_WIDTH),
                            v.reshape(bs, ts, NA_WIDTH), ck, cvv, pair_tab, row_mask)

        y_sample, _, _, _ = _trunk(y_sample, mods3, lambda t0: 1 + t0 // ts, na_attend,
                                   state_gdn[:, l], wts)
    return (y_prompt, y_sample, jnp.stack(new_k, axis=1), jnp.stack(new_v, axis=1),
            jnp.stack(new_s, axis=1))
```
